```python
import math
import jax, jax.numpy as jnp
from jax import lax
import numpy as np

D_MODEL = 1024
BATCH = 2
SEQ = 8192
DEPTH = 1

N_META = 16
GRID_W = 64
N_HEADS = 8
N_KV_HEADS = 2
HEAD_DIM = 64
Q_BLOCK = 128
ROPE_THETA = 10000.0
ATTN_WIDTH = N_HEADS * HEAD_DIM
KV_WIDTH = N_KV_HEADS * HEAD_DIM
SSM_WIDTH = D_MODEL // 2
SSM_GROUP = 16
SSM_GROUPS = SSM_WIDTH // SSM_GROUP
SSM_STATE = 64
DT_MIN = 1e-3
DT_MAX = 1e-1
N_EXPERTS = 16
EXPERT_FF = D_MODEL
CAPACITY_FACTOR = 2
IN_WIDTH = ATTN_WIDTH + 2 * KV_WIDTH + SSM_WIDTH + 2 * D_MODEL
SPLITS = (ATTN_WIDTH,
          ATTN_WIDTH + KV_WIDTH,
          ATTN_WIDTH + 2 * KV_WIDTH,
          ATTN_WIDTH + 2 * KV_WIDTH + SSM_WIDTH,
          ATTN_WIDTH + 2 * KV_WIDTH + SSM_WIDTH + D_MODEL)
DEEPNORM_ALPHA = (2.0 * DEPTH) ** 0.25
DEEPNORM_BETA = (8.0 * DEPTH) ** -0.25
LN_EPS = 1e-5
QK_EPS = 1e-6

kernel_name = "hybrid_gqa_s5_ecmoe_encoder"


def layer_norm(x, g, b):
    xf = x.astype(jnp.float32)
    mu = jnp.mean(xf, axis=-1, keepdims=True)
    var = jnp.mean(jnp.square(xf - mu), axis=-1, keepdims=True)
    y = (xf - mu) * lax.rsqrt(var + LN_EPS) * g.astype(jnp.float32) + b.astype(jnp.float32)
    return y.astype(x.dtype)


def rms_norm(x, g):
    xf = x.astype(jnp.float32)
    y = xf * lax.rsqrt(jnp.mean(jnp.square(xf), axis=-1, keepdims=True) + QK_EPS)
    return (y * g.astype(jnp.float32)).astype(x.dtype)


def axial_rope_angles(n_tokens):
    rows = n_tokens // GRID_W
    half = HEAD_DIM // 2
    inv_freq = ROPE_THETA ** (-jnp.arange(0, half, 2, dtype=jnp.float32) / half)
    row = jnp.repeat(jnp.arange(rows, dtype=jnp.float32), GRID_W)
    col = jnp.tile(jnp.arange(GRID_W, dtype=jnp.float32), rows)
    ang = jnp.concatenate([row[:, None] * inv_freq, col[:, None] * inv_freq], axis=-1)
    return jnp.concatenate([jnp.zeros((N_META, half), jnp.float32), ang], axis=0)


def apply_rope(x, cos, sin):
    xf = x.astype(jnp.float32).reshape(x.shape[:-1] + (HEAD_DIM // 2, 2))
    x0, x1 = xf[..., 0], xf[..., 1]
    c = cos[None, :, None, :]
    s = sin[None, :, None, :]
    out = jnp.stack([x0 * c - x1 * s, x0 * s + x1 * c], axis=-1)
    return out.reshape(x.shape).astype(x.dtype)


def block_attention(q, k, v):
    b_, l_ = q.shape[0], q.shape[1]
    n_blocks = -(-l_ // Q_BLOCK)
    lp = n_blocks * Q_BLOCK
    grp = N_HEADS // N_KV_HEADS
    qg = q.reshape(b_, l_, N_KV_HEADS, grp, HEAD_DIM)
    qg = jnp.pad(qg, ((0, 0), (0, lp - l_), (0, 0), (0, 0), (0, 0)))
    qb = qg.reshape(b_, n_blocks, Q_BLOCK, N_KV_HEADS, grp, HEAD_DIM).transpose(1, 0, 2, 3, 4, 5)
    scale = HEAD_DIM ** -0.5

    def one_block(qblk):
        s = jnp.einsum('bqkgd,bskd->bkgqs', qblk, k, preferred_element_type=jnp.float32) * scale
        p = jax.nn.softmax(s, axis=-1).astype(v.dtype)
        return jnp.einsum('bkgqs,bskd->bqkgd', p, v)

    ob = lax.map(one_block, qb)
    return ob.transpose(1, 0, 2, 3, 4, 5).reshape(b_, lp, ATTN_WIDTH)[:, :l_]


def zoh_discretise(a_re, a_im, log_dt, b_re, b_im):
    dt = jnp.exp(log_dt.astype(jnp.float32))[:, None]
    ar = a_re.astype(jnp.float32)
    ai = a_im.astype(jnp.float32)
    mag = jnp.exp(ar * dt)
    ang = ai * dt
    abar_re = mag * jnp.cos(ang)
    abar_im = mag * jnp.sin(ang)
    nr = abar_re - 1.0
    ni = abar_im
    den = ar * ar + ai * ai
    coef_re = (nr * ar + ni * ai) / den
    coef_im = (ni * ar - nr * ai) / den
    br = b_re.astype(jnp.float32)
    bi = b_im.astype(jnp.float32)
    bbar_re = coef_re[..., None] * br - coef_im[..., None] * bi
    bbar_im = coef_re[..., None] * bi + coef_im[..., None] * br
    return abar_re, abar_im, bbar_re, bbar_im


def complex_linear_scan(abar_re, abar_im, bu_re, bu_im):
    l_ = bu_re.shape[0]
    a_re = jnp.broadcast_to(abar_re, (l_,) + abar_re.shape)
    a_im = jnp.broadcast_to(abar_im, (l_,) + abar_im.shape)

    def combine(e1, e2):
        a1r, a1i, b1r, b1i = e1
        a2r, a2i, b2r, b2i = e2
        ar = a2r * a1r - a2i * a1i
        ai = a2r * a1i + a2i * a1r
        a2r_b = a2r[:, None]
        a2i_b = a2i[:, None]
        br = a2r_b * b1r - a2i_b * b1i + b2r
        bi = a2r_b * b1i + a2i_b * b1r + b2i
        return ar, ai, br, bi

    _, _, s_re, s_im = lax.associative_scan(combine, (a_re, a_im, bu_re, bu_im), axis=0)
    return s_re, s_im


def s5_branch(u, a_re, a_im, log_dt, b_re, b_im, c_re, c_im, d, w_glu, b_glu):
    b_, l_ = u.shape[0], u.shape[1]
    ug = u.reshape(b_, l_, SSM_GROUPS, SSM_GROUP).astype(jnp.float32)
    y = d.astype(jnp.float32) * ug
    for direction in range(2):
        abr, abi, bbr, bbi = zoh_discretise(a_re[direction], a_im[direction], log_dt[direction],
                                            b_re[direction], b_im[direction])
        src = ug if direction == 0 else jnp.flip(ug, axis=1)
        bu_re = jnp.einsum('blgh,gph->lbgp', src, bbr)
        bu_im = jnp.einsum('blgh,gph->lbgp', src, bbi)
        s_re, s_im = complex_linear_scan(abr, abi, bu_re, bu_im)
        yd = (jnp.einsum('lbgp,ghp->blgh', s_re, c_re[direction].astype(jnp.float32))
              - jnp.einsum('lbgp,ghp->blgh', s_im, c_im[direction].astype(jnp.float32)))
        if direction == 1:
            yd = jnp.flip(yd, axis=1)
        y = y + yd
    y = jax.nn.gelu(y.reshape(b_, l_, SSM_WIDTH))
    y = y * jax.nn.sigmoid(y @ w_glu.astype(jnp.float32) + b_glu.astype(jnp.float32))
    return y.astype(u.dtype)


def expert_choice_moe(t, w_router, w_gate, w_up, w_down):
    b_, s_, d_ = t.shape
    cap = CAPACITY_FACTOR * s_ // N_EXPERTS
    logits = jnp.einsum('bsd,de->bes', t, w_router, preferred_element_type=jnp.float32)
    affinity = jax.nn.softmax(logits, axis=1)
    gate, idx = lax.top_k(affinity, cap)
    xe = jax.vmap(lambda tb, ib: tb[ib])(t, idx)
    h = (jax.nn.silu(jnp.einsum('becd,edf->becf', xe, w_gate))
         * jnp.einsum('becd,edf->becf', xe, w_up))
    ye = jnp.einsum('becf,efd->becd', h, w_down) * gate[..., None].astype(t.dtype)
    out = jax.vmap(lambda ib, yb: jnp.zeros((s_, d_), t.dtype).at[ib.reshape(-1)].add(
        yb.reshape(-1, d_)))(idx, ye)
    return out


def setup_inputs(seed: int = 0) -> dict:
    key = jax.random.key(seed)
    ks = jax.random.split(key, 32)
    f32 = jnp.float32
    n = lambda k, shape: jax.random.normal(k, shape, f32)
    P = SSM_STATE
    G = SSM_GROUPS
    H = SSM_GROUP
    a_im_init = math.pi * jnp.arange(P, dtype=f32)
    return {
        "x": n(ks[0], (BATCH, SEQ, D_MODEL)),
        "meta_tokens": n(ks[1], (N_META, D_MODEL)),
        "ln_in_g": 1.0 + 0.02 * n(ks[2], (D_MODEL,)),
        "ln_in_b": 0.02 * n(ks[3], (D_MODEL,)),
        "w_in": n(ks[4], (DEPTH, D_MODEL, IN_WIDTH)) * D_MODEL ** -0.5,
        "q_norm_g": 1.0 + 0.02 * n(ks[5], (DEPTH, HEAD_DIM)),
        "k_norm_g": 1.0 + 0.02 * n(ks[6], (DEPTH, HEAD_DIM)),
        "ssm_a_re": -0.5 + 0.01 * n(ks[7], (DEPTH, 2, G, P)),
        "ssm_a_im": a_im_init + 0.01 * n(ks[8], (DEPTH, 2, G, P)),
        "ssm_log_dt": jax.random.uniform(ks[9], (DEPTH, 2, G), f32,
                                         math.log(DT_MIN), math.log(DT_MAX)),
        "ssm_b_re": n(ks[10], (DEPTH, 2, G, P, H)) * (2 * H) ** -0.5,
        "ssm_b_im": n(ks[11], (DEPTH, 2, G, P, H)) * (2 * H) ** -0.5,
        "ssm_c_re": n(ks[12], (DEPTH, 2, G, H, P)) * P ** -0.5,
        "ssm_c_im": n(ks[13], (DEPTH, 2, G, H, P)) * P ** -0.5,
        "ssm_d": n(ks[14], (DEPTH, G, H)),
        "w_glu": n(ks[15], (DEPTH, SSM_WIDTH, SSM_WIDTH)) * SSM_WIDTH ** -0.5,
        "b_glu": 0.02 * n(ks[16], (DEPTH, SSM_WIDTH)),
        "w_attn_br": n(ks[17], (DEPTH, ATTN_WIDTH, D_MODEL)) * ATTN_WIDTH ** -0.5,
        "w_ssm_br": n(ks[18], (DEPTH, SSM_WIDTH, D_MODEL)) * SSM_WIDTH ** -0.5,
        "w_o": n(ks[19], (DEPTH, D_MODEL, D_MODEL)) * D_MODEL ** -0.5 * DEEPNORM_BETA,
        "ln1_g": 1.0 + 0.02 * n(ks[20], (DEPTH, D_MODEL)),
        "ln1_b": 0.02 * n(ks[21], (DEPTH, D_MODEL)),
        "w_router": n(ks[22], (DEPTH, D_MODEL, N_EXPERTS)) * D_MODEL ** -0.5,
        "w_gate_e": n(ks[23], (DEPTH, N_EXPERTS, D_MODEL, EXPERT_FF)) * D_MODEL ** -0.5,
        "w_up_e": n(ks[24], (DEPTH, N_EXPERTS, D_MODEL, EXPERT_FF)) * D_MODEL ** -0.5,
        "w_down_e": n(ks[25], (DEPTH, N_EXPERTS, EXPERT_FF, D_MODEL)) * EXPERT_FF ** -0.5 * DEEPNORM_BETA,
        "ln2_g": 1.0 + 0.02 * n(ks[26], (DEPTH, D_MODEL)),
        "ln2_b": 0.02 * n(ks[27], (DEPTH, D_MODEL)),
    }


def reference(x, meta_tokens, ln_in_g, ln_in_b, w_in, q_norm_g, k_norm_g,
              ssm_a_re, ssm_a_im, ssm_log_dt, ssm_b_re, ssm_b_im, ssm_c_re, ssm_c_im,
              ssm_d, w_glu, b_glu, w_attn_br, w_ssm_br, w_o, ln1_g, ln1_b,
              w_router, w_gate_e, w_up_e, w_down_e, ln2_g, ln2_b):
    b_, s_, d_ = x.shape
    l_ = s_ + N_META
    meta = jnp.broadcast_to(meta_tokens.astype(x.dtype)[None], (b_, N_META, d_))
    h = layer_norm(jnp.concatenate([meta, x], axis=1), ln_in_g, ln_in_b)
    ang = axial_rope_angles(s_)
    cos, sin = jnp.cos(ang), jnp.sin(ang)
    for l in range(DEPTH):
        proj = h @ w_in[l]
        q, k, v, u, g_attn, g_ssm = jnp.split(proj, SPLITS, axis=-1)
        q = apply_rope(rms_norm(q.reshape(b_, l_, N_HEADS, HEAD_DIM), q_norm_g[l]), cos, sin)
        k = apply_rope(rms_norm(k.reshape(b_, l_, N_KV_HEADS, HEAD_DIM), k_norm_g[l]), cos, sin)
        v = v.reshape(b_, l_, N_KV_HEADS, HEAD_DIM)
        y_attn = block_attention(q, k, v)
        y_ssm = s5_branch(u, ssm_a_re[l], ssm_a_im[l], ssm_log_dt[l], ssm_b_re[l], ssm_b_im[l],
                          ssm_c_re[l], ssm_c_im[l], ssm_d[l], w_glu[l], b_glu[l])
        merged = (jax.nn.sigmoid(g_attn) * (y_attn @ w_attn_br[l])
                  + jax.nn.sigmoid(g_ssm) * (y_ssm @ w_ssm_br[l]))
        h = layer_norm(DEEPNORM_ALPHA * h + merged @ w_o[l], ln1_g[l], ln1_b[l])
        moe = expert_choice_moe(h[:, N_META:], w_router[l], w_gate_e[l], w_up_e[l], w_down_e[l])
        moe = jnp.concatenate([jnp.zeros((b_, N_META, d_), h.dtype), moe], axis=1)
        h = layer_norm(DEEPNORM_ALPHA * h + moe, ln2_g[l], ln2_b[l])
    return h[:, N_META:]
```

```python
import functools
import math

import numpy as np
import jax
import jax.numpy as jnp
from jax import lax
from jax.experimental import pallas as pl
from jax.experimental.pallas import tpu as pltpu

F32 = jnp.float32
BF16 = jnp.bfloat16

N_META = 16
GRID_W = 64
N_HEADS = 8
N_KV_HEADS = 2
HEAD_DIM = 64
ROPE_THETA = 10000.0
SSM_GROUP = 16
SSM_STATE = 64
N_EXPERTS = 16
CAPACITY_FACTOR = 2
LN_EPS = 1e-5
QK_EPS = 1e-6
DEPTH = 1
DEEPNORM_ALPHA = (2.0 * DEPTH) ** 0.25

SSM_CHUNK = 32
ROUTE_TILE = 256
FFN_WIN = ROUTE_TILE + 16
COMB_WIN = ROUTE_TILE + 128
META_PAD = 128
ATTN_SUB = 128
VMEM_LIMIT = 56 * 1024 * 1024


def _ln(x, g, b):
    mu = jnp.mean(x, axis=-1, keepdims=True)
    xc = x - mu
    var = jnp.mean(xc * xc, axis=-1, keepdims=True)
    return xc * lax.rsqrt(var + LN_EPS) * g + b


def _sigmoid(x):
    return 1.0 / (1.0 + jnp.exp(-x))


def _split(t):
    hi = t.astype(BF16)
    lo = (t - hi.astype(F32)).astype(BF16)
    return hi, lo


def _dot(a, b):
    return jnp.dot(a, b, preferred_element_type=F32)


def _dot_nt(a, b):
    return lax.dot_general(a, b, (((1,), (1,)), ((), ())), preferred_element_type=F32)


def _inproj_kernel(x_ref, cosr_ref, sinr_ref, cost_ref, sint_ref, wt_ref, wn_ref, lng_ref, lnb_ref,
                   qg_ref, kg_ref, bd_ref, qt_ref, k_ref, vt_ref, u_ref):
    aw = N_HEADS * HEAD_DIM
    kw = N_KV_HEADS * HEAD_DIM
    half = HEAD_DIM // 2
    h = _ln(x_ref[...], lng_ref[...], lnb_ref[...])
    hb = h.astype(BF16)
    pt = _dot_nt(wt_ref[...], hb)
    pn = _dot(hb, wn_ref[...])
    tm = hb.shape[0]

    qt = pt[0:aw].reshape(N_HEADS, HEAD_DIM, tm)
    ms = jnp.mean(qt * qt, axis=1, keepdims=True)
    qn = qt * lax.rsqrt(ms + QK_EPS) * qg_ref[...]
    x0 = qn[:, 0:half, :]
    x1 = qn[:, half:, :]
    c = cost_ref[...][None]
    s = sint_ref[...][None]
    qr = jnp.concatenate([x0 * c - x1 * s, x0 * s + x1 * c], axis=1)
    qt_ref[...] = qr.reshape(aw, tm).astype(BF16)
    vt_ref[...] = pt[aw:aw + kw].reshape(N_KV_HEADS, HEAD_DIM, tm).astype(BF16)

    kk = pn[:, 0:kw]
    hi, lo = _split(kk * kk)
    bd = bd_ref[...]
    msk = _dot(hi, bd) + _dot(lo, bd)
    kn = kk * lax.rsqrt(msk + QK_EPS) * kg_ref[...]
    lane = lax.broadcasted_iota(jnp.int32, kn.shape, 1)
    first = (lane & (HEAD_DIM - 1)) < half
    partner = jnp.where(first, pltpu.roll(kn, kw - half, 1), pltpu.roll(kn, half, 1))
    kr = (kn * cosr_ref[...] + partner * sinr_ref[...]).astype(BF16)
    for g in range(N_KV_HEADS):
        k_ref[g] = kr[:, g * HEAD_DIM:(g + 1) * HEAD_DIM]
    u_ref[...] = pn[:, kw:]


def _inproj(x2, cosr, sinr, cost, sint, wt, wn, lng, lnb, qg3, kg, bd, tm, n_tab_blocks):
    rows, d = x2.shape
    aw = N_HEADS * HEAD_DIM
    kw = N_KV_HEADS * HEAD_DIM
    uw = wn.shape[1] - kw
    half = HEAD_DIM // 2
    full = lambda a: pl.BlockSpec(a.shape, lambda i: (0,) * a.ndim)
    return pl.pallas_call(
        _inproj_kernel,
        grid=(rows // tm,),
        in_specs=[
            pl.BlockSpec((tm, d), lambda i: (i, 0)),
            pl.BlockSpec((tm, kw), lambda i: (i % n_tab_blocks, 0)),
            pl.BlockSpec((tm, kw), lambda i: (i % n_tab_blocks, 0)),
            pl.BlockSpec((half, tm), lambda i: (0, i % n_tab_blocks)),
            pl.BlockSpec((half, tm), lambda i: (0, i % n_tab_blocks)),
            full(wt), full(wn), full(lng), full(lnb), full(qg3), full(kg), full(bd),
        ],
        out_specs=[
            pl.BlockSpec((aw, tm), lambda i: (0, i)),
            pl.BlockSpec((N_KV_HEADS, tm, HEAD_DIM), lambda i: (0, i, 0)),
            pl.BlockSpec((N_KV_HEADS, HEAD_DIM, tm), lambda i: (0, 0, i)),
            pl.BlockSpec((tm, uw), lambda i: (i, 0)),
        ],
        out_shape=[
            jax.ShapeDtypeStruct((aw, rows), BF16),
            jax.ShapeDtypeStruct((N_KV_HEADS, rows, HEAD_DIM), BF16),
            jax.ShapeDtypeStruct((N_KV_HEADS, HEAD_DIM, rows), BF16),
            jax.ShapeDtypeStruct((rows, uw), F32),
        ],
        compiler_params=pltpu.CompilerParams(dimension_semantics=("arbitrary",),
                                             vmem_limit_bytes=VMEM_LIMIT),
        name="inproj",
    )(x2, cosr, sinr, cost, sint, wt, wn, lng, lnb, qg3, kg, bd)


def _attn_kernel(qt_ref, k_ref, vt_ref, km_ref, vtm_ref, o_ref, m_sc, l_sc, acc_sc, s_sc):
    ki = pl.program_id(2)
    nk = pl.num_programs(2)
    grp = N_HEADS // N_KV_HEADS

    @pl.when(ki == 0)
    def _init():
        m_sc[...] = jnp.full(m_sc.shape, -jnp.inf, F32)
        l_sc[...] = jnp.zeros(l_sc.shape, F32)
        acc_sc[...] = jnp.zeros(acc_sc.shape, F32)

    def softmax_pv(h, s, m_blk, vt_g):
        hs = slice(h * HEAD_DIM, (h + 1) * HEAD_DIM)
        m_prev = m_sc[h:h + 1, :]
        m_new = jnp.maximum(m_prev, m_blk)
        alpha = jnp.exp2(m_prev - m_new)
        p = jnp.exp2(s - m_new)
        l_sc[h:h + 1, :] = alpha * l_sc[h:h + 1, :] + jnp.sum(p, axis=0, keepdims=True)
        acc_sc[hs, :] = alpha * acc_sc[hs, :] + _dot(vt_g, p.astype(BF16))
        m_sc[h:h + 1, :] = m_new

    @pl.when(ki == 0)
    def _meta():
        for h in range(N_HEADS):
            g = h // grp
            s = _dot(km_ref[g], qt_ref[h * HEAD_DIM:(h + 1) * HEAD_DIM, :])
            softmax_pv(h, s, jnp.max(s, axis=0, keepdims=True), vtm_ref[g])

    m_blk = []
    for h in range(N_HEADS):
        s = _dot(k_ref[h // grp], qt_ref[h * HEAD_DIM:(h + 1) * HEAD_DIM, :])
        s_sc[h] = s
        m_blk.append(jnp.max(s, axis=0, keepdims=True))
    for h in range(N_HEADS):
        softmax_pv(h, s_sc[h], m_blk[h], vt_ref[h // grp])

    @pl.when(ki == nk - 1)
    def _fin():
        tq = acc_sc.shape[1]
        acc3 = acc_sc[...].reshape(N_HEADS, HEAD_DIM, tq)
        out_t = (acc3 / l_sc[...][:, None, :]).reshape(N_HEADS * HEAD_DIM, tq)
        o_ref[...] = out_t.T.astype(BF16)


def _attention(qt, kk, vt, km, vtm, nb, tq, tk):
    aw, rows = qt.shape
    s = rows // nb
    nq = s // tq
    nkb = s // tk
    return pl.pallas_call(
        _attn_kernel,
        grid=(nb, nq, nkb),
        in_specs=[
            pl.BlockSpec((aw, tq), lambda bi, qi, ki: (0, bi * nq + qi)),
            pl.BlockSpec((N_KV_HEADS, tk, HEAD_DIM), lambda bi, qi, ki: (0, bi * nkb + ki, 0)),
            pl.BlockSpec((N_KV_HEADS, HEAD_DIM, tk), lambda bi, qi, ki: (0, 0, bi * nkb + ki)),
            pl.BlockSpec(km.shape, lambda bi, qi, ki: (0, 0, 0)),
            pl.BlockSpec(vtm.shape, lambda bi, qi, ki: (0, 0, 0)),
        ],
        out_specs=pl.BlockSpec((tq, aw), lambda bi, qi, ki: (bi * nq + qi, 0)),
        out_shape=jax.ShapeDtypeStruct((rows, aw), BF16),
        scratch_shapes=[
            pltpu.VMEM((N_HEADS, tq), F32),
            pltpu.VMEM((N_HEADS, tq), F32),
            pltpu.VMEM((aw, tq), F32),
            pltpu.VMEM((N_HEADS, tk, tq), F32),
        ],
        compiler_params=pltpu.CompilerParams(
            dimension_semantics=("arbitrary", "arbitrary", "arbitrary"),
            vmem_limit_bytes=VMEM_LIMIT),
        name="attn",
    )(qt, kk, vt, km, vtm)


def _ssm_prep_kernel(arow_ref, acol_ref, ldt_ref, bt_ref, ct_ref, m_ref, w_ref, v_ref, at_ref):
    t_chunk = SSM_CHUNK
    shift = int(math.log2(SSM_GROUP))
    df = pl.program_id(0).astype(F32)
    dt = jnp.exp(ldt_ref[0, 0])

    def abar(ar, ai):
        mag = jnp.exp(ar * dt)
        ang = ai * dt
        return mag * jnp.cos(ang), mag * jnp.sin(ang)

    def cpow(ar, ai, e):
        mag = jnp.exp(ar * dt * e)
        ang = ai * dt * e
        return mag * jnp.cos(ang), mag * jnp.sin(ang)

    ar_r = arow_ref[0, 0, 0:1, :]
    ai_r = arow_ref[0, 0, 1:2, :]
    abr, abi = abar(ar_r, ai_r)
    nr = abr - 1.0
    ni = abi
    den = ar_r * ar_r + ai_r * ai_r
    cr = (nr * ar_r + ni * ai_r) / den
    ci = (ni * ar_r - nr * ai_r) / den
    btr = bt_ref[0, 0, 0]
    bti = bt_ref[0, 0, 1]
    bbr = cr * btr - ci * bti
    bbi = cr * bti + ci * btr
    nrow = btr.shape[0]
    nstate = btr.shape[1]
    jr = (lax.broadcasted_iota(jnp.int32, (nrow, 1), 0) >> shift).astype(F32)
    lj = jr + df * ((t_chunk - 1) - 2.0 * jr)
    step = lax.broadcasted_iota(jnp.int32, (t_chunk, 1), 0).astype(F32)
    lstep = step + df * ((t_chunk - 1) - 2.0 * step)

    def rep_rows(x):
        return jnp.broadcast_to(x[:, None, :], (t_chunk, SSM_GROUP, nstate)).reshape(nrow, nstate)

    enr, eni = [rep_rows(x) for x in cpow(ar_r, ai_r, -lstep)]
    bmr = enr * bbr - eni * bbi
    bmi = enr * bbi + eni * bbr
    ewr, ewi = [rep_rows(x) for x in cpow(ar_r, ai_r, (t_chunk - 1) - lstep)]
    w_ref[0, 0] = jnp.concatenate([ewr * bbr - ewi * bbi, ewr * bbi + ewi * bbr], axis=1).astype(BF16)
    atr, ati = cpow(ar_r, ai_r, float(t_chunk))
    at_ref[0, 0] = jnp.concatenate([atr, ati], axis=1)

    ar_c = acol_ref[0, 0, :, 0:1]
    ai_c = acol_ref[0, 0, :, 1:2]
    ctr = ct_ref[0, 0, 0]
    cti = ct_ref[0, 0, 1]
    ncol = ctr.shape[1]
    tci = lax.broadcasted_iota(jnp.int32, (1, ncol), 1) >> shift
    tc = tci.astype(F32)
    lt = tc + df * ((t_chunk - 1) - 2.0 * tc)
    stepl = lax.broadcasted_iota(jnp.int32, (1, t_chunk), 1).astype(F32)
    lstepl = stepl + df * ((t_chunk - 1) - 2.0 * stepl)
    rep = jnp.where(lax.broadcasted_iota(jnp.int32, (t_chunk, ncol), 0) == tci, 1.0, 0.0).astype(BF16)

    def rep_cols(x):
        hi, lo = _split(x)
        lo2 = (x - hi.astype(F32) - lo.astype(F32)).astype(BF16)
        return _dot(hi, rep) + _dot(lo, rep) + _dot(lo2, rep)

    ecr, eci = [rep_cols(x) for x in cpow(ar_c, ai_c, lstepl)]
    cmr = ctr * ecr - cti * eci
    cmi = ctr * eci + cti * ecr
    lhs_hi, lhs_lo = _split(jnp.concatenate([bmr, -bmi], axis=1))
    rhs_hi, rhs_lo = _split(jnp.concatenate([cmr, cmi], axis=0))
    m = _dot(lhs_hi, rhs_hi) + _dot(lhs_hi, rhs_lo) + _dot(lhs_lo, rhs_hi)
    m_ref[0, 0] = jnp.where(lj <= lt, m, 0.0).astype(BF16)
    abr_c, abi_c = abar(ar_c, ai_c)
    c1r = cmr * abr_c - cmi * abi_c
    c1i = cmr * abi_c + cmi * abr_c
    v_ref[0, 0] = jnp.concatenate([c1r, -c1i], axis=0).astype(BF16)


def _ssm_prep(arow, acol, ldt, bt, ct):
    nd, g = arow.shape[0], arow.shape[1]
    p = SSM_STATE
    n = SSM_CHUNK * SSM_GROUP
    blk = lambda a: pl.BlockSpec((1, 1) + a.shape[2:], lambda d, gi: (d, gi) + (0,) * (a.ndim - 2))
    return pl.pallas_call(
        _ssm_prep_kernel,
        grid=(nd, g),
        in_specs=[blk(arow), blk(acol), blk(ldt), blk(bt), blk(ct)],
        out_specs=[
            pl.BlockSpec((1, 1, n, n), lambda d, gi: (d, gi, 0, 0)),
            pl.BlockSpec((1, 1, n, 2 * p), lambda d, gi: (d, gi, 0, 0)),
            pl.BlockSpec((1, 1, 2 * p, n), lambda d, gi: (d, gi, 0, 0)),
            pl.BlockSpec((1, 1, 1, 2 * p), lambda d, gi: (d, gi, 0, 0)),
        ],
        out_shape=[
            jax.ShapeDtypeStruct((nd, g, n, n), BF16),
            jax.ShapeDtypeStruct((nd, g, n, 2 * p), BF16),
            jax.ShapeDtypeStruct((nd, g, 2 * p, n), BF16),
            jax.ShapeDtypeStruct((nd, g, 1, 2 * p), F32),
        ],
        compiler_params=pltpu.CompilerParams(dimension_semantics=("arbitrary", "arbitrary")),
        name="ssm_prep",
    )(arow, acol, ldt, bt, ct)


def _chunk_carry(z, a, s0, reverse):
    p = SSM_STATE
    nc = z.shape[0]
    row = lax.broadcasted_iota(jnp.int32, z.shape, 0)
    is_re = lax.broadcasted_iota(jnp.int32, a.shape, 1) < p
    sign = jnp.where(is_re, -1.0, 1.0)

    def parts(ap):
        sw = pltpu.roll(ap, p, 1)
        return jnp.where(is_re, ap, sw), sign * jnp.where(is_re, sw, ap)

    def cmul(x, ar_full, ai_sgn):
        return x * ar_full + pltpu.roll(x, p, 1) * ai_sgn

    ar_full, ai_sgn = parts(a)
    first = (nc - 1) if reverse else 0
    e = z + jnp.where(row == first, cmul(s0, ar_full, ai_sgn), 0.0)
    k = 1
    while k < nc:
        if reverse:
            sh = jnp.where(row < nc - k, pltpu.roll(e, nc - k, 0), 0.0)
        else:
            sh = jnp.where(row >= k, pltpu.roll(e, k, 0), 0.0)
        e = e + cmul(sh, ar_full, ai_sgn)
        k *= 2
        if k < nc:
            ar_full, ai_sgn = parts(cmul(jnp.where(is_re, ar_full, sign * ai_sgn), ar_full, ai_sgn))
    if reverse:
        return jnp.where(row == first, s0, pltpu.roll(e, nc - 1, 0))
    return jnp.where(row == first, s0, pltpu.roll(e, 1, 0))


def _ssm_kernel(u_ref, um_ref, m_ref, w_ref, v_ref, at_ref, dvec_ref, y_ref):
    p = SSM_STATE
    u = u_ref[0, 0]
    ub = u.astype(BF16)
    y = dvec_ref[0] * u
    for d in range(2):
        yin = _dot(ub, m_ref[d, 0])
        z = _dot(ub, w_ref[d, 0])
        if d == 0:
            s0 = _dot(um_ref[0].astype(BF16), w_ref[0, 0])[0:1, :]
        else:
            s0 = jnp.zeros((1, 2 * p), F32)
        s_in = _chunk_carry(z, at_ref[d, 0], s0, reverse=(d == 1))
        y = y + yin + _dot(s_in.astype(BF16), v_ref[d, 0])
    y_ref[0, 0] = y


def _ssm(u2, um2, m, w, v, at, dvec):
    b, g, nc, n = u2.shape
    p = SSM_STATE
    return pl.pallas_call(
        _ssm_kernel,
        grid=(b, g),
        in_specs=[
            pl.BlockSpec((1, 1, nc, n), lambda bi, gi: (bi, gi, 0, 0)),
            pl.BlockSpec((1, 8, n), lambda bi, gi: (gi, 0, 0)),
            pl.BlockSpec((2, 1, n, n), lambda bi, gi: (0, gi, 0, 0)),
            pl.BlockSpec((2, 1, n, 2 * p), lambda bi, gi: (0, gi, 0, 0)),
            pl.BlockSpec((2, 1, 2 * p, n), lambda bi, gi: (0, gi, 0, 0)),
            pl.BlockSpec((2, 1, 1, 2 * p), lambda bi, gi: (0, gi, 0, 0)),
            pl.BlockSpec((1, 1, n), lambda bi, gi: (gi, 0, 0)),
        ],
        out_specs=pl.BlockSpec((1, 1, nc, n), lambda bi, gi: (bi, gi, 0, 0)),
        out_shape=jax.ShapeDtypeStruct((b, g, nc, n), F32),
        compiler_params=pltpu.CompilerParams(dimension_semantics=("arbitrary", "arbitrary")),
        name="ssm",
    )(u2, um2, m, w, v, at, dvec)


def _merge_kernel(x_ref, ya_ref, ys_ref, wgt_ref, wglu_ref, bglu_ref, wab_ref, wsb_ref, wo_ref,
                  lng_ref, lnb_ref, l1g_ref, l1b_ref, wr_ref, wrt_ref,
                  h1_ref, h1b_ref, aff_ref, afft_ref):
    d = x_ref.shape[1]
    h = _ln(x_ref[...], lng_ref[...], lnb_ref[...])
    gates = _dot(h.astype(BF16), wgt_ref[...])
    ga = _sigmoid(gates[:, 0:d])
    gs = _sigmoid(gates[:, d:2 * d])
    ys = ys_ref[...]
    y = 0.5 * ys * (1.0 + jnp.tanh(math.sqrt(2.0 / math.pi) * (ys + 0.044715 * (ys * ys * ys))))
    yg = y * _sigmoid(_dot(y.astype(BF16), wglu_ref[...]) + bglu_ref[...])
    merged = ga * _dot(ya_ref[...], wab_ref[...]) + gs * _dot(yg.astype(BF16), wsb_ref[...])
    h1 = _ln(DEEPNORM_ALPHA * h + _dot(merged.astype(BF16), wo_ref[...]), l1g_ref[...], l1b_ref[...])
    h1_ref[...] = h1
    hi, lo = _split(h1)
    h1b_ref[...] = hi
    logits = _dot(hi, wr_ref[0]) + _dot(hi, wr_ref[1]) + _dot(lo, wr_ref[0])
    e = jnp.exp(logits - jnp.max(logits, axis=1, keepdims=True))
    aff_ref[...] = e / jnp.sum(e, axis=1, keepdims=True)
    lt = _dot_nt(wrt_ref[0], hi) + _dot_nt(wrt_ref[1], hi) + _dot_nt(wrt_ref[0], lo)
    et = jnp.exp(lt - jnp.max(lt, axis=0, keepdims=True))
    afft_ref[...] = et / jnp.sum(et, axis=0, keepdims=True)


def _merge(x2, ya, ys, wgt, wglu, bglu, wab, wsb, wo, lng, lnb, l1g, l1b, wr, wrt, tm):
    rows, d = x2.shape
    ne = wr.shape[2]
    full = lambda a: pl.BlockSpec(a.shape, lambda i: (0,) * a.ndim)
    return pl.pallas_call(
        _merge_kernel,
        grid=(rows // tm,),
        in_specs=[
            pl.BlockSpec((tm, d), lambda i: (i, 0)),
            pl.BlockSpec((tm, ya.shape[1]), lambda i: (i, 0)),
            pl.BlockSpec((tm, ys.shape[1]), lambda i: (i, 0)),
            full(wgt), full(wglu), full(bglu), full(wab), full(wsb), full(wo),
            full(lng), full(lnb), full(l1g), full(l1b), full(wr), full(wrt),
        ],
        out_specs=[
            pl.BlockSpec((tm, d), lambda i: (i, 0)),
            pl.BlockSpec((tm, d), lambda i: (i, 0)),
            pl.BlockSpec((tm, ne), lambda i: (i, 0)),
            pl.BlockSpec((ne, tm), lambda i: (0, i)),
        ],
        out_shape=[
            jax.ShapeDtypeStruct((rows, d), F32),
            jax.ShapeDtypeStruct((rows, d), BF16),
            jax.ShapeDtypeStruct((rows, ne), F32),
            jax.ShapeDtypeStruct((ne, rows), F32),
        ],
        compiler_params=pltpu.CompilerParams(dimension_semantics=("arbitrary",),
                                             vmem_limit_bytes=VMEM_LIMIT),
        name="merge",
    )(x2, ya, ys, wgt, wglu, bglu, wab, wsb, wo, lng, lnb, l1g, l1b, wr, wrt)


def _route_kernel(afft_ref, tri_ref, scl_ref, lo_ref, *, cap):
    aff = afft_ref[...]
    ne, s = aff.shape
    capf = float(cap)

    def as_float(bits):
        return lax.bitcast_convert_type(bits, F32)

    def search(i, t):
        cand = t | jnp.left_shift(jnp.int32(1), 30 - i)
        cnt = jnp.sum(jnp.where(aff >= as_float(cand), 1.0, 0.0), axis=1, keepdims=True)
        return jnp.where(cnt >= capf, cand, t)

    thr_bits = lax.fori_loop(0, 31, search, jnp.zeros((ne, 1), jnp.int32))
    gt = aff >= as_float(thr_bits + 1)
    eq = (aff >= as_float(thr_bits)) & jnp.logical_not(gt)
    need = capf - jnp.sum(jnp.where(gt, 1.0, 0.0), axis=1, keepdims=True)
    tri = tri_ref[...]
    rt = ROUTE_TILE
    nt = s // rt
    col = lax.broadcasted_iota(jnp.int32, (ne, nt), 1)
    carry_eq = jnp.zeros((ne, 1), F32)
    carry_sel = jnp.zeros((ne, 1), F32)
    lo_val = jnp.zeros((ne, nt), F32)
    for t in range(nt):
        sl = slice(t * rt, (t + 1) * rt)
        eq_b = eq[:, sl]
        ceq = _dot(jnp.where(eq_b, 1.0, 0.0).astype(BF16), tri)
        sel_b = gt[:, sl] | (eq_b & ((ceq + carry_eq) <= need))
        carry_eq = carry_eq + ceq[:, rt - 1:rt]
        csel = _dot(jnp.where(sel_b, 1.0, 0.0).astype(BF16), tri)
        scl_ref[0, :, sl] = jnp.where(sel_b, csel, 0.0)
        lo_val = jnp.where(col == t, carry_sel, lo_val)
        carry_sel = carry_sel + csel[:, rt - 1:rt]
    lo_ref[0] = lo_val


def _route(afft, tri, nb, cap):
    ne, rows = afft.shape
    s = rows // nb
    nt = s // ROUTE_TILE
    return pl.pallas_call(
        functools.partial(_route_kernel, cap=cap),
        grid=(nb,),
        in_specs=[
            pl.BlockSpec((ne, s), lambda b: (0, b)),
            pl.BlockSpec(tri.shape, lambda b: (0, 0)),
        ],
        out_specs=[
            pl.BlockSpec((1, ne, s), lambda b: (b, 0, 0)),
            pl.BlockSpec((1, ne, nt), lambda b: (b, 0, 0)),
        ],
        out_shape=[
            jax.ShapeDtypeStruct((nb, ne, s), F32),
            jax.ShapeDtypeStruct((nb, ne, nt), F32),
        ],
        compiler_params=pltpu.CompilerParams(dimension_semantics=("arbitrary",)),
        name="route",
    )(afft, tri)


def _ffn_kernel(lo_ref, t_ref, scl_ref, wg_ref, wu_ref, wd_ref, ye_ref, xe_sc, *, cap, f_chunk):
    b = pl.program_id(0)
    e = pl.program_id(1)
    ne = pl.num_programs(1)
    rt = ROUTE_TILE
    nt = t_ref.shape[1] // rt
    xe_sc[...] = jnp.zeros(xe_sc.shape, F32)
    r = lax.broadcasted_iota(jnp.int32, (FFN_WIN, rt), 0).astype(F32)

    def gather(tau, carry):
        lo = lo_ref[(b * ne + e) * nt + tau]
        off = lo & 7
        start = pl.multiple_of(lo - off, 8)
        scl = scl_ref[0, 0, pl.ds(tau, 1), :]
        onehot = jnp.where((scl > 0.0) & ((scl + (off - 1).astype(F32)) == r), 1.0, 0.0).astype(BF16)
        rows = _dot(onehot, t_ref[0, pl.ds(pl.multiple_of(tau * rt, rt), rt), :])
        xe_sc[pl.ds(start, FFN_WIN), :] += rows
        return carry

    lax.fori_loop(0, nt, gather, 0)
    xb = xe_sc[0:cap, :].astype(BF16)
    f = wg_ref.shape[2]
    acc = jnp.zeros((cap, wd_ref.shape[2]), F32)
    for c in range(f // f_chunk):
        sl = slice(c * f_chunk, (c + 1) * f_chunk)
        g = _dot(xb, wg_ref[0, :, sl])
        u = _dot(xb, wu_ref[0, :, sl])
        hh = (g * _sigmoid(g)) * u
        acc = acc + _dot(hh.astype(BF16), wd_ref[0, sl, :])
    ye_ref[0, 0] = acc.astype(BF16)


def _ffn(lo_i, t, scl4, wg, wu, wd, cap):
    b, s, d = t.shape
    ne, _, f = wg.shape
    nt = s // ROUTE_TILE
    grid_spec = pltpu.PrefetchScalarGridSpec(
        num_scalar_prefetch=1,
        grid=(b, ne),
        in_specs=[
            pl.BlockSpec((1, s, d), lambda bi, ei, lo: (bi, 0, 0), pipeline_mode=pl.Buffered(1)),
            pl.BlockSpec((1, 1, nt, ROUTE_TILE), lambda bi, ei, lo: (bi, ei, 0, 0)),
            pl.BlockSpec((1, d, f), lambda bi, ei, lo: (ei, 0, 0)),
            pl.BlockSpec((1, d, f), lambda bi, ei, lo: (ei, 0, 0)),
            pl.BlockSpec((1, f, d), lambda bi, ei, lo: (ei, 0, 0)),
        ],
        out_specs=pl.BlockSpec((1, 1, cap, d), lambda bi, ei, lo: (bi, ei, 0, 0)),
        scratch_shapes=[pltpu.VMEM((cap + FFN_WIN, d), F32)],
    )
    return pl.pallas_call(
        functools.partial(_ffn_kernel, cap=cap, f_chunk=512),
        grid_spec=grid_spec,
        out_shape=jax.ShapeDtypeStruct((b, ne, cap, d), BF16),
        compiler_params=pltpu.CompilerParams(dimension_semantics=("arbitrary", "arbitrary"),
                                             vmem_limit_bytes=VMEM_LIMIT),
        name="ffn",
    )(lo_i, t, scl4, wg, wu, wd)


def _combine_kernel(lo_ref, h1_ref, aff_ref, sclt_ref, ye_hbm, g_ref, b_ref, o_ref, win, sem, *, cap):
    b = pl.program_id(0)
    tau = pl.program_id(1)
    nt = pl.num_programs(1)
    ne = win.shape[0]

    def window(e):
        lo = lo_ref[(b * ne + e) * nt + tau]
        start = jnp.minimum(lo - (lo & 15), cap - COMB_WIN)
        start = pl.multiple_of(start, 16)
        cp = pltpu.make_async_copy(ye_hbm.at[b, e, pl.ds(start, COMB_WIN), :], win.at[e], sem.at[e])
        return cp, lo - start

    for e in range(ne):
        window(e)[0].start()
    acc = DEEPNORM_ALPHA * h1_ref[0]
    aff = aff_ref[0]
    sclt = sclt_ref[0]
    r = lax.broadcasted_iota(jnp.int32, (sclt.shape[0], COMB_WIN), 1).astype(F32)
    for e in range(ne):
        cp, rel = window(e)
        cp.wait()
        scl = sclt[:, e:e + 1]
        onehot = jnp.where((scl > 0.0) & ((scl + (rel - 1).astype(F32)) == r), 1.0, 0.0).astype(BF16)
        acc = acc + aff[:, e:e + 1] * _dot(onehot, win[e])
    o_ref[0] = _ln(acc, g_ref[...], b_ref[...])


def _combine(lo_i, h1, aff, sclt, ye, g, bb, cap):
    b, s, d = h1.shape
    ne = aff.shape[2]
    rt = ROUTE_TILE
    grid_spec = pltpu.PrefetchScalarGridSpec(
        num_scalar_prefetch=1,
        grid=(b, s // rt),
        in_specs=[
            pl.BlockSpec((1, rt, d), lambda bi, ti, lo: (bi, ti, 0)),
            pl.BlockSpec((1, rt, ne), lambda bi, ti, lo: (bi, ti, 0)),
            pl.BlockSpec((1, rt, ne), lambda bi, ti, lo: (bi, ti, 0)),
            pl.BlockSpec(memory_space=pl.ANY),
            pl.BlockSpec(g.shape, lambda bi, ti, lo: (0, 0)),
            pl.BlockSpec(bb.shape, lambda bi, ti, lo: (0, 0)),
        ],
        out_specs=pl.BlockSpec((1, rt, d), lambda bi, ti, lo: (bi, ti, 0)),
        scratch_shapes=[pltpu.VMEM((ne, COMB_WIN, d), BF16), pltpu.SemaphoreType.DMA((ne,))],
    )
    return pl.pallas_call(
        functools.partial(_combine_kernel, cap=cap),
        grid_spec=grid_spec,
        out_shape=jax.ShapeDtypeStruct((b, s, d), F32),
        compiler_params=pltpu.CompilerParams(dimension_semantics=("arbitrary", "arbitrary"),
                                             vmem_limit_bytes=VMEM_LIMIT),
        name="combine",
    )(lo_i, h1, aff, sclt, ye, g, bb)


def _rope_tables(n_tokens):
    half = HEAD_DIM // 2
    inv_freq = ROPE_THETA ** (-jnp.arange(0, half, 2, dtype=F32) / half)
    rows = n_tokens // GRID_W
    row = jnp.repeat(jnp.arange(rows, dtype=F32), GRID_W)
    colv = jnp.tile(jnp.arange(GRID_W, dtype=F32), rows)
    ang = jnp.concatenate([row[:, None] * inv_freq, colv[:, None] * inv_freq], axis=-1)
    return jnp.cos(ang), jnp.sin(ang)


def kernel(x, meta_tokens, ln_in_g, ln_in_b, w_in, q_norm_g, k_norm_g, ssm_a_re, ssm_a_im, ssm_log_dt,
           ssm_b_re, ssm_b_im, ssm_c_re, ssm_c_im, ssm_d, w_glu, b_glu, w_attn_br, w_ssm_br, w_o,
           ln1_g, ln1_b, w_router, w_gate_e, w_up_e, w_down_e, ln2_g, ln2_b):
    b, s, d = x.shape
    aw = N_HEADS * HEAD_DIM
    kw = N_KV_HEADS * HEAD_DIM
    sw = d // 2
    g = sw // SSM_GROUP
    half = HEAD_DIM // 2
    cap = CAPACITY_FACTOR * s // N_EXPERTS
    rows = b * s
    l = 0

    perm = np.concatenate([np.arange(0, HEAD_DIM, 2), np.arange(1, HEAD_DIM, 2)])
    qcols = np.concatenate([h * HEAD_DIM + perm for h in range(N_HEADS)])
    kcols = aw + np.concatenate([h * HEAD_DIM + perm for h in range(N_KV_HEADS)])
    wl = w_in[l]
    w_t = jnp.concatenate([wl[:, qcols], wl[:, aw + kw:aw + 2 * kw]], axis=1).T.astype(BF16)
    w_n = jnp.concatenate([wl[:, kcols], wl[:, aw + 2 * kw:aw + 2 * kw + sw]], axis=1).astype(BF16)
    w_gates = wl[:, aw + 2 * kw + sw:].astype(BF16)
    qg3 = (jnp.tile(q_norm_g[l][perm], N_HEADS) * (HEAD_DIM ** -0.5 * math.log2(math.e))).reshape(N_HEADS, HEAD_DIM, 1)
    kg = jnp.tile(k_norm_g[l][perm], N_KV_HEADS)[None, :]
    bd = jnp.asarray(np.kron(np.eye(N_KV_HEADS), np.full((HEAD_DIM, HEAD_DIM), 1.0 / HEAD_DIM)), BF16)
    lng = ln_in_g[None, :]
    lnb = ln_in_b[None, :]
    cos, sin = _rope_tables(s)
    cosr = jnp.tile(cos, (1, 2 * N_KV_HEADS))
    sinr = jnp.tile(jnp.concatenate([-sin, sin], axis=1), (1, N_KV_HEADS))
    cost = cos.T
    sint = sin.T

    x2 = x.reshape(rows, d)
    tm_in = 512
    qt, kk, vt, u2 = _inproj(x2, cosr, sinr, cost, sint, w_t, w_n, lng, lnb, qg3, kg, bd, tm_in, s // tm_in)
    meta_p = jnp.pad(meta_tokens, ((0, META_PAD - N_META), (0, 0)))
    ones_r = jnp.ones((META_PAD, kw), F32)
    _, km, vtm, um = _inproj(meta_p, ones_r, jnp.zeros_like(ones_r), jnp.ones((half, META_PAD), F32),
                             jnp.zeros((half, META_PAD), F32), w_t, w_n, lng, lnb, qg3, kg, bd, META_PAD, 1)
    um = um[:N_META]
    km = km[:, :N_META, :]
    vtm = vtm[:, :, :N_META]

    ya = _attention(qt, kk, vt, km, vtm, b, 256, 512)

    t_chunk = SSM_CHUNK
    hgrp = SSM_GROUP
    n = t_chunk * hgrp
    arow = jnp.stack([ssm_a_re[l], ssm_a_im[l]], axis=2)
    acol = jnp.stack([ssm_a_re[l], ssm_a_im[l]], axis=3)
    ldt = ssm_log_dt[l][:, :, None, None]
    bt = jnp.stack([ssm_b_re[l], ssm_b_im[l]], axis=2)
    bt = jnp.tile(jnp.swapaxes(bt, 3, 4), (1, 1, 1, t_chunk, 1))
    ct = jnp.stack([ssm_c_re[l], ssm_c_im[l]], axis=2)
    ct = jnp.tile(jnp.swapaxes(ct, 3, 4), (1, 1, 1, 1, t_chunk))
    mm, ww, vv, at = _ssm_prep(arow, acol, ldt, bt, ct)
    nc = s // t_chunk
    u_g = u2.reshape(b, nc, t_chunk, g, hgrp).transpose(0, 3, 1, 2, 4).reshape(b, g, nc, n)
    um_g = um.reshape(N_META, g, hgrp).transpose(1, 0, 2).reshape(g, 1, N_META * hgrp)
    um_g = jnp.pad(um_g, ((0, 0), (0, 7), (n - N_META * hgrp, 0)))
    dvec = jnp.tile(ssm_d[l], (1, t_chunk))[:, None, :]
    y_g = _ssm(u_g, um_g, mm, ww, vv, at, dvec)
    ys = y_g.reshape(b, g, nc, t_chunk, hgrp).transpose(0, 2, 3, 1, 4).reshape(rows, sw)

    wr_hi = w_router[l].astype(BF16)
    wr_lo = (w_router[l] - wr_hi.astype(F32)).astype(BF16)
    wr = jnp.stack([wr_hi, wr_lo])
    wrt = jnp.stack([wr_hi.T, wr_lo.T])
    h1, h1b, aff, afft = _merge(
        x2, ya, ys, w_gates, w_glu[l].astype(BF16), b_glu[l][None, :],
        w_attn_br[l].astype(BF16), w_ssm_br[l].astype(BF16), w_o[l].astype(BF16),
        lng, lnb, ln1_g[l][None, :], ln1_b[l][None, :], wr, wrt, 256)

    tri = jnp.asarray(np.triu(np.ones((ROUTE_TILE, ROUTE_TILE), np.float32)), BF16)
    scl, lo_f = _route(afft, tri, b, cap)
    nt = s // ROUTE_TILE
    lo_i = lo_f.astype(jnp.int32).reshape(-1)
    ye = _ffn(lo_i, h1b.reshape(b, s, d), scl.reshape(b, N_EXPERTS, nt, ROUTE_TILE),
              w_gate_e[l].astype(BF16), w_up_e[l].astype(BF16), w_down_e[l].astype(BF16), cap)
    out = _combine(lo_i, h1.reshape(b, s, d), aff.reshape(b, s, N_EXPERTS),
                   jnp.swapaxes(scl, 1, 2), ye, ln2_g[l][None, :], ln2_b[l][None, :], cap)
    return out
```

```python
import functools
import math

import numpy as np
import jax
import jax.numpy as jnp
from jax import lax
from jax.experimental import pallas as pl
from jax.experimental.pallas import tpu as pltpu

F32 = jnp.float32
BF16 = jnp.bfloat16

N_META = 16
GRID_W = 64
N_HEADS = 8
N_KV_HEADS = 2
HEAD_DIM = 64
ROPE_THETA = 10000.0
SSM_GROUP = 16
SSM_STATE = 64
N_EXPERTS = 16
CAPACITY_FACTOR = 2
LN_EPS = 1e-5
QK_EPS = 1e-6
DEPTH = 1
DEEPNORM_ALPHA = (2.0 * DEPTH) ** 0.25

SSM_CHUNK = 32
ROUTE_TILE = 256
GATE_LANES = 128
PIECE_SHIFT = 6
PIECE = 1 << PIECE_SHIFT
META_PAD = 128
ATTN_SUB = 128
VMEM_LIMIT = 56 * 1024 * 1024


def _ln(x, g, b):
    mu = jnp.mean(x, axis=-1, keepdims=True)
    xc = x - mu
    var = jnp.mean(xc * xc, axis=-1, keepdims=True)
    return xc * lax.rsqrt(var + LN_EPS) * g + b


def _sigmoid(x):
    return 1.0 / (1.0 + jnp.exp(-x))


def _split(t):
    hi = t.astype(BF16)
    lo = (t - hi.astype(F32)).astype(BF16)
    return hi, lo


def _dot(a, b):
    return jnp.dot(a, b, preferred_element_type=F32)


def _dot_nt(a, b):
    return lax.dot_general(a, b, (((1,), (1,)), ((), ())), preferred_element_type=F32)


def _inproj_kernel(x_ref, cosr_ref, sinr_ref, cost_ref, sint_ref, wt_ref, wn_ref, lng_ref, lnb_ref,
                   qg_ref, kg_ref, bd_ref, qt_ref, k_ref, vt_ref, u_ref):
    aw = N_HEADS * HEAD_DIM
    kw = N_KV_HEADS * HEAD_DIM
    half = HEAD_DIM // 2
    h = _ln(x_ref[...], lng_ref[...], lnb_ref[...])
    hb = h.astype(BF16)
    pt = _dot_nt(wt_ref[...], hb)
    pn = _dot(hb, wn_ref[...])
    tm = hb.shape[0]

    qt = pt[0:aw].reshape(N_HEADS, HEAD_DIM, tm)
    ms = jnp.mean(qt * qt, axis=1, keepdims=True)
    qn = qt * lax.rsqrt(ms + QK_EPS) * qg_ref[...]
    x0 = qn[:, 0:half, :]
    x1 = qn[:, half:, :]
    c = cost_ref[...][None]
    s = sint_ref[...][None]
    qr = jnp.concatenate([x0 * c - x1 * s, x0 * s + x1 * c], axis=1)
    qt_ref[...] = qr.reshape(aw, tm).astype(BF16)
    vt_ref[...] = pt[aw:aw + kw].reshape(N_KV_HEADS, HEAD_DIM, tm).astype(BF16)

    kk = pn[:, 0:kw]
    hi, lo = _split(kk * kk)
    bd = bd_ref[...]
    msk = _dot(hi, bd) + _dot(lo, bd)
    kn = kk * lax.rsqrt(msk + QK_EPS) * kg_ref[...]
    lane = lax.broadcasted_iota(jnp.int32, kn.shape, 1)
    first = (lane & (HEAD_DIM - 1)) < half
    partner = jnp.where(first, pltpu.roll(kn, kw - half, 1), pltpu.roll(kn, half, 1))
    kr = (kn * cosr_ref[...] + partner * sinr_ref[...]).astype(BF16)
    for g in range(N_KV_HEADS):
        k_ref[g] = kr[:, g * HEAD_DIM:(g + 1) * HEAD_DIM]
    u_ref[...] = pn[:, kw:]


def _inproj(x2, cosr, sinr, cost, sint, wt, wn, lng, lnb, qg3, kg, bd, tm, n_tab_blocks):
    rows, d = x2.shape
    aw = N_HEADS * HEAD_DIM
    kw = N_KV_HEADS * HEAD_DIM
    uw = wn.shape[1] - kw
    half = HEAD_DIM // 2
    full = lambda a: pl.BlockSpec(a.shape, lambda i: (0,) * a.ndim)
    return pl.pallas_call(
        _inproj_kernel,
        grid=(rows // tm,),
        in_specs=[
            pl.BlockSpec((tm, d), lambda i: (i, 0)),
            pl.BlockSpec((tm, kw), lambda i: (i % n_tab_blocks, 0)),
            pl.BlockSpec((tm, kw), lambda i: (i % n_tab_blocks, 0)),
            pl.BlockSpec((half, tm), lambda i: (0, i % n_tab_blocks)),
            pl.BlockSpec((half, tm), lambda i: (0, i % n_tab_blocks)),
            full(wt), full(wn), full(lng), full(lnb), full(qg3), full(kg), full(bd),
        ],
        out_specs=[
            pl.BlockSpec((aw, tm), lambda i: (0, i)),
            pl.BlockSpec((N_KV_HEADS, tm, HEAD_DIM), lambda i: (0, i, 0)),
            pl.BlockSpec((N_KV_HEADS, HEAD_DIM, tm), lambda i: (0, 0, i)),
            pl.BlockSpec((tm, uw), lambda i: (i, 0)),
        ],
        out_shape=[
            jax.ShapeDtypeStruct((aw, rows), BF16),
            jax.ShapeDtypeStruct((N_KV_HEADS, rows, HEAD_DIM), BF16),
            jax.ShapeDtypeStruct((N_KV_HEADS, HEAD_DIM, rows), BF16),
            jax.ShapeDtypeStruct((rows, uw), F32),
        ],
        compiler_params=pltpu.CompilerParams(dimension_semantics=("arbitrary",),
                                             vmem_limit_bytes=VMEM_LIMIT),
        name="inproj",
    )(x2, cosr, sinr, cost, sint, wt, wn, lng, lnb, qg3, kg, bd)


def _attn_kernel(qt_ref, k_ref, vt_ref, km_ref, vtm_ref, o_ref, m_sc, l_sc, acc_sc, s_sc):
    ki = pl.program_id(2)
    nk = pl.num_programs(2)
    grp = N_HEADS // N_KV_HEADS

    @pl.when(ki == 0)
    def _init():
        m_sc[...] = jnp.full(m_sc.shape, -jnp.inf, F32)
        l_sc[...] = jnp.zeros(l_sc.shape, F32)
        acc_sc[...] = jnp.zeros(acc_sc.shape, F32)

    def softmax_pv(h, s, m_blk, vt_g):
        hs = slice(h * HEAD_DIM, (h + 1) * HEAD_DIM)
        m_prev = m_sc[h:h + 1, :]
        m_new = jnp.maximum(m_prev, m_blk)
        alpha = jnp.exp2(m_prev - m_new)
        p = jnp.exp2(s - m_new)
        l_sc[h:h + 1, :] = alpha * l_sc[h:h + 1, :] + jnp.sum(p, axis=0, keepdims=True)
        acc_sc[hs, :] = alpha * acc_sc[hs, :] + _dot(vt_g, p.astype(BF16))
        m_sc[h:h + 1, :] = m_new

    @pl.when(ki == 0)
    def _meta():
        for h in range(N_HEADS):
            g = h // grp
            s = _dot(km_ref[g], qt_ref[h * HEAD_DIM:(h + 1) * HEAD_DIM, :])
            softmax_pv(h, s, jnp.max(s, axis=0, keepdims=True), vtm_ref[g])

    m_blk = []
    for h in range(N_HEADS):
        s = _dot(k_ref[h // grp], qt_ref[h * HEAD_DIM:(h + 1) * HEAD_DIM, :])
        s_sc[h] = s
        m_blk.append(jnp.max(s, axis=0, keepdims=True))
    for h in range(N_HEADS):
        softmax_pv(h, s_sc[h], m_blk[h], vt_ref[h // grp])

    @pl.when(ki == nk - 1)
    def _fin():
        tq = acc_sc.shape[1]
        acc3 = acc_sc[...].reshape(N_HEADS, HEAD_DIM, tq)
        out_t = (acc3 / l_sc[...][:, None, :]).reshape(N_HEADS * HEAD_DIM, tq)
        o_ref[...] = out_t.T.astype(BF16)


def _attention(qt, kk, vt, km, vtm, nb, tq, tk):
    aw, rows = qt.shape
    s = rows // nb
    nq = s // tq
    nkb = s // tk
    return pl.pallas_call(
        _attn_kernel,
        grid=(nb, nq, nkb),
        in_specs=[
            pl.BlockSpec((aw, tq), lambda bi, qi, ki: (0, bi * nq + qi)),
            pl.BlockSpec((N_KV_HEADS, tk, HEAD_DIM), lambda bi, qi, ki: (0, bi * nkb + ki, 0)),
            pl.BlockSpec((N_KV_HEADS, HEAD_DIM, tk), lambda bi, qi, ki: (0, 0, bi * nkb + ki)),
            pl.BlockSpec(km.shape, lambda bi, qi, ki: (0, 0, 0)),
            pl.BlockSpec(vtm.shape, lambda bi, qi, ki: (0, 0, 0)),
        ],
        out_specs=pl.BlockSpec((tq, aw), lambda bi, qi, ki: (bi * nq + qi, 0)),
        out_shape=jax.ShapeDtypeStruct((rows, aw), BF16),
        scratch_shapes=[
            pltpu.VMEM((N_HEADS, tq), F32),
            pltpu.VMEM((N_HEADS, tq), F32),
            pltpu.VMEM((aw, tq), F32),
            pltpu.VMEM((N_HEADS, tk, tq), F32),
        ],
        compiler_params=pltpu.CompilerParams(
            dimension_semantics=("arbitrary", "arbitrary", "arbitrary"),
            vmem_limit_bytes=VMEM_LIMIT),
        name="attn",
    )(qt, kk, vt, km, vtm)


def _ssm_prep_kernel(arow_ref, acol_ref, ldt_ref, bt_ref, ct_ref, m_ref, w_ref, v_ref, at_ref):
    t_chunk = SSM_CHUNK
    shift = int(math.log2(SSM_GROUP))
    df = pl.program_id(0).astype(F32)
    dt = jnp.exp(ldt_ref[0, 0])

    def abar(ar, ai):
        mag = jnp.exp(ar * dt)
        ang = ai * dt
        return mag * jnp.cos(ang), mag * jnp.sin(ang)

    def cpow(ar, ai, e):
        mag = jnp.exp(ar * dt * e)
        ang = ai * dt * e
        return mag * jnp.cos(ang), mag * jnp.sin(ang)

    ar_r = arow_ref[0, 0, 0:1, :]
    ai_r = arow_ref[0, 0, 1:2, :]
    abr, abi = abar(ar_r, ai_r)
    nr = abr - 1.0
    ni = abi
    den = ar_r * ar_r + ai_r * ai_r
    cr = (nr * ar_r + ni * ai_r) / den
    ci = (ni * ar_r - nr * ai_r) / den
    btr = bt_ref[0, 0, 0]
    bti = bt_ref[0, 0, 1]
    bbr = cr * btr - ci * bti
    bbi = cr * bti + ci * btr
    nrow = btr.shape[0]
    nstate = btr.shape[1]
    jr = (lax.broadcasted_iota(jnp.int32, (nrow, 1), 0) >> shift).astype(F32)
    lj = jr + df * ((t_chunk - 1) - 2.0 * jr)
    step = lax.broadcasted_iota(jnp.int32, (t_chunk, 1), 0).astype(F32)
    lstep = step + df * ((t_chunk - 1) - 2.0 * step)

    def rep_rows(x):
        return jnp.broadcast_to(x[:, None, :], (t_chunk, SSM_GROUP, nstate)).reshape(nrow, nstate)

    enr, eni = [rep_rows(x) for x in cpow(ar_r, ai_r, -lstep)]
    bmr = enr * bbr - eni * bbi
    bmi = enr * bbi + eni * bbr
    ewr, ewi = [rep_rows(x) for x in cpow(ar_r, ai_r, (t_chunk - 1) - lstep)]
    w_ref[0, 0] = jnp.concatenate([ewr * bbr - ewi * bbi, ewr * bbi + ewi * bbr], axis=1).astype(BF16)
    atr, ati = cpow(ar_r, ai_r, float(t_chunk))
    at_ref[0, 0] = jnp.concatenate([atr, ati], axis=1)

    ar_c = acol_ref[0, 0, :, 0:1]
    ai_c = acol_ref[0, 0, :, 1:2]
    ctr = ct_ref[0, 0, 0]
    cti = ct_ref[0, 0, 1]
    ncol = ctr.shape[1]
    tci = lax.broadcasted_iota(jnp.int32, (1, ncol), 1) >> shift
    tc = tci.astype(F32)
    lt = tc + df * ((t_chunk - 1) - 2.0 * tc)
    stepl = lax.broadcasted_iota(jnp.int32, (1, t_chunk), 1).astype(F32)
    lstepl = stepl + df * ((t_chunk - 1) - 2.0 * stepl)
    rep = jnp.where(lax.broadcasted_iota(jnp.int32, (t_chunk, ncol), 0) == tci, 1.0, 0.0).astype(BF16)

    def rep_cols(x):
        hi, lo = _split(x)
        lo2 = (x - hi.astype(F32) - lo.astype(F32)).astype(BF16)
        return _dot(hi, rep) + _dot(lo, rep) + _dot(lo2, rep)

    ecr, eci = [rep_cols(x) for x in cpow(ar_c, ai_c, lstepl)]
    cmr = ctr * ecr - cti * eci
    cmi = ctr * eci + cti * ecr
    lhs_hi, lhs_lo = _split(jnp.concatenate([bmr, -bmi], axis=1))
    rhs_hi, rhs_lo = _split(jnp.concatenate([cmr, cmi], axis=0))
    m = _dot(lhs_hi, rhs_hi) + _dot(lhs_hi, rhs_lo) + _dot(lhs_lo, rhs_hi)
    m_ref[0, 0] = jnp.where(lj <= lt, m, 0.0).astype(BF16)
    abr_c, abi_c = abar(ar_c, ai_c)
    c1r = cmr * abr_c - cmi * abi_c
    c1i = cmr * abi_c + cmi * abr_c
    v_ref[0, 0] = jnp.concatenate([c1r, -c1i], axis=0).astype(BF16)


def _ssm_prep(arow, acol, ldt, bt, ct):
    nd, g = arow.shape[0], arow.shape[1]
    p = SSM_STATE
    n = SSM_CHUNK * SSM_GROUP
    blk = lambda a: pl.BlockSpec((1, 1) + a.shape[2:], lambda d, gi: (d, gi) + (0,) * (a.ndim - 2))
    return pl.pallas_call(
        _ssm_prep_kernel,
        grid=(nd, g),
        in_specs=[blk(arow), blk(acol), blk(ldt), blk(bt), blk(ct)],
        out_specs=[
            pl.BlockSpec((1, 1, n, n), lambda d, gi: (d, gi, 0, 0)),
            pl.BlockSpec((1, 1, n, 2 * p), lambda d, gi: (d, gi, 0, 0)),
            pl.BlockSpec((1, 1, 2 * p, n), lambda d, gi: (d, gi, 0, 0)),
            pl.BlockSpec((1, 1, 1, 2 * p), lambda d, gi: (d, gi, 0, 0)),
        ],
        out_shape=[
            jax.ShapeDtypeStruct((nd, g, n, n), BF16),
            jax.ShapeDtypeStruct((nd, g, n, 2 * p), BF16),
            jax.ShapeDtypeStruct((nd, g, 2 * p, n), BF16),
            jax.ShapeDtypeStruct((nd, g, 1, 2 * p), F32),
        ],
        compiler_params=pltpu.CompilerParams(dimension_semantics=("arbitrary", "arbitrary")),
        name="ssm_prep",
    )(arow, acol, ldt, bt, ct)


def _chunk_carry(z, a, s0, reverse):
    p = SSM_STATE
    nc = z.shape[0]
    row = lax.broadcasted_iota(jnp.int32, z.shape, 0)
    is_re = lax.broadcasted_iota(jnp.int32, a.shape, 1) < p
    sign = jnp.where(is_re, -1.0, 1.0)

    def parts(ap):
        sw = pltpu.roll(ap, p, 1)
        return jnp.where(is_re, ap, sw), sign * jnp.where(is_re, sw, ap)

    def cmul(x, ar_full, ai_sgn):
        return x * ar_full + pltpu.roll(x, p, 1) * ai_sgn

    ar_full, ai_sgn = parts(a)
    first = (nc - 1) if reverse else 0
    e = z + jnp.where(row == first, cmul(s0, ar_full, ai_sgn), 0.0)
    k = 1
    while k < nc:
        if reverse:
            sh = jnp.where(row < nc - k, pltpu.roll(e, nc - k, 0), 0.0)
        else:
            sh = jnp.where(row >= k, pltpu.roll(e, k, 0), 0.0)
        e = e + cmul(sh, ar_full, ai_sgn)
        k *= 2
        if k < nc:
            ar_full, ai_sgn = parts(cmul(jnp.where(is_re, ar_full, sign * ai_sgn), ar_full, ai_sgn))
    if reverse:
        return jnp.where(row == first, s0, pltpu.roll(e, nc - 1, 0))
    return jnp.where(row == first, s0, pltpu.roll(e, 1, 0))


def _ssm_kernel(u_ref, um_ref, m_ref, w_ref, v_ref, at_ref, dvec_ref, y_ref):
    p = SSM_STATE
    u = u_ref[0, 0]
    ub = u.astype(BF16)
    y = dvec_ref[0] * u
    for d in range(2):
        yin = _dot(ub, m_ref[d, 0])
        z = _dot(ub, w_ref[d, 0])
        if d == 0:
            s0 = _dot(um_ref[0].astype(BF16), w_ref[0, 0])[0:1, :]
        else:
            s0 = jnp.zeros((1, 2 * p), F32)
        s_in = _chunk_carry(z, at_ref[d, 0], s0, reverse=(d == 1))
        y = y + yin + _dot(s_in.astype(BF16), v_ref[d, 0])
    y_ref[0, 0] = y


def _ssm(u2, um2, m, w, v, at, dvec):
    b, g, nc, n = u2.shape
    p = SSM_STATE
    return pl.pallas_call(
        _ssm_kernel,
        grid=(b, g),
        in_specs=[
            pl.BlockSpec((1, 1, nc, n), lambda bi, gi: (bi, gi, 0, 0)),
            pl.BlockSpec((1, 8, n), lambda bi, gi: (gi, 0, 0)),
            pl.BlockSpec((2, 1, n, n), lambda bi, gi: (0, gi, 0, 0)),
            pl.BlockSpec((2, 1, n, 2 * p), lambda bi, gi: (0, gi, 0, 0)),
            pl.BlockSpec((2, 1, 2 * p, n), lambda bi, gi: (0, gi, 0, 0)),
            pl.BlockSpec((2, 1, 1, 2 * p), lambda bi, gi: (0, gi, 0, 0)),
            pl.BlockSpec((1, 1, n), lambda bi, gi: (gi, 0, 0)),
        ],
        out_specs=pl.BlockSpec((1, 1, nc, n), lambda bi, gi: (bi, gi, 0, 0)),
        out_shape=jax.ShapeDtypeStruct((b, g, nc, n), F32),
        compiler_params=pltpu.CompilerParams(dimension_semantics=("arbitrary", "arbitrary")),
        name="ssm",
    )(u2, um2, m, w, v, at, dvec)


def _merge_kernel(x_ref, ya_ref, ys_ref, wgt_ref, wglu_ref, bglu_ref, wab_ref, wsb_ref, wo_ref,
                  lng_ref, lnb_ref, l1g_ref, l1b_ref, wr_ref, wrt_ref,
                  h1_ref, tok_ref, afft_ref):
    d = x_ref.shape[1]
    h = _ln(x_ref[...], lng_ref[...], lnb_ref[...])
    gates = _dot(h.astype(BF16), wgt_ref[...])
    ga = _sigmoid(gates[:, 0:d])
    gs = _sigmoid(gates[:, d:2 * d])
    ys = ys_ref[...]
    y = 0.5 * ys * (1.0 + jnp.tanh(math.sqrt(2.0 / math.pi) * (ys + 0.044715 * (ys * ys * ys))))
    yg = y * _sigmoid(_dot(y.astype(BF16), wglu_ref[...]) + bglu_ref[...])
    merged = ga * _dot(ya_ref[...], wab_ref[...]) + gs * _dot(yg.astype(BF16), wsb_ref[...])
    h1 = _ln(DEEPNORM_ALPHA * h + _dot(merged.astype(BF16), wo_ref[...]), l1g_ref[...], l1b_ref[...])
    h1_ref[...] = h1
    hi, lo = _split(h1)
    ne = afft_ref.shape[0]
    logits = _dot(hi, wr_ref[0]) + _dot(hi, wr_ref[1]) + _dot(lo, wr_ref[0])
    lane = lax.broadcasted_iota(jnp.int32, logits.shape, 1)
    logits = jnp.where(lane < 3 * ne, logits, -jnp.inf)
    ex = jnp.exp(logits - jnp.max(logits, axis=1, keepdims=True))
    aff = ex / jnp.sum(jnp.where(lane < ne, ex, 0.0), axis=1, keepdims=True)
    p0 = aff.astype(BF16)
    r1 = aff - p0.astype(F32)
    p1 = r1.astype(BF16)
    p2 = (r1 - p1.astype(F32)).astype(BF16)
    pieces = jnp.where(lane < ne, p0, jnp.where(lane < 2 * ne, p1, p2))
    tok_ref[...] = jnp.concatenate([hi, pieces], axis=1)
    lt = _dot_nt(wrt_ref[0], hi) + _dot_nt(wrt_ref[1], hi) + _dot_nt(wrt_ref[0], lo)
    et = jnp.exp(lt - jnp.max(lt, axis=0, keepdims=True))
    afft_ref[...] = et / jnp.sum(et, axis=0, keepdims=True)


def _merge(x2, ya, ys, wgt, wglu, bglu, wab, wsb, wo, lng, lnb, l1g, l1b, wr, wrt, tm):
    rows, d = x2.shape
    ne = wrt.shape[1]
    full = lambda a: pl.BlockSpec(a.shape, lambda i: (0,) * a.ndim)
    return pl.pallas_call(
        _merge_kernel,
        grid=(rows // tm,),
        in_specs=[
            pl.BlockSpec((tm, d), lambda i: (i, 0)),
            pl.BlockSpec((tm, ya.shape[1]), lambda i: (i, 0)),
            pl.BlockSpec((tm, ys.shape[1]), lambda i: (i, 0)),
            full(wgt), full(wglu), full(bglu), full(wab), full(wsb), full(wo),
            full(lng), full(lnb), full(l1g), full(l1b), full(wr), full(wrt),
        ],
        out_specs=[
            pl.BlockSpec((tm, d), lambda i: (i, 0)),
            pl.BlockSpec((tm, d + GATE_LANES), lambda i: (i, 0)),
            pl.BlockSpec((ne, tm), lambda i: (0, i)),
        ],
        out_shape=[
            jax.ShapeDtypeStruct((rows, d), F32),
            jax.ShapeDtypeStruct((rows, d + GATE_LANES), BF16),
            jax.ShapeDtypeStruct((ne, rows), F32),
        ],
        compiler_params=pltpu.CompilerParams(dimension_semantics=("arbitrary",),
                                             vmem_limit_bytes=VMEM_LIMIT),
        name="merge",
    )(x2, ya, ys, wgt, wglu, bglu, wab, wsb, wo, lng, lnb, l1g, l1b, wr, wrt)


def _route_kernel(afft_ref, tri_ref, scl_ref, lo_ref, *, cap):
    aff = afft_ref[...]
    ne, s = aff.shape
    capf = float(cap)

    def as_float(bits):
        return lax.bitcast_convert_type(bits, F32)

    def search(i, t):
        cand = t | jnp.left_shift(jnp.int32(1), 30 - i)
        cnt = jnp.sum(jnp.where(aff >= as_float(cand), 1.0, 0.0), axis=1, keepdims=True)
        return jnp.where(cnt >= capf, cand, t)

    thr_bits = lax.fori_loop(0, 31, search, jnp.zeros((ne, 1), jnp.int32))
    gt = aff >= as_float(thr_bits + 1)
    eq = (aff >= as_float(thr_bits)) & jnp.logical_not(gt)
    need = capf - jnp.sum(jnp.where(gt, 1.0, 0.0), axis=1, keepdims=True)
    tri = tri_ref[...]
    rt = ROUTE_TILE
    nt = s // rt
    col = lax.broadcasted_iota(jnp.int32, (ne, nt), 1)
    carry_eq = jnp.zeros((ne, 1), F32)
    carry_sel = jnp.zeros((ne, 1), F32)
    lo_val = jnp.zeros((ne, nt), F32)
    for t in range(nt):
        sl = slice(t * rt, (t + 1) * rt)
        eq_b = eq[:, sl]
        ceq = _dot(jnp.where(eq_b, 1.0, 0.0).astype(BF16), tri)
        sel_b = gt[:, sl] | (eq_b & ((ceq + carry_eq) <= need))
        carry_eq = carry_eq + ceq[:, rt - 1:rt]
        csel = _dot(jnp.where(sel_b, 1.0, 0.0).astype(BF16), tri)
        scl_ref[0, :, sl] = jnp.where(sel_b, csel, 0.0)
        lo_val = jnp.where(col == t, carry_sel, lo_val)
        carry_sel = carry_sel + csel[:, rt - 1:rt]
    lo_ref[0] = lo_val


def _route(afft, tri, nb, cap):
    ne, rows = afft.shape
    s = rows // nb
    nt = s // ROUTE_TILE
    return pl.pallas_call(
        functools.partial(_route_kernel, cap=cap),
        grid=(nb,),
        in_specs=[
            pl.BlockSpec((ne, s), lambda b: (0, b)),
            pl.BlockSpec(tri.shape, lambda b: (0, 0)),
        ],
        out_specs=[
            pl.BlockSpec((1, ne, s), lambda b: (b, 0, 0)),
            pl.BlockSpec((1, ne, nt), lambda b: (b, 0, 0)),
        ],
        out_shape=[
            jax.ShapeDtypeStruct((nb, ne, s), F32),
            jax.ShapeDtypeStruct((nb, ne, nt), F32),
        ],
        compiler_params=pltpu.CompilerParams(dimension_semantics=("arbitrary",)),
        name="route",
    )(afft, tri)


def _ffn_kernel(lo_ref, t_ref, scl_ref, wg_ref, wu_ref, wd_ref, ye_ref, xe_sc, *, cap, f_chunk):
    b = pl.program_id(0)
    e = pl.program_id(1)
    ne = pl.num_programs(1)
    rt = ROUTE_TILE
    nt = t_ref.shape[1] // rt
    xe_sc[...] = jnp.zeros(xe_sc.shape, F32)
    r = lax.broadcasted_iota(jnp.int32, (PIECE, rt), 0).astype(F32)

    def tile_info(tau):
        base = (b * ne + e) * (nt + 1) + tau
        lo = lo_ref[base]
        n_pieces = jnp.right_shift((lo & 7) + (lo_ref[base + 1] - lo) + (PIECE - 1), PIECE_SHIFT)
        return lo, n_pieces

    def add_piece(tau, lo, p):
        off = lo & 7
        scl = scl_ref[0, 0, pl.ds(tau, 1), :]
        tok = t_ref[0, pl.ds(pl.multiple_of(tau * rt, rt), rt), :]
        shift = (off - 1 - p * PIECE).astype(F32)
        onehot = jnp.where((scl > 0.0) & ((scl + shift) == r), 1.0, 0.0).astype(BF16)
        xe_sc[pl.ds(pl.multiple_of(lo - off + p * PIECE, 8), PIECE), :] += _dot(onehot, tok)

    def first_piece(tau, carry):
        add_piece(tau, tile_info(tau)[0], 0)
        return carry

    def more_pieces(tau, carry):
        lo, n_pieces = tile_info(tau)
        lax.fori_loop(1, n_pieces, lambda p, c: (add_piece(tau, lo, p), c)[1], 0)
        return carry

    lax.fori_loop(0, nt, first_piece, 0, unroll=4)
    lax.fori_loop(0, nt, more_pieces, 0)
    d = wg_ref.shape[1]
    xb = xe_sc[0:cap, 0:d].astype(BF16)
    gl = xe_sc[0:cap, d:d + GATE_LANES]
    lane = lax.broadcasted_iota(jnp.int32, gl.shape, 1)
    mine = ((lane & (N_EXPERTS - 1)) == e) & (lane < 3 * N_EXPERTS)
    gate = jnp.sum(jnp.where(mine, gl, 0.0), axis=1, keepdims=True)
    f = wg_ref.shape[2]
    acc = jnp.zeros((cap, wd_ref.shape[2]), F32)
    for c in range(f // f_chunk):
        sl = slice(c * f_chunk, (c + 1) * f_chunk)
        g = _dot(xb, wg_ref[0, :, sl])
        u = _dot(xb, wu_ref[0, :, sl])
        hh = (g * _sigmoid(g)) * u
        acc = acc + _dot(hh.astype(BF16), wd_ref[0, sl, :])
    ye_ref[0, 0] = (acc * gate).astype(BF16)


def _ffn(lo_i, t, scl4, wg, wu, wd, cap):
    b, s, dx = t.shape
    ne, d, f = wg.shape
    nt = s // ROUTE_TILE
    grid_spec = pltpu.PrefetchScalarGridSpec(
        num_scalar_prefetch=1,
        grid=(b, ne),
        in_specs=[
            pl.BlockSpec((1, s, dx), lambda bi, ei, lo: (bi, 0, 0), pipeline_mode=pl.Buffered(1)),
            pl.BlockSpec((1, 1, nt, ROUTE_TILE), lambda bi, ei, lo: (bi, ei, 0, 0)),
            pl.BlockSpec((1, d, f), lambda bi, ei, lo: (ei, 0, 0)),
            pl.BlockSpec((1, d, f), lambda bi, ei, lo: (ei, 0, 0)),
            pl.BlockSpec((1, f, d), lambda bi, ei, lo: (ei, 0, 0)),
        ],
        out_specs=pl.BlockSpec((1, 1, cap, d), lambda bi, ei, lo: (bi, ei, 0, 0)),
        scratch_shapes=[pltpu.VMEM((cap + PIECE, dx), F32)],
    )
    return pl.pallas_call(
        functools.partial(_ffn_kernel, cap=cap, f_chunk=512),
        grid_spec=grid_spec,
        out_shape=jax.ShapeDtypeStruct((b, ne, cap, d), BF16),
        compiler_params=pltpu.CompilerParams(dimension_semantics=("arbitrary", "arbitrary"),
                                             vmem_limit_bytes=VMEM_LIMIT),
        name="ffn",
    )(lo_i, t, scl4, wg, wu, wd)


def _combine_kernel(lo_ref, h1_ref, sclt_ref, ye_hbm, g_ref, b_ref, o_ref, win, xwin, acc_sc, sem, xsem, *, cap):
    b = pl.program_id(0)
    tau = pl.program_id(1)
    nb = pl.num_programs(0)
    nt = pl.num_programs(1)
    ne = win.shape[1]
    d = win.shape[3]
    step = b * nt + tau
    slot = step & 1

    def pieces(bi, ti, e):
        base = (bi * ne + e) * (nt + 1) + ti
        lo = lo_ref[base]
        n_sel = lo_ref[base + 1] - lo
        n_pieces = jnp.maximum(jnp.right_shift((lo & 15) + n_sel + (PIECE - 1), PIECE_SHIFT), 1)
        first = jnp.minimum(lo - (lo & 15), cap - n_pieces * PIECE)
        return lo, first, n_pieces

    def copy(bi, e, first, p, buf, s):
        start = pl.multiple_of(first + p * PIECE, 16)
        return pltpu.make_async_copy(ye_hbm.at[bi, e, pl.ds(start, PIECE), :], buf, s)

    def first_pieces(bi, ti, sl):
        return [copy(bi, e, pieces(bi, ti, e)[1], 0, win.at[sl, e], sem.at[sl, e]) for e in range(ne)]

    @pl.when(step == 0)
    def _prime():
        for cp in first_pieces(b, tau, slot):
            cp.start()

    @pl.when(step + 1 < nb * nt)
    def _prefetch():
        wrap = tau + 1 == nt
        for cp in first_pieces(jnp.where(wrap, b + 1, b), jnp.where(wrap, 0, tau + 1), 1 - slot):
            cp.start()

    for cp in first_pieces(b, tau, slot):
        cp.wait()

    sclt = sclt_ref[0]
    col = lax.broadcasted_iota(jnp.int32, (1, ne * PIECE), 1)
    grp = jnp.right_shift(col, PIECE_SHIFT)
    expand = jnp.where(lax.broadcasted_iota(jnp.int32, (ne, ne * PIECE), 0) == grp, 1.0, 0.0).astype(BF16)
    scl = _dot(sclt.astype(BF16), expand)
    shift = jnp.zeros(col.shape, F32)
    for e in range(ne):
        lo, first, _ = pieces(b, tau, e)
        shift = jnp.where(grp == e, (lo - first - 1).astype(F32), shift)
    r = (col & (PIECE - 1)).astype(F32)
    onehot = jnp.where((scl > 0.0) & ((scl + shift) == r), 1.0, 0.0).astype(BF16)
    acc_sc[...] = DEEPNORM_ALPHA * h1_ref[0] + _dot(onehot, win[slot].reshape(ne * PIECE, d))

    for e in range(ne):
        lo, first, n_pieces = pieces(b, tau, e)

        def extra(p, c, e=e, lo=lo, first=first):
            cp = copy(b, e, first, p, xwin, xsem.at[0])
            cp.start()
            cp.wait()
            se = sclt_ref[0][:, e:e + 1]
            rr = lax.broadcasted_iota(jnp.int32, (se.shape[0], PIECE), 1).astype(F32)
            oh = jnp.where((se > 0.0) & ((se + (lo - first - p * PIECE - 1).astype(F32)) == rr), 1.0, 0.0)
            acc_sc[...] += _dot(oh.astype(BF16), xwin[...])
            return c

        lax.fori_loop(1, n_pieces, extra, 0)
    o_ref[0] = _ln(acc_sc[...], g_ref[...], b_ref[...])


def _combine(lo_i, h1, sclt, ye, g, bb, cap):
    b, s, d = h1.shape
    ne = sclt.shape[2]
    rt = ROUTE_TILE
    grid_spec = pltpu.PrefetchScalarGridSpec(
        num_scalar_prefetch=1,
        grid=(b, s // rt),
        in_specs=[
            pl.BlockSpec((1, rt, d), lambda bi, ti, lo: (bi, ti, 0)),
            pl.BlockSpec((1, rt, ne), lambda bi, ti, lo: (bi, ti, 0)),
            pl.BlockSpec(memory_space=pl.ANY),
            pl.BlockSpec(g.shape, lambda bi, ti, lo: (0, 0)),
            pl.BlockSpec(bb.shape, lambda bi, ti, lo: (0, 0)),
        ],
        out_specs=pl.BlockSpec((1, rt, d), lambda bi, ti, lo: (bi, ti, 0)),
        scratch_shapes=[pltpu.VMEM((2, ne, PIECE, d), BF16), pltpu.VMEM((PIECE, d), BF16),
                        pltpu.VMEM((rt, d), F32),
                        pltpu.SemaphoreType.DMA((2, ne)), pltpu.SemaphoreType.DMA((1,))],
    )
    return pl.pallas_call(
        functools.partial(_combine_kernel, cap=cap),
        grid_spec=grid_spec,
        out_shape=jax.ShapeDtypeStruct((b, s, d), F32),
        compiler_params=pltpu.CompilerParams(dimension_semantics=("arbitrary", "arbitrary"),
                                             vmem_limit_bytes=VMEM_LIMIT),
        name="combine",
    )(lo_i, h1, sclt, ye, g, bb)


def _rope_tables(n_tokens):
    half = HEAD_DIM // 2
    inv_freq = ROPE_THETA ** (-jnp.arange(0, half, 2, dtype=F32) / half)
    rows = n_tokens // GRID_W
    row = jnp.repeat(jnp.arange(rows, dtype=F32), GRID_W)
    colv = jnp.tile(jnp.arange(GRID_W, dtype=F32), rows)
    ang = jnp.concatenate([row[:, None] * inv_freq, colv[:, None] * inv_freq], axis=-1)
    return jnp.cos(ang), jnp.sin(ang)


def kernel(x, meta_tokens, ln_in_g, ln_in_b, w_in, q_norm_g, k_norm_g, ssm_a_re, ssm_a_im, ssm_log_dt,
           ssm_b_re, ssm_b_im, ssm_c_re, ssm_c_im, ssm_d, w_glu, b_glu, w_attn_br, w_ssm_br, w_o,
           ln1_g, ln1_b, w_router, w_gate_e, w_up_e, w_down_e, ln2_g, ln2_b):
    b, s, d = x.shape
    aw = N_HEADS * HEAD_DIM
    kw = N_KV_HEADS * HEAD_DIM
    sw = d // 2
    g = sw // SSM_GROUP
    half = HEAD_DIM // 2
    cap = CAPACITY_FACTOR * s // N_EXPERTS
    rows = b * s
    l = 0

    perm = np.concatenate([np.arange(0, HEAD_DIM, 2), np.arange(1, HEAD_DIM, 2)])
    qcols = np.concatenate([h * HEAD_DIM + perm for h in range(N_HEADS)])
    kcols = aw + np.concatenate([h * HEAD_DIM + perm for h in range(N_KV_HEADS)])
    wl = w_in[l]
    w_t = jnp.concatenate([wl[:, qcols], wl[:, aw + kw:aw + 2 * kw]], axis=1).T.astype(BF16)
    w_n = jnp.concatenate([wl[:, kcols], wl[:, aw + 2 * kw:aw + 2 * kw + sw]], axis=1).astype(BF16)
    w_gates = wl[:, aw + 2 * kw + sw:].astype(BF16)
    qg3 = (jnp.tile(q_norm_g[l][perm], N_HEADS) * (HEAD_DIM ** -0.5 * math.log2(math.e))).reshape(N_HEADS, HEAD_DIM, 1)
    kg = jnp.tile(k_norm_g[l][perm], N_KV_HEADS)[None, :]
    bd = jnp.asarray(np.kron(np.eye(N_KV_HEADS), np.full((HEAD_DIM, HEAD_DIM), 1.0 / HEAD_DIM)), BF16)
    lng = ln_in_g[None, :]
    lnb = ln_in_b[None, :]
    cos, sin = _rope_tables(s)
    cosr = jnp.tile(cos, (1, 2 * N_KV_HEADS))
    sinr = jnp.tile(jnp.concatenate([-sin, sin], axis=1), (1, N_KV_HEADS))
    cost = cos.T
    sint = sin.T

    x2 = x.reshape(rows, d)
    tm_in = 512
    qt, kk, vt, u2 = _inproj(x2, cosr, sinr, cost, sint, w_t, w_n, lng, lnb, qg3, kg, bd, tm_in, s // tm_in)
    meta_p = jnp.pad(meta_tokens, ((0, META_PAD - N_META), (0, 0)))
    ones_r = jnp.ones((META_PAD, kw), F32)
    _, km, vtm, um = _inproj(meta_p, ones_r, jnp.zeros_like(ones_r), jnp.ones((half, META_PAD), F32),
                             jnp.zeros((half, META_PAD), F32), w_t, w_n, lng, lnb, qg3, kg, bd, META_PAD, 1)
    um = um[:N_META]
    km = km[:, :N_META, :]
    vtm = vtm[:, :, :N_META]

    ya = _attention(qt, kk, vt, km, vtm, b, 512, 512)

    t_chunk = SSM_CHUNK
    hgrp = SSM_GROUP
    n = t_chunk * hgrp
    arow = jnp.stack([ssm_a_re[l], ssm_a_im[l]], axis=2)
    acol = jnp.stack([ssm_a_re[l], ssm_a_im[l]], axis=3)
    ldt = ssm_log_dt[l][:, :, None, None]
    bt = jnp.stack([ssm_b_re[l], ssm_b_im[l]], axis=2)
    bt = jnp.tile(jnp.swapaxes(bt, 3, 4), (1, 1, 1, t_chunk, 1))
    ct = jnp.stack([ssm_c_re[l], ssm_c_im[l]], axis=2)
    ct = jnp.tile(jnp.swapaxes(ct, 3, 4), (1, 1, 1, 1, t_chunk))
    mm, ww, vv, at = _ssm_prep(arow, acol, ldt, bt, ct)
    nc = s // t_chunk
    u_g = u2.reshape(b, nc, t_chunk, g, hgrp).transpose(0, 3, 1, 2, 4).reshape(b, g, nc, n)
    um_g = um.reshape(N_META, g, hgrp).transpose(1, 0, 2).reshape(g, 1, N_META * hgrp)
    um_g = jnp.pad(um_g, ((0, 0), (0, 7), (n - N_META * hgrp, 0)))
    dvec = jnp.tile(ssm_d[l], (1, t_chunk))[:, None, :]
    y_g = _ssm(u_g, um_g, mm, ww, vv, at, dvec)
    ys = y_g.reshape(b, g, nc, t_chunk, hgrp).transpose(0, 2, 3, 1, 4).reshape(rows, sw)

    wr_hi = w_router[l].astype(BF16)
    wr_lo = (w_router[l] - wr_hi.astype(F32)).astype(BF16)
    wrt = jnp.stack([wr_hi.T, wr_lo.T])
    wr = jnp.pad(jnp.tile(jnp.stack([wr_hi, wr_lo]), (1, 1, 3)), ((0, 0), (0, 0), (0, GATE_LANES - 3 * N_EXPERTS)))
    h1, tok, afft = _merge(
        x2, ya, ys, w_gates, w_glu[l].astype(BF16), b_glu[l][None, :],
        w_attn_br[l].astype(BF16), w_ssm_br[l].astype(BF16), w_o[l].astype(BF16),
        lng, lnb, ln1_g[l][None, :], ln1_b[l][None, :], wr, wrt, 256)

    tri = jnp.asarray(np.triu(np.ones((ROUTE_TILE, ROUTE_TILE), np.float32)), BF16)
    scl, lo_f = _route(afft, tri, b, cap)
    nt = s // ROUTE_TILE
    lo_i = jnp.pad(lo_f.astype(jnp.int32), ((0, 0), (0, 0), (0, 1)), constant_values=cap).reshape(-1)
    ye = _ffn(lo_i, tok.reshape(b, s, d + GATE_LANES), scl.reshape(b, N_EXPERTS, nt, ROUTE_TILE),
              w_gate_e[l].astype(BF16), w_up_e[l].astype(BF16), w_down_e[l].astype(BF16), cap)
    out = _combine(lo_i, h1.reshape(b, s, d), jnp.swapaxes(scl, 1, 2), ye,
                   ln2_g[l][None, :], ln2_b[l][None, :], cap)
    return out
```

```python
import functools
import math

import numpy as np
import jax
import jax.numpy as jnp
from jax import lax
from jax.experimental import pallas as pl
from jax.experimental.pallas import tpu as pltpu

F32 = jnp.float32
BF16 = jnp.bfloat16

N_META = 16
GRID_W = 64
N_HEADS = 8
N_KV_HEADS = 2
HEAD_DIM = 64
ROPE_THETA = 10000.0
SSM_GROUP = 16
SSM_STATE = 64
N_EXPERTS = 16
CAPACITY_FACTOR = 2
LN_EPS = 1e-5
QK_EPS = 1e-6
DEPTH = 1
DEEPNORM_ALPHA = (2.0 * DEPTH) ** 0.25

SSM_CHUNK = 32
ROUTE_TILE = 256
GATE_LANES = 128
PIECE_SHIFT = 6
PIECE = 1 << PIECE_SHIFT
META_PAD = 128
V_ROWS = HEAD_DIM + 16
VMEM_LIMIT = 56 * 1024 * 1024


def _ln(x, g, b):
    mu = jnp.mean(x, axis=-1, keepdims=True)
    xc = x - mu
    var = jnp.mean(xc * xc, axis=-1, keepdims=True)
    return xc * lax.rsqrt(var + LN_EPS) * g + b


def _sigmoid(x):
    return 1.0 / (1.0 + jnp.exp(-x))


def _split(t):
    hi = t.astype(BF16)
    lo = (t - hi.astype(F32)).astype(BF16)
    return hi, lo


def _dot(a, b):
    return jnp.dot(a, b, preferred_element_type=F32)


def _dot_nt(a, b):
    return lax.dot_general(a, b, (((1,), (1,)), ((), ())), preferred_element_type=F32)


def _inproj_kernel(x_ref, cosr_ref, sinr_ref, cost_ref, sint_ref, wt_ref, wn_ref, lng_ref, lnb_ref,
                   qg_ref, kg_ref, bd_ref, qt_ref, k_ref, vt_ref, u_ref):
    aw = N_HEADS * HEAD_DIM
    kw = N_KV_HEADS * HEAD_DIM
    half = HEAD_DIM // 2
    h = _ln(x_ref[...], lng_ref[...], lnb_ref[...])
    hb = h.astype(BF16)
    pt = _dot_nt(wt_ref[...], hb)
    pn = _dot(hb, wn_ref[...])
    tm = hb.shape[0]

    qt = pt[0:aw].reshape(N_HEADS, HEAD_DIM, tm)
    ms = jnp.mean(qt * qt, axis=1, keepdims=True)
    qn = qt * lax.rsqrt(ms + QK_EPS) * qg_ref[...]
    x0 = qn[:, 0:half, :]
    x1 = qn[:, half:, :]
    c = cost_ref[...][None]
    s = sint_ref[...][None]
    qr = jnp.concatenate([x0 * c - x1 * s, x0 * s + x1 * c], axis=1)
    qt_ref[...] = qr.reshape(aw, tm).astype(BF16)
    vrow = lax.broadcasted_iota(jnp.int32, (N_KV_HEADS, V_ROWS - HEAD_DIM, tm), 1)
    vt_ref[...] = jnp.concatenate([pt[aw:aw + kw].reshape(N_KV_HEADS, HEAD_DIM, tm),
                                   jnp.where(vrow == 0, 1.0, 0.0)], axis=1).astype(BF16)

    kk = pn[:, 0:kw]
    hi, lo = _split(kk * kk)
    bd = bd_ref[...]
    msk = _dot(hi, bd) + _dot(lo, bd)
    kn = kk * lax.rsqrt(msk + QK_EPS) * kg_ref[...]
    lane = lax.broadcasted_iota(jnp.int32, kn.shape, 1)
    first = (lane & (HEAD_DIM - 1)) < half
    partner = jnp.where(first, pltpu.roll(kn, kw - half, 1), pltpu.roll(kn, half, 1))
    kr = (kn * cosr_ref[...] + partner * sinr_ref[...]).astype(BF16)
    for g in range(N_KV_HEADS):
        k_ref[g] = kr[:, g * HEAD_DIM:(g + 1) * HEAD_DIM]
    u_ref[...] = pn[:, kw:]


def _inproj(x2, cosr, sinr, cost, sint, wt, wn, lng, lnb, qg3, kg, bd, tm, n_tab_blocks):
    rows, d = x2.shape
    aw = N_HEADS * HEAD_DIM
    kw = N_KV_HEADS * HEAD_DIM
    uw = wn.shape[1] - kw
    half = HEAD_DIM // 2
    full = lambda a: pl.BlockSpec(a.shape, lambda i: (0,) * a.ndim)
    return pl.pallas_call(
        _inproj_kernel,
        grid=(rows // tm,),
        in_specs=[
            pl.BlockSpec((tm, d), lambda i: (i, 0)),
            pl.BlockSpec((tm, kw), lambda i: (i % n_tab_blocks, 0)),
            pl.BlockSpec((tm, kw), lambda i: (i % n_tab_blocks, 0)),
            pl.BlockSpec((half, tm), lambda i: (0, i % n_tab_blocks)),
            pl.BlockSpec((half, tm), lambda i: (0, i % n_tab_blocks)),
            full(wt), full(wn), full(lng), full(lnb), full(qg3), full(kg), full(bd),
        ],
        out_specs=[
            pl.BlockSpec((aw, tm), lambda i: (0, i)),
            pl.BlockSpec((N_KV_HEADS, tm, HEAD_DIM), lambda i: (0, i, 0)),
            pl.BlockSpec((N_KV_HEADS, V_ROWS, tm), lambda i: (0, 0, i)),
            pl.BlockSpec((tm, uw), lambda i: (i, 0)),
        ],
        out_shape=[
            jax.ShapeDtypeStruct((aw, rows), BF16),
            jax.ShapeDtypeStruct((N_KV_HEADS, rows, HEAD_DIM), BF16),
            jax.ShapeDtypeStruct((N_KV_HEADS, V_ROWS, rows), BF16),
            jax.ShapeDtypeStruct((rows, uw), F32),
        ],
        compiler_params=pltpu.CompilerParams(dimension_semantics=("arbitrary",),
                                             vmem_limit_bytes=VMEM_LIMIT),
        name="inproj",
    )(x2, cosr, sinr, cost, sint, wt, wn, lng, lnb, qg3, kg, bd)


def _attn_kernel(qt_ref, k0_ref, k1_ref, vta_ref, vtb_ref, km_ref, vtm_ref, o_ref, m_sc, acc_sc, s_sc, mb_sc):
    j = pl.program_id(2)
    last = pl.num_programs(2) - 1
    grp = N_HEADS // N_KV_HEADS
    heads = lambda h: slice(h * HEAD_DIM, (h + 1) * HEAD_DIM)

    def score(k_ref, slot):
        for h in range(N_HEADS):
            s = _dot(k_ref[h // grp], qt_ref[heads(h), :])
            s_sc[slot, h] = s
            mb_sc[slot, h:h + 1, :] = jnp.max(s, axis=0, keepdims=True)

    def softmax_pv(h, s, m_blk, vt_g):
        m_prev = m_sc[h:h + 1, :]
        m_new = jnp.maximum(m_prev, m_blk)
        alpha = jnp.exp2(m_prev - m_new)
        p = jnp.exp2(s - m_new).astype(BF16)
        acc_sc[h] = alpha * acc_sc[h] + _dot(vt_g, p)
        m_sc[h:h + 1, :] = m_new

    def consume(vt_ref, slot):
        for h in range(N_HEADS):
            softmax_pv(h, s_sc[slot, h], mb_sc[slot, h:h + 1, :], vt_ref[h // grp])

    @pl.when(j == 0)
    def _first():
        m_sc[...] = jnp.full(m_sc.shape, -jnp.inf, F32)
        acc_sc[...] = jnp.zeros(acc_sc.shape, F32)
        for h in range(N_HEADS):
            s = _dot(km_ref[h // grp], qt_ref[heads(h), :])
            softmax_pv(h, s, jnp.max(s, axis=0, keepdims=True), vtm_ref[h // grp])
        score(k0_ref, 0)
        score(k1_ref, 1)
        consume(vtb_ref, 0)

    @pl.when((j > 0) & (j < last))
    def _middle():
        score(k0_ref, 0)
        consume(vta_ref, 1)
        score(k1_ref, 1)
        consume(vtb_ref, 0)

    @pl.when(j == last)
    def _last():
        consume(vta_ref, 1)
        acc = acc_sc[...]
        tq = acc.shape[2]
        out_t = (acc[:, 0:HEAD_DIM, :] / acc[:, HEAD_DIM:HEAD_DIM + 1, :]).reshape(N_HEADS * HEAD_DIM, tq)
        o_ref[...] = out_t.T.astype(BF16)


def _attention(qt, kk, vt, km, vtm, nb, tq, tk):
    aw, rows = qt.shape
    s = rows // nb
    nq = s // tq
    nkb = s // tk
    kblk = lambda f: pl.BlockSpec((N_KV_HEADS, tk, HEAD_DIM),
                                  lambda bi, qi, j: (0, bi * nkb + jnp.clip(f(j), 0, nkb - 1), 0))
    vblk = lambda f: pl.BlockSpec((N_KV_HEADS, V_ROWS, tk),
                                  lambda bi, qi, j: (0, 0, bi * nkb + jnp.clip(f(j), 0, nkb - 1)))
    return pl.pallas_call(
        _attn_kernel,
        grid=(nb, nq, nkb // 2 + 1),
        in_specs=[
            pl.BlockSpec((aw, tq), lambda bi, qi, j: (0, bi * nq + qi)),
            kblk(lambda j: 2 * j), kblk(lambda j: 2 * j + 1),
            vblk(lambda j: 2 * j - 1), vblk(lambda j: 2 * j),
            pl.BlockSpec(km.shape, lambda bi, qi, j: (0, 0, 0)),
            pl.BlockSpec(vtm.shape, lambda bi, qi, j: (0, 0, 0)),
        ],
        out_specs=pl.BlockSpec((tq, aw), lambda bi, qi, j: (bi * nq + qi, 0)),
        out_shape=jax.ShapeDtypeStruct((rows, aw), BF16),
        scratch_shapes=[
            pltpu.VMEM((N_HEADS, tq), F32),
            pltpu.VMEM((N_HEADS, V_ROWS, tq), F32),
            pltpu.VMEM((2, N_HEADS, tk, tq), F32),
            pltpu.VMEM((2, N_HEADS, tq), F32),
        ],
        compiler_params=pltpu.CompilerParams(
            dimension_semantics=("arbitrary", "arbitrary", "arbitrary"),
            vmem_limit_bytes=VMEM_LIMIT),
        name="attn",
    )(qt, kk, kk, vt, vt, km, vtm)


def _ssm_prep_kernel(arow_ref, acol_ref, ldt_ref, bt_ref, ct_ref, m_ref, w_ref, v_ref, at_ref):
    t_chunk = SSM_CHUNK
    shift = int(math.log2(SSM_GROUP))
    df = pl.program_id(0).astype(F32)
    dt = jnp.exp(ldt_ref[0, 0])

    def abar(ar, ai):
        mag = jnp.exp(ar * dt)
        ang = ai * dt
        return mag * jnp.cos(ang), mag * jnp.sin(ang)

    def cpow(ar, ai, e):
        mag = jnp.exp(ar * dt * e)
        ang = ai * dt * e
        return mag * jnp.cos(ang), mag * jnp.sin(ang)

    ar_r = arow_ref[0, 0, 0:1, :]
    ai_r = arow_ref[0, 0, 1:2, :]
    abr, abi = abar(ar_r, ai_r)
    nr = abr - 1.0
    ni = abi
    den = ar_r * ar_r + ai_r * ai_r
    cr = (nr * ar_r + ni * ai_r) / den
    ci = (ni * ar_r - nr * ai_r) / den
    btr = bt_ref[0, 0, 0]
    bti = bt_ref[0, 0, 1]
    bbr = cr * btr - ci * bti
    bbi = cr * bti + ci * btr
    nrow = btr.shape[0]
    nstate = btr.shape[1]
    jr = (lax.broadcasted_iota(jnp.int32, (nrow, 1), 0) >> shift).astype(F32)
    lj = jr + df * ((t_chunk - 1) - 2.0 * jr)
    step = lax.broadcasted_iota(jnp.int32, (t_chunk, 1), 0).astype(F32)
    lstep = step + df * ((t_chunk - 1) - 2.0 * step)

    def rep_rows(x):
        return jnp.broadcast_to(x[:, None, :], (t_chunk, SSM_GROUP, nstate)).reshape(nrow, nstate)

    enr, eni = [rep_rows(x) for x in cpow(ar_r, ai_r, -lstep)]
    bmr = enr * bbr - eni * bbi
    bmi = enr * bbi + eni * bbr
    ewr, ewi = [rep_rows(x) for x in cpow(ar_r, ai_r, (t_chunk - 1) - lstep)]
    w_ref[0, 0] = jnp.concatenate([ewr * bbr - ewi * bbi, ewr * bbi + ewi * bbr], axis=1).astype(BF16)
    atr, ati = cpow(ar_r, ai_r, float(t_chunk))
    at_ref[0, 0] = jnp.concatenate([atr, ati], axis=1)

    ar_c = acol_ref[0, 0, :, 0:1]
    ai_c = acol_ref[0, 0, :, 1:2]
    ctr = ct_ref[0, 0, 0]
    cti = ct_ref[0, 0, 1]
    ncol = ctr.shape[1]
    tci = lax.broadcasted_iota(jnp.int32, (1, ncol), 1) >> shift
    tc = tci.astype(F32)
    lt = tc + df * ((t_chunk - 1) - 2.0 * tc)
    stepl = lax.broadcasted_iota(jnp.int32, (1, t_chunk), 1).astype(F32)
    lstepl = stepl + df * ((t_chunk - 1) - 2.0 * stepl)
    rep = jnp.where(lax.broadcasted_iota(jnp.int32, (t_chunk, ncol), 0) == tci, 1.0, 0.0).astype(BF16)

    def rep_cols(x):
        hi, lo = _split(x)
        lo2 = (x - hi.astype(F32) - lo.astype(F32)).astype(BF16)
        return _dot(hi, rep) + _dot(lo, rep) + _dot(lo2, rep)

    ecr, eci = [rep_cols(x) for x in cpow(ar_c, ai_c, lstepl)]
    cmr = ctr * ecr - cti * eci
    cmi = ctr * eci + cti * ecr
    lhs_hi, lhs_lo = _split(jnp.concatenate([bmr, -bmi], axis=1))
    rhs_hi, rhs_lo = _split(jnp.concatenate([cmr, cmi], axis=0))
    m = _dot(lhs_hi, rhs_hi) + _dot(lhs_hi, rhs_lo) + _dot(lhs_lo, rhs_hi)
    m_ref[0, 0] = jnp.where(lj <= lt, m, 0.0).astype(BF16)
    abr_c, abi_c = abar(ar_c, ai_c)
    c1r = cmr * abr_c - cmi * abi_c
    c1i = cmr * abi_c + cmi * abr_c
    v_ref[0, 0] = jnp.concatenate([c1r, -c1i], axis=0).astype(BF16)


def _ssm_prep(arow, acol, ldt, bt, ct):
    nd, g = arow.shape[0], arow.shape[1]
    p = SSM_STATE
    n = SSM_CHUNK * SSM_GROUP
    blk = lambda a: pl.BlockSpec((1, 1) + a.shape[2:], lambda d, gi: (d, gi) + (0,) * (a.ndim - 2))
    return pl.pallas_call(
        _ssm_prep_kernel,
        grid=(nd, g),
        in_specs=[blk(arow), blk(acol), blk(ldt), blk(bt), blk(ct)],
        out_specs=[
            pl.BlockSpec((1, 1, n, n), lambda d, gi: (d, gi, 0, 0)),
            pl.BlockSpec((1, 1, n, 2 * p), lambda d, gi: (d, gi, 0, 0)),
            pl.BlockSpec((1, 1, 2 * p, n), lambda d, gi: (d, gi, 0, 0)),
            pl.BlockSpec((1, 1, 1, 2 * p), lambda d, gi: (d, gi, 0, 0)),
        ],
        out_shape=[
            jax.ShapeDtypeStruct((nd, g, n, n), BF16),
            jax.ShapeDtypeStruct((nd, g, n, 2 * p), BF16),
            jax.ShapeDtypeStruct((nd, g, 2 * p, n), BF16),
            jax.ShapeDtypeStruct((nd, g, 1, 2 * p), F32),
        ],
        compiler_params=pltpu.CompilerParams(dimension_semantics=("arbitrary", "arbitrary")),
        name="ssm_prep",
    )(arow, acol, ldt, bt, ct)


def _chunk_carry(z, a, s0, reverse):
    p = SSM_STATE
    nc = z.shape[0]
    row = lax.broadcasted_iota(jnp.int32, z.shape, 0)
    is_re = lax.broadcasted_iota(jnp.int32, a.shape, 1) < p
    sign = jnp.where(is_re, -1.0, 1.0)

    def parts(ap):
        sw = pltpu.roll(ap, p, 1)
        return jnp.where(is_re, ap, sw), sign * jnp.where(is_re, sw, ap)

    def cmul(x, ar_full, ai_sgn):
        return x * ar_full + pltpu.roll(x, p, 1) * ai_sgn

    ar_full, ai_sgn = parts(a)
    first = (nc - 1) if reverse else 0
    e = z + jnp.where(row == first, cmul(s0, ar_full, ai_sgn), 0.0)
    k = 1
    while k < nc:
        if reverse:
            sh = jnp.where(row < nc - k, pltpu.roll(e, nc - k, 0), 0.0)
        else:
            sh = jnp.where(row >= k, pltpu.roll(e, k, 0), 0.0)
        e = e + cmul(sh, ar_full, ai_sgn)
        k *= 2
        if k < nc:
            ar_full, ai_sgn = parts(cmul(jnp.where(is_re, ar_full, sign * ai_sgn), ar_full, ai_sgn))
    if reverse:
        return jnp.where(row == first, s0, pltpu.roll(e, nc - 1, 0))
    return jnp.where(row == first, s0, pltpu.roll(e, 1, 0))


def _ssm_kernel(u_ref, um_ref, m_ref, w_ref, v_ref, at_ref, dvec_ref, y_ref):
    p = SSM_STATE
    u = u_ref[0, 0]
    ub = u.astype(BF16)
    y = dvec_ref[0] * u
    for d in range(2):
        yin = _dot(ub, m_ref[d, 0])
        z = _dot(ub, w_ref[d, 0])
        if d == 0:
            s0 = _dot(um_ref[0].astype(BF16), w_ref[0, 0])[0:1, :]
        else:
            s0 = jnp.zeros((1, 2 * p), F32)
        s_in = _chunk_carry(z, at_ref[d, 0], s0, reverse=(d == 1))
        y = y + yin + _dot(s_in.astype(BF16), v_ref[d, 0])
    y_ref[0, 0] = y


def _ssm(u2, um2, m, w, v, at, dvec):
    b, g, nc, n = u2.shape
    p = SSM_STATE
    return pl.pallas_call(
        _ssm_kernel,
        grid=(b, g),
        in_specs=[
            pl.BlockSpec((1, 1, nc, n), lambda bi, gi: (bi, gi, 0, 0)),
            pl.BlockSpec((1, 8, n), lambda bi, gi: (gi, 0, 0)),
            pl.BlockSpec((2, 1, n, n), lambda bi, gi: (0, gi, 0, 0)),
            pl.BlockSpec((2, 1, n, 2 * p), lambda bi, gi: (0, gi, 0, 0)),
            pl.BlockSpec((2, 1, 2 * p, n), lambda bi, gi: (0, gi, 0, 0)),
            pl.BlockSpec((2, 1, 1, 2 * p), lambda bi, gi: (0, gi, 0, 0)),
            pl.BlockSpec((1, 1, n), lambda bi, gi: (gi, 0, 0)),
        ],
        out_specs=pl.BlockSpec((1, 1, nc, n), lambda bi, gi: (bi, gi, 0, 0)),
        out_shape=jax.ShapeDtypeStruct((b, g, nc, n), F32),
        compiler_params=pltpu.CompilerParams(dimension_semantics=("arbitrary", "arbitrary")),
        name="ssm",
    )(u2, um2, m, w, v, at, dvec)


def _merge_kernel(x_ref, ya_ref, ys_ref, wgt_ref, wglu_ref, bglu_ref, wab_ref, wsb_ref, wo_ref,
                  lng_ref, lnb_ref, l1g_ref, l1b_ref, wr_ref, wrt_ref,
                  h1_ref, tok_ref, afft_ref):
    d = x_ref.shape[1]
    h = _ln(x_ref[...], lng_ref[...], lnb_ref[...])
    gates = _dot(h.astype(BF16), wgt_ref[...])
    ga = _sigmoid(gates[:, 0:d])
    gs = _sigmoid(gates[:, d:2 * d])
    ys = ys_ref[...]
    y = 0.5 * ys * (1.0 + jnp.tanh(math.sqrt(2.0 / math.pi) * (ys + 0.044715 * (ys * ys * ys))))
    yg = y * _sigmoid(_dot(y.astype(BF16), wglu_ref[...]) + bglu_ref[...])
    merged = ga * _dot(ya_ref[...], wab_ref[...]) + gs * _dot(yg.astype(BF16), wsb_ref[...])
    h1 = _ln(DEEPNORM_ALPHA * h + _dot(merged.astype(BF16), wo_ref[...]), l1g_ref[...], l1b_ref[...])
    h1_ref[...] = h1
    hi, lo = _split(h1)
    ne = afft_ref.shape[0]
    logits = _dot(hi, wr_ref[0]) + _dot(hi, wr_ref[1]) + _dot(lo, wr_ref[0])
    lane = lax.broadcasted_iota(jnp.int32, logits.shape, 1)
    logits = jnp.where(lane < 3 * ne, logits, -jnp.inf)
    ex = jnp.exp(logits - jnp.max(logits, axis=1, keepdims=True))
    aff = ex / jnp.sum(jnp.where(lane < ne, ex, 0.0), axis=1, keepdims=True)
    p0 = aff.astype(BF16)
    r1 = aff - p0.astype(F32)
    p1 = r1.astype(BF16)
    p2 = (r1 - p1.astype(F32)).astype(BF16)
    pieces = jnp.where(lane < ne, p0, jnp.where(lane < 2 * ne, p1, p2))
    tok_ref[...] = jnp.concatenate([hi, pieces], axis=1)
    lt = _dot_nt(wrt_ref[0], hi) + _dot_nt(wrt_ref[1], hi) + _dot_nt(wrt_ref[0], lo)
    et = jnp.exp(lt - jnp.max(lt, axis=0, keepdims=True))
    afft_ref[...] = et / jnp.sum(et, axis=0, keepdims=True)


def _merge(x2, ya, ys, wgt, wglu, bglu, wab, wsb, wo, lng, lnb, l1g, l1b, wr, wrt, tm):
    rows, d = x2.shape
    ne = wrt.shape[1]
    full = lambda a: pl.BlockSpec(a.shape, lambda i: (0,) * a.ndim)
    return pl.pallas_call(
        _merge_kernel,
        grid=(rows // tm,),
        in_specs=[
            pl.BlockSpec((tm, d), lambda i: (i, 0)),
            pl.BlockSpec((tm, ya.shape[1]), lambda i: (i, 0)),
            pl.BlockSpec((tm, ys.shape[1]), lambda i: (i, 0)),
            full(wgt), full(wglu), full(bglu), full(wab), full(wsb), full(wo),
            full(lng), full(lnb), full(l1g), full(l1b), full(wr), full(wrt),
        ],
        out_specs=[
            pl.BlockSpec((tm, d), lambda i: (i, 0)),
            pl.BlockSpec((tm, d + GATE_LANES), lambda i: (i, 0)),
            pl.BlockSpec((ne, tm), lambda i: (0, i)),
        ],
        out_shape=[
            jax.ShapeDtypeStruct((rows, d), F32),
            jax.ShapeDtypeStruct((rows, d + GATE_LANES), BF16),
            jax.ShapeDtypeStruct((ne, rows), F32),
        ],
        compiler_params=pltpu.CompilerParams(dimension_semantics=("arbitrary",),
                                             vmem_limit_bytes=VMEM_LIMIT),
        name="merge",
    )(x2, ya, ys, wgt, wglu, bglu, wab, wsb, wo, lng, lnb, l1g, l1b, wr, wrt)


def _route_kernel(afft_ref, tri_ref, scl_ref, lo_ref, *, cap):
    aff = afft_ref[...]
    ne, s = aff.shape
    capf = float(cap)

    def as_float(bits):
        return lax.bitcast_convert_type(bits, F32)

    def search(i, t):
        cand = t | jnp.left_shift(jnp.int32(1), 30 - i)
        cnt = jnp.sum(jnp.where(aff >= as_float(cand), 1.0, 0.0), axis=1, keepdims=True)
        return jnp.where(cnt >= capf, cand, t)

    thr_bits = lax.fori_loop(0, 31, search, jnp.zeros((ne, 1), jnp.int32))
    gt = aff >= as_float(thr_bits + 1)
    eq = (aff >= as_float(thr_bits)) & jnp.logical_not(gt)
    need = capf - jnp.sum(jnp.where(gt, 1.0, 0.0), axis=1, keepdims=True)
    tri = tri_ref[...]
    rt = ROUTE_TILE
    nt = s // rt
    col = lax.broadcasted_iota(jnp.int32, (ne, nt), 1)
    carry_eq = jnp.zeros((ne, 1), F32)
    carry_sel = jnp.zeros((ne, 1), F32)
    lo_val = jnp.zeros((ne, nt), F32)
    for t in range(nt):
        sl = slice(t * rt, (t + 1) * rt)
        eq_b = eq[:, sl]
        ceq = _dot(jnp.where(eq_b, 1.0, 0.0).astype(BF16), tri)
        sel_b = gt[:, sl] | (eq_b & ((ceq + carry_eq) <= need))
        carry_eq = carry_eq + ceq[:, rt - 1:rt]
        csel = _dot(jnp.where(sel_b, 1.0, 0.0).astype(BF16), tri)
        scl_ref[0, :, sl] = jnp.where(sel_b, csel, 0.0)
        lo_val = jnp.where(col == t, carry_sel, lo_val)
        carry_sel = carry_sel + csel[:, rt - 1:rt]
    lo_ref[0] = lo_val


def _route(afft, tri, nb, cap):
    ne, rows = afft.shape
    s = rows // nb
    nt = s // ROUTE_TILE
    return pl.pallas_call(
        functools.partial(_route_kernel, cap=cap),
        grid=(nb,),
        in_specs=[
            pl.BlockSpec((ne, s), lambda b: (0, b)),
            pl.BlockSpec(tri.shape, lambda b: (0, 0)),
        ],
        out_specs=[
            pl.BlockSpec((1, ne, s), lambda b: (b, 0, 0)),
            pl.BlockSpec((1, ne, nt), lambda b: (b, 0, 0)),
        ],
        out_shape=[
            jax.ShapeDtypeStruct((nb, ne, s), F32),
            jax.ShapeDtypeStruct((nb, ne, nt), F32),
        ],
        compiler_params=pltpu.CompilerParams(dimension_semantics=("arbitrary",)),
        name="route",
    )(afft, tri)


def _ffn_kernel(lo_ref, t_ref, scl_ref, wg_ref, wu_ref, wd_ref, ye_ref, xe_sc, *, cap, f_chunk):
    b = pl.program_id(0)
    e = pl.program_id(1)
    ne = pl.num_programs(1)
    rt = ROUTE_TILE
    nt = t_ref.shape[1] // rt
    xe_sc[...] = jnp.zeros(xe_sc.shape, F32)
    r = lax.broadcasted_iota(jnp.int32, (PIECE, rt), 0).astype(F32)

    def tile_info(tau):
        base = (b * ne + e) * (nt + 1) + tau
        lo = lo_ref[base]
        n_pieces = jnp.right_shift((lo & 7) + (lo_ref[base + 1] - lo) + (PIECE - 1), PIECE_SHIFT)
        return lo, n_pieces

    def add_piece(tau, lo, p):
        off = lo & 7
        scl = scl_ref[0, 0, pl.ds(tau, 1), :]
        tok = t_ref[0, pl.ds(pl.multiple_of(tau * rt, rt), rt), :]
        shift = (off - 1 - p * PIECE).astype(F32)
        onehot = jnp.where((scl > 0.0) & ((scl + shift) == r), 1.0, 0.0).astype(BF16)
        xe_sc[pl.ds(pl.multiple_of(lo - off + p * PIECE, 8), PIECE), :] += _dot(onehot, tok)

    def first_piece(tau, carry):
        add_piece(tau, tile_info(tau)[0], 0)
        return carry

    def more_pieces(tau, carry):
        lo, n_pieces = tile_info(tau)
        lax.fori_loop(1, n_pieces, lambda p, c: (add_piece(tau, lo, p), c)[1], 0)
        return carry

    lax.fori_loop(0, nt, first_piece, 0, unroll=4)
    lax.fori_loop(0, nt, more_pieces, 0)
    d = wg_ref.shape[1]
    xb = xe_sc[0:cap, 0:d].astype(BF16)
    gl = xe_sc[0:cap, d:d + GATE_LANES]
    lane = lax.broadcasted_iota(jnp.int32, gl.shape, 1)
    mine = ((lane & (N_EXPERTS - 1)) == e) & (lane < 3 * N_EXPERTS)
    gate = jnp.sum(jnp.where(mine, gl, 0.0), axis=1, keepdims=True)
    f = wg_ref.shape[2]
    acc = jnp.zeros((cap, wd_ref.shape[2]), F32)
    for c in range(f // f_chunk):
        sl = slice(c * f_chunk, (c + 1) * f_chunk)
        g = _dot(xb, wg_ref[0, :, sl])
        u = _dot(xb, wu_ref[0, :, sl])
        hh = (g * _sigmoid(g)) * u
        acc = acc + _dot(hh.astype(BF16), wd_ref[0, sl, :])
    ye_ref[0, 0] = (acc * gate).astype(BF16)


def _ffn(lo_i, t, scl4, wg, wu, wd, cap):
    b, s, dx = t.shape
    ne, d, f = wg.shape
    nt = s // ROUTE_TILE
    grid_spec = pltpu.PrefetchScalarGridSpec(
        num_scalar_prefetch=1,
        grid=(b, ne),
        in_specs=[
            pl.BlockSpec((1, s, dx), lambda bi, ei, lo: (bi, 0, 0), pipeline_mode=pl.Buffered(1)),
            pl.BlockSpec((1, 1, nt, ROUTE_TILE), lambda bi, ei, lo: (bi, ei, 0, 0)),
            pl.BlockSpec((1, d, f), lambda bi, ei, lo: (ei, 0, 0)),
            pl.BlockSpec((1, d, f), lambda bi, ei, lo: (ei, 0, 0)),
            pl.BlockSpec((1, f, d), lambda bi, ei, lo: (ei, 0, 0)),
        ],
        out_specs=pl.BlockSpec((1, 1, cap, d), lambda bi, ei, lo: (bi, ei, 0, 0)),
        scratch_shapes=[pltpu.VMEM((cap + PIECE, dx), F32)],
    )
    return pl.pallas_call(
        functools.partial(_ffn_kernel, cap=cap, f_chunk=512),
        grid_spec=grid_spec,
        out_shape=jax.ShapeDtypeStruct((b, ne, cap, d), BF16),
        compiler_params=pltpu.CompilerParams(dimension_semantics=("arbitrary", "arbitrary"),
                                             vmem_limit_bytes=VMEM_LIMIT),
        name="ffn",
    )(lo_i, t, scl4, wg, wu, wd)


def _combine_kernel(lo_ref, h1_ref, sclt_ref, ye_hbm, g_ref, b_ref, o_ref, win, xwin, acc_sc, sem, xsem, *, cap):
    b = pl.program_id(0)
    tau = pl.program_id(1)
    nb = pl.num_programs(0)
    nt = pl.num_programs(1)
    ne = win.shape[1]
    d = win.shape[3]
    step = b * nt + tau
    slot = step & 1

    def pieces(bi, ti, e):
        base = (bi * ne + e) * (nt + 1) + ti
        lo = lo_ref[base]
        n_sel = lo_ref[base + 1] - lo
        n_pieces = jnp.maximum(jnp.right_shift((lo & 15) + n_sel + (PIECE - 1), PIECE_SHIFT), 1)
        first = jnp.minimum(lo - (lo & 15), cap - n_pieces * PIECE)
        return lo, first, n_pieces

    def copy(bi, e, first, p, buf, s):
        start = pl.multiple_of(first + p * PIECE, 16)
        return pltpu.make_async_copy(ye_hbm.at[bi, e, pl.ds(start, PIECE), :], buf, s)

    def first_pieces(bi, ti, sl):
        return [copy(bi, e, pieces(bi, ti, e)[1], 0, win.at[sl, e], sem.at[sl, e]) for e in range(ne)]

    @pl.when(step == 0)
    def _prime():
        for cp in first_pieces(b, tau, slot):
            cp.start()

    @pl.when(step + 1 < nb * nt)
    def _prefetch():
        wrap = tau + 1 == nt
        for cp in first_pieces(jnp.where(wrap, b + 1, b), jnp.where(wrap, 0, tau + 1), 1 - slot):
            cp.start()

    for cp in first_pieces(b, tau, slot):
        cp.wait()

    sclt = sclt_ref[0]
    col = lax.broadcasted_iota(jnp.int32, (1, ne * PIECE), 1)
    grp = jnp.right_shift(col, PIECE_SHIFT)
    expand = jnp.where(lax.broadcasted_iota(jnp.int32, (ne, ne * PIECE), 0) == grp, 1.0, 0.0).astype(BF16)
    scl = _dot(sclt.astype(BF16), expand)
    shift = jnp.zeros(col.shape, F32)
    for e in range(ne):
        lo, first, _ = pieces(b, tau, e)
        shift = jnp.where(grp == e, (lo - first - 1).astype(F32), shift)
    r = (col & (PIECE - 1)).astype(F32)
    onehot = jnp.where((scl > 0.0) & ((scl + shift) == r), 1.0, 0.0).astype(BF16)
    acc_sc[...] = DEEPNORM_ALPHA * h1_ref[0] + _dot(onehot, win[slot].reshape(ne * PIECE, d))

    for e in range(ne):
        lo, first, n_pieces = pieces(b, tau, e)

        def extra(p, c, e=e, lo=lo, first=first):
            cp = copy(b, e, first, p, xwin, xsem.at[0])
            cp.start()
            cp.wait()
            se = sclt_ref[0][:, e:e + 1]
            rr = lax.broadcasted_iota(jnp.int32, (se.shape[0], PIECE), 1).astype(F32)
            oh = jnp.where((se > 0.0) & ((se + (lo - first - p * PIECE - 1).astype(F32)) == rr), 1.0, 0.0)
            acc_sc[...] += _dot(oh.astype(BF16), xwin[...])
            return c

        lax.fori_loop(1, n_pieces, extra, 0)
    o_ref[0] = _ln(acc_sc[...], g_ref[...], b_ref[...])


def _combine(lo_i, h1, sclt, ye, g, bb, cap):
    b, s, d = h1.shape
    ne = sclt.shape[2]
    rt = ROUTE_TILE
    grid_spec = pltpu.PrefetchScalarGridSpec(
        num_scalar_prefetch=1,
        grid=(b, s // rt),
        in_specs=[
            pl.BlockSpec((1, rt, d), lambda bi, ti, lo: (bi, ti, 0)),
            pl.BlockSpec((1, rt, ne), lambda bi, ti, lo: (bi, ti, 0)),
            pl.BlockSpec(memory_space=pl.ANY),
            pl.BlockSpec(g.shape, lambda bi, ti, lo: (0, 0)),
            pl.BlockSpec(bb.shape, lambda bi, ti, lo: (0, 0)),
        ],
        out_specs=pl.BlockSpec((1, rt, d), lambda bi, ti, lo: (bi, ti, 0)),
        scratch_shapes=[pltpu.VMEM((2, ne, PIECE, d), BF16), pltpu.VMEM((PIECE, d), BF16),
                        pltpu.VMEM((rt, d), F32),
                        pltpu.SemaphoreType.DMA((2, ne)), pltpu.SemaphoreType.DMA((1,))],
    )
    return pl.pallas_call(
        functools.partial(_combine_kernel, cap=cap),
        grid_spec=grid_spec,
        out_shape=jax.ShapeDtypeStruct((b, s, d), F32),
        compiler_params=pltpu.CompilerParams(dimension_semantics=("arbitrary", "arbitrary"),
                                             vmem_limit_bytes=VMEM_LIMIT),
        name="combine",
    )(lo_i, h1, sclt, ye, g, bb)


def _rope_tables(n_tokens):
    half = HEAD_DIM // 2
    inv_freq = ROPE_THETA ** (-jnp.arange(0, half, 2, dtype=F32) / half)
    rows = n_tokens // GRID_W
    row = jnp.repeat(jnp.arange(rows, dtype=F32), GRID_W)
    colv = jnp.tile(jnp.arange(GRID_W, dtype=F32), rows)
    ang = jnp.concatenate([row[:, None] * inv_freq, colv[:, None] * inv_freq], axis=-1)
    return jnp.cos(ang), jnp.sin(ang)


def kernel(x, meta_tokens, ln_in_g, ln_in_b, w_in, q_norm_g, k_norm_g, ssm_a_re, ssm_a_im, ssm_log_dt,
           ssm_b_re, ssm_b_im, ssm_c_re, ssm_c_im, ssm_d, w_glu, b_glu, w_attn_br, w_ssm_br, w_o,
           ln1_g, ln1_b, w_router, w_gate_e, w_up_e, w_down_e, ln2_g, ln2_b):
    b, s, d = x.shape
    aw = N_HEADS * HEAD_DIM
    kw = N_KV_HEADS * HEAD_DIM
    sw = d // 2
    g = sw // SSM_GROUP
    half = HEAD_DIM // 2
    cap = CAPACITY_FACTOR * s // N_EXPERTS
    rows = b * s
    l = 0

    perm = np.concatenate([np.arange(0, HEAD_DIM, 2), np.arange(1, HEAD_DIM, 2)])
    qcols = np.concatenate([h * HEAD_DIM + perm for h in range(N_HEADS)])
    kcols = aw + np.concatenate([h * HEAD_DIM + perm for h in range(N_KV_HEADS)])
    wl = w_in[l]
    w_t = jnp.concatenate([wl[:, qcols], wl[:, aw + kw:aw + 2 * kw]], axis=1).T.astype(BF16)
    w_n = jnp.concatenate([wl[:, kcols], wl[:, aw + 2 * kw:aw + 2 * kw + sw]], axis=1).astype(BF16)
    w_gates = wl[:, aw + 2 * kw + sw:].astype(BF16)
    qg3 = (jnp.tile(q_norm_g[l][perm], N_HEADS) * (HEAD_DIM ** -0.5 * math.log2(math.e))).reshape(N_HEADS, HEAD_DIM, 1)
    kg = jnp.tile(k_norm_g[l][perm], N_KV_HEADS)[None, :]
    bd = jnp.asarray(np.kron(np.eye(N_KV_HEADS), np.full((HEAD_DIM, HEAD_DIM), 1.0 / HEAD_DIM)), BF16)
    lng = ln_in_g[None, :]
    lnb = ln_in_b[None, :]
    cos, sin = _rope_tables(s)
    cosr = jnp.tile(cos, (1, 2 * N_KV_HEADS))
    sinr = jnp.tile(jnp.concatenate([-sin, sin], axis=1), (1, N_KV_HEADS))
    cost = cos.T
    sint = sin.T

    x2 = x.reshape(rows, d)
    tm_in = 512
    qt, kk, vt, u2 = _inproj(x2, cosr, sinr, cost, sint, w_t, w_n, lng, lnb, qg3, kg, bd, tm_in, s // tm_in)
    meta_p = jnp.pad(meta_tokens, ((0, META_PAD - N_META), (0, 0)))
    ones_r = jnp.ones((META_PAD, kw), F32)
    _, km, vtm, um = _inproj(meta_p, ones_r, jnp.zeros_like(ones_r), jnp.ones((half, META_PAD), F32),
                             jnp.zeros((half, META_PAD), F32), w_t, w_n, lng, lnb, qg3, kg, bd, META_PAD, 1)
    um = um[:N_META]
    km = km[:, :N_META, :]
    vtm = vtm[:, :, :N_META]

    ya = _attention(qt, kk, vt, km, vtm, b, 512, 512)

    t_chunk = SSM_CHUNK
    hgrp = SSM_GROUP
    n = t_chunk * hgrp
    arow = jnp.stack([ssm_a_re[l], ssm_a_im[l]], axis=2)
    acol = jnp.stack([ssm_a_re[l], ssm_a_im[l]], axis=3)
    ldt = ssm_log_dt[l][:, :, None, None]
    bt = jnp.stack([ssm_b_re[l], ssm_b_im[l]], axis=2)
    bt = jnp.tile(jnp.swapaxes(bt, 3, 4), (1, 1, 1, t_chunk, 1))
    ct = jnp.stack([ssm_c_re[l], ssm_c_im[l]], axis=2)
    ct = jnp.tile(jnp.swapaxes(ct, 3, 4), (1, 1, 1, 1, t_chunk))
    mm, ww, vv, at = _ssm_prep(arow, acol, ldt, bt, ct)
    nc = s // t_chunk
    u_g = u2.reshape(b, nc, t_chunk, g, hgrp).transpose(0, 3, 1, 2, 4).reshape(b, g, nc, n)
    um_g = um.reshape(N_META, g, hgrp).transpose(1, 0, 2).reshape(g, 1, N_META * hgrp)
    um_g = jnp.pad(um_g, ((0, 0), (0, 7), (n - N_META * hgrp, 0)))
    dvec = jnp.tile(ssm_d[l], (1, t_chunk))[:, None, :]
    y_g = _ssm(u_g, um_g, mm, ww, vv, at, dvec)
    ys = y_g.reshape(b, g, nc, t_chunk, hgrp).transpose(0, 2, 3, 1, 4).reshape(rows, sw)

    wr_hi = w_router[l].astype(BF16)
    wr_lo = (w_router[l] - wr_hi.astype(F32)).astype(BF16)
    wrt = jnp.stack([wr_hi.T, wr_lo.T])
    wr = jnp.pad(jnp.tile(jnp.stack([wr_hi, wr_lo]), (1, 1, 3)), ((0, 0), (0, 0), (0, GATE_LANES - 3 * N_EXPERTS)))
    h1, tok, afft = _merge(
        x2, ya, ys, w_gates, w_glu[l].astype(BF16), b_glu[l][None, :],
        w_attn_br[l].astype(BF16), w_ssm_br[l].astype(BF16), w_o[l].astype(BF16),
        lng, lnb, ln1_g[l][None, :], ln1_b[l][None, :], wr, wrt, 256)

    tri = jnp.asarray(np.triu(np.ones((ROUTE_TILE, ROUTE_TILE), np.float32)), BF16)
    scl, lo_f = _route(afft, tri, b, cap)
    nt = s // ROUTE_TILE
    lo_i = jnp.pad(lo_f.astype(jnp.int32), ((0, 0), (0, 0), (0, 1)), constant_values=cap).reshape(-1)
    ye = _ffn(lo_i, tok.reshape(b, s, d + GATE_LANES), scl.reshape(b, N_EXPERTS, nt, ROUTE_TILE),
              w_gate_e[l].astype(BF16), w_up_e[l].astype(BF16), w_down_e[l].astype(BF16), cap)
    out = _combine(lo_i, h1.reshape(b, s, d), jnp.swapaxes(scl, 1, 2), ye,
                   ln2_g[l][None, :], ln2_b[l][None, :], cap)
    return out
```

```python
import functools
import math

import numpy as np
import jax
import jax.numpy as jnp
from jax import lax
from jax.experimental import pallas as pl
from jax.experimental.pallas import tpu as pltpu

F32 = jnp.float32
BF16 = jnp.bfloat16

N_META = 16
GRID_W = 64
N_HEADS = 8
N_KV_HEADS = 2
HEAD_DIM = 64
ROPE_THETA = 10000.0
SSM_GROUP = 16
SSM_STATE = 64
N_EXPERTS = 16
CAPACITY_FACTOR = 2
LN_EPS = 1e-5
QK_EPS = 1e-6
DEPTH = 1
DEEPNORM_ALPHA = (2.0 * DEPTH) ** 0.25

SSM_CHUNK = 32
ROUTE_TILE = 256
GATE_LANES = 128
PIECE_SHIFT = 6
PIECE = 1 << PIECE_SHIFT
META_PAD = 128
V_ROWS = HEAD_DIM + 16
VMEM_LIMIT = 56 * 1024 * 1024


def _ln(x, g, b):
    mu = jnp.mean(x, axis=-1, keepdims=True)
    xc = x - mu
    var = jnp.mean(xc * xc, axis=-1, keepdims=True)
    return xc * lax.rsqrt(var + LN_EPS) * g + b


def _sigmoid(x):
    return 1.0 / (1.0 + jnp.exp(-x))


def _split(t):
    hi = t.astype(BF16)
    lo = (t - hi.astype(F32)).astype(BF16)
    return hi, lo


def _dot(a, b):
    return jnp.dot(a, b, preferred_element_type=F32)


def _dot_nt(a, b):
    return lax.dot_general(a, b, (((1,), (1,)), ((), ())), preferred_element_type=F32)


def _inproj_kernel(x_ref, cosr_ref, sinr_ref, cost_ref, sint_ref, wt_ref, wn_ref, lng_ref, lnb_ref,
                   qg_ref, kg_ref, bd_ref, qt_ref, k_ref, vt_ref, u_ref):
    aw = N_HEADS * HEAD_DIM
    kw = N_KV_HEADS * HEAD_DIM
    half = HEAD_DIM // 2
    h = _ln(x_ref[...], lng_ref[...], lnb_ref[...])
    hb = h.astype(BF16)
    pt = _dot_nt(wt_ref[...], hb)
    pn = _dot(hb, wn_ref[...])
    tm = hb.shape[0]

    qt = pt[0:aw].reshape(N_HEADS, HEAD_DIM, tm)
    ms = jnp.mean(qt * qt, axis=1, keepdims=True)
    qn = qt * lax.rsqrt(ms + QK_EPS) * qg_ref[...]
    x0 = qn[:, 0:half, :]
    x1 = qn[:, half:, :]
    c = cost_ref[...][None]
    s = sint_ref[...][None]
    qr = jnp.concatenate([x0 * c - x1 * s, x0 * s + x1 * c], axis=1)
    qt_ref[...] = qr.reshape(aw, tm).astype(BF16)
    vrow = lax.broadcasted_iota(jnp.int32, (N_KV_HEADS, V_ROWS - HEAD_DIM, tm), 1)
    vt_ref[...] = jnp.concatenate([pt[aw:aw + kw].reshape(N_KV_HEADS, HEAD_DIM, tm),
                                   jnp.where(vrow == 0, 1.0, 0.0)], axis=1).astype(BF16)

    kk = pn[:, 0:kw]
    hi, lo = _split(kk * kk)
    bd = bd_ref[...]
    msk = _dot(hi, bd) + _dot(lo, bd)
    kn = kk * lax.rsqrt(msk + QK_EPS) * kg_ref[...]
    lane = lax.broadcasted_iota(jnp.int32, kn.shape, 1)
    first = (lane & (HEAD_DIM - 1)) < half
    partner = jnp.where(first, pltpu.roll(kn, kw - half, 1), pltpu.roll(kn, half, 1))
    kr = (kn * cosr_ref[...] + partner * sinr_ref[...]).astype(BF16)
    for g in range(N_KV_HEADS):
        k_ref[g] = kr[:, g * HEAD_DIM:(g + 1) * HEAD_DIM]
    u_ref[...] = pn[:, kw:]


def _inproj(x2, cosr, sinr, cost, sint, wt, wn, lng, lnb, qg3, kg, bd, tm, n_tab_blocks):
    rows, d = x2.shape
    aw = N_HEADS * HEAD_DIM
    kw = N_KV_HEADS * HEAD_DIM
    uw = wn.shape[1] - kw
    half = HEAD_DIM // 2
    full = lambda a: pl.BlockSpec(a.shape, lambda i: (0,) * a.ndim)
    return pl.pallas_call(
        _inproj_kernel,
        grid=(rows // tm,),
        in_specs=[
            pl.BlockSpec((tm, d), lambda i: (i, 0)),
            pl.BlockSpec((tm, kw), lambda i: (i % n_tab_blocks, 0)),
            pl.BlockSpec((tm, kw), lambda i: (i % n_tab_blocks, 0)),
            pl.BlockSpec((half, tm), lambda i: (0, i % n_tab_blocks)),
            pl.BlockSpec((half, tm), lambda i: (0, i % n_tab_blocks)),
            full(wt), full(wn), full(lng), full(lnb), full(qg3), full(kg), full(bd),
        ],
        out_specs=[
            pl.BlockSpec((aw, tm), lambda i: (0, i)),
            pl.BlockSpec((N_KV_HEADS, tm, HEAD_DIM), lambda i: (0, i, 0)),
            pl.BlockSpec((N_KV_HEADS, V_ROWS, tm), lambda i: (0, 0, i)),
            pl.BlockSpec((tm, uw), lambda i: (i, 0)),
        ],
        out_shape=[
            jax.ShapeDtypeStruct((aw, rows), BF16),
            jax.ShapeDtypeStruct((N_KV_HEADS, rows, HEAD_DIM), BF16),
            jax.ShapeDtypeStruct((N_KV_HEADS, V_ROWS, rows), BF16),
            jax.ShapeDtypeStruct((rows, uw), F32),
        ],
        compiler_params=pltpu.CompilerParams(dimension_semantics=("arbitrary",),
                                             vmem_limit_bytes=VMEM_LIMIT),
        name="inproj",
    )(x2, cosr, sinr, cost, sint, wt, wn, lng, lnb, qg3, kg, bd)


def _attn_kernel(qt_ref, k0_ref, k1_ref, vta_ref, vtb_ref, km_ref, vtm_ref, o_ref, m_sc, acc_sc, s_sc, mb_sc):
    j = pl.program_id(2)
    last = pl.num_programs(2) - 1
    grp = N_HEADS // N_KV_HEADS
    heads = lambda h: slice(h * HEAD_DIM, (h + 1) * HEAD_DIM)

    def score(k_ref, slot, h):
        s = _dot(k_ref[h // grp], qt_ref[heads(h), :])
        s_sc[slot, h] = s
        mb_sc[slot, h:h + 1, :] = jnp.max(s, axis=0, keepdims=True)

    def softmax_pv(h, s, m_blk, vt_g):
        m_prev = m_sc[h:h + 1, :]
        m_new = jnp.maximum(m_prev, m_blk)
        alpha = jnp.exp2(m_prev - m_new)
        p = jnp.exp2(s - m_new).astype(BF16)
        acc_sc[h] = alpha * acc_sc[h] + _dot(vt_g, p)
        m_sc[h:h + 1, :] = m_new

    def consume(vt_ref, slot, h):
        softmax_pv(h, s_sc[slot, h], mb_sc[slot, h:h + 1, :], vt_ref[h // grp])

    @pl.when(j == 0)
    def _first():
        m_sc[...] = jnp.full(m_sc.shape, -jnp.inf, F32)
        acc_sc[...] = jnp.zeros(acc_sc.shape, F32)
        for h in range(N_HEADS):
            s = _dot(km_ref[h // grp], qt_ref[heads(h), :])
            softmax_pv(h, s, jnp.max(s, axis=0, keepdims=True), vtm_ref[h // grp])
            score(k0_ref, 0, h)
        for h in range(N_HEADS):
            score(k1_ref, 1, h)
            consume(vtb_ref, 0, h)

    @pl.when((j > 0) & (j < last))
    def _middle():
        for h in range(N_HEADS):
            score(k0_ref, 0, h)
            consume(vta_ref, 1, h)
        for h in range(N_HEADS):
            score(k1_ref, 1, h)
            consume(vtb_ref, 0, h)

    @pl.when(j == last)
    def _last():
        for h in range(N_HEADS):
            consume(vta_ref, 1, h)
        acc = acc_sc[...]
        tq = acc.shape[2]
        out_t = (acc[:, 0:HEAD_DIM, :] / acc[:, HEAD_DIM:HEAD_DIM + 1, :]).reshape(N_HEADS * HEAD_DIM, tq)
        o_ref[...] = out_t.T.astype(BF16)


def _attention(qt, kk, vt, km, vtm, nb, tq, tk):
    aw, rows = qt.shape
    s = rows // nb
    nq = s // tq
    nkb = s // tk
    kblk = lambda f: pl.BlockSpec((N_KV_HEADS, tk, HEAD_DIM),
                                  lambda bi, qi, j: (0, bi * nkb + jnp.clip(f(j), 0, nkb - 1), 0))
    vblk = lambda f: pl.BlockSpec((N_KV_HEADS, V_ROWS, tk),
                                  lambda bi, qi, j: (0, 0, bi * nkb + jnp.clip(f(j), 0, nkb - 1)))
    return pl.pallas_call(
        _attn_kernel,
        grid=(nb, nq, nkb // 2 + 1),
        in_specs=[
            pl.BlockSpec((aw, tq), lambda bi, qi, j: (0, bi * nq + qi)),
            kblk(lambda j: 2 * j), kblk(lambda j: 2 * j + 1),
            vblk(lambda j: 2 * j - 1), vblk(lambda j: 2 * j),
            pl.BlockSpec(km.shape, lambda bi, qi, j: (0, 0, 0)),
            pl.BlockSpec(vtm.shape, lambda bi, qi, j: (0, 0, 0)),
        ],
        out_specs=pl.BlockSpec((tq, aw), lambda bi, qi, j: (bi * nq + qi, 0)),
        out_shape=jax.ShapeDtypeStruct((rows, aw), BF16),
        scratch_shapes=[
            pltpu.VMEM((N_HEADS, tq), F32),
            pltpu.VMEM((N_HEADS, V_ROWS, tq), F32),
            pltpu.VMEM((2, N_HEADS, tk, tq), F32),
            pltpu.VMEM((2, N_HEADS, tq), F32),
        ],
        compiler_params=pltpu.CompilerParams(
            dimension_semantics=("arbitrary", "arbitrary", "arbitrary"),
            vmem_limit_bytes=VMEM_LIMIT),
        name="attn",
    )(qt, kk, kk, vt, vt, km, vtm)


def _ssm_prep_kernel(arow_ref, acol_ref, ldt_ref, bt_ref, ct_ref, m_ref, w_ref, v_ref, at_ref):
    t_chunk = SSM_CHUNK
    shift = int(math.log2(SSM_GROUP))
    df = pl.program_id(0).astype(F32)
    dt = jnp.exp(ldt_ref[0, 0])

    def abar(ar, ai):
        mag = jnp.exp(ar * dt)
        ang = ai * dt
        return mag * jnp.cos(ang), mag * jnp.sin(ang)

    def cpow(ar, ai, e):
        mag = jnp.exp(ar * dt * e)
        ang = ai * dt * e
        return mag * jnp.cos(ang), mag * jnp.sin(ang)

    ar_r = arow_ref[0, 0, 0:1, :]
    ai_r = arow_ref[0, 0, 1:2, :]
    abr, abi = abar(ar_r, ai_r)
    nr = abr - 1.0
    ni = abi
    den = ar_r * ar_r + ai_r * ai_r
    cr = (nr * ar_r + ni * ai_r) / den
    ci = (ni * ar_r - nr * ai_r) / den
    btr = bt_ref[0, 0, 0]
    bti = bt_ref[0, 0, 1]
    bbr = cr * btr - ci * bti
    bbi = cr * bti + ci * btr
    nrow = btr.shape[0]
    nstate = btr.shape[1]
    jr = (lax.broadcasted_iota(jnp.int32, (nrow, 1), 0) >> shift).astype(F32)
    lj = jr + df * ((t_chunk - 1) - 2.0 * jr)
    step = lax.broadcasted_iota(jnp.int32, (t_chunk, 1), 0).astype(F32)
    lstep = step + df * ((t_chunk - 1) - 2.0 * step)

    def rep_rows(x):
        return jnp.broadcast_to(x[:, None, :], (t_chunk, SSM_GROUP, nstate)).reshape(nrow, nstate)

    enr, eni = [rep_rows(x) for x in cpow(ar_r, ai_r, -lstep)]
    bmr = enr * bbr - eni * bbi
    bmi = enr * bbi + eni * bbr
    ewr, ewi = [rep_rows(x) for x in cpow(ar_r, ai_r, (t_chunk - 1) - lstep)]
    w_ref[0, 0] = jnp.concatenate([ewr * bbr - ewi * bbi, ewr * bbi + ewi * bbr], axis=1).astype(BF16)
    atr, ati = cpow(ar_r, ai_r, float(t_chunk))
    at_ref[0, 0] = jnp.concatenate([atr, ati], axis=1)

    ar_c = acol_ref[0, 0, :, 0:1]
    ai_c = acol_ref[0, 0, :, 1:2]
    ctr = ct_ref[0, 0, 0]
    cti = ct_ref[0, 0, 1]
    ncol = ctr.shape[1]
    tci = lax.broadcasted_iota(jnp.int32, (1, ncol), 1) >> shift
    tc = tci.astype(F32)
    lt = tc + df * ((t_chunk - 1) - 2.0 * tc)
    stepl = lax.broadcasted_iota(jnp.int32, (1, t_chunk), 1).astype(F32)
    lstepl = stepl + df * ((t_chunk - 1) - 2.0 * stepl)
    rep = jnp.where(lax.broadcasted_iota(jnp.int32, (t_chunk, ncol), 0) == tci, 1.0, 0.0).astype(BF16)

    def rep_cols(x):
        hi, lo = _split(x)
        lo2 = (x - hi.astype(F32) - lo.astype(F32)).astype(BF16)
        return _dot(hi, rep) + _dot(lo, rep) + _dot(lo2, rep)

    ecr, eci = [rep_cols(x) for x in cpow(ar_c, ai_c, lstepl)]
    cmr = ctr * ecr - cti * eci
    cmi = ctr * eci + cti * ecr
    lhs_hi, lhs_lo = _split(jnp.concatenate([bmr, -bmi], axis=1))
    rhs_hi, rhs_lo = _split(jnp.concatenate([cmr, cmi], axis=0))
    m = _dot(lhs_hi, rhs_hi) + _dot(lhs_hi, rhs_lo) + _dot(lhs_lo, rhs_hi)
    m_ref[0, 0] = jnp.where(lj <= lt, m, 0.0).astype(BF16)
    abr_c, abi_c = abar(ar_c, ai_c)
    c1r = cmr * abr_c - cmi * abi_c
    c1i = cmr * abi_c + cmi * abr_c
    v_ref[0, 0] = jnp.concatenate([c1r, -c1i], axis=0).astype(BF16)


def _ssm_prep(arow, acol, ldt, bt, ct):
    nd, g = arow.shape[0], arow.shape[1]
    p = SSM_STATE
    n = SSM_CHUNK * SSM_GROUP
    blk = lambda a: pl.BlockSpec((1, 1) + a.shape[2:], lambda d, gi: (d, gi) + (0,) * (a.ndim - 2))
    return pl.pallas_call(
        _ssm_prep_kernel,
        grid=(nd, g),
        in_specs=[blk(arow), blk(acol), blk(ldt), blk(bt), blk(ct)],
        out_specs=[
            pl.BlockSpec((1, 1, n, n), lambda d, gi: (d, gi, 0, 0)),
            pl.BlockSpec((1, 1, n, 2 * p), lambda d, gi: (d, gi, 0, 0)),
            pl.BlockSpec((1, 1, 2 * p, n), lambda d, gi: (d, gi, 0, 0)),
            pl.BlockSpec((1, 1, 1, 2 * p), lambda d, gi: (d, gi, 0, 0)),
        ],
        out_shape=[
            jax.ShapeDtypeStruct((nd, g, n, n), BF16),
            jax.ShapeDtypeStruct((nd, g, n, 2 * p), BF16),
            jax.ShapeDtypeStruct((nd, g, 2 * p, n), BF16),
            jax.ShapeDtypeStruct((nd, g, 1, 2 * p), F32),
        ],
        compiler_params=pltpu.CompilerParams(dimension_semantics=("arbitrary", "arbitrary")),
        name="ssm_prep",
    )(arow, acol, ldt, bt, ct)


def _chunk_carry(z, a, s0, reverse):
    p = SSM_STATE
    nc = z.shape[0]
    row = lax.broadcasted_iota(jnp.int32, z.shape, 0)
    is_re = lax.broadcasted_iota(jnp.int32, a.shape, 1) < p
    sign = jnp.where(is_re, -1.0, 1.0)

    def parts(ap):
        sw = pltpu.roll(ap, p, 1)
        return jnp.where(is_re, ap, sw), sign * jnp.where(is_re, sw, ap)

    def cmul(x, ar_full, ai_sgn):
        return x * ar_full + pltpu.roll(x, p, 1) * ai_sgn

    ar_full, ai_sgn = parts(a)
    first = (nc - 1) if reverse else 0
    e = z + jnp.where(row == first, cmul(s0, ar_full, ai_sgn), 0.0)
    k = 1
    while k < nc:
        if reverse:
            sh = jnp.where(row < nc - k, pltpu.roll(e, nc - k, 0), 0.0)
        else:
            sh = jnp.where(row >= k, pltpu.roll(e, k, 0), 0.0)
        e = e + cmul(sh, ar_full, ai_sgn)
        k *= 2
        if k < nc:
            ar_full, ai_sgn = parts(cmul(jnp.where(is_re, ar_full, sign * ai_sgn), ar_full, ai_sgn))
    if reverse:
        return jnp.where(row == first, s0, pltpu.roll(e, nc - 1, 0))
    return jnp.where(row == first, s0, pltpu.roll(e, 1, 0))


def _ssm_kernel(u_ref, um_ref, m_ref, w_ref, v_ref, at_ref, dvec_ref, y_ref):
    p = SSM_STATE
    u = u_ref[0, 0]
    ub = u.astype(BF16)
    y = dvec_ref[0] * u
    for d in range(2):
        yin = _dot(ub, m_ref[d, 0])
        z = _dot(ub, w_ref[d, 0])
        if d == 0:
            s0 = _dot(um_ref[0].astype(BF16), w_ref[0, 0])[0:1, :]
        else:
            s0 = jnp.zeros((1, 2 * p), F32)
        s_in = _chunk_carry(z, at_ref[d, 0], s0, reverse=(d == 1))
        y = y + yin + _dot(s_in.astype(BF16), v_ref[d, 0])
    y_ref[0, 0] = y


def _ssm(u2, um2, m, w, v, at, dvec):
    b, g, nc, n = u2.shape
    p = SSM_STATE
    return pl.pallas_call(
        _ssm_kernel,
        grid=(b, g),
        in_specs=[
            pl.BlockSpec((1, 1, nc, n), lambda bi, gi: (bi, gi, 0, 0)),
            pl.BlockSpec((1, 8, n), lambda bi, gi: (gi, 0, 0)),
            pl.BlockSpec((2, 1, n, n), lambda bi, gi: (0, gi, 0, 0)),
            pl.BlockSpec((2, 1, n, 2 * p), lambda bi, gi: (0, gi, 0, 0)),
            pl.BlockSpec((2, 1, 2 * p, n), lambda bi, gi: (0, gi, 0, 0)),
            pl.BlockSpec((2, 1, 1, 2 * p), lambda bi, gi: (0, gi, 0, 0)),
            pl.BlockSpec((1, 1, n), lambda bi, gi: (gi, 0, 0)),
        ],
        out_specs=pl.BlockSpec((1, 1, nc, n), lambda bi, gi: (bi, gi, 0, 0)),
        out_shape=jax.ShapeDtypeStruct((b, g, nc, n), F32),
        compiler_params=pltpu.CompilerParams(dimension_semantics=("arbitrary", "arbitrary")),
        name="ssm",
    )(u2, um2, m, w, v, at, dvec)


def _merge_kernel(x_ref, ya_ref, ys_ref, wgt_ref, wglu_ref, bglu_ref, wab_ref, wsb_ref, wo_ref,
                  lng_ref, lnb_ref, l1g_ref, l1b_ref, wr_ref, wrt_ref,
                  h1_ref, tok_ref, afft_ref):
    d = x_ref.shape[1]
    h = _ln(x_ref[...], lng_ref[...], lnb_ref[...])
    gates = _dot(h.astype(BF16), wgt_ref[...])
    ga = _sigmoid(gates[:, 0:d])
    gs = _sigmoid(gates[:, d:2 * d])
    ys = ys_ref[...]
    y = 0.5 * ys * (1.0 + jnp.tanh(math.sqrt(2.0 / math.pi) * (ys + 0.044715 * (ys * ys * ys))))
    yg = y * _sigmoid(_dot(y.astype(BF16), wglu_ref[...]) + bglu_ref[...])
    merged = ga * _dot(ya_ref[...], wab_ref[...]) + gs * _dot(yg.astype(BF16), wsb_ref[...])
    h1 = _ln(DEEPNORM_ALPHA * h + _dot(merged.astype(BF16), wo_ref[...]), l1g_ref[...], l1b_ref[...])
    h1_ref[...] = h1
    hi, lo = _split(h1)
    ne = afft_ref.shape[0]
    logits = _dot(hi, wr_ref[0]) + _dot(hi, wr_ref[1]) + _dot(lo, wr_ref[0])
    lane = lax.broadcasted_iota(jnp.int32, logits.shape, 1)
    logits = jnp.where(lane < 3 * ne, logits, -jnp.inf)
    ex = jnp.exp(logits - jnp.max(logits, axis=1, keepdims=True))
    aff = ex / jnp.sum(jnp.where(lane < ne, ex, 0.0), axis=1, keepdims=True)
    p0 = aff.astype(BF16)
    r1 = aff - p0.astype(F32)
    p1 = r1.astype(BF16)
    p2 = (r1 - p1.astype(F32)).astype(BF16)
    pieces = jnp.where(lane < ne, p0, jnp.where(lane < 2 * ne, p1, p2))
    tok_ref[...] = jnp.concatenate([hi, pieces], axis=1)
    lt = _dot_nt(wrt_ref[0], hi) + _dot_nt(wrt_ref[1], hi) + _dot_nt(wrt_ref[0], lo)
    et = jnp.exp(lt - jnp.max(lt, axis=0, keepdims=True))
    afft_ref[...] = et / jnp.sum(et, axis=0, keepdims=True)


def _merge(x2, ya, ys, wgt, wglu, bglu, wab, wsb, wo, lng, lnb, l1g, l1b, wr, wrt, tm):
    rows, d = x2.shape
    ne = wrt.shape[1]
    full = lambda a: pl.BlockSpec(a.shape, lambda i: (0,) * a.ndim)
    return pl.pallas_call(
        _merge_kernel,
        grid=(rows // tm,),
        in_specs=[
            pl.BlockSpec((tm, d), lambda i: (i, 0)),
            pl.BlockSpec((tm, ya.shape[1]), lambda i: (i, 0)),
            pl.BlockSpec((tm, ys.shape[1]), lambda i: (i, 0)),
            full(wgt), full(wglu), full(bglu), full(wab), full(wsb), full(wo),
            full(lng), full(lnb), full(l1g), full(l1b), full(wr), full(wrt),
        ],
        out_specs=[
            pl.BlockSpec((tm, d), lambda i: (i, 0)),
            pl.BlockSpec((tm, d + GATE_LANES), lambda i: (i, 0)),
            pl.BlockSpec((ne, tm), lambda i: (0, i)),
        ],
        out_shape=[
            jax.ShapeDtypeStruct((rows, d), F32),
            jax.ShapeDtypeStruct((rows, d + GATE_LANES), BF16),
            jax.ShapeDtypeStruct((ne, rows), F32),
        ],
        compiler_params=pltpu.CompilerParams(dimension_semantics=("arbitrary",),
                                             vmem_limit_bytes=VMEM_LIMIT),
        name="merge",
    )(x2, ya, ys, wgt, wglu, bglu, wab, wsb, wo, lng, lnb, l1g, l1b, wr, wrt)


def _route_kernel(afft_ref, tri_ref, scl_ref, lo_ref, *, cap):
    aff = afft_ref[...]
    ne, s = aff.shape
    capf = float(cap)

    def as_float(bits):
        return lax.bitcast_convert_type(bits, F32)

    def search(i, t):
        cand = t | jnp.left_shift(jnp.int32(1), 30 - i)
        cnt = jnp.sum(jnp.where(aff >= as_float(cand), 1.0, 0.0), axis=1, keepdims=True)
        return jnp.where(cnt >= capf, cand, t)

    thr_bits = lax.fori_loop(0, 31, search, jnp.zeros((ne, 1), jnp.int32))
    gt = aff >= as_float(thr_bits + 1)
    eq = (aff >= as_float(thr_bits)) & jnp.logical_not(gt)
    need = capf - jnp.sum(jnp.where(gt, 1.0, 0.0), axis=1, keepdims=True)
    tri = tri_ref[...]
    rt = ROUTE_TILE
    nt = s // rt
    col = lax.broadcasted_iota(jnp.int32, (ne, nt), 1)
    carry_eq = jnp.zeros((ne, 1), F32)
    carry_sel = jnp.zeros((ne, 1), F32)
    lo_val = jnp.zeros((ne, nt), F32)
    for t in range(nt):
        sl = slice(t * rt, (t + 1) * rt)
        eq_b = eq[:, sl]
        ceq = _dot(jnp.where(eq_b, 1.0, 0.0).astype(BF16), tri)
        sel_b = gt[:, sl] | (eq_b & ((ceq + carry_eq) <= need))
        carry_eq = carry_eq + ceq[:, rt - 1:rt]
        csel = _dot(jnp.where(sel_b, 1.0, 0.0).astype(BF16), tri)
        scl_ref[0, :, sl] = jnp.where(sel_b, csel, 0.0)
        lo_val = jnp.where(col == t, carry_sel, lo_val)
        carry_sel = carry_sel + csel[:, rt - 1:rt]
    lo_ref[0] = lo_val


def _route(afft, tri, nb, cap):
    ne, rows = afft.shape
    s = rows // nb
    nt = s // ROUTE_TILE
    return pl.pallas_call(
        functools.partial(_route_kernel, cap=cap),
        grid=(nb,),
        in_specs=[
            pl.BlockSpec((ne, s), lambda b: (0, b)),
            pl.BlockSpec(tri.shape, lambda b: (0, 0)),
        ],
        out_specs=[
            pl.BlockSpec((1, ne, s), lambda b: (b, 0, 0)),
            pl.BlockSpec((1, ne, nt), lambda b: (b, 0, 0)),
        ],
        out_shape=[
            jax.ShapeDtypeStruct((nb, ne, s), F32),
            jax.ShapeDtypeStruct((nb, ne, nt), F32),
        ],
        compiler_params=pltpu.CompilerParams(dimension_semantics=("arbitrary",)),
        name="route",
    )(afft, tri)


def _ffn_kernel(lo_ref, t_ref, scl_ref, wg_ref, wu_ref, wd_ref, ye_ref, xe_sc, xb_sc, gate_sc, acc_sc, *, cap):
    fc = pl.program_id(2)

    @pl.when(fc == 0)
    def _gather():
        _ffn_gather(lo_ref, t_ref, scl_ref, xe_sc, xb_sc, gate_sc, cap=cap)
        acc_sc[...] = jnp.zeros(acc_sc.shape, F32)

    xb = xb_sc[...]
    g = _dot(xb, wg_ref[0].astype(BF16))
    u = _dot(xb, wu_ref[0].astype(BF16))
    hh = (g * _sigmoid(g)) * u
    acc_sc[...] += _dot(hh.astype(BF16), wd_ref[0].astype(BF16))

    @pl.when(fc == pl.num_programs(2) - 1)
    def _emit():
        ye_ref[0, 0] = (acc_sc[...] * gate_sc[...]).astype(BF16)


def _ffn_gather(lo_ref, t_ref, scl_ref, xe_sc, xb_sc, gate_sc, *, cap):
    b = pl.program_id(0)
    e = pl.program_id(1)
    ne = pl.num_programs(1)
    rt = ROUTE_TILE
    nt = t_ref.shape[1] // rt
    xe_sc[...] = jnp.zeros(xe_sc.shape, F32)
    r = lax.broadcasted_iota(jnp.int32, (PIECE, rt), 0).astype(F32)

    def tile_info(tau):
        base = (b * ne + e) * (nt + 1) + tau
        lo = lo_ref[base]
        n_pieces = jnp.right_shift((lo & 7) + (lo_ref[base + 1] - lo) + (PIECE - 1), PIECE_SHIFT)
        return lo, n_pieces

    def add_piece(tau, lo, p):
        off = lo & 7
        scl = scl_ref[0, 0, pl.ds(tau, 1), :]
        tok = t_ref[0, pl.ds(pl.multiple_of(tau * rt, rt), rt), :]
        shift = (off - 1 - p * PIECE).astype(F32)
        onehot = jnp.where((scl > 0.0) & ((scl + shift) == r), 1.0, 0.0).astype(BF16)
        xe_sc[pl.ds(pl.multiple_of(lo - off + p * PIECE, 8), PIECE), :] += _dot(onehot, tok)

    def first_piece(tau, carry):
        add_piece(tau, tile_info(tau)[0], 0)
        return carry

    def more_pieces(tau, carry):
        lo, n_pieces = tile_info(tau)
        lax.fori_loop(1, n_pieces, lambda p, c: (add_piece(tau, lo, p), c)[1], 0)
        return carry

    lax.fori_loop(0, nt, first_piece, 0, unroll=4)
    lax.fori_loop(0, nt, more_pieces, 0)
    d = xb_sc.shape[1]
    xb_sc[...] = xe_sc[0:cap, 0:d].astype(BF16)
    gl = xe_sc[0:cap, d:d + GATE_LANES]
    lane = lax.broadcasted_iota(jnp.int32, gl.shape, 1)
    mine = ((lane & (N_EXPERTS - 1)) == e) & (lane < 3 * N_EXPERTS)
    gate_sc[...] = jnp.sum(jnp.where(mine, gl, 0.0), axis=1, keepdims=True)


def _ffn(lo_i, t, scl4, wg, wu, wd, cap, f_chunk):
    b, s, dx = t.shape
    ne, d, f = wg.shape
    nt = s // ROUTE_TILE
    grid_spec = pltpu.PrefetchScalarGridSpec(
        num_scalar_prefetch=1,
        grid=(b, ne, f // f_chunk),
        in_specs=[
            pl.BlockSpec((1, s, dx), lambda bi, ei, fi, lo: (bi, 0, 0), pipeline_mode=pl.Buffered(1)),
            pl.BlockSpec((1, 1, nt, ROUTE_TILE), lambda bi, ei, fi, lo: (bi, ei, 0, 0)),
            pl.BlockSpec((1, d, f_chunk), lambda bi, ei, fi, lo: (ei, 0, fi)),
            pl.BlockSpec((1, d, f_chunk), lambda bi, ei, fi, lo: (ei, 0, fi)),
            pl.BlockSpec((1, f_chunk, d), lambda bi, ei, fi, lo: (ei, fi, 0)),
        ],
        out_specs=pl.BlockSpec((1, 1, cap, d), lambda bi, ei, fi, lo: (bi, ei, 0, 0)),
        scratch_shapes=[pltpu.VMEM((cap + PIECE, dx), F32), pltpu.VMEM((cap, d), BF16),
                        pltpu.VMEM((cap, 1), F32), pltpu.VMEM((cap, d), F32)],
    )
    return pl.pallas_call(
        functools.partial(_ffn_kernel, cap=cap),
        grid_spec=grid_spec,
        out_shape=jax.ShapeDtypeStruct((b, ne, cap, d), BF16),
        compiler_params=pltpu.CompilerParams(dimension_semantics=("arbitrary", "arbitrary", "arbitrary"),
                                             vmem_limit_bytes=VMEM_LIMIT),
        name="ffn",
    )(lo_i, t, scl4, wg, wu, wd)


def _combine_kernel(lo_ref, h1_ref, sclt_ref, ye_hbm, g_ref, b_ref, o_ref, win, xwin, acc_sc, sem, xsem, *, cap):
    b = pl.program_id(0)
    tau = pl.program_id(1)
    nb = pl.num_programs(0)
    nt = pl.num_programs(1)
    ne = win.shape[1]
    d = win.shape[3]
    step = b * nt + tau
    slot = step & 1

    def pieces(bi, ti, e):
        base = (bi * ne + e) * (nt + 1) + ti
        lo = lo_ref[base]
        n_sel = lo_ref[base + 1] - lo
        n_pieces = jnp.maximum(jnp.right_shift((lo & 15) + n_sel + (PIECE - 1), PIECE_SHIFT), 1)
        first = jnp.minimum(lo - (lo & 15), cap - n_pieces * PIECE)
        return lo, first, n_pieces

    def copy(bi, e, first, p, buf, s):
        start = pl.multiple_of(first + p * PIECE, 16)
        return pltpu.make_async_copy(ye_hbm.at[bi, e, pl.ds(start, PIECE), :], buf, s)

    def first_pieces(bi, ti, sl):
        return [copy(bi, e, pieces(bi, ti, e)[1], 0, win.at[sl, e], sem.at[sl, e]) for e in range(ne)]

    @pl.when(step == 0)
    def _prime():
        for cp in first_pieces(b, tau, slot):
            cp.start()

    @pl.when(step + 1 < nb * nt)
    def _prefetch():
        wrap = tau + 1 == nt
        for cp in first_pieces(jnp.where(wrap, b + 1, b), jnp.where(wrap, 0, tau + 1), 1 - slot):
            cp.start()

    for cp in first_pieces(b, tau, slot):
        cp.wait()

    sclt = sclt_ref[0]
    col = lax.broadcasted_iota(jnp.int32, (1, ne * PIECE), 1)
    grp = jnp.right_shift(col, PIECE_SHIFT)
    expand = jnp.where(lax.broadcasted_iota(jnp.int32, (ne, ne * PIECE), 0) == grp, 1.0, 0.0).astype(BF16)
    scl = _dot(sclt.astype(BF16), expand)
    shift = jnp.zeros(col.shape, F32)
    for e in range(ne):
        lo, first, _ = pieces(b, tau, e)
        shift = jnp.where(grp == e, (lo - first - 1).astype(F32), shift)
    r = (col & (PIECE - 1)).astype(F32)
    onehot = jnp.where((scl > 0.0) & ((scl + shift) == r), 1.0, 0.0).astype(BF16)
    acc_sc[...] = DEEPNORM_ALPHA * h1_ref[0] + _dot(onehot, win[slot].reshape(ne * PIECE, d))

    for e in range(ne):
        lo, first, n_pieces = pieces(b, tau, e)

        def extra(p, c, e=e, lo=lo, first=first):
            cp = copy(b, e, first, p, xwin, xsem.at[0])
            cp.start()
            cp.wait()
            se = sclt_ref[0][:, e:e + 1]
            rr = lax.broadcasted_iota(jnp.int32, (se.shape[0], PIECE), 1).astype(F32)
            oh = jnp.where((se > 0.0) & ((se + (lo - first - p * PIECE - 1).astype(F32)) == rr), 1.0, 0.0)
            acc_sc[...] += _dot(oh.astype(BF16), xwin[...])
            return c

        lax.fori_loop(1, n_pieces, extra, 0)
    o_ref[0] = _ln(acc_sc[...], g_ref[...], b_ref[...])


def _combine(lo_i, h1, sclt, ye, g, bb, cap):
    b, s, d = h1.shape
    ne = sclt.shape[2]
    rt = ROUTE_TILE
    grid_spec = pltpu.PrefetchScalarGridSpec(
        num_scalar_prefetch=1,
        grid=(b, s // rt),
        in_specs=[
            pl.BlockSpec((1, rt, d), lambda bi, ti, lo: (bi, ti, 0)),
            pl.BlockSpec((1, rt, ne), lambda bi, ti, lo: (bi, ti, 0)),
            pl.BlockSpec(memory_space=pl.ANY),
            pl.BlockSpec(g.shape, lambda bi, ti, lo: (0, 0)),
            pl.BlockSpec(bb.shape, lambda bi, ti, lo: (0, 0)),
        ],
        out_specs=pl.BlockSpec((1, rt, d), lambda bi, ti, lo: (bi, ti, 0)),
        scratch_shapes=[pltpu.VMEM((2, ne, PIECE, d), BF16), pltpu.VMEM((PIECE, d), BF16),
                        pltpu.VMEM((rt, d), F32),
                        pltpu.SemaphoreType.DMA((2, ne)), pltpu.SemaphoreType.DMA((1,))],
    )
    return pl.pallas_call(
        functools.partial(_combine_kernel, cap=cap),
        grid_spec=grid_spec,
        out_shape=jax.ShapeDtypeStruct((b, s, d), F32),
        compiler_params=pltpu.CompilerParams(dimension_semantics=("arbitrary", "arbitrary"),
                                             vmem_limit_bytes=VMEM_LIMIT),
        name="combine",
    )(lo_i, h1, sclt, ye, g, bb)


def _rope_tables(n_tokens):
    half = HEAD_DIM // 2
    inv_freq = ROPE_THETA ** (-jnp.arange(0, half, 2, dtype=F32) / half)
    rows = n_tokens // GRID_W
    row = jnp.repeat(jnp.arange(rows, dtype=F32), GRID_W)
    colv = jnp.tile(jnp.arange(GRID_W, dtype=F32), rows)
    ang = jnp.concatenate([row[:, None] * inv_freq, colv[:, None] * inv_freq], axis=-1)
    return jnp.cos(ang), jnp.sin(ang)


def kernel(x, meta_tokens, ln_in_g, ln_in_b, w_in, q_norm_g, k_norm_g, ssm_a_re, ssm_a_im, ssm_log_dt,
           ssm_b_re, ssm_b_im, ssm_c_re, ssm_c_im, ssm_d, w_glu, b_glu, w_attn_br, w_ssm_br, w_o,
           ln1_g, ln1_b, w_router, w_gate_e, w_up_e, w_down_e, ln2_g, ln2_b):
    b, s, d = x.shape
    aw = N_HEADS * HEAD_DIM
    kw = N_KV_HEADS * HEAD_DIM
    sw = d // 2
    g = sw // SSM_GROUP
    half = HEAD_DIM // 2
    cap = CAPACITY_FACTOR * s // N_EXPERTS
    rows = b * s
    l = 0

    perm = np.concatenate([np.arange(0, HEAD_DIM, 2), np.arange(1, HEAD_DIM, 2)])
    qcols = np.concatenate([h * HEAD_DIM + perm for h in range(N_HEADS)])
    kcols = aw + np.concatenate([h * HEAD_DIM + perm for h in range(N_KV_HEADS)])
    wl = w_in[l]
    w_t = jnp.concatenate([wl[:, qcols], wl[:, aw + kw:aw + 2 * kw]], axis=1).T.astype(BF16)
    w_n = jnp.concatenate([wl[:, kcols], wl[:, aw + 2 * kw:aw + 2 * kw + sw]], axis=1).astype(BF16)
    w_gates = wl[:, aw + 2 * kw + sw:].astype(BF16)
    qg3 = (jnp.tile(q_norm_g[l][perm], N_HEADS) * (HEAD_DIM ** -0.5 * math.log2(math.e))).reshape(N_HEADS, HEAD_DIM, 1)
    kg = jnp.tile(k_norm_g[l][perm], N_KV_HEADS)[None, :]
    bd = jnp.asarray(np.kron(np.eye(N_KV_HEADS), np.full((HEAD_DIM, HEAD_DIM), 1.0 / HEAD_DIM)), BF16)
    lng = ln_in_g[None, :]
    lnb = ln_in_b[None, :]
    cos, sin = _rope_tables(s)
    cosr = jnp.tile(cos, (1, 2 * N_KV_HEADS))
    sinr = jnp.tile(jnp.concatenate([-sin, sin], axis=1), (1, N_KV_HEADS))
    cost = cos.T
    sint = sin.T

    x2 = x.reshape(rows, d)
    tm_in = 512
    qt, kk, vt, u2 = _inproj(x2, cosr, sinr, cost, sint, w_t, w_n, lng, lnb, qg3, kg, bd, tm_in, s // tm_in)
    meta_p = jnp.pad(meta_tokens, ((0, META_PAD - N_META), (0, 0)))
    ones_r = jnp.ones((META_PAD, kw), F32)
    _, km, vtm, um = _inproj(meta_p, ones_r, jnp.zeros_like(ones_r), jnp.ones((half, META_PAD), F32),
                             jnp.zeros((half, META_PAD), F32), w_t, w_n, lng, lnb, qg3, kg, bd, META_PAD, 1)
    um = um[:N_META]
    km = km[:, :N_META, :]
    vtm = vtm[:, :, :N_META]

    ya = _attention(qt, kk, vt, km, vtm, b, 512, 512)

    t_chunk = SSM_CHUNK
    hgrp = SSM_GROUP
    n = t_chunk * hgrp
    arow = jnp.stack([ssm_a_re[l], ssm_a_im[l]], axis=2)
    acol = jnp.stack([ssm_a_re[l], ssm_a_im[l]], axis=3)
    ldt = ssm_log_dt[l][:, :, None, None]
    bt = jnp.stack([ssm_b_re[l], ssm_b_im[l]], axis=2)
    bt = jnp.tile(jnp.swapaxes(bt, 3, 4), (1, 1, 1, t_chunk, 1))
    ct = jnp.stack([ssm_c_re[l], ssm_c_im[l]], axis=2)
    ct = jnp.tile(jnp.swapaxes(ct, 3, 4), (1, 1, 1, 1, t_chunk))
    mm, ww, vv, at = _ssm_prep(arow, acol, ldt, bt, ct)
    nc = s // t_chunk
    u_g = u2.reshape(b, nc, t_chunk, g, hgrp).transpose(0, 3, 1, 2, 4).reshape(b, g, nc, n)
    um_g = um.reshape(N_META, g, hgrp).transpose(1, 0, 2).reshape(g, 1, N_META * hgrp)
    um_g = jnp.pad(um_g, ((0, 0), (0, 7), (n - N_META * hgrp, 0)))
    dvec = jnp.tile(ssm_d[l], (1, t_chunk))[:, None, :]
    y_g = _ssm(u_g, um_g, mm, ww, vv, at, dvec)
    ys = y_g.reshape(b, g, nc, t_chunk, hgrp).transpose(0, 2, 3, 1, 4).reshape(rows, sw)

    wr_hi = w_router[l].astype(BF16)
    wr_lo = (w_router[l] - wr_hi.astype(F32)).astype(BF16)
    wrt = jnp.stack([wr_hi.T, wr_lo.T])
    wr = jnp.pad(jnp.tile(jnp.stack([wr_hi, wr_lo]), (1, 1, 3)), ((0, 0), (0, 0), (0, GATE_LANES - 3 * N_EXPERTS)))
    h1, tok, afft = _merge(
        x2, ya, ys, w_gates, w_glu[l].astype(BF16), b_glu[l][None, :],
        w_attn_br[l].astype(BF16), w_ssm_br[l].astype(BF16), w_o[l].astype(BF16),
        lng, lnb, ln1_g[l][None, :], ln1_b[l][None, :], wr, wrt, 256)

    tri = jnp.asarray(np.triu(np.ones((ROUTE_TILE, ROUTE_TILE), np.float32)), BF16)
    scl, lo_f = _route(afft, tri, b, cap)
    nt = s // ROUTE_TILE
    lo_i = jnp.pad(lo_f.astype(jnp.int32), ((0, 0), (0, 0), (0, 1)), constant_values=cap).reshape(-1)
    ye = _ffn(lo_i, tok.reshape(b, s, d + GATE_LANES), scl.reshape(b, N_EXPERTS, nt, ROUTE_TILE),
              w_gate_e[l], w_up_e[l], w_down_e[l], cap, 512)
    out = _combine(lo_i, h1.reshape(b, s, d), jnp.swapaxes(scl, 1, 2), ye,
                   ln2_g[l][None, :], ln2_b[l][None, :], cap)
    return out
```

```python
import functools
import math

import numpy as np
import jax
import jax.numpy as jnp
from jax import lax
from jax.experimental import pallas as pl
from jax.experimental.pallas import tpu as pltpu

F32 = jnp.float32
BF16 = jnp.bfloat16

N_META = 16
GRID_W = 64
N_HEADS = 8
N_KV_HEADS = 2
HEAD_DIM = 64
ROPE_THETA = 10000.0
SSM_GROUP = 16
SSM_STATE = 64
N_EXPERTS = 16
CAPACITY_FACTOR = 2
LN_EPS = 1e-5
QK_EPS = 1e-6
DEPTH = 1
DEEPNORM_ALPHA = (2.0 * DEPTH) ** 0.25

SSM_CHUNK = 32
SSM_GROUPS_PER_STEP = 2
ROUTE_TILE = 256
GATE_LANES = 128
PIECE_SHIFT = 6
PIECE = 1 << PIECE_SHIFT
META_PAD = 128
V_ROWS = HEAD_DIM + 16
VMEM_LIMIT = 56 * 1024 * 1024


def _ln(x, g, b):
    mu = jnp.mean(x, axis=-1, keepdims=True)
    xc = x - mu
    var = jnp.mean(xc * xc, axis=-1, keepdims=True)
    return xc * lax.rsqrt(var + LN_EPS) * g + b


def _sigmoid(x):
    return 1.0 / (1.0 + jnp.exp(-x))


def _split(t):
    hi = t.astype(BF16)
    lo = (t - hi.astype(F32)).astype(BF16)
    return hi, lo


def _dot(a, b):
    return jnp.dot(a, b, preferred_element_type=F32)


def _dot_nt(a, b):
    return lax.dot_general(a, b, (((1,), (1,)), ((), ())), preferred_element_type=F32)


def _inproj_kernel(x_ref, cosr_ref, sinr_ref, cost_ref, sint_ref, wt_ref, wn_ref, lng_ref, lnb_ref,
                   qg_ref, kg_ref, bd_ref, qt_ref, k_ref, vt_ref, u_ref):
    aw = N_HEADS * HEAD_DIM
    kw = N_KV_HEADS * HEAD_DIM
    half = HEAD_DIM // 2
    h = _ln(x_ref[...], lng_ref[...], lnb_ref[...])
    hb = h.astype(BF16)
    pt = _dot_nt(wt_ref[...], hb)
    pn = _dot(hb, wn_ref[...])
    tm = hb.shape[0]

    qt = pt[0:aw].reshape(N_HEADS, HEAD_DIM, tm)
    ms = jnp.mean(qt * qt, axis=1, keepdims=True)
    qn = qt * lax.rsqrt(ms + QK_EPS) * qg_ref[...]
    x0 = qn[:, 0:half, :]
    x1 = qn[:, half:, :]
    c = cost_ref[...][None]
    s = sint_ref[...][None]
    qr = jnp.concatenate([x0 * c - x1 * s, x0 * s + x1 * c], axis=1)
    qt_ref[...] = qr.reshape(aw, tm).astype(BF16)
    vrow = lax.broadcasted_iota(jnp.int32, (N_KV_HEADS, V_ROWS - HEAD_DIM, tm), 1)
    vt_ref[...] = jnp.concatenate([pt[aw:aw + kw].reshape(N_KV_HEADS, HEAD_DIM, tm),
                                   jnp.where(vrow == 0, 1.0, 0.0)], axis=1).astype(BF16)

    kk = pn[:, 0:kw]
    hi, lo = _split(kk * kk)
    bd = bd_ref[...]
    msk = _dot(hi, bd) + _dot(lo, bd)
    kn = kk * lax.rsqrt(msk + QK_EPS) * kg_ref[...]
    lane = lax.broadcasted_iota(jnp.int32, kn.shape, 1)
    first = (lane & (HEAD_DIM - 1)) < half
    partner = jnp.where(first, pltpu.roll(kn, kw - half, 1), pltpu.roll(kn, half, 1))
    kr = (kn * cosr_ref[...] + partner * sinr_ref[...]).astype(BF16)
    for g in range(N_KV_HEADS):
        k_ref[g] = kr[:, g * HEAD_DIM:(g + 1) * HEAD_DIM]
    u_ref[...] = pn[:, kw:]


def _inproj(x2, cosr, sinr, cost, sint, wt, wn, lng, lnb, qg3, kg, bd, tm, n_tab_blocks):
    rows, d = x2.shape
    aw = N_HEADS * HEAD_DIM
    kw = N_KV_HEADS * HEAD_DIM
    uw = wn.shape[1] - kw
    half = HEAD_DIM // 2
    full = lambda a: pl.BlockSpec(a.shape, lambda i: (0,) * a.ndim)
    return pl.pallas_call(
        _inproj_kernel,
        grid=(rows // tm,),
        in_specs=[
            pl.BlockSpec((tm, d), lambda i: (i, 0)),
            pl.BlockSpec((tm, kw), lambda i: (i % n_tab_blocks, 0)),
            pl.BlockSpec((tm, kw), lambda i: (i % n_tab_blocks, 0)),
            pl.BlockSpec((half, tm), lambda i: (0, i % n_tab_blocks)),
            pl.BlockSpec((half, tm), lambda i: (0, i % n_tab_blocks)),
            full(wt), full(wn), full(lng), full(lnb), full(qg3), full(kg), full(bd),
        ],
        out_specs=[
            pl.BlockSpec((aw, tm), lambda i: (0, i)),
            pl.BlockSpec((N_KV_HEADS, tm, HEAD_DIM), lambda i: (0, i, 0)),
            pl.BlockSpec((N_KV_HEADS, V_ROWS, tm), lambda i: (0, 0, i)),
            pl.BlockSpec((tm, uw), lambda i: (i, 0)),
        ],
        out_shape=[
            jax.ShapeDtypeStruct((aw, rows), BF16),
            jax.ShapeDtypeStruct((N_KV_HEADS, rows, HEAD_DIM), BF16),
            jax.ShapeDtypeStruct((N_KV_HEADS, V_ROWS, rows), BF16),
            jax.ShapeDtypeStruct((rows, uw), F32),
        ],
        compiler_params=pltpu.CompilerParams(dimension_semantics=("arbitrary",),
                                             vmem_limit_bytes=VMEM_LIMIT),
        name="inproj",
    )(x2, cosr, sinr, cost, sint, wt, wn, lng, lnb, qg3, kg, bd)


def _attn_kernel(qt_ref, k0_ref, k1_ref, vta_ref, vtb_ref, km_ref, vtm_ref, o_ref, m_sc, acc_sc, s_sc, mb_sc):
    j = pl.program_id(2)
    last = pl.num_programs(2) - 1
    grp = N_HEADS // N_KV_HEADS
    heads = lambda h: slice(h * HEAD_DIM, (h + 1) * HEAD_DIM)

    def score(k_ref, slot, h):
        s = _dot(k_ref[h // grp], qt_ref[heads(h), :])
        s_sc[slot, h] = s
        mb_sc[slot, h:h + 1, :] = jnp.max(s, axis=0, keepdims=True)

    def softmax_pv(h, s, m_blk, vt_g):
        m_prev = m_sc[h:h + 1, :]
        m_new = jnp.maximum(m_prev, m_blk)
        alpha = jnp.exp2(m_prev - m_new)
        p = jnp.exp2(s - m_new).astype(BF16)
        acc_sc[h] = alpha * acc_sc[h] + _dot(vt_g, p)
        m_sc[h:h + 1, :] = m_new

    def consume(vt_ref, slot, h):
        softmax_pv(h, s_sc[slot, h], mb_sc[slot, h:h + 1, :], vt_ref[h // grp])

    @pl.when(j == 0)
    def _first():
        m_sc[...] = jnp.full(m_sc.shape, -jnp.inf, F32)
        acc_sc[...] = jnp.zeros(acc_sc.shape, F32)
        for h in range(N_HEADS):
            s = _dot(km_ref[h // grp], qt_ref[heads(h), :])
            softmax_pv(h, s, jnp.max(s, axis=0, keepdims=True), vtm_ref[h // grp])
            score(k0_ref, 0, h)
        for h in range(N_HEADS):
            score(k1_ref, 1, h)
            consume(vtb_ref, 0, h)

    @pl.when((j > 0) & (j < last))
    def _middle():
        for h in range(N_HEADS):
            score(k0_ref, 0, h)
            consume(vta_ref, 1, h)
        for h in range(N_HEADS):
            score(k1_ref, 1, h)
            consume(vtb_ref, 0, h)

    @pl.when(j == last)
    def _last():
        for h in range(N_HEADS):
            consume(vta_ref, 1, h)
        acc = acc_sc[...]
        tq = acc.shape[2]
        out_t = (acc[:, 0:HEAD_DIM, :] / acc[:, HEAD_DIM:HEAD_DIM + 1, :]).reshape(N_HEADS * HEAD_DIM, tq)
        o_ref[...] = out_t.T.astype(BF16)


def _attention(qt, kk, vt, km, vtm, nb, tq, tk):
    aw, rows = qt.shape
    s = rows // nb
    nq = s // tq
    nkb = s // tk
    kblk = lambda f: pl.BlockSpec((N_KV_HEADS, tk, HEAD_DIM),
                                  lambda bi, qi, j: (0, bi * nkb + jnp.clip(f(j), 0, nkb - 1), 0))
    vblk = lambda f: pl.BlockSpec((N_KV_HEADS, V_ROWS, tk),
                                  lambda bi, qi, j: (0, 0, bi * nkb + jnp.clip(f(j), 0, nkb - 1)))
    return pl.pallas_call(
        _attn_kernel,
        grid=(nb, nq, nkb // 2 + 1),
        in_specs=[
            pl.BlockSpec((aw, tq), lambda bi, qi, j: (0, bi * nq + qi)),
            kblk(lambda j: 2 * j), kblk(lambda j: 2 * j + 1),
            vblk(lambda j: 2 * j - 1), vblk(lambda j: 2 * j),
            pl.BlockSpec(km.shape, lambda bi, qi, j: (0, 0, 0)),
            pl.BlockSpec(vtm.shape, lambda bi, qi, j: (0, 0, 0)),
        ],
        out_specs=pl.BlockSpec((tq, aw), lambda bi, qi, j: (bi * nq + qi, 0)),
        out_shape=jax.ShapeDtypeStruct((rows, aw), BF16),
        scratch_shapes=[
            pltpu.VMEM((N_HEADS, tq), F32),
            pltpu.VMEM((N_HEADS, V_ROWS, tq), F32),
            pltpu.VMEM((2, N_HEADS, tk, tq), F32),
            pltpu.VMEM((2, N_HEADS, tq), F32),
        ],
        compiler_params=pltpu.CompilerParams(
            dimension_semantics=("arbitrary", "arbitrary", "arbitrary"),
            vmem_limit_bytes=VMEM_LIMIT),
        name="attn",
    )(qt, kk, kk, vt, vt, km, vtm)


def _ssm_prep_kernel(arow_ref, acol_ref, ldt_ref, bt_ref, ct_ref, m_ref, w_ref, v_ref, at_ref):
    t_chunk = SSM_CHUNK
    shift = int(math.log2(SSM_GROUP))
    df = pl.program_id(0).astype(F32)
    dt = jnp.exp(ldt_ref[0, 0])

    def abar(ar, ai):
        mag = jnp.exp(ar * dt)
        ang = ai * dt
        return mag * jnp.cos(ang), mag * jnp.sin(ang)

    def cpow(ar, ai, e):
        mag = jnp.exp(ar * dt * e)
        ang = ai * dt * e
        return mag * jnp.cos(ang), mag * jnp.sin(ang)

    ar_r = arow_ref[0, 0, 0:1, :]
    ai_r = arow_ref[0, 0, 1:2, :]
    abr, abi = abar(ar_r, ai_r)
    nr = abr - 1.0
    ni = abi
    den = ar_r * ar_r + ai_r * ai_r
    cr = (nr * ar_r + ni * ai_r) / den
    ci = (ni * ar_r - nr * ai_r) / den
    nstate = bt_ref.shape[4]
    nrow = t_chunk * SSM_GROUP

    def tile_rows(x):
        return jnp.broadcast_to(x[None], (t_chunk, SSM_GROUP, nstate)).reshape(nrow, nstate)

    btr = tile_rows(bt_ref[0, 0, 0])
    bti = tile_rows(bt_ref[0, 0, 1])
    bbr = cr * btr - ci * bti
    bbi = cr * bti + ci * btr
    jr = (lax.broadcasted_iota(jnp.int32, (nrow, 1), 0) >> shift).astype(F32)
    lj = jr + df * ((t_chunk - 1) - 2.0 * jr)
    step = lax.broadcasted_iota(jnp.int32, (t_chunk, 1), 0).astype(F32)
    lstep = step + df * ((t_chunk - 1) - 2.0 * step)

    def rep_rows(x):
        return jnp.broadcast_to(x[:, None, :], (t_chunk, SSM_GROUP, nstate)).reshape(nrow, nstate)

    enr, eni = [rep_rows(x) for x in cpow(ar_r, ai_r, -lstep)]
    bmr = enr * bbr - eni * bbi
    bmi = enr * bbi + eni * bbr
    ewr, ewi = [rep_rows(x) for x in cpow(ar_r, ai_r, (t_chunk - 1) - lstep)]
    w_ref[0, 0] = jnp.concatenate([ewr * bbr - ewi * bbi, ewr * bbi + ewi * bbr], axis=1).astype(BF16)
    atr, ati = cpow(ar_r, ai_r, float(t_chunk))
    at_ref[0, 0] = jnp.concatenate([atr, ati], axis=1)

    ar_c = acol_ref[0, 0, :, 0:1]
    ai_c = acol_ref[0, 0, :, 1:2]
    ncol = t_chunk * SSM_GROUP
    lane = lax.broadcasted_iota(jnp.int32, (1, ncol), 1)
    tci = lane >> shift
    tile = jnp.where(lax.broadcasted_iota(jnp.int32, (SSM_GROUP, ncol), 0) == (lane & (SSM_GROUP - 1)),
                     1.0, 0.0).astype(BF16)

    def tile_cols(x):
        hi, lo = _split(x)
        lo2 = (x - hi.astype(F32) - lo.astype(F32)).astype(BF16)
        return _dot(hi, tile) + _dot(lo, tile) + _dot(lo2, tile)

    ctr = tile_cols(ct_ref[0, 0, 0])
    cti = tile_cols(ct_ref[0, 0, 1])
    tc = tci.astype(F32)
    lt = tc + df * ((t_chunk - 1) - 2.0 * tc)
    stepl = lax.broadcasted_iota(jnp.int32, (1, t_chunk), 1).astype(F32)
    lstepl = stepl + df * ((t_chunk - 1) - 2.0 * stepl)
    rep = jnp.where(lax.broadcasted_iota(jnp.int32, (t_chunk, ncol), 0) == tci, 1.0, 0.0).astype(BF16)

    def rep_cols(x):
        hi, lo = _split(x)
        lo2 = (x - hi.astype(F32) - lo.astype(F32)).astype(BF16)
        return _dot(hi, rep) + _dot(lo, rep) + _dot(lo2, rep)

    ecr, eci = [rep_cols(x) for x in cpow(ar_c, ai_c, lstepl)]
    cmr = ctr * ecr - cti * eci
    cmi = ctr * eci + cti * ecr
    lhs_hi, lhs_lo = _split(jnp.concatenate([bmr, -bmi], axis=1))
    rhs_hi, rhs_lo = _split(jnp.concatenate([cmr, cmi], axis=0))
    m = _dot(lhs_hi, rhs_hi) + _dot(lhs_hi, rhs_lo) + _dot(lhs_lo, rhs_hi)
    m_ref[0, 0] = jnp.where(lj <= lt, m, 0.0).astype(BF16)
    abr_c, abi_c = abar(ar_c, ai_c)
    c1r = cmr * abr_c - cmi * abi_c
    c1i = cmr * abi_c + cmi * abr_c
    v_ref[0, 0] = jnp.concatenate([c1r, -c1i], axis=0).astype(BF16)


def _ssm_prep(arow, acol, ldt, bt, ct):
    nd, g = arow.shape[0], arow.shape[1]
    p = SSM_STATE
    n = SSM_CHUNK * SSM_GROUP
    blk = lambda a: pl.BlockSpec((1, 1) + a.shape[2:], lambda d, gi: (d, gi) + (0,) * (a.ndim - 2))
    return pl.pallas_call(
        _ssm_prep_kernel,
        grid=(nd, g),
        in_specs=[blk(arow), blk(acol), blk(ldt), blk(bt), blk(ct)],
        out_specs=[
            pl.BlockSpec((1, 1, n, n), lambda d, gi: (d, gi, 0, 0)),
            pl.BlockSpec((1, 1, n, 2 * p), lambda d, gi: (d, gi, 0, 0)),
            pl.BlockSpec((1, 1, 2 * p, n), lambda d, gi: (d, gi, 0, 0)),
            pl.BlockSpec((1, 1, 1, 2 * p), lambda d, gi: (d, gi, 0, 0)),
        ],
        out_shape=[
            jax.ShapeDtypeStruct((nd, g, n, n), BF16),
            jax.ShapeDtypeStruct((nd, g, n, 2 * p), BF16),
            jax.ShapeDtypeStruct((nd, g, 2 * p, n), BF16),
            jax.ShapeDtypeStruct((nd, g, 1, 2 * p), F32),
        ],
        compiler_params=pltpu.CompilerParams(dimension_semantics=("arbitrary", "arbitrary")),
        name="ssm_prep",
    )(arow, acol, ldt, bt, ct)


def _chunk_carry(z, a, s0, reverse):
    p = SSM_STATE
    nc = z.shape[0]
    row = lax.broadcasted_iota(jnp.int32, z.shape, 0)
    is_re = lax.broadcasted_iota(jnp.int32, a.shape, 1) < p
    sign = jnp.where(is_re, -1.0, 1.0)

    def parts(ap):
        sw = pltpu.roll(ap, p, 1)
        return jnp.where(is_re, ap, sw), sign * jnp.where(is_re, sw, ap)

    def cmul(x, ar_full, ai_sgn):
        return x * ar_full + pltpu.roll(x, p, 1) * ai_sgn

    ar_full, ai_sgn = parts(a)
    first = (nc - 1) if reverse else 0
    e = z + jnp.where(row == first, cmul(s0, ar_full, ai_sgn), 0.0)
    k = 1
    while k < nc:
        if reverse:
            sh = jnp.where(row < nc - k, pltpu.roll(e, nc - k, 0), 0.0)
        else:
            sh = jnp.where(row >= k, pltpu.roll(e, k, 0), 0.0)
        e = e + cmul(sh, ar_full, ai_sgn)
        k *= 2
        if k < nc:
            ar_full, ai_sgn = parts(cmul(jnp.where(is_re, ar_full, sign * ai_sgn), ar_full, ai_sgn))
    if reverse:
        return jnp.where(row == first, s0, pltpu.roll(e, nc - 1, 0))
    return jnp.where(row == first, s0, pltpu.roll(e, 1, 0))


def _ssm_kernel(u_ref, um_ref, m_ref, w_ref, v_ref, at_ref, dvec_ref, y_ref):
    p = SSM_STATE
    for gi in range(SSM_GROUPS_PER_STEP):
        u = u_ref[0, gi]
        ub = u.astype(BF16)
        y = dvec_ref[gi] * u
        for d in range(2):
            yin = _dot(ub, m_ref[d, gi])
            z = _dot(ub, w_ref[d, gi])
            if d == 0:
                s0 = _dot(um_ref[gi].astype(BF16), w_ref[0, gi])[0:1, :]
            else:
                s0 = jnp.zeros((1, 2 * p), F32)
            s_in = _chunk_carry(z, at_ref[d, gi], s0, reverse=(d == 1))
            y = y + yin + _dot(s_in.astype(BF16), v_ref[d, gi])
        y_ref[0, gi] = y


def _ssm(u2, um2, m, w, v, at, dvec):
    b, g, nc, n = u2.shape
    p = SSM_STATE
    gb = SSM_GROUPS_PER_STEP
    return pl.pallas_call(
        _ssm_kernel,
        grid=(b, g // gb),
        in_specs=[
            pl.BlockSpec((1, gb, nc, n), lambda bi, gi: (bi, gi, 0, 0)),
            pl.BlockSpec((gb, 8, n), lambda bi, gi: (gi, 0, 0)),
            pl.BlockSpec((2, gb, n, n), lambda bi, gi: (0, gi, 0, 0)),
            pl.BlockSpec((2, gb, n, 2 * p), lambda bi, gi: (0, gi, 0, 0)),
            pl.BlockSpec((2, gb, 2 * p, n), lambda bi, gi: (0, gi, 0, 0)),
            pl.BlockSpec((2, gb, 1, 2 * p), lambda bi, gi: (0, gi, 0, 0)),
            pl.BlockSpec((gb, 1, n), lambda bi, gi: (gi, 0, 0)),
        ],
        out_specs=pl.BlockSpec((1, gb, nc, n), lambda bi, gi: (bi, gi, 0, 0)),
        out_shape=jax.ShapeDtypeStruct((b, g, nc, n), F32),
        compiler_params=pltpu.CompilerParams(dimension_semantics=("arbitrary", "arbitrary")),
        name="ssm",
    )(u2, um2, m, w, v, at, dvec)


def _merge_kernel(x_ref, ya_ref, ys_ref, wgt_ref, wglu_ref, bglu_ref, wab_ref, wsb_ref, wo_ref,
                  lng_ref, lnb_ref, l1g_ref, l1b_ref, wr_ref, wrt_ref,
                  h1_ref, tok_ref, afft_ref):
    d = x_ref.shape[1]
    h = _ln(x_ref[...], lng_ref[...], lnb_ref[...])
    gates = _dot(h.astype(BF16), wgt_ref[...])
    ga = _sigmoid(gates[:, 0:d])
    gs = _sigmoid(gates[:, d:2 * d])
    ys = ys_ref[...]
    y = 0.5 * ys * (1.0 + jnp.tanh(math.sqrt(2.0 / math.pi) * (ys + 0.044715 * (ys * ys * ys))))
    yg = y * _sigmoid(_dot(y.astype(BF16), wglu_ref[...]) + bglu_ref[...])
    merged = ga * _dot(ya_ref[...], wab_ref[...]) + gs * _dot(yg.astype(BF16), wsb_ref[...])
    h1 = _ln(DEEPNORM_ALPHA * h + _dot(merged.astype(BF16), wo_ref[...]), l1g_ref[...], l1b_ref[...])
    h1_ref[...] = h1
    hi, lo = _split(h1)
    ne = afft_ref.shape[0]
    logits = _dot(hi, wr_ref[0]) + _dot(hi, wr_ref[1]) + _dot(lo, wr_ref[0])
    lane = lax.broadcasted_iota(jnp.int32, logits.shape, 1)
    logits = jnp.where(lane < 3 * ne, logits, -jnp.inf)
    ex = jnp.exp(logits - jnp.max(logits, axis=1, keepdims=True))
    aff = ex / jnp.sum(jnp.where(lane < ne, ex, 0.0), axis=1, keepdims=True)
    p0 = aff.astype(BF16)
    r1 = aff - p0.astype(F32)
    p1 = r1.astype(BF16)
    p2 = (r1 - p1.astype(F32)).astype(BF16)
    pieces = jnp.where(lane < ne, p0, jnp.where(lane < 2 * ne, p1, p2))
    tok_ref[...] = jnp.concatenate([hi, pieces], axis=1)
    lt = _dot_nt(wrt_ref[0], hi) + _dot_nt(wrt_ref[1], hi) + _dot_nt(wrt_ref[0], lo)
    et = jnp.exp(lt - jnp.max(lt, axis=0, keepdims=True))
    afft_ref[...] = et / jnp.sum(et, axis=0, keepdims=True)


def _merge(x2, ya, ys, wgt, wglu, bglu, wab, wsb, wo, lng, lnb, l1g, l1b, wr, wrt, tm):
    rows, d = x2.shape
    ne = wrt.shape[1]
    full = lambda a: pl.BlockSpec(a.shape, lambda i: (0,) * a.ndim)
    return pl.pallas_call(
        _merge_kernel,
        grid=(rows // tm,),
        in_specs=[
            pl.BlockSpec((tm, d), lambda i: (i, 0)),
            pl.BlockSpec((tm, ya.shape[1]), lambda i: (i, 0)),
            pl.BlockSpec((tm, ys.shape[1]), lambda i: (i, 0)),
            full(wgt), full(wglu), full(bglu), full(wab), full(wsb), full(wo),
            full(lng), full(lnb), full(l1g), full(l1b), full(wr), full(wrt),
        ],
        out_specs=[
            pl.BlockSpec((tm, d), lambda i: (i, 0)),
            pl.BlockSpec((tm, d + GATE_LANES), lambda i: (i, 0)),
            pl.BlockSpec((ne, tm), lambda i: (0, i)),
        ],
        out_shape=[
            jax.ShapeDtypeStruct((rows, d), F32),
            jax.ShapeDtypeStruct((rows, d + GATE_LANES), BF16),
            jax.ShapeDtypeStruct((ne, rows), F32),
        ],
        compiler_params=pltpu.CompilerParams(dimension_semantics=("arbitrary",),
                                             vmem_limit_bytes=VMEM_LIMIT),
        name="merge",
    )(x2, ya, ys, wgt, wglu, bglu, wab, wsb, wo, lng, lnb, l1g, l1b, wr, wrt)


def _route_kernel(afft_ref, tri_ref, scl_ref, lo_ref, *, cap):
    aff = afft_ref[...]
    ne, s = aff.shape
    capf = float(cap)

    def as_float(bits):
        return lax.bitcast_convert_type(bits, F32)

    def search(i, t):
        cand = t | jnp.left_shift(jnp.int32(1), 30 - i)
        cnt = jnp.sum(jnp.where(aff >= as_float(cand), 1.0, 0.0), axis=1, keepdims=True)
        return jnp.where(cnt >= capf, cand, t)

    thr_bits = lax.fori_loop(0, 31, search, jnp.zeros((ne, 1), jnp.int32))
    gt = aff >= as_float(thr_bits + 1)
    eq = (aff >= as_float(thr_bits)) & jnp.logical_not(gt)
    need = capf - jnp.sum(jnp.where(gt, 1.0, 0.0), axis=1, keepdims=True)
    tri = tri_ref[...]
    rt = ROUTE_TILE
    nt = s // rt
    col = lax.broadcasted_iota(jnp.int32, (ne, nt), 1)
    carry_eq = jnp.zeros((ne, 1), F32)
    carry_sel = jnp.zeros((ne, 1), F32)
    lo_val = jnp.zeros((ne, nt), F32)
    for t in range(nt):
        sl = slice(t * rt, (t + 1) * rt)
        eq_b = eq[:, sl]
        ceq = _dot(jnp.where(eq_b, 1.0, 0.0).astype(BF16), tri)
        sel_b = gt[:, sl] | (eq_b & ((ceq + carry_eq) <= need))
        carry_eq = carry_eq + ceq[:, rt - 1:rt]
        csel = _dot(jnp.where(sel_b, 1.0, 0.0).astype(BF16), tri)
        scl_ref[0, :, sl] = jnp.where(sel_b, csel, 0.0)
        lo_val = jnp.where(col == t, carry_sel, lo_val)
        carry_sel = carry_sel + csel[:, rt - 1:rt]
    lo_ref[0] = lo_val


def _route(afft, tri, nb, cap):
    ne, rows = afft.shape
    s = rows // nb
    nt = s // ROUTE_TILE
    return pl.pallas_call(
        functools.partial(_route_kernel, cap=cap),
        grid=(nb,),
        in_specs=[
            pl.BlockSpec((ne, s), lambda b: (0, b)),
            pl.BlockSpec(tri.shape, lambda b: (0, 0)),
        ],
        out_specs=[
            pl.BlockSpec((1, ne, s), lambda b: (b, 0, 0)),
            pl.BlockSpec((1, ne, nt), lambda b: (b, 0, 0)),
        ],
        out_shape=[
            jax.ShapeDtypeStruct((nb, ne, s), F32),
            jax.ShapeDtypeStruct((nb, ne, nt), F32),
        ],
        compiler_params=pltpu.CompilerParams(dimension_semantics=("arbitrary",)),
        name="route",
    )(afft, tri)


def _ffn_kernel(lo_ref, t_ref, scl_ref, wg_ref, wu_ref, wd_ref, ye_ref, xe_sc, xb_sc, gate_sc, acc_sc, *, cap):
    fc = pl.program_id(2)

    @pl.when(fc == 0)
    def _gather():
        _ffn_gather(lo_ref, t_ref, scl_ref, xe_sc, xb_sc, gate_sc, cap=cap)
        acc_sc[...] = jnp.zeros(acc_sc.shape, F32)

    xb = xb_sc[...]
    g = _dot(xb, wg_ref[0, 0].astype(BF16))
    u = _dot(xb, wu_ref[0, 0].astype(BF16))
    hh = (g * _sigmoid(g)) * u
    acc_sc[...] += _dot(hh.astype(BF16), wd_ref[0, 0].astype(BF16))

    @pl.when(fc == pl.num_programs(2) - 1)
    def _emit():
        ye_ref[0, 0] = (acc_sc[...] * gate_sc[...]).astype(BF16)


def _ffn_gather(lo_ref, t_ref, scl_ref, xe_sc, xb_sc, gate_sc, *, cap):
    b = pl.program_id(0)
    e = pl.program_id(1)
    ne = pl.num_programs(1)
    rt = ROUTE_TILE
    nt = t_ref.shape[1] // rt
    xe_sc[...] = jnp.zeros(xe_sc.shape, F32)
    r = lax.broadcasted_iota(jnp.int32, (PIECE, rt), 0).astype(F32)

    def tile_info(tau):
        base = (b * ne + e) * (nt + 1) + tau
        lo = lo_ref[base]
        n_pieces = jnp.right_shift((lo & 7) + (lo_ref[base + 1] - lo) + (PIECE - 1), PIECE_SHIFT)
        return lo, n_pieces

    def add_piece(tau, lo, p):
        off = lo & 7
        scl = scl_ref[0, 0, pl.ds(tau, 1), :]
        tok = t_ref[0, pl.ds(pl.multiple_of(tau * rt, rt), rt), :]
        shift = (off - 1 - p * PIECE).astype(F32)
        onehot = jnp.where((scl > 0.0) & ((scl + shift) == r), 1.0, 0.0).astype(BF16)
        xe_sc[pl.ds(pl.multiple_of(lo - off + p * PIECE, 8), PIECE), :] += _dot(onehot, tok)

    def first_piece(tau, carry):
        add_piece(tau, tile_info(tau)[0], 0)
        return carry

    def more_pieces(tau, carry):
        lo, n_pieces = tile_info(tau)
        lax.fori_loop(1, n_pieces, lambda p, c: (add_piece(tau, lo, p), c)[1], 0)
        return carry

    lax.fori_loop(0, nt, first_piece, 0, unroll=4)
    lax.fori_loop(0, nt, more_pieces, 0)
    d = xb_sc.shape[1]
    xb_sc[...] = xe_sc[0:cap, 0:d].astype(BF16)
    gl = xe_sc[0:cap, d:d + GATE_LANES]
    lane = lax.broadcasted_iota(jnp.int32, gl.shape, 1)
    mine = ((lane & (N_EXPERTS - 1)) == e) & (lane < 3 * N_EXPERTS)
    gate_sc[...] = jnp.sum(jnp.where(mine, gl, 0.0), axis=1, keepdims=True)


def _ffn(lo_i, t, scl4, wg, wu, wd, layer, cap, f_chunk):
    b, s, dx = t.shape
    _, ne, d, f = wg.shape
    nt = s // ROUTE_TILE
    grid_spec = pltpu.PrefetchScalarGridSpec(
        num_scalar_prefetch=1,
        grid=(b, ne, f // f_chunk),
        in_specs=[
            pl.BlockSpec((1, s, dx), lambda bi, ei, fi, lo: (bi, 0, 0), pipeline_mode=pl.Buffered(1)),
            pl.BlockSpec((1, 1, nt, ROUTE_TILE), lambda bi, ei, fi, lo: (bi, ei, 0, 0)),
            pl.BlockSpec((1, 1, d, f_chunk), lambda bi, ei, fi, lo: (layer, ei, 0, fi)),
            pl.BlockSpec((1, 1, d, f_chunk), lambda bi, ei, fi, lo: (layer, ei, 0, fi)),
            pl.BlockSpec((1, 1, f_chunk, d), lambda bi, ei, fi, lo: (layer, ei, fi, 0)),
        ],
        out_specs=pl.BlockSpec((1, 1, cap, d), lambda bi, ei, fi, lo: (bi, ei, 0, 0)),
        scratch_shapes=[pltpu.VMEM((cap + PIECE, dx), F32), pltpu.VMEM((cap, d), BF16),
                        pltpu.VMEM((cap, 1), F32), pltpu.VMEM((cap, d), F32)],
    )
    return pl.pallas_call(
        functools.partial(_ffn_kernel, cap=cap),
        grid_spec=grid_spec,
        out_shape=jax.ShapeDtypeStruct((b, ne, cap, d), BF16),
        compiler_params=pltpu.CompilerParams(dimension_semantics=("arbitrary", "arbitrary", "arbitrary"),
                                             vmem_limit_bytes=VMEM_LIMIT),
        name="ffn",
    )(lo_i, t, scl4, wg, wu, wd)


def _combine_kernel(lo_ref, h1_ref, sclt_ref, ye_hbm, g_ref, b_ref, o_ref, win, xwin, acc_sc, sem, xsem, *, cap):
    b = pl.program_id(0)
    tau = pl.program_id(1)
    nb = pl.num_programs(0)
    nt = pl.num_programs(1)
    ne = win.shape[1]
    d = win.shape[3]
    step = b * nt + tau
    slot = step & 1

    def pieces(bi, ti, e):
        base = (bi * ne + e) * (nt + 1) + ti
        lo = lo_ref[base]
        n_sel = lo_ref[base + 1] - lo
        n_pieces = jnp.maximum(jnp.right_shift((lo & 15) + n_sel + (PIECE - 1), PIECE_SHIFT), 1)
        first = jnp.minimum(lo - (lo & 15), cap - n_pieces * PIECE)
        return lo, first, n_pieces

    def copy(bi, e, first, p, buf, s):
        start = pl.multiple_of(first + p * PIECE, 16)
        return pltpu.make_async_copy(ye_hbm.at[bi, e, pl.ds(start, PIECE), :], buf, s)

    def first_pieces(bi, ti, sl):
        return [copy(bi, e, pieces(bi, ti, e)[1], 0, win.at[sl, e], sem.at[sl, e]) for e in range(ne)]

    @pl.when(step == 0)
    def _prime():
        for cp in first_pieces(b, tau, slot):
            cp.start()

    @pl.when(step + 1 < nb * nt)
    def _prefetch():
        wrap = tau + 1 == nt
        for cp in first_pieces(jnp.where(wrap, b + 1, b), jnp.where(wrap, 0, tau + 1), 1 - slot):
            cp.start()

    for cp in first_pieces(b, tau, slot):
        cp.wait()

    sclt = sclt_ref[0]
    col = lax.broadcasted_iota(jnp.int32, (1, ne * PIECE), 1)
    grp = jnp.right_shift(col, PIECE_SHIFT)
    expand = jnp.where(lax.broadcasted_iota(jnp.int32, (ne, ne * PIECE), 0) == grp, 1.0, 0.0).astype(BF16)
    scl = _dot(sclt.astype(BF16), expand)
    shift = jnp.zeros(col.shape, F32)
    for e in range(ne):
        lo, first, _ = pieces(b, tau, e)
        shift = jnp.where(grp == e, (lo - first - 1).astype(F32), shift)
    r = (col & (PIECE - 1)).astype(F32)
    onehot = jnp.where((scl > 0.0) & ((scl + shift) == r), 1.0, 0.0).astype(BF16)
    acc_sc[...] = DEEPNORM_ALPHA * h1_ref[0] + _dot(onehot, win[slot].reshape(ne * PIECE, d))

    for e in range(ne):
        lo, first, n_pieces = pieces(b, tau, e)

        def extra(p, c, e=e, lo=lo, first=first):
            cp = copy(b, e, first, p, xwin, xsem.at[0])
            cp.start()
            cp.wait()
            se = sclt_ref[0][:, e:e + 1]
            rr = lax.broadcasted_iota(jnp.int32, (se.shape[0], PIECE), 1).astype(F32)
            oh = jnp.where((se > 0.0) & ((se + (lo - first - p * PIECE - 1).astype(F32)) == rr), 1.0, 0.0)
            acc_sc[...] += _dot(oh.astype(BF16), xwin[...])
            return c

        lax.fori_loop(1, n_pieces, extra, 0)
    o_ref[0] = _ln(acc_sc[...], g_ref[...], b_ref[...])


def _combine(lo_i, h1, sclt, ye, g, bb, cap):
    b, s, d = h1.shape
    ne = sclt.shape[2]
    rt = ROUTE_TILE
    grid_spec = pltpu.PrefetchScalarGridSpec(
        num_scalar_prefetch=1,
        grid=(b, s // rt),
        in_specs=[
            pl.BlockSpec((1, rt, d), lambda bi, ti, lo: (bi, ti, 0)),
            pl.BlockSpec((1, rt, ne), lambda bi, ti, lo: (bi, ti, 0)),
            pl.BlockSpec(memory_space=pl.ANY),
            pl.BlockSpec(g.shape, lambda bi, ti, lo: (0, 0)),
            pl.BlockSpec(bb.shape, lambda bi, ti, lo: (0, 0)),
        ],
        out_specs=pl.BlockSpec((1, rt, d), lambda bi, ti, lo: (bi, ti, 0)),
        scratch_shapes=[pltpu.VMEM((2, ne, PIECE, d), BF16), pltpu.VMEM((PIECE, d), BF16),
                        pltpu.VMEM((rt, d), F32),
                        pltpu.SemaphoreType.DMA((2, ne)), pltpu.SemaphoreType.DMA((1,))],
    )
    return pl.pallas_call(
        functools.partial(_combine_kernel, cap=cap),
        grid_spec=grid_spec,
        out_shape=jax.ShapeDtypeStruct((b, s, d), F32),
        compiler_params=pltpu.CompilerParams(dimension_semantics=("arbitrary", "arbitrary"),
                                             vmem_limit_bytes=VMEM_LIMIT),
        name="combine",
    )(lo_i, h1, sclt, ye, g, bb)


def _rope_tables(n_tokens):
    half = HEAD_DIM // 2
    inv_freq = ROPE_THETA ** (-jnp.arange(0, half, 2, dtype=F32) / half)
    rows = n_tokens // GRID_W
    row = jnp.repeat(jnp.arange(rows, dtype=F32), GRID_W)
    colv = jnp.tile(jnp.arange(GRID_W, dtype=F32), rows)
    ang = jnp.concatenate([row[:, None] * inv_freq, colv[:, None] * inv_freq], axis=-1)
    return jnp.cos(ang), jnp.sin(ang)


def kernel(x, meta_tokens, ln_in_g, ln_in_b, w_in, q_norm_g, k_norm_g, ssm_a_re, ssm_a_im, ssm_log_dt,
           ssm_b_re, ssm_b_im, ssm_c_re, ssm_c_im, ssm_d, w_glu, b_glu, w_attn_br, w_ssm_br, w_o,
           ln1_g, ln1_b, w_router, w_gate_e, w_up_e, w_down_e, ln2_g, ln2_b):
    b, s, d = x.shape
    aw = N_HEADS * HEAD_DIM
    kw = N_KV_HEADS * HEAD_DIM
    sw = d // 2
    g = sw // SSM_GROUP
    half = HEAD_DIM // 2
    cap = CAPACITY_FACTOR * s // N_EXPERTS
    rows = b * s
    l = 0

    perm = np.concatenate([np.arange(0, HEAD_DIM, 2), np.arange(1, HEAD_DIM, 2)])
    qcols = np.concatenate([h * HEAD_DIM + perm for h in range(N_HEADS)])
    kcols = aw + np.concatenate([h * HEAD_DIM + perm for h in range(N_KV_HEADS)])
    wl = w_in[l]
    w_t = jnp.concatenate([wl[:, qcols], wl[:, aw + kw:aw + 2 * kw]], axis=1).T.astype(BF16)
    w_n = jnp.concatenate([wl[:, kcols], wl[:, aw + 2 * kw:aw + 2 * kw + sw]], axis=1).astype(BF16)
    w_gates = wl[:, aw + 2 * kw + sw:].astype(BF16)
    qg3 = (jnp.tile(q_norm_g[l][perm], N_HEADS) * (HEAD_DIM ** -0.5 * math.log2(math.e))).reshape(N_HEADS, HEAD_DIM, 1)
    kg = jnp.tile(k_norm_g[l][perm], N_KV_HEADS)[None, :]
    bd = jnp.asarray(np.kron(np.eye(N_KV_HEADS), np.full((HEAD_DIM, HEAD_DIM), 1.0 / HEAD_DIM)), BF16)
    lng = ln_in_g[None, :]
    lnb = ln_in_b[None, :]
    cos, sin = _rope_tables(s)
    cosr = jnp.tile(cos, (1, 2 * N_KV_HEADS))
    sinr = jnp.tile(jnp.concatenate([-sin, sin], axis=1), (1, N_KV_HEADS))
    cost = cos.T
    sint = sin.T

    x2 = x.reshape(rows, d)
    tm_in = 512
    qt, kk, vt, u2 = _inproj(x2, cosr, sinr, cost, sint, w_t, w_n, lng, lnb, qg3, kg, bd, tm_in, s // tm_in)
    meta_p = jnp.pad(meta_tokens, ((0, META_PAD - N_META), (0, 0)))
    ones_r = jnp.ones((META_PAD, kw), F32)
    _, km, vtm, um = _inproj(meta_p, ones_r, jnp.zeros_like(ones_r), jnp.ones((half, META_PAD), F32),
                             jnp.zeros((half, META_PAD), F32), w_t, w_n, lng, lnb, qg3, kg, bd, META_PAD, 1)
    um = um[:N_META]
    km = km[:, :N_META, :]
    vtm = vtm[:, :, :N_META]

    ya = _attention(qt, kk, vt, km, vtm, b, 512, 512)

    t_chunk = SSM_CHUNK
    hgrp = SSM_GROUP
    n = t_chunk * hgrp
    arow = jnp.stack([ssm_a_re[l], ssm_a_im[l]], axis=2)
    acol = jnp.stack([ssm_a_re[l], ssm_a_im[l]], axis=3)
    ldt = ssm_log_dt[l][:, :, None, None]
    bt = jnp.stack([ssm_b_re[l], ssm_b_im[l]], axis=2)
    bt = jnp.swapaxes(bt, 3, 4)
    ct = jnp.stack([ssm_c_re[l], ssm_c_im[l]], axis=2)
    ct = jnp.swapaxes(ct, 3, 4)
    mm, ww, vv, at = _ssm_prep(arow, acol, ldt, bt, ct)
    nc = s // t_chunk
    u_g = u2.reshape(b, nc, t_chunk, g, hgrp).transpose(0, 3, 1, 2, 4).reshape(b, g, nc, n)
    um_g = um.reshape(N_META, g, hgrp).transpose(1, 0, 2).reshape(g, 1, N_META * hgrp)
    um_g = jnp.pad(um_g, ((0, 0), (0, 7), (n - N_META * hgrp, 0)))
    dvec = jnp.tile(ssm_d[l], (1, t_chunk))[:, None, :]
    y_g = _ssm(u_g, um_g, mm, ww, vv, at, dvec)
    ys = y_g.reshape(b, g, nc, t_chunk, hgrp).transpose(0, 2, 3, 1, 4).reshape(rows, sw)

    wr_hi = w_router[l].astype(BF16)
    wr_lo = (w_router[l] - wr_hi.astype(F32)).astype(BF16)
    wrt = jnp.stack([wr_hi.T, wr_lo.T])
    wr = jnp.pad(jnp.tile(jnp.stack([wr_hi, wr_lo]), (1, 1, 3)), ((0, 0), (0, 0), (0, GATE_LANES - 3 * N_EXPERTS)))
    h1, tok, afft = _merge(
        x2, ya, ys, w_gates, w_glu[l].astype(BF16), b_glu[l][None, :],
        w_attn_br[l].astype(BF16), w_ssm_br[l].astype(BF16), w_o[l].astype(BF16),
        lng, lnb, ln1_g[l][None, :], ln1_b[l][None, :], wr, wrt, 512)

    tri = jnp.asarray(np.triu(np.ones((ROUTE_TILE, ROUTE_TILE), np.float32)), BF16)
    scl, lo_f = _route(afft, tri, b, cap)
    nt = s // ROUTE_TILE
    lo_i = jnp.pad(lo_f.astype(jnp.int32), ((0, 0), (0, 0), (0, 1)), constant_values=cap).reshape(-1)
    ye = _ffn(lo_i, tok.reshape(b, s, d + GATE_LANES), scl.reshape(b, N_EXPERTS, nt, ROUTE_TILE),
              w_gate_e, w_up_e, w_down_e, l, cap, 512)
    out = _combine(lo_i, h1.reshape(b, s, d), jnp.swapaxes(scl, 1, 2), ye,
                   ln2_g[l][None, :], ln2_b[l][None, :], cap)
    return out
```

```python
import functools
import math

import numpy as np
import jax
import jax.numpy as jnp
from jax import lax
from jax.experimental import pallas as pl
from jax.experimental.pallas import tpu as pltpu

F32 = jnp.float32
BF16 = jnp.bfloat16

N_META = 16
GRID_W = 64
N_HEADS = 8
N_KV_HEADS = 2
HEAD_DIM = 64
ROPE_THETA = 10000.0
SSM_GROUP = 16
SSM_STATE = 64
N_EXPERTS = 16
CAPACITY_FACTOR = 2
LN_EPS = 1e-5
QK_EPS = 1e-6
DEPTH = 1
DEEPNORM_ALPHA = (2.0 * DEPTH) ** 0.25

SSM_CHUNK = 32
SSM_GROUPS_PER_STEP = 2
ROUTE_TILE = 256
GATE_LANES = 128
PIECE_SHIFT = 6
PIECE = 1 << PIECE_SHIFT
META_PAD = 128
V_ROWS = HEAD_DIM + 16
VMEM_LIMIT = 56 * 1024 * 1024


def _ln(x, g, b):
    mu = jnp.mean(x, axis=-1, keepdims=True)
    xc = x - mu
    var = jnp.mean(xc * xc, axis=-1, keepdims=True)
    return xc * lax.rsqrt(var + LN_EPS) * g + b


def _sigmoid(x):
    return 1.0 / (1.0 + jnp.exp(-x))


def _split(t):
    hi = t.astype(BF16)
    lo = (t - hi.astype(F32)).astype(BF16)
    return hi, lo


def _dot(a, b):
    return jnp.dot(a, b, preferred_element_type=F32)


def _dot_nt(a, b):
    return lax.dot_general(a, b, (((1,), (1,)), ((), ())), preferred_element_type=F32)


def _inproj_kernel(x_ref, cosr_ref, sinr_ref, cost_ref, sint_ref, wt_ref, wn_ref, lng_ref, lnb_ref,
                   qg_ref, kg_ref, bd_ref, qt_ref, k_ref, vt_ref, u_ref):
    aw = N_HEADS * HEAD_DIM
    kw = N_KV_HEADS * HEAD_DIM
    half = HEAD_DIM // 2
    h = _ln(x_ref[...], lng_ref[...], lnb_ref[...])
    hb = h.astype(BF16)
    pt = _dot_nt(wt_ref[...], hb)
    pn = _dot(hb, wn_ref[...])
    tm = hb.shape[0]

    qt = pt[0:aw].reshape(N_HEADS, HEAD_DIM, tm)
    ms = jnp.mean(qt * qt, axis=1, keepdims=True)
    qn = qt * lax.rsqrt(ms + QK_EPS) * qg_ref[...]
    x0 = qn[:, 0:half, :]
    x1 = qn[:, half:, :]
    c = cost_ref[...][None]
    s = sint_ref[...][None]
    qr = jnp.concatenate([x0 * c - x1 * s, x0 * s + x1 * c], axis=1)
    qt_ref[...] = qr.reshape(aw, tm).astype(BF16)
    vrow = lax.broadcasted_iota(jnp.int32, (N_KV_HEADS, V_ROWS - HEAD_DIM, tm), 1)
    vt_ref[...] = jnp.concatenate([pt[aw:aw + kw].reshape(N_KV_HEADS, HEAD_DIM, tm),
                                   jnp.where(vrow == 0, 1.0, 0.0)], axis=1).astype(BF16)

    kk = pn[:, 0:kw]
    hi, lo = _split(kk * kk)
    bd = bd_ref[...]
    msk = _dot(hi, bd) + _dot(lo, bd)
    kn = kk * lax.rsqrt(msk + QK_EPS) * kg_ref[...]
    lane = lax.broadcasted_iota(jnp.int32, kn.shape, 1)
    first = (lane & (HEAD_DIM - 1)) < half
    partner = jnp.where(first, pltpu.roll(kn, kw - half, 1), pltpu.roll(kn, half, 1))
    kr = (kn * cosr_ref[...] + partner * sinr_ref[...]).astype(BF16)
    for g in range(N_KV_HEADS):
        k_ref[g] = kr[:, g * HEAD_DIM:(g + 1) * HEAD_DIM]
    u_ref[...] = pn[:, kw:]


def _inproj(x2, cosr, sinr, cost, sint, wt, wn, lng, lnb, qg3, kg, bd, tm, n_tab_blocks):
    rows, d = x2.shape
    aw = N_HEADS * HEAD_DIM
    kw = N_KV_HEADS * HEAD_DIM
    uw = wn.shape[1] - kw
    half = HEAD_DIM // 2
    full = lambda a: pl.BlockSpec(a.shape, lambda i: (0,) * a.ndim)
    return pl.pallas_call(
        _inproj_kernel,
        grid=(rows // tm,),
        in_specs=[
            pl.BlockSpec((tm, d), lambda i: (i, 0)),
            pl.BlockSpec((tm, kw), lambda i: (i % n_tab_blocks, 0)),
            pl.BlockSpec((tm, kw), lambda i: (i % n_tab_blocks, 0)),
            pl.BlockSpec((half, tm), lambda i: (0, i % n_tab_blocks)),
            pl.BlockSpec((half, tm), lambda i: (0, i % n_tab_blocks)),
            full(wt), full(wn), full(lng), full(lnb), full(qg3), full(kg), full(bd),
        ],
        out_specs=[
            pl.BlockSpec((aw, tm), lambda i: (0, i)),
            pl.BlockSpec((N_KV_HEADS, tm, HEAD_DIM), lambda i: (0, i, 0)),
            pl.BlockSpec((N_KV_HEADS, V_ROWS, tm), lambda i: (0, 0, i)),
            pl.BlockSpec((tm, uw), lambda i: (i, 0)),
        ],
        out_shape=[
            jax.ShapeDtypeStruct((aw, rows), BF16),
            jax.ShapeDtypeStruct((N_KV_HEADS, rows, HEAD_DIM), BF16),
            jax.ShapeDtypeStruct((N_KV_HEADS, V_ROWS, rows), BF16),
            jax.ShapeDtypeStruct((rows, uw), F32),
        ],
        compiler_params=pltpu.CompilerParams(dimension_semantics=("arbitrary",),
                                             vmem_limit_bytes=VMEM_LIMIT),
        name="inproj",
    )(x2, cosr, sinr, cost, sint, wt, wn, lng, lnb, qg3, kg, bd)


def _attn_kernel(qt_ref, k0_ref, k1_ref, vta_ref, vtb_ref, km_ref, vtm_ref, o_ref, m_sc, acc_sc, s_sc, mb_sc):
    j = pl.program_id(2)
    last = pl.num_programs(2) - 1
    grp = N_HEADS // N_KV_HEADS
    heads = lambda h: slice(h * HEAD_DIM, (h + 1) * HEAD_DIM)

    def score(k_ref, slot, h):
        s = _dot(k_ref[h // grp], qt_ref[heads(h), :])
        s_sc[slot, h] = s
        mb_sc[slot, h:h + 1, :] = jnp.max(s, axis=0, keepdims=True)

    def softmax_pv(h, s, m_blk, vt_g):
        m_prev = m_sc[h:h + 1, :]
        m_new = jnp.maximum(m_prev, m_blk)
        alpha = jnp.exp2(m_prev - m_new)
        p = jnp.exp2(s - m_new).astype(BF16)
        acc_sc[h] = alpha * acc_sc[h] + _dot(vt_g, p)
        m_sc[h:h + 1, :] = m_new

    def consume(vt_ref, slot, h):
        softmax_pv(h, s_sc[slot, h], mb_sc[slot, h:h + 1, :], vt_ref[h // grp])

    @pl.when(j == 0)
    def _first():
        m_sc[...] = jnp.full(m_sc.shape, -jnp.inf, F32)
        acc_sc[...] = jnp.zeros(acc_sc.shape, F32)
        for h in range(N_HEADS):
            s = _dot(km_ref[h // grp], qt_ref[heads(h), :])
            softmax_pv(h, s, jnp.max(s, axis=0, keepdims=True), vtm_ref[h // grp])
            score(k0_ref, 0, h)
        for h in range(N_HEADS):
            score(k1_ref, 1, h)
            consume(vtb_ref, 0, h)

    @pl.when((j > 0) & (j < last))
    def _middle():
        for h in range(N_HEADS):
            score(k0_ref, 0, h)
            consume(vta_ref, 1, h)
        for h in range(N_HEADS):
            score(k1_ref, 1, h)
            consume(vtb_ref, 0, h)

    @pl.when(j == last)
    def _last():
        for h in range(N_HEADS):
            consume(vta_ref, 1, h)
        acc = acc_sc[...]
        tq = acc.shape[2]
        out_t = (acc[:, 0:HEAD_DIM, :] / acc[:, HEAD_DIM:HEAD_DIM + 1, :]).reshape(N_HEADS * HEAD_DIM, tq)
        o_ref[...] = out_t.T.astype(BF16)


def _attention(qt, kk, vt, km, vtm, nb, tq, tk):
    aw, rows = qt.shape
    s = rows // nb
    nq = s // tq
    nkb = s // tk
    kblk = lambda f: pl.BlockSpec((N_KV_HEADS, tk, HEAD_DIM),
                                  lambda bi, qi, j: (0, bi * nkb + jnp.clip(f(j), 0, nkb - 1), 0))
    vblk = lambda f: pl.BlockSpec((N_KV_HEADS, V_ROWS, tk),
                                  lambda bi, qi, j: (0, 0, bi * nkb + jnp.clip(f(j), 0, nkb - 1)))
    return pl.pallas_call(
        _attn_kernel,
        grid=(nb, nq, nkb // 2 + 1),
        in_specs=[
            pl.BlockSpec((aw, tq), lambda bi, qi, j: (0, bi * nq + qi)),
            kblk(lambda j: 2 * j), kblk(lambda j: 2 * j + 1),
            vblk(lambda j: 2 * j - 1), vblk(lambda j: 2 * j),
            pl.BlockSpec(km.shape, lambda bi, qi, j: (0, 0, 0)),
            pl.BlockSpec(vtm.shape, lambda bi, qi, j: (0, 0, 0)),
        ],
        out_specs=pl.BlockSpec((tq, aw), lambda bi, qi, j: (bi * nq + qi, 0)),
        out_shape=jax.ShapeDtypeStruct((rows, aw), BF16),
        scratch_shapes=[
            pltpu.VMEM((N_HEADS, tq), F32),
            pltpu.VMEM((N_HEADS, V_ROWS, tq), F32),
            pltpu.VMEM((2, N_HEADS, tk, tq), F32),
            pltpu.VMEM((2, N_HEADS, tq), F32),
        ],
        compiler_params=pltpu.CompilerParams(
            dimension_semantics=("arbitrary", "arbitrary", "arbitrary"),
            vmem_limit_bytes=VMEM_LIMIT),
        name="attn",
    )(qt, kk, kk, vt, vt, km, vtm)


def _ssm_prep_kernel(arow_ref, acol_ref, ldt_ref, bt_ref, ct_ref, m_ref, w_ref, v_ref, at_ref):
    t_chunk = SSM_CHUNK
    shift = int(math.log2(SSM_GROUP))
    df = pl.program_id(0).astype(F32)
    dt = jnp.exp(ldt_ref[0, 0])

    def abar(ar, ai):
        mag = jnp.exp(ar * dt)
        ang = ai * dt
        return mag * jnp.cos(ang), mag * jnp.sin(ang)

    def cpow(ar, ai, e):
        mag = jnp.exp(ar * dt * e)
        ang = ai * dt * e
        return mag * jnp.cos(ang), mag * jnp.sin(ang)

    ar_r = arow_ref[0, 0, 0:1, :]
    ai_r = arow_ref[0, 0, 1:2, :]
    abr, abi = abar(ar_r, ai_r)
    nr = abr - 1.0
    ni = abi
    den = ar_r * ar_r + ai_r * ai_r
    cr = (nr * ar_r + ni * ai_r) / den
    ci = (ni * ar_r - nr * ai_r) / den
    nstate = bt_ref.shape[4]
    nrow = t_chunk * SSM_GROUP

    def tile_rows(x):
        return jnp.broadcast_to(x[None], (t_chunk, SSM_GROUP, nstate)).reshape(nrow, nstate)

    btr = tile_rows(bt_ref[0, 0, 0])
    bti = tile_rows(bt_ref[0, 0, 1])
    bbr = cr * btr - ci * bti
    bbi = cr * bti + ci * btr
    jr = (lax.broadcasted_iota(jnp.int32, (nrow, 1), 0) >> shift).astype(F32)
    lj = jr + df * ((t_chunk - 1) - 2.0 * jr)
    step = lax.broadcasted_iota(jnp.int32, (t_chunk, 1), 0).astype(F32)
    lstep = step + df * ((t_chunk - 1) - 2.0 * step)

    def rep_rows(x):
        return jnp.broadcast_to(x[:, None, :], (t_chunk, SSM_GROUP, nstate)).reshape(nrow, nstate)

    enr, eni = [rep_rows(x) for x in cpow(ar_r, ai_r, -lstep)]
    bmr = enr * bbr - eni * bbi
    bmi = enr * bbi + eni * bbr
    ewr, ewi = [rep_rows(x) for x in cpow(ar_r, ai_r, (t_chunk - 1) - lstep)]
    w_ref[0, 0] = jnp.concatenate([ewr * bbr - ewi * bbi, ewr * bbi + ewi * bbr], axis=1).astype(BF16)
    atr, ati = cpow(ar_r, ai_r, float(t_chunk))
    at_ref[0, 0] = jnp.concatenate([atr, ati], axis=1)

    ar_c = acol_ref[0, 0, :, 0:1]
    ai_c = acol_ref[0, 0, :, 1:2]
    ncol = t_chunk * SSM_GROUP
    lane = lax.broadcasted_iota(jnp.int32, (1, ncol), 1)
    tci = lane >> shift
    tile = jnp.where(lax.broadcasted_iota(jnp.int32, (SSM_GROUP, ncol), 0) == (lane & (SSM_GROUP - 1)),
                     1.0, 0.0).astype(BF16)

    def tile_cols(x):
        hi, lo = _split(x)
        lo2 = (x - hi.astype(F32) - lo.astype(F32)).astype(BF16)
        return _dot(hi, tile) + _dot(lo, tile) + _dot(lo2, tile)

    ctr = tile_cols(ct_ref[0, 0, 0])
    cti = tile_cols(ct_ref[0, 0, 1])
    tc = tci.astype(F32)
    lt = tc + df * ((t_chunk - 1) - 2.0 * tc)
    stepl = lax.broadcasted_iota(jnp.int32, (1, t_chunk), 1).astype(F32)
    lstepl = stepl + df * ((t_chunk - 1) - 2.0 * stepl)
    rep = jnp.where(lax.broadcasted_iota(jnp.int32, (t_chunk, ncol), 0) == tci, 1.0, 0.0).astype(BF16)

    def rep_cols(x):
        hi, lo = _split(x)
        lo2 = (x - hi.astype(F32) - lo.astype(F32)).astype(BF16)
        return _dot(hi, rep) + _dot(lo, rep) + _dot(lo2, rep)

    ecr, eci = [rep_cols(x) for x in cpow(ar_c, ai_c, lstepl)]
    cmr = ctr * ecr - cti * eci
    cmi = ctr * eci + cti * ecr
    lhs_hi, lhs_lo = _split(jnp.concatenate([bmr, -bmi], axis=1))
    rhs_hi, rhs_lo = _split(jnp.concatenate([cmr, cmi], axis=0))
    m = _dot(lhs_hi, rhs_hi) + _dot(lhs_hi, rhs_lo) + _dot(lhs_lo, rhs_hi)
    m_ref[0, 0] = jnp.where(lj <= lt, m, 0.0).astype(BF16)
    abr_c, abi_c = abar(ar_c, ai_c)
    c1r = cmr * abr_c - cmi * abi_c
    c1i = cmr * abi_c + cmi * abr_c
    v_ref[0, 0] = jnp.concatenate([c1r, -c1i], axis=0).astype(BF16)


def _ssm_prep(arow, acol, ldt, bt, ct):
    nd, g = arow.shape[0], arow.shape[1]
    p = SSM_STATE
    n = SSM_CHUNK * SSM_GROUP
    blk = lambda a: pl.BlockSpec((1, 1) + a.shape[2:], lambda d, gi: (d, gi) + (0,) * (a.ndim - 2))
    return pl.pallas_call(
        _ssm_prep_kernel,
        grid=(nd, g),
        in_specs=[blk(arow), blk(acol), blk(ldt), blk(bt), blk(ct)],
        out_specs=[
            pl.BlockSpec((1, 1, n, n), lambda d, gi: (d, gi, 0, 0)),
            pl.BlockSpec((1, 1, n, 2 * p), lambda d, gi: (d, gi, 0, 0)),
            pl.BlockSpec((1, 1, 2 * p, n), lambda d, gi: (d, gi, 0, 0)),
            pl.BlockSpec((1, 1, 1, 2 * p), lambda d, gi: (d, gi, 0, 0)),
        ],
        out_shape=[
            jax.ShapeDtypeStruct((nd, g, n, n), BF16),
            jax.ShapeDtypeStruct((nd, g, n, 2 * p), BF16),
            jax.ShapeDtypeStruct((nd, g, 2 * p, n), BF16),
            jax.ShapeDtypeStruct((nd, g, 1, 2 * p), F32),
        ],
        compiler_params=pltpu.CompilerParams(dimension_semantics=("arbitrary", "arbitrary")),
        name="ssm_prep",
    )(arow, acol, ldt, bt, ct)


def _chunk_carry(z, a, s0, reverse):
    p = SSM_STATE
    nc = z.shape[0]
    row = lax.broadcasted_iota(jnp.int32, z.shape, 0)
    is_re = lax.broadcasted_iota(jnp.int32, a.shape, 1) < p
    sign = jnp.where(is_re, -1.0, 1.0)

    def parts(ap):
        sw = pltpu.roll(ap, p, 1)
        return jnp.where(is_re, ap, sw), sign * jnp.where(is_re, sw, ap)

    def cmul(x, ar_full, ai_sgn):
        return x * ar_full + pltpu.roll(x, p, 1) * ai_sgn

    ar_full, ai_sgn = parts(a)
    first = (nc - 1) if reverse else 0
    e = z + jnp.where(row == first, cmul(s0, ar_full, ai_sgn), 0.0)
    k = 1
    while k < nc:
        if reverse:
            sh = jnp.where(row < nc - k, pltpu.roll(e, nc - k, 0), 0.0)
        else:
            sh = jnp.where(row >= k, pltpu.roll(e, k, 0), 0.0)
        e = e + cmul(sh, ar_full, ai_sgn)
        k *= 2
        if k < nc:
            ar_full, ai_sgn = parts(cmul(jnp.where(is_re, ar_full, sign * ai_sgn), ar_full, ai_sgn))
    if reverse:
        return jnp.where(row == first, s0, pltpu.roll(e, nc - 1, 0))
    return jnp.where(row == first, s0, pltpu.roll(e, 1, 0))


def _ssm_kernel(u_ref, um_ref, m_ref, w_ref, v_ref, at_ref, dvec_ref, y_ref, ug_sc, yg_sc):
    p = SSM_STATE
    t_chunk = SSM_CHUNK
    hgrp = SSM_GROUP
    gb = u_ref.shape[2] // hgrp
    per_tile = u_ref.shape[2] // hgrp
    nc = u_ref.shape[1] // t_chunk
    lane_blk = lax.broadcasted_iota(jnp.int32, (nc, u_ref.shape[2]), 1) >> int(math.log2(hgrp))

    width = u_ref.shape[2]

    def block_transpose(arrs):
        arrs = list(arrs)
        d = len(arrs) // 2
        while d >= 1:
            low_bit = (lane_blk & d) == 0
            for i in range(len(arrs)):
                if i & d == 0:
                    lo, hi = arrs[i], arrs[i + d]
                    arrs[i] = jnp.where(low_bit, lo, pltpu.roll(hi, d * hgrp, 1))
                    arrs[i + d] = jnp.where(low_bit, pltpu.roll(lo, width - d * hgrp, 1), hi)
            d //= 2
        return arrs

    for q in range(t_chunk // per_tile):
        steps = [u_ref[0, pl.ds(q * per_tile + r, nc, stride=t_chunk), :] for r in range(per_tile)]
        for g, tile in enumerate(block_transpose(steps)):
            ug_sc[g, :, q * width:(q + 1) * width] = tile

    def group(g, carry):
        u = ug_sc[g]
        ub = u.astype(BF16)
        y = dvec_ref[g] * u
        for d in range(2):
            yin = _dot(ub, m_ref[d, g])
            z = _dot(ub, w_ref[d, g])
            if d == 0:
                s0 = _dot(um_ref[g].astype(BF16), w_ref[0, g])[0:1, :]
            else:
                s0 = jnp.zeros((1, 2 * p), F32)
            s_in = _chunk_carry(z, at_ref[d, g], s0, reverse=(d == 1))
            y = y + yin + _dot(s_in.astype(BF16), v_ref[d, g])
        yg_sc[g] = y
        return carry

    lax.fori_loop(0, gb, group, 0)

    for q in range(t_chunk // per_tile):
        srcs = [yg_sc[g, :, q * width:(q + 1) * width] for g in range(gb)]
        for r, row in enumerate(block_transpose(srcs)):
            y_ref[0, pl.ds(q * per_tile + r, nc, stride=t_chunk), :] = row


def _ssm(u3, um2, m, w, v, at, dvec):
    b, s, sw = u3.shape
    g = sw // SSM_GROUP
    n = SSM_CHUNK * SSM_GROUP
    p = SSM_STATE
    lanes = 128
    gb = lanes // SSM_GROUP
    return pl.pallas_call(
        _ssm_kernel,
        grid=(b, sw // lanes),
        in_specs=[
            pl.BlockSpec((1, s, lanes), lambda bi, gi: (bi, 0, gi)),
            pl.BlockSpec((gb, 8, n), lambda bi, gi: (gi, 0, 0)),
            pl.BlockSpec((2, gb, n, n), lambda bi, gi: (0, gi, 0, 0)),
            pl.BlockSpec((2, gb, n, 2 * p), lambda bi, gi: (0, gi, 0, 0)),
            pl.BlockSpec((2, gb, 2 * p, n), lambda bi, gi: (0, gi, 0, 0)),
            pl.BlockSpec((2, gb, 1, 2 * p), lambda bi, gi: (0, gi, 0, 0)),
            pl.BlockSpec((gb, 1, n), lambda bi, gi: (gi, 0, 0)),
        ],
        out_specs=pl.BlockSpec((1, s, lanes), lambda bi, gi: (bi, 0, gi)),
        out_shape=jax.ShapeDtypeStruct((b, s, sw), F32),
        scratch_shapes=[pltpu.VMEM((gb, s // SSM_CHUNK, n), F32), pltpu.VMEM((gb, s // SSM_CHUNK, n), F32)],
        compiler_params=pltpu.CompilerParams(dimension_semantics=("arbitrary", "arbitrary"),
                                             vmem_limit_bytes=VMEM_LIMIT),
        name="ssm",
    )(u3, um2, m, w, v, at, dvec)


def _merge_kernel(x_ref, ya_ref, ys_ref, wgt_ref, wglu_ref, bglu_ref, wab_ref, wsb_ref, wo_ref,
                  lng_ref, lnb_ref, l1g_ref, l1b_ref, wr_ref, wrt_ref,
                  h1_ref, tok_ref, afft_ref):
    d = x_ref.shape[1]
    h = _ln(x_ref[...], lng_ref[...], lnb_ref[...])
    gates = _dot(h.astype(BF16), wgt_ref[...])
    ga = _sigmoid(gates[:, 0:d])
    gs = _sigmoid(gates[:, d:2 * d])
    ys = ys_ref[...]
    y = 0.5 * ys * (1.0 + jnp.tanh(math.sqrt(2.0 / math.pi) * (ys + 0.044715 * (ys * ys * ys))))
    yg = y * _sigmoid(_dot(y.astype(BF16), wglu_ref[...]) + bglu_ref[...])
    merged = ga * _dot(ya_ref[...], wab_ref[...]) + gs * _dot(yg.astype(BF16), wsb_ref[...])
    h1 = _ln(DEEPNORM_ALPHA * h + _dot(merged.astype(BF16), wo_ref[...]), l1g_ref[...], l1b_ref[...])
    h1_ref[...] = h1
    hi, lo = _split(h1)
    ne = afft_ref.shape[0]
    logits = _dot(hi, wr_ref[0]) + _dot(hi, wr_ref[1]) + _dot(lo, wr_ref[0])
    lane = lax.broadcasted_iota(jnp.int32, logits.shape, 1)
    logits = jnp.where(lane < 3 * ne, logits, -jnp.inf)
    ex = jnp.exp(logits - jnp.max(logits, axis=1, keepdims=True))
    aff = ex / jnp.sum(jnp.where(lane < ne, ex, 0.0), axis=1, keepdims=True)
    p0 = aff.astype(BF16)
    r1 = aff - p0.astype(F32)
    p1 = r1.astype(BF16)
    p2 = (r1 - p1.astype(F32)).astype(BF16)
    pieces = jnp.where(lane < ne, p0, jnp.where(lane < 2 * ne, p1, p2))
    tok_ref[...] = jnp.concatenate([hi, pieces], axis=1)
    lt = _dot_nt(wrt_ref[0], hi) + _dot_nt(wrt_ref[1], hi) + _dot_nt(wrt_ref[0], lo)
    et = jnp.exp(lt - jnp.max(lt, axis=0, keepdims=True))
    afft_ref[...] = et / jnp.sum(et, axis=0, keepdims=True)


def _merge(x2, ya, ys, wgt, wglu, bglu, wab, wsb, wo, lng, lnb, l1g, l1b, wr, wrt, tm):
    rows, d = x2.shape
    ne = wrt.shape[1]
    full = lambda a: pl.BlockSpec(a.shape, lambda i: (0,) * a.ndim)
    return pl.pallas_call(
        _merge_kernel,
        grid=(rows // tm,),
        in_specs=[
            pl.BlockSpec((tm, d), lambda i: (i, 0)),
            pl.BlockSpec((tm, ya.shape[1]), lambda i: (i, 0)),
            pl.BlockSpec((tm, ys.shape[1]), lambda i: (i, 0)),
            full(wgt), full(wglu), full(bglu), full(wab), full(wsb), full(wo),
            full(lng), full(lnb), full(l1g), full(l1b), full(wr), full(wrt),
        ],
        out_specs=[
            pl.BlockSpec((tm, d), lambda i: (i, 0)),
            pl.BlockSpec((tm, d + GATE_LANES), lambda i: (i, 0)),
            pl.BlockSpec((ne, tm), lambda i: (0, i)),
        ],
        out_shape=[
            jax.ShapeDtypeStruct((rows, d), F32),
            jax.ShapeDtypeStruct((rows, d + GATE_LANES), BF16),
            jax.ShapeDtypeStruct((ne, rows), F32),
        ],
        compiler_params=pltpu.CompilerParams(dimension_semantics=("arbitrary",),
                                             vmem_limit_bytes=VMEM_LIMIT),
        name="merge",
    )(x2, ya, ys, wgt, wglu, bglu, wab, wsb, wo, lng, lnb, l1g, l1b, wr, wrt)


def _route_kernel(afft_ref, tri_ref, scl_ref, lo_ref, *, cap):
    aff = afft_ref[...]
    ne, s = aff.shape
    capf = float(cap)

    def as_float(bits):
        return lax.bitcast_convert_type(bits, F32)

    def search(i, t):
        cand = t | jnp.left_shift(jnp.int32(1), 30 - i)
        cnt = jnp.sum(jnp.where(aff >= as_float(cand), 1.0, 0.0), axis=1, keepdims=True)
        return jnp.where(cnt >= capf, cand, t)

    thr_bits = lax.fori_loop(0, 31, search, jnp.zeros((ne, 1), jnp.int32))
    gt = aff >= as_float(thr_bits + 1)
    eq = (aff >= as_float(thr_bits)) & jnp.logical_not(gt)
    need = capf - jnp.sum(jnp.where(gt, 1.0, 0.0), axis=1, keepdims=True)
    tri = tri_ref[...]
    rt = ROUTE_TILE
    nt = s // rt
    col = lax.broadcasted_iota(jnp.int32, (ne, nt), 1)
    carry_eq = jnp.zeros((ne, 1), F32)
    carry_sel = jnp.zeros((ne, 1), F32)
    lo_val = jnp.zeros((ne, nt), F32)
    for t in range(nt):
        sl = slice(t * rt, (t + 1) * rt)
        eq_b = eq[:, sl]
        ceq = _dot(jnp.where(eq_b, 1.0, 0.0).astype(BF16), tri)
        sel_b = gt[:, sl] | (eq_b & ((ceq + carry_eq) <= need))
        carry_eq = carry_eq + ceq[:, rt - 1:rt]
        csel = _dot(jnp.where(sel_b, 1.0, 0.0).astype(BF16), tri)
        scl_ref[0, :, sl] = jnp.where(sel_b, csel, 0.0)
        lo_val = jnp.where(col == t, carry_sel, lo_val)
        carry_sel = carry_sel + csel[:, rt - 1:rt]
    lo_ref[0] = lo_val


def _route(afft, tri, nb, cap):
    ne, rows = afft.shape
    s = rows // nb
    nt = s // ROUTE_TILE
    return pl.pallas_call(
        functools.partial(_route_kernel, cap=cap),
        grid=(nb,),
        in_specs=[
            pl.BlockSpec((ne, s), lambda b: (0, b)),
            pl.BlockSpec(tri.shape, lambda b: (0, 0)),
        ],
        out_specs=[
            pl.BlockSpec((1, ne, s), lambda b: (b, 0, 0)),
            pl.BlockSpec((1, ne, nt), lambda b: (b, 0, 0)),
        ],
        out_shape=[
            jax.ShapeDtypeStruct((nb, ne, s), F32),
            jax.ShapeDtypeStruct((nb, ne, nt), F32),
        ],
        compiler_params=pltpu.CompilerParams(dimension_semantics=("arbitrary",)),
        name="route",
    )(afft, tri)


def _ffn_kernel(lo_ref, t_ref, scl_ref, wg_ref, wu_ref, wd_ref, ye_ref, xe_sc, xb_sc, gate_sc, acc_sc, *, cap):
    fc = pl.program_id(2)

    @pl.when(fc == 0)
    def _gather():
        _ffn_gather(lo_ref, t_ref, scl_ref, xe_sc, xb_sc, gate_sc, cap=cap)
        acc_sc[...] = jnp.zeros(acc_sc.shape, F32)

    xb = xb_sc[...]
    g = _dot(xb, wg_ref[0, 0].astype(BF16))
    u = _dot(xb, wu_ref[0, 0].astype(BF16))
    hh = (g * _sigmoid(g)) * u
    acc_sc[...] += _dot(hh.astype(BF16), wd_ref[0, 0].astype(BF16))

    @pl.when(fc == pl.num_programs(2) - 1)
    def _emit():
        ye_ref[0, 0] = (acc_sc[...] * gate_sc[...]).astype(BF16)


def _ffn_gather(lo_ref, t_ref, scl_ref, xe_sc, xb_sc, gate_sc, *, cap):
    b = pl.program_id(0)
    e = pl.program_id(1)
    ne = pl.num_programs(1)
    rt = ROUTE_TILE
    nt = t_ref.shape[1] // rt
    xe_sc[...] = jnp.zeros(xe_sc.shape, F32)
    r = lax.broadcasted_iota(jnp.int32, (PIECE, rt), 0).astype(F32)

    def tile_info(tau):
        base = (b * ne + e) * (nt + 1) + tau
        lo = lo_ref[base]
        n_pieces = jnp.right_shift((lo & 7) + (lo_ref[base + 1] - lo) + (PIECE - 1), PIECE_SHIFT)
        return lo, n_pieces

    def add_piece(tau, lo, p):
        off = lo & 7
        scl = scl_ref[0, 0, pl.ds(tau, 1), :]
        tok = t_ref[0, pl.ds(pl.multiple_of(tau * rt, rt), rt), :]
        shift = (off - 1 - p * PIECE).astype(F32)
        onehot = jnp.where((scl > 0.0) & ((scl + shift) == r), 1.0, 0.0).astype(BF16)
        xe_sc[pl.ds(pl.multiple_of(lo - off + p * PIECE, 8), PIECE), :] += _dot(onehot, tok)

    def first_piece(tau, carry):
        add_piece(tau, tile_info(tau)[0], 0)
        return carry

    def more_pieces(tau, carry):
        lo, n_pieces = tile_info(tau)
        lax.fori_loop(1, n_pieces, lambda p, c: (add_piece(tau, lo, p), c)[1], 0)
        return carry

    lax.fori_loop(0, nt, first_piece, 0, unroll=4)
    lax.fori_loop(0, nt, more_pieces, 0)
    d = xb_sc.shape[1]
    xb_sc[...] = xe_sc[0:cap, 0:d].astype(BF16)
    gl = xe_sc[0:cap, d:d + GATE_LANES]
    lane = lax.broadcasted_iota(jnp.int32, gl.shape, 1)
    mine = ((lane & (N_EXPERTS - 1)) == e) & (lane < 3 * N_EXPERTS)
    gate_sc[...] = jnp.sum(jnp.where(mine, gl, 0.0), axis=1, keepdims=True)


def _ffn(lo_i, t, scl4, wg, wu, wd, layer, cap, f_chunk):
    b, s, dx = t.shape
    _, ne, d, f = wg.shape
    nt = s // ROUTE_TILE
    grid_spec = pltpu.PrefetchScalarGridSpec(
        num_scalar_prefetch=1,
        grid=(b, ne, f // f_chunk),
        in_specs=[
            pl.BlockSpec((1, s, dx), lambda bi, ei, fi, lo: (bi, 0, 0), pipeline_mode=pl.Buffered(1)),
            pl.BlockSpec((1, 1, nt, ROUTE_TILE), lambda bi, ei, fi, lo: (bi, ei, 0, 0)),
            pl.BlockSpec((1, 1, d, f_chunk), lambda bi, ei, fi, lo: (layer, ei, 0, fi)),
            pl.BlockSpec((1, 1, d, f_chunk), lambda bi, ei, fi, lo: (layer, ei, 0, fi)),
            pl.BlockSpec((1, 1, f_chunk, d), lambda bi, ei, fi, lo: (layer, ei, fi, 0)),
        ],
        out_specs=pl.BlockSpec((1, 1, cap, d), lambda bi, ei, fi, lo: (bi, ei, 0, 0)),
        scratch_shapes=[pltpu.VMEM((cap + PIECE, dx), F32), pltpu.VMEM((cap, d), BF16),
                        pltpu.VMEM((cap, 1), F32), pltpu.VMEM((cap, d), F32)],
    )
    return pl.pallas_call(
        functools.partial(_ffn_kernel, cap=cap),
        grid_spec=grid_spec,
        out_shape=jax.ShapeDtypeStruct((b, ne, cap, d), BF16),
        compiler_params=pltpu.CompilerParams(dimension_semantics=("arbitrary", "arbitrary", "arbitrary"),
                                             vmem_limit_bytes=VMEM_LIMIT),
        name="ffn",
    )(lo_i, t, scl4, wg, wu, wd)


def _combine_kernel(lo_ref, h1_ref, sclt_ref, ye_hbm, g_ref, b_ref, o_ref, win, xwin, acc_sc, sem, xsem, *, cap):
    b = pl.program_id(0)
    tau = pl.program_id(1)
    nb = pl.num_programs(0)
    nt = pl.num_programs(1)
    ne = win.shape[1]
    d = win.shape[3]
    step = b * nt + tau
    slot = step & 1

    def pieces(bi, ti, e):
        base = (bi * ne + e) * (nt + 1) + ti
        lo = lo_ref[base]
        n_sel = lo_ref[base + 1] - lo
        n_pieces = jnp.maximum(jnp.right_shift((lo & 15) + n_sel + (PIECE - 1), PIECE_SHIFT), 1)
        first = jnp.minimum(lo - (lo & 15), cap - n_pieces * PIECE)
        return lo, first, n_pieces

    def copy(bi, e, first, p, buf, s):
        start = pl.multiple_of(first + p * PIECE, 16)
        return pltpu.make_async_copy(ye_hbm.at[bi, e, pl.ds(start, PIECE), :], buf, s)

    def first_pieces(bi, ti, sl):
        return [copy(bi, e, pieces(bi, ti, e)[1], 0, win.at[sl, e], sem.at[sl, e]) for e in range(ne)]

    @pl.when(step == 0)
    def _prime():
        for cp in first_pieces(b, tau, slot):
            cp.start()

    @pl.when(step + 1 < nb * nt)
    def _prefetch():
        wrap = tau + 1 == nt
        for cp in first_pieces(jnp.where(wrap, b + 1, b), jnp.where(wrap, 0, tau + 1), 1 - slot):
            cp.start()

    for cp in first_pieces(b, tau, slot):
        cp.wait()

    sclt = sclt_ref[0]
    col = lax.broadcasted_iota(jnp.int32, (1, ne * PIECE), 1)
    grp = jnp.right_shift(col, PIECE_SHIFT)
    expand = jnp.where(lax.broadcasted_iota(jnp.int32, (ne, ne * PIECE), 0) == grp, 1.0, 0.0).astype(BF16)
    scl = _dot(sclt.astype(BF16), expand)
    shift = jnp.zeros(col.shape, F32)
    for e in range(ne):
        lo, first, _ = pieces(b, tau, e)
        shift = jnp.where(grp == e, (lo - first - 1).astype(F32), shift)
    r = (col & (PIECE - 1)).astype(F32)
    onehot = jnp.where((scl > 0.0) & ((scl + shift) == r), 1.0, 0.0).astype(BF16)
    acc_sc[...] = DEEPNORM_ALPHA * h1_ref[0] + _dot(onehot, win[slot].reshape(ne * PIECE, d))

    for e in range(ne):
        lo, first, n_pieces = pieces(b, tau, e)

        def extra(p, c, e=e, lo=lo, first=first):
            cp = copy(b, e, first, p, xwin, xsem.at[0])
            cp.start()
            cp.wait()
            se = sclt_ref[0][:, e:e + 1]
            rr = lax.broadcasted_iota(jnp.int32, (se.shape[0], PIECE), 1).astype(F32)
            oh = jnp.where((se > 0.0) & ((se + (lo - first - p * PIECE - 1).astype(F32)) == rr), 1.0, 0.0)
            acc_sc[...] += _dot(oh.astype(BF16), xwin[...])
            return c

        lax.fori_loop(1, n_pieces, extra, 0)
    o_ref[0] = _ln(acc_sc[...], g_ref[...], b_ref[...])


def _combine(lo_i, h1, sclt, ye, g, bb, cap):
    b, s, d = h1.shape
    ne = sclt.shape[2]
    rt = ROUTE_TILE
    grid_spec = pltpu.PrefetchScalarGridSpec(
        num_scalar_prefetch=1,
        grid=(b, s // rt),
        in_specs=[
            pl.BlockSpec((1, rt, d), lambda bi, ti, lo: (bi, ti, 0)),
            pl.BlockSpec((1, rt, ne), lambda bi, ti, lo: (bi, ti, 0)),
            pl.BlockSpec(memory_space=pl.ANY),
            pl.BlockSpec(g.shape, lambda bi, ti, lo: (0, 0)),
            pl.BlockSpec(bb.shape, lambda bi, ti, lo: (0, 0)),
        ],
        out_specs=pl.BlockSpec((1, rt, d), lambda bi, ti, lo: (bi, ti, 0)),
        scratch_shapes=[pltpu.VMEM((2, ne, PIECE, d), BF16), pltpu.VMEM((PIECE, d), BF16),
                        pltpu.VMEM((rt, d), F32),
                        pltpu.SemaphoreType.DMA((2, ne)), pltpu.SemaphoreType.DMA((1,))],
    )
    return pl.pallas_call(
        functools.partial(_combine_kernel, cap=cap),
        grid_spec=grid_spec,
        out_shape=jax.ShapeDtypeStruct((b, s, d), F32),
        compiler_params=pltpu.CompilerParams(dimension_semantics=("arbitrary", "arbitrary"),
                                             vmem_limit_bytes=VMEM_LIMIT),
        name="combine",
    )(lo_i, h1, sclt, ye, g, bb)


def _rope_tables(n_tokens):
    half = HEAD_DIM // 2
    inv_freq = ROPE_THETA ** (-jnp.arange(0, half, 2, dtype=F32) / half)
    rows = n_tokens // GRID_W
    row = jnp.repeat(jnp.arange(rows, dtype=F32), GRID_W)
    colv = jnp.tile(jnp.arange(GRID_W, dtype=F32), rows)
    ang = jnp.concatenate([row[:, None] * inv_freq, colv[:, None] * inv_freq], axis=-1)
    return jnp.cos(ang), jnp.sin(ang)


def kernel(x, meta_tokens, ln_in_g, ln_in_b, w_in, q_norm_g, k_norm_g, ssm_a_re, ssm_a_im, ssm_log_dt,
           ssm_b_re, ssm_b_im, ssm_c_re, ssm_c_im, ssm_d, w_glu, b_glu, w_attn_br, w_ssm_br, w_o,
           ln1_g, ln1_b, w_router, w_gate_e, w_up_e, w_down_e, ln2_g, ln2_b):
    b, s, d = x.shape
    aw = N_HEADS * HEAD_DIM
    kw = N_KV_HEADS * HEAD_DIM
    sw = d // 2
    g = sw // SSM_GROUP
    half = HEAD_DIM // 2
    cap = CAPACITY_FACTOR * s // N_EXPERTS
    rows = b * s
    l = 0

    perm = np.concatenate([np.arange(0, HEAD_DIM, 2), np.arange(1, HEAD_DIM, 2)])
    qcols = np.concatenate([h * HEAD_DIM + perm for h in range(N_HEADS)])
    kcols = aw + np.concatenate([h * HEAD_DIM + perm for h in range(N_KV_HEADS)])
    wl = w_in[l]
    w_t = jnp.concatenate([wl[:, qcols], wl[:, aw + kw:aw + 2 * kw]], axis=1).T.astype(BF16)
    w_n = jnp.concatenate([wl[:, kcols], wl[:, aw + 2 * kw:aw + 2 * kw + sw]], axis=1).astype(BF16)
    w_gates = wl[:, aw + 2 * kw + sw:].astype(BF16)
    qg3 = (jnp.tile(q_norm_g[l][perm], N_HEADS) * (HEAD_DIM ** -0.5 * math.log2(math.e))).reshape(N_HEADS, HEAD_DIM, 1)
    kg = jnp.tile(k_norm_g[l][perm], N_KV_HEADS)[None, :]
    bd = jnp.asarray(np.kron(np.eye(N_KV_HEADS), np.full((HEAD_DIM, HEAD_DIM), 1.0 / HEAD_DIM)), BF16)
    lng = ln_in_g[None, :]
    lnb = ln_in_b[None, :]
    cos, sin = _rope_tables(s)
    cosr = jnp.tile(cos, (1, 2 * N_KV_HEADS))
    sinr = jnp.tile(jnp.concatenate([-sin, sin], axis=1), (1, N_KV_HEADS))
    cost = cos.T
    sint = sin.T

    x2 = x.reshape(rows, d)
    tm_in = 512
    qt, kk, vt, u2 = _inproj(x2, cosr, sinr, cost, sint, w_t, w_n, lng, lnb, qg3, kg, bd, tm_in, s // tm_in)
    meta_p = jnp.pad(meta_tokens, ((0, META_PAD - N_META), (0, 0)))
    ones_r = jnp.ones((META_PAD, kw), F32)
    _, km, vtm, um = _inproj(meta_p, ones_r, jnp.zeros_like(ones_r), jnp.ones((half, META_PAD), F32),
                             jnp.zeros((half, META_PAD), F32), w_t, w_n, lng, lnb, qg3, kg, bd, META_PAD, 1)
    um = um[:N_META]
    km = km[:, :N_META, :]
    vtm = vtm[:, :, :N_META]

    ya = _attention(qt, kk, vt, km, vtm, b, 512, 512)

    t_chunk = SSM_CHUNK
    hgrp = SSM_GROUP
    n = t_chunk * hgrp
    arow = jnp.stack([ssm_a_re[l], ssm_a_im[l]], axis=2)
    acol = jnp.stack([ssm_a_re[l], ssm_a_im[l]], axis=3)
    ldt = ssm_log_dt[l][:, :, None, None]
    bt = jnp.stack([ssm_b_re[l], ssm_b_im[l]], axis=2)
    bt = jnp.swapaxes(bt, 3, 4)
    ct = jnp.stack([ssm_c_re[l], ssm_c_im[l]], axis=2)
    ct = jnp.swapaxes(ct, 3, 4)
    mm, ww, vv, at = _ssm_prep(arow, acol, ldt, bt, ct)
    nc = s // t_chunk
    um_g = um.reshape(N_META, g, hgrp).transpose(1, 0, 2).reshape(g, 1, N_META * hgrp)
    um_g = jnp.pad(um_g, ((0, 0), (0, 7), (n - N_META * hgrp, 0)))
    dvec = jnp.tile(ssm_d[l], (1, t_chunk))[:, None, :]
    ys = _ssm(u2.reshape(b, s, sw), um_g, mm, ww, vv, at, dvec).reshape(rows, sw)

    wr_hi = w_router[l].astype(BF16)
    wr_lo = (w_router[l] - wr_hi.astype(F32)).astype(BF16)
    wrt = jnp.stack([wr_hi.T, wr_lo.T])
    wr = jnp.pad(jnp.tile(jnp.stack([wr_hi, wr_lo]), (1, 1, 3)), ((0, 0), (0, 0), (0, GATE_LANES - 3 * N_EXPERTS)))
    h1, tok, afft = _merge(
        x2, ya, ys, w_gates, w_glu[l].astype(BF16), b_glu[l][None, :],
        w_attn_br[l].astype(BF16), w_ssm_br[l].astype(BF16), w_o[l].astype(BF16),
        lng, lnb, ln1_g[l][None, :], ln1_b[l][None, :], wr, wrt, 512)

    tri = jnp.asarray(np.triu(np.ones((ROUTE_TILE, ROUTE_TILE), np.float32)), BF16)
    scl, lo_f = _route(afft, tri, b, cap)
    nt = s // ROUTE_TILE
    lo_i = jnp.pad(lo_f.astype(jnp.int32), ((0, 0), (0, 0), (0, 1)), constant_values=cap).reshape(-1)
    ye = _ffn(lo_i, tok.reshape(b, s, d + GATE_LANES), scl.reshape(b, N_EXPERTS, nt, ROUTE_TILE),
              w_gate_e, w_up_e, w_down_e, l, cap, 512)
    out = _combine(lo_i, h1.reshape(b, s, d), jnp.swapaxes(scl, 1, 2), ye,
                   ln2_g[l][None, :], ln2_b[l][None, :], cap)
    return out
```

```python
import functools
import math

import numpy as np
import jax
import jax.numpy as jnp
from jax import lax
from jax.experimental import pallas as pl
from jax.experimental.pallas import tpu as pltpu

F32 = jnp.float32
BF16 = jnp.bfloat16

N_META = 16
GRID_W = 64
N_HEADS = 8
N_KV_HEADS = 2
HEAD_DIM = 64
ROPE_THETA = 10000.0
SSM_GROUP = 16
SSM_STATE = 64
N_EXPERTS = 16
CAPACITY_FACTOR = 2
LN_EPS = 1e-5
QK_EPS = 1e-6
DEPTH = 1
DEEPNORM_ALPHA = (2.0 * DEPTH) ** 0.25

SSM_CHUNK = 32
SSM_GROUPS_PER_STEP = 2
ROUTE_TILE = 256
GATE_LANES = 128
PIECE_SHIFT = 6
PIECE = 1 << PIECE_SHIFT
META_PAD = 128
V_ROWS = HEAD_DIM + 16
VMEM_LIMIT = 56 * 1024 * 1024


def _ln(x, g, b):
    mu = jnp.mean(x, axis=-1, keepdims=True)
    xc = x - mu
    var = jnp.mean(xc * xc, axis=-1, keepdims=True)
    return xc * lax.rsqrt(var + LN_EPS) * g + b


def _sigmoid(x):
    return 1.0 / (1.0 + jnp.exp(-x))


def _split(t):
    hi = t.astype(BF16)
    lo = (t - hi.astype(F32)).astype(BF16)
    return hi, lo


def _dot(a, b):
    return jnp.dot(a, b, preferred_element_type=F32)


def _dot_nt(a, b):
    return lax.dot_general(a, b, (((1,), (1,)), ((), ())), preferred_element_type=F32)


def _inproj_kernel(x_ref, cosr_ref, sinr_ref, cost_ref, sint_ref, wt_ref, wn_ref, lng_ref, lnb_ref,
                   qg_ref, kg_ref, bd_ref, qt_ref, k_ref, vt_ref, u_ref):
    aw = N_HEADS * HEAD_DIM
    kw = N_KV_HEADS * HEAD_DIM
    half = HEAD_DIM // 2
    h = _ln(x_ref[...], lng_ref[...], lnb_ref[...])
    hb = h.astype(BF16)
    pt = _dot_nt(wt_ref[...], hb)
    pn = _dot(hb, wn_ref[...])
    tm = hb.shape[0]

    qt = pt[0:aw].reshape(N_HEADS, HEAD_DIM, tm)
    ms = jnp.mean(qt * qt, axis=1, keepdims=True)
    qn = qt * lax.rsqrt(ms + QK_EPS) * qg_ref[...]
    x0 = qn[:, 0:half, :]
    x1 = qn[:, half:, :]
    c = cost_ref[...][None]
    s = sint_ref[...][None]
    qr = jnp.concatenate([x0 * c - x1 * s, x0 * s + x1 * c], axis=1)
    qt_ref[...] = qr.reshape(aw, tm).astype(BF16)
    vrow = lax.broadcasted_iota(jnp.int32, (N_KV_HEADS, V_ROWS - HEAD_DIM, tm), 1)
    vt_ref[...] = jnp.concatenate([pt[aw:aw + kw].reshape(N_KV_HEADS, HEAD_DIM, tm),
                                   jnp.where(vrow == 0, 1.0, 0.0)], axis=1).astype(BF16)

    kk = pn[:, 0:kw]
    hi, lo = _split(kk * kk)
    bd = bd_ref[...]
    msk = _dot(hi, bd) + _dot(lo, bd)
    kn = kk * lax.rsqrt(msk + QK_EPS) * kg_ref[...]
    lane = lax.broadcasted_iota(jnp.int32, kn.shape, 1)
    first = (lane & (HEAD_DIM - 1)) < half
    partner = jnp.where(first, pltpu.roll(kn, kw - half, 1), pltpu.roll(kn, half, 1))
    kr = (kn * cosr_ref[...] + partner * sinr_ref[...]).astype(BF16)
    for g in range(N_KV_HEADS):
        k_ref[g] = kr[:, g * HEAD_DIM:(g + 1) * HEAD_DIM]
    u_ref[...] = pn[:, kw:]


def _inproj(x2, cosr, sinr, cost, sint, wt, wn, lng, lnb, qg3, kg, bd, tm, n_tab_blocks):
    rows, d = x2.shape
    aw = N_HEADS * HEAD_DIM
    kw = N_KV_HEADS * HEAD_DIM
    uw = wn.shape[1] - kw
    half = HEAD_DIM // 2
    full = lambda a: pl.BlockSpec(a.shape, lambda i: (0,) * a.ndim)
    return pl.pallas_call(
        _inproj_kernel,
        grid=(rows // tm,),
        in_specs=[
            pl.BlockSpec((tm, d), lambda i: (i, 0)),
            pl.BlockSpec((tm, kw), lambda i: (i % n_tab_blocks, 0)),
            pl.BlockSpec((tm, kw), lambda i: (i % n_tab_blocks, 0)),
            pl.BlockSpec((half, tm), lambda i: (0, i % n_tab_blocks)),
            pl.BlockSpec((half, tm), lambda i: (0, i % n_tab_blocks)),
            full(wt), full(wn), full(lng), full(lnb), full(qg3), full(kg), full(bd),
        ],
        out_specs=[
            pl.BlockSpec((aw, tm), lambda i: (0, i)),
            pl.BlockSpec((N_KV_HEADS, tm, HEAD_DIM), lambda i: (0, i, 0)),
            pl.BlockSpec((N_KV_HEADS, V_ROWS, tm), lambda i: (0, 0, i)),
            pl.BlockSpec((tm, uw), lambda i: (i, 0)),
        ],
        out_shape=[
            jax.ShapeDtypeStruct((aw, rows), BF16),
            jax.ShapeDtypeStruct((N_KV_HEADS, rows, HEAD_DIM), BF16),
            jax.ShapeDtypeStruct((N_KV_HEADS, V_ROWS, rows), BF16),
            jax.ShapeDtypeStruct((rows, uw), F32),
        ],
        compiler_params=pltpu.CompilerParams(dimension_semantics=("arbitrary",),
                                             vmem_limit_bytes=VMEM_LIMIT),
        name="inproj",
    )(x2, cosr, sinr, cost, sint, wt, wn, lng, lnb, qg3, kg, bd)


def _attn_kernel(qt_ref, k0_ref, k1_ref, vta_ref, vtb_ref, km_ref, vtm_ref, o_ref, m_sc, acc_sc, s_sc, mb_sc):
    j = pl.program_id(2)
    last = pl.num_programs(2) - 1
    grp = N_HEADS // N_KV_HEADS
    heads = lambda h: slice(h * HEAD_DIM, (h + 1) * HEAD_DIM)

    def score(k_ref, slot, h):
        s = _dot(k_ref[h // grp], qt_ref[heads(h), :])
        s_sc[slot, h] = s
        mb_sc[slot, h:h + 1, :] = jnp.max(s, axis=0, keepdims=True)

    def softmax_pv(h, s, m_blk, vt_g):
        m_prev = m_sc[h:h + 1, :]
        m_new = jnp.maximum(m_prev, m_blk)
        alpha = jnp.exp2(m_prev - m_new)
        p = jnp.exp2(s - m_new).astype(BF16)
        acc_sc[h] = alpha * acc_sc[h] + _dot(vt_g, p)
        m_sc[h:h + 1, :] = m_new

    def consume(vt_ref, slot, h):
        softmax_pv(h, s_sc[slot, h], mb_sc[slot, h:h + 1, :], vt_ref[h // grp])

    @pl.when(j == 0)
    def _first():
        m_sc[...] = jnp.full(m_sc.shape, -jnp.inf, F32)
        acc_sc[...] = jnp.zeros(acc_sc.shape, F32)
        for h in range(N_HEADS):
            s = _dot(km_ref[h // grp], qt_ref[heads(h), :])
            softmax_pv(h, s, jnp.max(s, axis=0, keepdims=True), vtm_ref[h // grp])
            score(k0_ref, 0, h)
        for h in range(N_HEADS):
            score(k1_ref, 1, h)
            consume(vtb_ref, 0, h)

    @pl.when((j > 0) & (j < last))
    def _middle():
        for h in range(N_HEADS):
            score(k0_ref, 0, h)
            consume(vta_ref, 1, h)
        for h in range(N_HEADS):
            score(k1_ref, 1, h)
            consume(vtb_ref, 0, h)

    @pl.when(j == last)
    def _last():
        for h in range(N_HEADS):
            consume(vta_ref, 1, h)
        acc = acc_sc[...]
        tq = acc.shape[2]
        out_t = (acc[:, 0:HEAD_DIM, :] / acc[:, HEAD_DIM:HEAD_DIM + 1, :]).reshape(N_HEADS * HEAD_DIM, tq)
        o_ref[...] = out_t.T.astype(BF16)


def _attention(qt, kk, vt, km, vtm, nb, tq, tk):
    aw, rows = qt.shape
    s = rows // nb
    nq = s // tq
    nkb = s // tk
    kblk = lambda f: pl.BlockSpec((N_KV_HEADS, tk, HEAD_DIM),
                                  lambda bi, qi, j: (0, bi * nkb + jnp.clip(f(j), 0, nkb - 1), 0))
    vblk = lambda f: pl.BlockSpec((N_KV_HEADS, V_ROWS, tk),
                                  lambda bi, qi, j: (0, 0, bi * nkb + jnp.clip(f(j), 0, nkb - 1)))
    return pl.pallas_call(
        _attn_kernel,
        grid=(nb, nq, nkb // 2 + 1),
        in_specs=[
            pl.BlockSpec((aw, tq), lambda bi, qi, j: (0, bi * nq + qi)),
            kblk(lambda j: 2 * j), kblk(lambda j: 2 * j + 1),
            vblk(lambda j: 2 * j - 1), vblk(lambda j: 2 * j),
            pl.BlockSpec(km.shape, lambda bi, qi, j: (0, 0, 0)),
            pl.BlockSpec(vtm.shape, lambda bi, qi, j: (0, 0, 0)),
        ],
        out_specs=pl.BlockSpec((tq, aw), lambda bi, qi, j: (bi * nq + qi, 0)),
        out_shape=jax.ShapeDtypeStruct((rows, aw), BF16),
        scratch_shapes=[
            pltpu.VMEM((N_HEADS, tq), F32),
            pltpu.VMEM((N_HEADS, V_ROWS, tq), F32),
            pltpu.VMEM((2, N_HEADS, tk, tq), F32),
            pltpu.VMEM((2, N_HEADS, tq), F32),
        ],
        compiler_params=pltpu.CompilerParams(
            dimension_semantics=("arbitrary", "arbitrary", "arbitrary"),
            vmem_limit_bytes=VMEM_LIMIT),
        name="attn",
    )(qt, kk, kk, vt, vt, km, vtm)


def _ssm_prep_kernel(arow_ref, acol_ref, ldt_ref, bt_ref, ct_ref, m_ref, w_ref, v_ref, at_ref):
    t_chunk = SSM_CHUNK
    shift = int(math.log2(SSM_GROUP))
    df = pl.program_id(0).astype(F32)
    dt = jnp.exp(ldt_ref[0, 0])

    def abar(ar, ai):
        mag = jnp.exp(ar * dt)
        ang = ai * dt
        return mag * jnp.cos(ang), mag * jnp.sin(ang)

    def cpow(ar, ai, e):
        mag = jnp.exp(ar * dt * e)
        ang = ai * dt * e
        return mag * jnp.cos(ang), mag * jnp.sin(ang)

    ar_r = arow_ref[0, 0, 0:1, :]
    ai_r = arow_ref[0, 0, 1:2, :]
    abr, abi = abar(ar_r, ai_r)
    nr = abr - 1.0
    ni = abi
    den = ar_r * ar_r + ai_r * ai_r
    cr = (nr * ar_r + ni * ai_r) / den
    ci = (ni * ar_r - nr * ai_r) / den
    nstate = bt_ref.shape[4]
    nrow = t_chunk * SSM_GROUP

    def tile_rows(x):
        return jnp.broadcast_to(x[None], (t_chunk, SSM_GROUP, nstate)).reshape(nrow, nstate)

    btr = tile_rows(bt_ref[0, 0, 0])
    bti = tile_rows(bt_ref[0, 0, 1])
    bbr = cr * btr - ci * bti
    bbi = cr * bti + ci * btr
    jr = (lax.broadcasted_iota(jnp.int32, (nrow, 1), 0) >> shift).astype(F32)
    lj = jr + df * ((t_chunk - 1) - 2.0 * jr)
    step = lax.broadcasted_iota(jnp.int32, (t_chunk, 1), 0).astype(F32)
    lstep = step + df * ((t_chunk - 1) - 2.0 * step)

    def rep_rows(x):
        return jnp.broadcast_to(x[:, None, :], (t_chunk, SSM_GROUP, nstate)).reshape(nrow, nstate)

    enr, eni = [rep_rows(x) for x in cpow(ar_r, ai_r, -lstep)]
    bmr = enr * bbr - eni * bbi
    bmi = enr * bbi + eni * bbr
    ewr, ewi = [rep_rows(x) for x in cpow(ar_r, ai_r, (t_chunk - 1) - lstep)]
    w_ref[0, 0] = jnp.concatenate([ewr * bbr - ewi * bbi, ewr * bbi + ewi * bbr], axis=1).astype(BF16)
    atr, ati = cpow(ar_r, ai_r, float(t_chunk))
    at_ref[0, 0] = jnp.concatenate([atr, ati], axis=1)

    ar_c = acol_ref[0, 0, :, 0:1]
    ai_c = acol_ref[0, 0, :, 1:2]
    ncol = t_chunk * SSM_GROUP
    lane = lax.broadcasted_iota(jnp.int32, (1, ncol), 1)
    tci = lane >> shift
    tile = jnp.where(lax.broadcasted_iota(jnp.int32, (SSM_GROUP, ncol), 0) == (lane & (SSM_GROUP - 1)),
                     1.0, 0.0).astype(BF16)

    def tile_cols(x):
        hi, lo = _split(x)
        lo2 = (x - hi.astype(F32) - lo.astype(F32)).astype(BF16)
        return _dot(hi, tile) + _dot(lo, tile) + _dot(lo2, tile)

    ctr = tile_cols(ct_ref[0, 0, 0])
    cti = tile_cols(ct_ref[0, 0, 1])
    tc = tci.astype(F32)
    lt = tc + df * ((t_chunk - 1) - 2.0 * tc)
    stepl = lax.broadcasted_iota(jnp.int32, (1, t_chunk), 1).astype(F32)
    lstepl = stepl + df * ((t_chunk - 1) - 2.0 * stepl)
    rep = jnp.where(lax.broadcasted_iota(jnp.int32, (t_chunk, ncol), 0) == tci, 1.0, 0.0).astype(BF16)

    def rep_cols(x):
        hi, lo = _split(x)
        lo2 = (x - hi.astype(F32) - lo.astype(F32)).astype(BF16)
        return _dot(hi, rep) + _dot(lo, rep) + _dot(lo2, rep)

    ecr, eci = [rep_cols(x) for x in cpow(ar_c, ai_c, lstepl)]
    cmr = ctr * ecr - cti * eci
    cmi = ctr * eci + cti * ecr
    lhs_hi, lhs_lo = _split(jnp.concatenate([bmr, -bmi], axis=1))
    rhs_hi, rhs_lo = _split(jnp.concatenate([cmr, cmi], axis=0))
    m = _dot(lhs_hi, rhs_hi) + _dot(lhs_hi, rhs_lo) + _dot(lhs_lo, rhs_hi)
    m_ref[0, 0] = jnp.where(lj <= lt, m, 0.0).astype(BF16)
    abr_c, abi_c = abar(ar_c, ai_c)
    c1r = cmr * abr_c - cmi * abi_c
    c1i = cmr * abi_c + cmi * abr_c
    v_ref[0, 0] = jnp.concatenate([c1r, -c1i], axis=0).astype(BF16)


def _ssm_prep(arow, acol, ldt, bt, ct):
    nd, g = arow.shape[0], arow.shape[1]
    p = SSM_STATE
    n = SSM_CHUNK * SSM_GROUP
    blk = lambda a: pl.BlockSpec((1, 1) + a.shape[2:], lambda d, gi: (d, gi) + (0,) * (a.ndim - 2))
    return pl.pallas_call(
        _ssm_prep_kernel,
        grid=(nd, g),
        in_specs=[blk(arow), blk(acol), blk(ldt), blk(bt), blk(ct)],
        out_specs=[
            pl.BlockSpec((1, 1, n, n), lambda d, gi: (d, gi, 0, 0)),
            pl.BlockSpec((1, 1, n, 2 * p), lambda d, gi: (d, gi, 0, 0)),
            pl.BlockSpec((1, 1, 2 * p, n), lambda d, gi: (d, gi, 0, 0)),
            pl.BlockSpec((1, 1, 1, 2 * p), lambda d, gi: (d, gi, 0, 0)),
        ],
        out_shape=[
            jax.ShapeDtypeStruct((nd, g, n, n), BF16),
            jax.ShapeDtypeStruct((nd, g, n, 2 * p), BF16),
            jax.ShapeDtypeStruct((nd, g, 2 * p, n), BF16),
            jax.ShapeDtypeStruct((nd, g, 1, 2 * p), F32),
        ],
        compiler_params=pltpu.CompilerParams(dimension_semantics=("arbitrary", "arbitrary")),
        name="ssm_prep",
    )(arow, acol, ldt, bt, ct)


def _chunk_carry(z, a, s0, reverse):
    p = SSM_STATE
    nc = z.shape[0]
    row = lax.broadcasted_iota(jnp.int32, z.shape, 0)
    is_re = lax.broadcasted_iota(jnp.int32, a.shape, 1) < p
    sign = jnp.where(is_re, -1.0, 1.0)

    def parts(ap):
        sw = pltpu.roll(ap, p, 1)
        return jnp.where(is_re, ap, sw), sign * jnp.where(is_re, sw, ap)

    def cmul(x, ar_full, ai_sgn):
        return x * ar_full + pltpu.roll(x, p, 1) * ai_sgn

    ar_full, ai_sgn = parts(a)
    first = (nc - 1) if reverse else 0
    e = z + jnp.where(row == first, cmul(s0, ar_full, ai_sgn), 0.0)
    k = 1
    while k < nc:
        if reverse:
            sh = jnp.where(row < nc - k, pltpu.roll(e, nc - k, 0), 0.0)
        else:
            sh = jnp.where(row >= k, pltpu.roll(e, k, 0), 0.0)
        e = e + cmul(sh, ar_full, ai_sgn)
        k *= 2
        if k < nc:
            ar_full, ai_sgn = parts(cmul(jnp.where(is_re, ar_full, sign * ai_sgn), ar_full, ai_sgn))
    if reverse:
        return jnp.where(row == first, s0, pltpu.roll(e, nc - 1, 0))
    return jnp.where(row == first, s0, pltpu.roll(e, 1, 0))


def _ssm_kernel(u_ref, um_ref, m_ref, w_ref, v_ref, at_ref, dvec_ref, y_ref, ug_sc, yg_sc):
    p = SSM_STATE
    t_chunk = SSM_CHUNK
    hgrp = SSM_GROUP
    gb = u_ref.shape[2] // hgrp
    per_tile = u_ref.shape[2] // hgrp
    nc = u_ref.shape[1] // t_chunk
    lane_blk = lax.broadcasted_iota(jnp.int32, (nc, u_ref.shape[2]), 1) >> int(math.log2(hgrp))

    width = u_ref.shape[2]

    def block_transpose(arrs):
        arrs = list(arrs)
        d = len(arrs) // 2
        while d >= 1:
            low_bit = (lane_blk & d) == 0
            for i in range(len(arrs)):
                if i & d == 0:
                    lo, hi = arrs[i], arrs[i + d]
                    arrs[i] = jnp.where(low_bit, lo, pltpu.roll(hi, d * hgrp, 1))
                    arrs[i + d] = jnp.where(low_bit, pltpu.roll(lo, width - d * hgrp, 1), hi)
            d //= 2
        return arrs

    for q in range(t_chunk // per_tile):
        steps = [u_ref[0, pl.ds(q * per_tile + r, nc, stride=t_chunk), :] for r in range(per_tile)]
        for g, tile in enumerate(block_transpose(steps)):
            ug_sc[g, :, q * width:(q + 1) * width] = tile

    def group(g, carry):
        u = ug_sc[g]
        ub = u.astype(BF16)
        y = dvec_ref[g] * u
        for d in range(2):
            yin = _dot(ub, m_ref[d, g])
            z = _dot(ub, w_ref[d, g])
            if d == 0:
                s0 = _dot(um_ref[g].astype(BF16), w_ref[0, g])[0:1, :]
            else:
                s0 = jnp.zeros((1, 2 * p), F32)
            s_in = _chunk_carry(z, at_ref[d, g], s0, reverse=(d == 1))
            y = y + yin + _dot(s_in.astype(BF16), v_ref[d, g])
        yg_sc[g] = y
        return carry

    lax.fori_loop(0, gb, group, 0, unroll=2)

    for q in range(t_chunk // per_tile):
        srcs = [yg_sc[g, :, q * width:(q + 1) * width] for g in range(gb)]
        for r, row in enumerate(block_transpose(srcs)):
            y_ref[0, pl.ds(q * per_tile + r, nc, stride=t_chunk), :] = row


def _ssm(u3, um2, m, w, v, at, dvec):
    b, s, sw = u3.shape
    g = sw // SSM_GROUP
    n = SSM_CHUNK * SSM_GROUP
    p = SSM_STATE
    lanes = 128
    gb = lanes // SSM_GROUP
    return pl.pallas_call(
        _ssm_kernel,
        grid=(b, sw // lanes),
        in_specs=[
            pl.BlockSpec((1, s, lanes), lambda bi, gi: (bi, 0, gi)),
            pl.BlockSpec((gb, 8, n), lambda bi, gi: (gi, 0, 0)),
            pl.BlockSpec((2, gb, n, n), lambda bi, gi: (0, gi, 0, 0)),
            pl.BlockSpec((2, gb, n, 2 * p), lambda bi, gi: (0, gi, 0, 0)),
            pl.BlockSpec((2, gb, 2 * p, n), lambda bi, gi: (0, gi, 0, 0)),
            pl.BlockSpec((2, gb, 1, 2 * p), lambda bi, gi: (0, gi, 0, 0)),
            pl.BlockSpec((gb, 1, n), lambda bi, gi: (gi, 0, 0)),
        ],
        out_specs=pl.BlockSpec((1, s, lanes), lambda bi, gi: (bi, 0, gi)),
        out_shape=jax.ShapeDtypeStruct((b, s, sw), F32),
        scratch_shapes=[pltpu.VMEM((gb, s // SSM_CHUNK, n), F32), pltpu.VMEM((gb, s // SSM_CHUNK, n), F32)],
        compiler_params=pltpu.CompilerParams(dimension_semantics=("arbitrary", "arbitrary"),
                                             vmem_limit_bytes=VMEM_LIMIT),
        name="ssm",
    )(u3, um2, m, w, v, at, dvec)


def _merge_kernel(x_ref, ya_ref, ys_ref, wgt_ref, wglu_ref, bglu_ref, wab_ref, wsb_ref, wo_ref,
                  lng_ref, lnb_ref, l1g_ref, l1b_ref, wr_ref,
                  h1_ref, tok_ref, afft_ref):
    d = x_ref.shape[1]
    h = _ln(x_ref[...], lng_ref[...], lnb_ref[...])
    gates = _dot(h.astype(BF16), wgt_ref[...])
    ga = _sigmoid(gates[:, 0:d])
    gs = _sigmoid(gates[:, d:2 * d])
    ys = ys_ref[...]
    y = 0.5 * ys * (1.0 + jnp.tanh(math.sqrt(2.0 / math.pi) * (ys + 0.044715 * (ys * ys * ys))))
    yg = y * _sigmoid(_dot(y.astype(BF16), wglu_ref[...]) + bglu_ref[...])
    merged = ga * _dot(ya_ref[...], wab_ref[...]) + gs * _dot(yg.astype(BF16), wsb_ref[...])
    h1 = _ln(DEEPNORM_ALPHA * h + _dot(merged.astype(BF16), wo_ref[...]), l1g_ref[...], l1b_ref[...])
    h1_ref[...] = h1
    hi, lo = _split(h1)
    ne = afft_ref.shape[0]
    cross = _dot(hi, wr_ref[...])
    logits = cross[:, 0:GATE_LANES] + cross[:, GATE_LANES:] + _dot(lo, wr_ref[:, 0:GATE_LANES])
    lane = lax.broadcasted_iota(jnp.int32, logits.shape, 1)
    logits = jnp.where(lane < 3 * ne, logits, -jnp.inf)
    ex = jnp.exp(logits - jnp.max(logits, axis=1, keepdims=True))
    aff = ex / jnp.sum(jnp.where(lane < ne, ex, 0.0), axis=1, keepdims=True)
    p0 = aff.astype(BF16)
    r1 = aff - p0.astype(F32)
    p1 = r1.astype(BF16)
    p2 = (r1 - p1.astype(F32)).astype(BF16)
    pieces = jnp.where(lane < ne, p0, jnp.where(lane < 2 * ne, p1, p2))
    tok_ref[...] = jnp.concatenate([hi, pieces], axis=1)
    afft_ref[...] = aff.T[0:ne, :]


def _merge(x2, ya, ys, wgt, wglu, bglu, wab, wsb, wo, lng, lnb, l1g, l1b, wr, ne, tm):
    rows, d = x2.shape
    full = lambda a: pl.BlockSpec(a.shape, lambda i: (0,) * a.ndim)
    return pl.pallas_call(
        _merge_kernel,
        grid=(rows // tm,),
        in_specs=[
            pl.BlockSpec((tm, d), lambda i: (i, 0)),
            pl.BlockSpec((tm, ya.shape[1]), lambda i: (i, 0)),
            pl.BlockSpec((tm, ys.shape[1]), lambda i: (i, 0)),
            full(wgt), full(wglu), full(bglu), full(wab), full(wsb), full(wo),
            full(lng), full(lnb), full(l1g), full(l1b), full(wr),
        ],
        out_specs=[
            pl.BlockSpec((tm, d), lambda i: (i, 0)),
            pl.BlockSpec((tm, d + GATE_LANES), lambda i: (i, 0)),
            pl.BlockSpec((ne, tm), lambda i: (0, i)),
        ],
        out_shape=[
            jax.ShapeDtypeStruct((rows, d), F32),
            jax.ShapeDtypeStruct((rows, d + GATE_LANES), BF16),
            jax.ShapeDtypeStruct((ne, rows), F32),
        ],
        compiler_params=pltpu.CompilerParams(dimension_semantics=("arbitrary",),
                                             vmem_limit_bytes=VMEM_LIMIT),
        name="merge",
    )(x2, ya, ys, wgt, wglu, bglu, wab, wsb, wo, lng, lnb, l1g, l1b, wr)


def _route_kernel(afft_ref, tri_ref, scl_ref, lo_ref, *, cap):
    aff = afft_ref[...]
    ne, s = aff.shape
    capf = float(cap)

    def as_float(bits):
        return lax.bitcast_convert_type(bits, F32)

    def search(i, t):
        cand = t | jnp.left_shift(jnp.int32(1), 30 - i)
        cnt = jnp.sum(jnp.where(aff >= as_float(cand), 1.0, 0.0), axis=1, keepdims=True)
        return jnp.where(cnt >= capf, cand, t)

    thr_bits = lax.fori_loop(0, 31, search, jnp.zeros((ne, 1), jnp.int32))
    gt = aff >= as_float(thr_bits + 1)
    eq = (aff >= as_float(thr_bits)) & jnp.logical_not(gt)
    need = capf - jnp.sum(jnp.where(gt, 1.0, 0.0), axis=1, keepdims=True)
    tri = tri_ref[...]
    rt = ROUTE_TILE
    nt = s // rt
    col = lax.broadcasted_iota(jnp.int32, (ne, nt), 1)
    carry_eq = jnp.zeros((ne, 1), F32)
    carry_sel = jnp.zeros((ne, 1), F32)
    lo_val = jnp.zeros((ne, nt), F32)
    for t in range(nt):
        sl = slice(t * rt, (t + 1) * rt)
        eq_b = eq[:, sl]
        ceq = _dot(jnp.where(eq_b, 1.0, 0.0).astype(BF16), tri)
        sel_b = gt[:, sl] | (eq_b & ((ceq + carry_eq) <= need))
        carry_eq = carry_eq + ceq[:, rt - 1:rt]
        csel = _dot(jnp.where(sel_b, 1.0, 0.0).astype(BF16), tri)
        scl_ref[0, :, sl] = jnp.where(sel_b, csel, 0.0)
        lo_val = jnp.where(col == t, carry_sel, lo_val)
        carry_sel = carry_sel + csel[:, rt - 1:rt]
    lo_ref[0] = lo_val


def _route(afft, tri, nb, cap):
    ne, rows = afft.shape
    s = rows // nb
    nt = s // ROUTE_TILE
    return pl.pallas_call(
        functools.partial(_route_kernel, cap=cap),
        grid=(nb,),
        in_specs=[
            pl.BlockSpec((ne, s), lambda b: (0, b)),
            pl.BlockSpec(tri.shape, lambda b: (0, 0)),
        ],
        out_specs=[
            pl.BlockSpec((1, ne, s), lambda b: (b, 0, 0)),
            pl.BlockSpec((1, ne, nt), lambda b: (b, 0, 0)),
        ],
        out_shape=[
            jax.ShapeDtypeStruct((nb, ne, s), F32),
            jax.ShapeDtypeStruct((nb, ne, nt), F32),
        ],
        compiler_params=pltpu.CompilerParams(dimension_semantics=("arbitrary",)),
        name="route",
    )(afft, tri)


def _ffn_kernel(lo_ref, t_ref, scl_ref, wg_ref, wu_ref, wd_ref, ye_ref, xe_sc, xb_sc, gate_sc, acc_sc, *, cap):
    fc = pl.program_id(2)

    @pl.when(fc == 0)
    def _gather():
        _ffn_gather(lo_ref, t_ref, scl_ref, xe_sc, xb_sc, gate_sc, cap=cap)
        acc_sc[...] = jnp.zeros(acc_sc.shape, F32)

    xb = xb_sc[...]
    g = _dot(xb, wg_ref[0, 0].astype(BF16))
    u = _dot(xb, wu_ref[0, 0].astype(BF16))
    hh = (g * _sigmoid(g)) * u
    acc_sc[...] += _dot(hh.astype(BF16), wd_ref[0, 0].astype(BF16))

    @pl.when(fc == pl.num_programs(2) - 1)
    def _emit():
        ye_ref[0, 0] = (acc_sc[...] * gate_sc[...]).astype(BF16)


def _ffn_gather(lo_ref, t_ref, scl_ref, xe_sc, xb_sc, gate_sc, *, cap):
    b = pl.program_id(0)
    e = pl.program_id(1)
    ne = pl.num_programs(1)
    rt = ROUTE_TILE
    nt = t_ref.shape[1] // rt
    xe_sc[...] = jnp.zeros(xe_sc.shape, F32)
    r = lax.broadcasted_iota(jnp.int32, (PIECE, rt), 0).astype(F32)

    def tile_info(tau):
        base = (b * ne + e) * (nt + 1) + tau
        lo = lo_ref[base]
        n_pieces = jnp.right_shift((lo & 7) + (lo_ref[base + 1] - lo) + (PIECE - 1), PIECE_SHIFT)
        return lo, n_pieces

    def add_piece(tau, lo, p):
        off = lo & 7
        scl = scl_ref[0, 0, pl.ds(tau, 1), :]
        tok = t_ref[0, pl.ds(pl.multiple_of(tau * rt, rt), rt), :]
        shift = (off - 1 - p * PIECE).astype(F32)
        onehot = jnp.where((scl > 0.0) & ((scl + shift) == r), 1.0, 0.0).astype(BF16)
        xe_sc[pl.ds(pl.multiple_of(lo - off + p * PIECE, 8), PIECE), :] += _dot(onehot, tok)

    def first_piece(tau, carry):
        add_piece(tau, tile_info(tau)[0], 0)
        return carry

    def more_pieces(tau, carry):
        lo, n_pieces = tile_info(tau)
        lax.fori_loop(1, n_pieces, lambda p, c: (add_piece(tau, lo, p), c)[1], 0)
        return carry

    lax.fori_loop(0, nt, first_piece, 0, unroll=4)
    lax.fori_loop(0, nt, more_pieces, 0)
    d = xb_sc.shape[1]
    xb_sc[...] = xe_sc[0:cap, 0:d].astype(BF16)
    gl = xe_sc[0:cap, d:d + GATE_LANES]
    lane = lax.broadcasted_iota(jnp.int32, gl.shape, 1)
    mine = ((lane & (N_EXPERTS - 1)) == e) & (lane < 3 * N_EXPERTS)
    gate_sc[...] = jnp.sum(jnp.where(mine, gl, 0.0), axis=1, keepdims=True)


def _ffn(lo_i, t, scl4, wg, wu, wd, layer, cap, f_chunk):
    b, s, dx = t.shape
    _, ne, d, f = wg.shape
    nt = s // ROUTE_TILE
    grid_spec = pltpu.PrefetchScalarGridSpec(
        num_scalar_prefetch=1,
        grid=(b, ne, f // f_chunk),
        in_specs=[
            pl.BlockSpec((1, s, dx), lambda bi, ei, fi, lo: (bi, 0, 0), pipeline_mode=pl.Buffered(1)),
            pl.BlockSpec((1, 1, nt, ROUTE_TILE), lambda bi, ei, fi, lo: (bi, ei, 0, 0)),
            pl.BlockSpec((1, 1, d, f_chunk), lambda bi, ei, fi, lo: (layer, ei, 0, fi)),
            pl.BlockSpec((1, 1, d, f_chunk), lambda bi, ei, fi, lo: (layer, ei, 0, fi)),
            pl.BlockSpec((1, 1, f_chunk, d), lambda bi, ei, fi, lo: (layer, ei, fi, 0)),
        ],
        out_specs=pl.BlockSpec((1, 1, cap, d), lambda bi, ei, fi, lo: (bi, ei, 0, 0)),
        scratch_shapes=[pltpu.VMEM((cap + PIECE, dx), F32), pltpu.VMEM((cap, d), BF16),
                        pltpu.VMEM((cap, 1), F32), pltpu.VMEM((cap, d), F32)],
    )
    return pl.pallas_call(
        functools.partial(_ffn_kernel, cap=cap),
        grid_spec=grid_spec,
        out_shape=jax.ShapeDtypeStruct((b, ne, cap, d), BF16),
        compiler_params=pltpu.CompilerParams(dimension_semantics=("arbitrary", "arbitrary", "arbitrary"),
                                             vmem_limit_bytes=VMEM_LIMIT),
        name="ffn",
    )(lo_i, t, scl4, wg, wu, wd)


def _combine_kernel(lo_ref, h1_ref, sclt_ref, ye_hbm, g_ref, b_ref, o_ref, win, xwin, acc_sc, sem, xsem, *, cap):
    b = pl.program_id(0)
    tau = pl.program_id(1)
    nb = pl.num_programs(0)
    nt = pl.num_programs(1)
    ne = win.shape[1]
    d = win.shape[3]
    step = b * nt + tau
    slot = step & 1

    def pieces(bi, ti, e):
        base = (bi * ne + e) * (nt + 1) + ti
        lo = lo_ref[base]
        n_sel = lo_ref[base + 1] - lo
        n_pieces = jnp.maximum(jnp.right_shift((lo & 15) + n_sel + (PIECE - 1), PIECE_SHIFT), 1)
        first = jnp.minimum(lo - (lo & 15), cap - n_pieces * PIECE)
        return lo, first, n_pieces

    def copy(bi, e, first, p, buf, s):
        start = pl.multiple_of(first + p * PIECE, 16)
        return pltpu.make_async_copy(ye_hbm.at[bi, e, pl.ds(start, PIECE), :], buf, s)

    def first_pieces(bi, ti, sl):
        return [copy(bi, e, pieces(bi, ti, e)[1], 0, win.at[sl, e], sem.at[sl, e]) for e in range(ne)]

    @pl.when(step == 0)
    def _prime():
        for cp in first_pieces(b, tau, slot):
            cp.start()

    @pl.when(step + 1 < nb * nt)
    def _prefetch():
        wrap = tau + 1 == nt
        for cp in first_pieces(jnp.where(wrap, b + 1, b), jnp.where(wrap, 0, tau + 1), 1 - slot):
            cp.start()

    for cp in first_pieces(b, tau, slot):
        cp.wait()

    sclt = sclt_ref[0]
    col = lax.broadcasted_iota(jnp.int32, (1, ne * PIECE), 1)
    grp = jnp.right_shift(col, PIECE_SHIFT)
    expand = jnp.where(lax.broadcasted_iota(jnp.int32, (ne, ne * PIECE), 0) == grp, 1.0, 0.0).astype(BF16)
    scl = _dot(sclt.astype(BF16), expand)
    shift = jnp.zeros(col.shape, F32)
    for e in range(ne):
        lo, first, _ = pieces(b, tau, e)
        shift = jnp.where(grp == e, (lo - first - 1).astype(F32), shift)
    r = (col & (PIECE - 1)).astype(F32)
    onehot = jnp.where((scl > 0.0) & ((scl + shift) == r), 1.0, 0.0).astype(BF16)
    acc_sc[...] = DEEPNORM_ALPHA * h1_ref[0] + _dot(onehot, win[slot].reshape(ne * PIECE, d))

    for e in range(ne):
        lo, first, n_pieces = pieces(b, tau, e)

        def extra(p, c, e=e, lo=lo, first=first):
            cp = copy(b, e, first, p, xwin, xsem.at[0])
            cp.start()
            cp.wait()
            se = sclt_ref[0][:, e:e + 1]
            rr = lax.broadcasted_iota(jnp.int32, (se.shape[0], PIECE), 1).astype(F32)
            oh = jnp.where((se > 0.0) & ((se + (lo - first - p * PIECE - 1).astype(F32)) == rr), 1.0, 0.0)
            acc_sc[...] += _dot(oh.astype(BF16), xwin[...])
            return c

        lax.fori_loop(1, n_pieces, extra, 0)
    o_ref[0] = _ln(acc_sc[...], g_ref[...], b_ref[...])


def _combine(lo_i, h1, sclt, ye, g, bb, cap):
    b, s, d = h1.shape
    ne = sclt.shape[2]
    rt = ROUTE_TILE
    grid_spec = pltpu.PrefetchScalarGridSpec(
        num_scalar_prefetch=1,
        grid=(b, s // rt),
        in_specs=[
            pl.BlockSpec((1, rt, d), lambda bi, ti, lo: (bi, ti, 0)),
            pl.BlockSpec((1, rt, ne), lambda bi, ti, lo: (bi, ti, 0)),
            pl.BlockSpec(memory_space=pl.ANY),
            pl.BlockSpec(g.shape, lambda bi, ti, lo: (0, 0)),
            pl.BlockSpec(bb.shape, lambda bi, ti, lo: (0, 0)),
        ],
        out_specs=pl.BlockSpec((1, rt, d), lambda bi, ti, lo: (bi, ti, 0)),
        scratch_shapes=[pltpu.VMEM((2, ne, PIECE, d), BF16), pltpu.VMEM((PIECE, d), BF16),
                        pltpu.VMEM((rt, d), F32),
                        pltpu.SemaphoreType.DMA((2, ne)), pltpu.SemaphoreType.DMA((1,))],
    )
    return pl.pallas_call(
        functools.partial(_combine_kernel, cap=cap),
        grid_spec=grid_spec,
        out_shape=jax.ShapeDtypeStruct((b, s, d), F32),
        compiler_params=pltpu.CompilerParams(dimension_semantics=("arbitrary", "arbitrary"),
                                             vmem_limit_bytes=VMEM_LIMIT),
        name="combine",
    )(lo_i, h1, sclt, ye, g, bb)


def _rope_tables(n_tokens):
    half = HEAD_DIM // 2
    inv_freq = ROPE_THETA ** (-jnp.arange(0, half, 2, dtype=F32) / half)
    rows = n_tokens // GRID_W
    row = jnp.repeat(jnp.arange(rows, dtype=F32), GRID_W)
    colv = jnp.tile(jnp.arange(GRID_W, dtype=F32), rows)
    ang = jnp.concatenate([row[:, None] * inv_freq, colv[:, None] * inv_freq], axis=-1)
    return jnp.cos(ang), jnp.sin(ang)


def kernel(x, meta_tokens, ln_in_g, ln_in_b, w_in, q_norm_g, k_norm_g, ssm_a_re, ssm_a_im, ssm_log_dt,
           ssm_b_re, ssm_b_im, ssm_c_re, ssm_c_im, ssm_d, w_glu, b_glu, w_attn_br, w_ssm_br, w_o,
           ln1_g, ln1_b, w_router, w_gate_e, w_up_e, w_down_e, ln2_g, ln2_b):
    b, s, d = x.shape
    aw = N_HEADS * HEAD_DIM
    kw = N_KV_HEADS * HEAD_DIM
    sw = d // 2
    g = sw // SSM_GROUP
    half = HEAD_DIM // 2
    cap = CAPACITY_FACTOR * s // N_EXPERTS
    rows = b * s
    l = 0

    perm = np.concatenate([np.arange(0, HEAD_DIM, 2), np.arange(1, HEAD_DIM, 2)])
    qcols = np.concatenate([h * HEAD_DIM + perm for h in range(N_HEADS)])
    kcols = aw + np.concatenate([h * HEAD_DIM + perm for h in range(N_KV_HEADS)])
    wl = w_in[l]
    w_t = jnp.concatenate([wl[:, qcols], wl[:, aw + kw:aw + 2 * kw]], axis=1).T.astype(BF16)
    w_n = jnp.concatenate([wl[:, kcols], wl[:, aw + 2 * kw:aw + 2 * kw + sw]], axis=1).astype(BF16)
    w_gates = wl[:, aw + 2 * kw + sw:].astype(BF16)
    qg3 = (jnp.tile(q_norm_g[l][perm], N_HEADS) * (HEAD_DIM ** -0.5 * math.log2(math.e))).reshape(N_HEADS, HEAD_DIM, 1)
    kg = jnp.tile(k_norm_g[l][perm], N_KV_HEADS)[None, :]
    bd = jnp.asarray(np.kron(np.eye(N_KV_HEADS), np.full((HEAD_DIM, HEAD_DIM), 1.0 / HEAD_DIM)), BF16)
    lng = ln_in_g[None, :]
    lnb = ln_in_b[None, :]
    cos, sin = _rope_tables(s)
    cosr = jnp.tile(cos, (1, 2 * N_KV_HEADS))
    sinr = jnp.tile(jnp.concatenate([-sin, sin], axis=1), (1, N_KV_HEADS))
    cost = cos.T
    sint = sin.T

    x2 = x.reshape(rows, d)
    tm_in = 512
    qt, kk, vt, u2 = _inproj(x2, cosr, sinr, cost, sint, w_t, w_n, lng, lnb, qg3, kg, bd, tm_in, s // tm_in)
    meta_p = jnp.pad(meta_tokens, ((0, META_PAD - N_META), (0, 0)))
    ones_r = jnp.ones((META_PAD, kw), F32)
    _, km, vtm, um = _inproj(meta_p, ones_r, jnp.zeros_like(ones_r), jnp.ones((half, META_PAD), F32),
                             jnp.zeros((half, META_PAD), F32), w_t, w_n, lng, lnb, qg3, kg, bd, META_PAD, 1)
    um = um[:N_META]
    km = km[:, :N_META, :]
    vtm = vtm[:, :, :N_META]

    ya = _attention(qt, kk, vt, km, vtm, b, 512, 512)

    t_chunk = SSM_CHUNK
    hgrp = SSM_GROUP
    n = t_chunk * hgrp
    arow = jnp.stack([ssm_a_re[l], ssm_a_im[l]], axis=2)
    acol = jnp.stack([ssm_a_re[l], ssm_a_im[l]], axis=3)
    ldt = ssm_log_dt[l][:, :, None, None]
    bt = jnp.stack([ssm_b_re[l], ssm_b_im[l]], axis=2)
    bt = jnp.swapaxes(bt, 3, 4)
    ct = jnp.stack([ssm_c_re[l], ssm_c_im[l]], axis=2)
    ct = jnp.swapaxes(ct, 3, 4)
    mm, ww, vv, at = _ssm_prep(arow, acol, ldt, bt, ct)
    nc = s // t_chunk
    um_g = um.reshape(N_META, g, hgrp).transpose(1, 0, 2).reshape(g, 1, N_META * hgrp)
    um_g = jnp.pad(um_g, ((0, 0), (0, 7), (n - N_META * hgrp, 0)))
    dvec = jnp.tile(ssm_d[l], (1, t_chunk))[:, None, :]
    ys = _ssm(u2.reshape(b, s, sw), um_g, mm, ww, vv, at, dvec).reshape(rows, sw)

    wr_hi = w_router[l].astype(BF16)
    wr_lo = (w_router[l] - wr_hi.astype(F32)).astype(BF16)
    lane_pad = ((0, 0), (0, GATE_LANES - 3 * N_EXPERTS))
    wr = jnp.concatenate([jnp.pad(jnp.tile(wr_hi, (1, 3)), lane_pad), jnp.pad(jnp.tile(wr_lo, (1, 3)), lane_pad)],
                         axis=1)
    h1, tok, afft = _merge(
        x2, ya, ys, w_gates, w_glu[l].astype(BF16), b_glu[l][None, :],
        w_attn_br[l].astype(BF16), w_ssm_br[l].astype(BF16), w_o[l].astype(BF16),
        lng, lnb, ln1_g[l][None, :], ln1_b[l][None, :], wr, N_EXPERTS, 512)

    tri = jnp.asarray(np.triu(np.ones((ROUTE_TILE, ROUTE_TILE), np.float32)), BF16)
    scl, lo_f = _route(afft, tri, b, cap)
    nt = s // ROUTE_TILE
    lo_i = jnp.pad(lo_f.astype(jnp.int32), ((0, 0), (0, 0), (0, 1)), constant_values=cap).reshape(-1)
    ye = _ffn(lo_i, tok.reshape(b, s, d + GATE_LANES), scl.reshape(b, N_EXPERTS, nt, ROUTE_TILE),
              w_gate_e, w_up_e, w_down_e, l, cap, 512)
    out = _combine(lo_i, h1.reshape(b, s, d), jnp.swapaxes(scl, 1, 2), ye,
                   ln2_g[l][None, :], ln2_b[l][None, :], cap)
    return out
```

```python
import functools
import math

import numpy as np
import jax
import jax.numpy as jnp
from jax import lax
from jax.experimental import pallas as pl
from jax.experimental.pallas import tpu as pltpu

F32 = jnp.float32
BF16 = jnp.bfloat16

N_META = 16
GRID_W = 64
N_HEADS = 8
N_KV_HEADS = 2
HEAD_DIM = 64
ROPE_THETA = 10000.0
SSM_GROUP = 16
SSM_STATE = 64
N_EXPERTS = 16
CAPACITY_FACTOR = 2
LN_EPS = 1e-5
QK_EPS = 1e-6
DEPTH = 1
DEEPNORM_ALPHA = (2.0 * DEPTH) ** 0.25

SSM_CHUNK = 32
ROUTE_TILE = 256
GATE_LANES = 128
PIECE_SHIFT = 6
PIECE = 1 << PIECE_SHIFT
META_PAD = 128
V_ROWS = HEAD_DIM + 16
VMEM_LIMIT = 56 * 1024 * 1024


def _ln(x, g, b):
    mu = jnp.mean(x, axis=-1, keepdims=True)
    xc = x - mu
    var = jnp.mean(xc * xc, axis=-1, keepdims=True)
    return xc * lax.rsqrt(var + LN_EPS) * g + b


def _sigmoid(x):
    return 1.0 / (1.0 + jnp.exp(-x))


def _split(t):
    hi = t.astype(BF16)
    lo = (t - hi.astype(F32)).astype(BF16)
    return hi, lo


def _dot(a, b):
    return jnp.dot(a, b, preferred_element_type=F32)


def _dot_nt(a, b):
    return lax.dot_general(a, b, (((1,), (1,)), ((), ())), preferred_element_type=F32)


def _inproj_kernel(x_ref, cosr_ref, sinr_ref, cost_ref, sint_ref, wt_ref, wn_ref, lng_ref, lnb_ref,
                   qg_ref, kg_ref, bd_ref, qt_ref, k_ref, vt_ref, u_ref):
    aw = N_HEADS * HEAD_DIM
    kw = N_KV_HEADS * HEAD_DIM
    half = HEAD_DIM // 2
    h = _ln(x_ref[...], lng_ref[...], lnb_ref[...])
    hb = h.astype(BF16)
    pt = _dot_nt(wt_ref[...], hb)
    pn = _dot(hb, wn_ref[...])
    tm = hb.shape[0]

    qt = pt[0:aw].reshape(N_HEADS, HEAD_DIM, tm)
    ms = jnp.mean(qt * qt, axis=1, keepdims=True)
    qn = qt * lax.rsqrt(ms + QK_EPS) * qg_ref[...]
    x0 = qn[:, 0:half, :]
    x1 = qn[:, half:, :]
    c = cost_ref[...][None]
    s = sint_ref[...][None]
    qr = jnp.concatenate([x0 * c - x1 * s, x0 * s + x1 * c], axis=1)
    qt_ref[...] = qr.reshape(aw, tm).astype(BF16)
    vrow = lax.broadcasted_iota(jnp.int32, (N_KV_HEADS, V_ROWS - HEAD_DIM, tm), 1)
    vt_ref[...] = jnp.concatenate([pt[aw:aw + kw].reshape(N_KV_HEADS, HEAD_DIM, tm),
                                   jnp.where(vrow == 0, 1.0, 0.0)], axis=1).astype(BF16)

    kk = pn[:, 0:kw]
    hi, lo = _split(kk * kk)
    bd = bd_ref[...]
    msk = _dot(hi, bd) + _dot(lo, bd)
    kn = kk * lax.rsqrt(msk + QK_EPS) * kg_ref[...]
    lane = lax.broadcasted_iota(jnp.int32, kn.shape, 1)
    first = (lane & (HEAD_DIM - 1)) < half
    partner = jnp.where(first, pltpu.roll(kn, kw - half, 1), pltpu.roll(kn, half, 1))
    kr = (kn * cosr_ref[...] + partner * sinr_ref[...]).astype(BF16)
    for g in range(N_KV_HEADS):
        k_ref[g] = kr[:, g * HEAD_DIM:(g + 1) * HEAD_DIM]
    u_ref[...] = pn[:, kw:]


def _inproj(x2, cosr, sinr, cost, sint, wt, wn, lng, lnb, qg3, kg, bd, tm, n_tab_blocks):
    rows, d = x2.shape
    aw = N_HEADS * HEAD_DIM
    kw = N_KV_HEADS * HEAD_DIM
    uw = wn.shape[1] - kw
    half = HEAD_DIM // 2
    full = lambda a: pl.BlockSpec(a.shape, lambda i: (0,) * a.ndim)
    return pl.pallas_call(
        _inproj_kernel,
        grid=(rows // tm,),
        in_specs=[
            pl.BlockSpec((tm, d), lambda i: (i, 0)),
            pl.BlockSpec((tm, kw), lambda i: (i % n_tab_blocks, 0)),
            pl.BlockSpec((tm, kw), lambda i: (i % n_tab_blocks, 0)),
            pl.BlockSpec((half, tm), lambda i: (0, i % n_tab_blocks)),
            pl.BlockSpec((half, tm), lambda i: (0, i % n_tab_blocks)),
            full(wt), full(wn), full(lng), full(lnb), full(qg3), full(kg), full(bd),
        ],
        out_specs=[
            pl.BlockSpec((aw, tm), lambda i: (0, i)),
            pl.BlockSpec((N_KV_HEADS, tm, HEAD_DIM), lambda i: (0, i, 0)),
            pl.BlockSpec((N_KV_HEADS, V_ROWS, tm), lambda i: (0, 0, i)),
            pl.BlockSpec((tm, uw), lambda i: (i, 0)),
        ],
        out_shape=[
            jax.ShapeDtypeStruct((aw, rows), BF16),
            jax.ShapeDtypeStruct((N_KV_HEADS, rows, HEAD_DIM), BF16),
            jax.ShapeDtypeStruct((N_KV_HEADS, V_ROWS, rows), BF16),
            jax.ShapeDtypeStruct((rows, uw), F32),
        ],
        compiler_params=pltpu.CompilerParams(dimension_semantics=("arbitrary",),
                                             vmem_limit_bytes=VMEM_LIMIT),
        name="inproj",
    )(x2, cosr, sinr, cost, sint, wt, wn, lng, lnb, qg3, kg, bd)


def _attn_kernel(qt_ref, k0_ref, k1_ref, vta_ref, vtb_ref, km_ref, vtm_ref, o_ref, m_sc, acc_sc, s_sc, mb_sc):
    j = pl.program_id(2)
    last = pl.num_programs(2) - 1
    grp = N_HEADS // N_KV_HEADS
    heads = lambda h: slice(h * HEAD_DIM, (h + 1) * HEAD_DIM)

    def score(k_ref, slot, h):
        s = _dot(k_ref[h // grp], qt_ref[heads(h), :])
        s_sc[slot, h] = s
        mb_sc[slot, h:h + 1, :] = jnp.max(s, axis=0, keepdims=True)

    def softmax_pv(h, s, m_blk, vt_g):
        m_prev = m_sc[h:h + 1, :]
        m_new = jnp.maximum(m_prev, m_blk)
        alpha = jnp.exp2(m_prev - m_new)
        p = jnp.exp2(s - m_new).astype(BF16)
        acc_sc[h] = alpha * acc_sc[h] + _dot(vt_g, p)
        m_sc[h:h + 1, :] = m_new

    def consume(vt_ref, slot, h):
        softmax_pv(h, s_sc[slot, h], mb_sc[slot, h:h + 1, :], vt_ref[h // grp])

    @pl.when(j == 0)
    def _first():
        m_sc[...] = jnp.full(m_sc.shape, -jnp.inf, F32)
        acc_sc[...] = jnp.zeros(acc_sc.shape, F32)
        for h in range(N_HEADS):
            s = _dot(km_ref[h // grp], qt_ref[heads(h), :])
            softmax_pv(h, s, jnp.max(s, axis=0, keepdims=True), vtm_ref[h // grp])
            score(k0_ref, 0, h)
        for h in range(N_HEADS):
            score(k1_ref, 1, h)
            consume(vtb_ref, 0, h)

    @pl.when((j > 0) & (j < last))
    def _middle():
        for h in range(N_HEADS):
            score(k0_ref, 0, h)
            consume(vta_ref, 1, h)
        for h in range(N_HEADS):
            score(k1_ref, 1, h)
            consume(vtb_ref, 0, h)

    @pl.when(j == last)
    def _last():
        for h in range(N_HEADS):
            consume(vta_ref, 1, h)
        acc = acc_sc[...]
        tq = acc.shape[2]
        out_t = (acc[:, 0:HEAD_DIM, :] / acc[:, HEAD_DIM:HEAD_DIM + 1, :]).reshape(N_HEADS * HEAD_DIM, tq)
        o_ref[...] = out_t.T.astype(BF16)


def _attention(qt, kk, vt, km, vtm, nb, tq, tk):
    aw, rows = qt.shape
    s = rows // nb
    nq = s // tq
    nkb = s // tk
    kblk = lambda f: pl.BlockSpec((N_KV_HEADS, tk, HEAD_DIM),
                                  lambda bi, qi, j: (0, bi * nkb + jnp.clip(f(j), 0, nkb - 1), 0))
    vblk = lambda f: pl.BlockSpec((N_KV_HEADS, V_ROWS, tk),
                                  lambda bi, qi, j: (0, 0, bi * nkb + jnp.clip(f(j), 0, nkb - 1)))
    return pl.pallas_call(
        _attn_kernel,
        grid=(nb, nq, nkb // 2 + 1),
        in_specs=[
            pl.BlockSpec((aw, tq), lambda bi, qi, j: (0, bi * nq + qi)),
            kblk(lambda j: 2 * j), kblk(lambda j: 2 * j + 1),
            vblk(lambda j: 2 * j - 1), vblk(lambda j: 2 * j),
            pl.BlockSpec(km.shape, lambda bi, qi, j: (0, 0, 0)),
            pl.BlockSpec(vtm.shape, lambda bi, qi, j: (0, 0, 0)),
        ],
        out_specs=pl.BlockSpec((tq, aw), lambda bi, qi, j: (bi * nq + qi, 0)),
        out_shape=jax.ShapeDtypeStruct((rows, aw), BF16),
        scratch_shapes=[
            pltpu.VMEM((N_HEADS, tq), F32),
            pltpu.VMEM((N_HEADS, V_ROWS, tq), F32),
            pltpu.VMEM((2, N_HEADS, tk, tq), F32),
            pltpu.VMEM((2, N_HEADS, tq), F32),
        ],
        compiler_params=pltpu.CompilerParams(
            dimension_semantics=("arbitrary", "arbitrary", "arbitrary"),
            vmem_limit_bytes=VMEM_LIMIT),
        name="attn",
    )(qt, kk, kk, vt, vt, km, vtm)


def _ssm_prep_kernel(*refs):
    for d in range(2):
        _ssm_prep_direction(d, *refs)


def _ssm_prep_direction(d, arow_ref, acol_ref, ldt_ref, bt_ref, ct_ref, m_ref, w_ref, v_ref, at_ref):
    t_chunk = SSM_CHUNK
    shift = int(math.log2(SSM_GROUP))
    df = float(d)
    dt = jnp.exp(ldt_ref[d, 0])

    def abar(ar, ai):
        mag = jnp.exp(ar * dt)
        ang = ai * dt
        return mag * jnp.cos(ang), mag * jnp.sin(ang)

    def cpow(ar, ai, e):
        mag = jnp.exp(ar * dt * e)
        ang = ai * dt * e
        return mag * jnp.cos(ang), mag * jnp.sin(ang)

    ar_r = arow_ref[d, 0, 0:1, :]
    ai_r = arow_ref[d, 0, 1:2, :]
    abr, abi = abar(ar_r, ai_r)
    nr = abr - 1.0
    ni = abi
    den = ar_r * ar_r + ai_r * ai_r
    cr = (nr * ar_r + ni * ai_r) / den
    ci = (ni * ar_r - nr * ai_r) / den
    nstate = bt_ref.shape[4]
    nrow = t_chunk * SSM_GROUP

    def tile_rows(x):
        return jnp.broadcast_to(x[None], (t_chunk, SSM_GROUP, nstate)).reshape(nrow, nstate)

    btr = tile_rows(bt_ref[d, 0, 0])
    bti = tile_rows(bt_ref[d, 0, 1])
    bbr = cr * btr - ci * bti
    bbi = cr * bti + ci * btr
    jr = (lax.broadcasted_iota(jnp.int32, (nrow, 1), 0) >> shift).astype(F32)
    lj = jr + df * ((t_chunk - 1) - 2.0 * jr)
    step = lax.broadcasted_iota(jnp.int32, (t_chunk, 1), 0).astype(F32)
    lstep = step + df * ((t_chunk - 1) - 2.0 * step)

    def rep_rows(x):
        return jnp.broadcast_to(x[:, None, :], (t_chunk, SSM_GROUP, nstate)).reshape(nrow, nstate)

    enr, eni = [rep_rows(x) for x in cpow(ar_r, ai_r, -lstep)]
    bmr = enr * bbr - eni * bbi
    bmi = enr * bbi + eni * bbr
    ewr, ewi = [rep_rows(x) for x in cpow(ar_r, ai_r, (t_chunk - 1) - lstep)]
    w_ref[d, 0] = jnp.concatenate([ewr * bbr - ewi * bbi, ewr * bbi + ewi * bbr], axis=1).astype(BF16)
    atr, ati = cpow(ar_r, ai_r, float(t_chunk))
    at_ref[d, 0] = jnp.concatenate([atr, ati], axis=1)

    ar_c = acol_ref[d, 0, :, 0:1]
    ai_c = acol_ref[d, 0, :, 1:2]
    ncol = t_chunk * SSM_GROUP
    lane = lax.broadcasted_iota(jnp.int32, (1, ncol), 1)
    tci = lane >> shift
    tile = jnp.where(lax.broadcasted_iota(jnp.int32, (SSM_GROUP, ncol), 0) == (lane & (SSM_GROUP - 1)),
                     1.0, 0.0).astype(BF16)

    def tile_cols(x):
        hi, lo = _split(x)
        lo2 = (x - hi.astype(F32) - lo.astype(F32)).astype(BF16)
        return _dot(hi, tile) + _dot(lo, tile) + _dot(lo2, tile)

    ctr = tile_cols(ct_ref[d, 0, 0])
    cti = tile_cols(ct_ref[d, 0, 1])
    tc = tci.astype(F32)
    lt = tc + df * ((t_chunk - 1) - 2.0 * tc)
    stepl = lax.broadcasted_iota(jnp.int32, (1, t_chunk), 1).astype(F32)
    lstepl = stepl + df * ((t_chunk - 1) - 2.0 * stepl)
    rep = jnp.where(lax.broadcasted_iota(jnp.int32, (t_chunk, ncol), 0) == tci, 1.0, 0.0).astype(BF16)

    def rep_cols(x):
        hi, lo = _split(x)
        lo2 = (x - hi.astype(F32) - lo.astype(F32)).astype(BF16)
        return _dot(hi, rep) + _dot(lo, rep) + _dot(lo2, rep)

    ecr, eci = [rep_cols(x) for x in cpow(ar_c, ai_c, lstepl)]
    cmr = ctr * ecr - cti * eci
    cmi = ctr * eci + cti * ecr
    lhs_hi, lhs_lo = _split(jnp.concatenate([bmr, -bmi], axis=1))
    rhs_hi, rhs_lo = _split(jnp.concatenate([cmr, cmi], axis=0))
    m = _dot(lhs_hi, rhs_hi) + _dot(lhs_hi, rhs_lo) + _dot(lhs_lo, rhs_hi)
    m_ref[d, 0] = jnp.where(lj <= lt, m, 0.0).astype(BF16)
    abr_c, abi_c = abar(ar_c, ai_c)
    c1r = cmr * abr_c - cmi * abi_c
    c1i = cmr * abi_c + cmi * abr_c
    v_ref[d, 0] = jnp.concatenate([c1r, -c1i], axis=0).astype(BF16)


def _ssm_prep(arow, acol, ldt, bt, ct):
    nd, g = arow.shape[0], arow.shape[1]
    p = SSM_STATE
    n = SSM_CHUNK * SSM_GROUP
    blk = lambda a: pl.BlockSpec((nd, 1) + a.shape[2:], lambda gi: (0, gi) + (0,) * (a.ndim - 2))
    return pl.pallas_call(
        _ssm_prep_kernel,
        grid=(g,),
        in_specs=[blk(arow), blk(acol), blk(ldt), blk(bt), blk(ct)],
        out_specs=[
            pl.BlockSpec((nd, 1, n, n), lambda gi: (0, gi, 0, 0)),
            pl.BlockSpec((nd, 1, n, 2 * p), lambda gi: (0, gi, 0, 0)),
            pl.BlockSpec((nd, 1, 2 * p, n), lambda gi: (0, gi, 0, 0)),
            pl.BlockSpec((nd, 1, 1, 2 * p), lambda gi: (0, gi, 0, 0)),
        ],
        out_shape=[
            jax.ShapeDtypeStruct((nd, g, n, n), BF16),
            jax.ShapeDtypeStruct((nd, g, n, 2 * p), BF16),
            jax.ShapeDtypeStruct((nd, g, 2 * p, n), BF16),
            jax.ShapeDtypeStruct((nd, g, 1, 2 * p), F32),
        ],
        compiler_params=pltpu.CompilerParams(dimension_semantics=("arbitrary",)),
        name="ssm_prep",
    )(arow, acol, ldt, bt, ct)


def _chunk_carry(z, a, s0, reverse):
    p = SSM_STATE
    nc = z.shape[0]
    row = lax.broadcasted_iota(jnp.int32, z.shape, 0)
    is_re = lax.broadcasted_iota(jnp.int32, a.shape, 1) < p
    sign = jnp.where(is_re, -1.0, 1.0)

    def parts(ap):
        sw = pltpu.roll(ap, p, 1)
        return jnp.where(is_re, ap, sw), sign * jnp.where(is_re, sw, ap)

    def cmul(x, ar_full, ai_sgn):
        return x * ar_full + pltpu.roll(x, p, 1) * ai_sgn

    ar_full, ai_sgn = parts(a)
    first = (nc - 1) if reverse else 0
    e = z + jnp.where(row == first, cmul(s0, ar_full, ai_sgn), 0.0)
    k = 1
    while k < nc:
        if reverse:
            sh = jnp.where(row < nc - k, pltpu.roll(e, nc - k, 0), 0.0)
        else:
            sh = jnp.where(row >= k, pltpu.roll(e, k, 0), 0.0)
        e = e + cmul(sh, ar_full, ai_sgn)
        k *= 2
        if k < nc:
            ar_full, ai_sgn = parts(cmul(jnp.where(is_re, ar_full, sign * ai_sgn), ar_full, ai_sgn))
    if reverse:
        return jnp.where(row == first, s0, pltpu.roll(e, nc - 1, 0))
    return jnp.where(row == first, s0, pltpu.roll(e, 1, 0))


def _ssm_kernel(u_ref, um_ref, m_ref, w_ref, v_ref, at_ref, dvec_ref, y_ref, ug_sc, yg_sc):
    p = SSM_STATE
    t_chunk = SSM_CHUNK
    hgrp = SSM_GROUP
    gb = u_ref.shape[2] // hgrp
    per_tile = u_ref.shape[2] // hgrp
    nc = u_ref.shape[1] // t_chunk
    lane_blk = lax.broadcasted_iota(jnp.int32, (nc, u_ref.shape[2]), 1) >> int(math.log2(hgrp))

    width = u_ref.shape[2]

    def block_transpose(arrs):
        arrs = list(arrs)
        d = len(arrs) // 2
        while d >= 1:
            low_bit = (lane_blk & d) == 0
            for i in range(len(arrs)):
                if i & d == 0:
                    lo, hi = arrs[i], arrs[i + d]
                    arrs[i] = jnp.where(low_bit, lo, pltpu.roll(hi, d * hgrp, 1))
                    arrs[i + d] = jnp.where(low_bit, pltpu.roll(lo, width - d * hgrp, 1), hi)
            d //= 2
        return arrs

    for q in range(t_chunk // per_tile):
        steps = [u_ref[0, pl.ds(q * per_tile + r, nc, stride=t_chunk), :] for r in range(per_tile)]
        for g, tile in enumerate(block_transpose(steps)):
            ug_sc[g, :, q * width:(q + 1) * width] = tile

    def group(g, carry):
        u = ug_sc[g]
        ub = u.astype(BF16)
        y = dvec_ref[g] * u
        for d in range(2):
            yin = _dot(ub, m_ref[d, g])
            z = _dot(ub, w_ref[d, g])
            if d == 0:
                s0 = _dot(um_ref[g].astype(BF16), w_ref[0, g])[0:1, :]
            else:
                s0 = jnp.zeros((1, 2 * p), F32)
            s_in = _chunk_carry(z, at_ref[d, g], s0, reverse=(d == 1))
            y = y + yin + _dot(s_in.astype(BF16), v_ref[d, g])
        yg_sc[g] = y
        return carry

    lax.fori_loop(0, gb, group, 0, unroll=2)

    for q in range(t_chunk // per_tile):
        srcs = [yg_sc[g, :, q * width:(q + 1) * width] for g in range(gb)]
        for r, row in enumerate(block_transpose(srcs)):
            y_ref[0, pl.ds(q * per_tile + r, nc, stride=t_chunk), :] = row


def _ssm(u3, um2, m, w, v, at, dvec):
    b, s, sw = u3.shape
    g = sw // SSM_GROUP
    n = SSM_CHUNK * SSM_GROUP
    p = SSM_STATE
    lanes = 128
    gb = lanes // SSM_GROUP
    return pl.pallas_call(
        _ssm_kernel,
        grid=(b, sw // lanes),
        in_specs=[
            pl.BlockSpec((1, s, lanes), lambda bi, gi: (bi, 0, gi)),
            pl.BlockSpec((gb, 8, n), lambda bi, gi: (gi, 0, 0)),
            pl.BlockSpec((2, gb, n, n), lambda bi, gi: (0, gi, 0, 0)),
            pl.BlockSpec((2, gb, n, 2 * p), lambda bi, gi: (0, gi, 0, 0)),
            pl.BlockSpec((2, gb, 2 * p, n), lambda bi, gi: (0, gi, 0, 0)),
            pl.BlockSpec((2, gb, 1, 2 * p), lambda bi, gi: (0, gi, 0, 0)),
            pl.BlockSpec((gb, 1, n), lambda bi, gi: (gi, 0, 0)),
        ],
        out_specs=pl.BlockSpec((1, s, lanes), lambda bi, gi: (bi, 0, gi)),
        out_shape=jax.ShapeDtypeStruct((b, s, sw), F32),
        scratch_shapes=[pltpu.VMEM((gb, s // SSM_CHUNK, n), F32), pltpu.VMEM((gb, s // SSM_CHUNK, n), F32)],
        compiler_params=pltpu.CompilerParams(dimension_semantics=("arbitrary", "arbitrary"),
                                             vmem_limit_bytes=VMEM_LIMIT),
        name="ssm",
    )(u3, um2, m, w, v, at, dvec)


def _merge_kernel(x_ref, ya_ref, ys_ref, wgt_ref, wglu_ref, bglu_ref, wab_ref, wsb_ref, wo_ref,
                  lng_ref, lnb_ref, l1g_ref, l1b_ref, wr_ref,
                  h1_ref, tok_ref, afft_ref):
    d = x_ref.shape[1]
    h = _ln(x_ref[...], lng_ref[...], lnb_ref[...])
    gates = _dot(h.astype(BF16), wgt_ref[...])
    ga = _sigmoid(gates[:, 0:d])
    gs = _sigmoid(gates[:, d:2 * d])
    ys = ys_ref[...]
    y = 0.5 * ys * (1.0 + jnp.tanh(math.sqrt(2.0 / math.pi) * (ys + 0.044715 * (ys * ys * ys))))
    yg = y * _sigmoid(_dot(y.astype(BF16), wglu_ref[...]) + bglu_ref[...])
    merged = ga * _dot(ya_ref[...], wab_ref[...]) + gs * _dot(yg.astype(BF16), wsb_ref[...])
    h1 = _ln(DEEPNORM_ALPHA * h + _dot(merged.astype(BF16), wo_ref[...]), l1g_ref[...], l1b_ref[...])
    h1_ref[...] = h1
    hi, lo = _split(h1)
    ne = afft_ref.shape[0]
    cross = _dot(hi, wr_ref[...])
    logits = cross[:, 0:GATE_LANES] + cross[:, GATE_LANES:] + _dot(lo, wr_ref[:, 0:GATE_LANES])
    lane = lax.broadcasted_iota(jnp.int32, logits.shape, 1)
    logits = jnp.where(lane < 3 * ne, logits, -jnp.inf)
    ex = jnp.exp(logits - jnp.max(logits, axis=1, keepdims=True))
    aff = ex / jnp.sum(jnp.where(lane < ne, ex, 0.0), axis=1, keepdims=True)
    p0 = aff.astype(BF16)
    r1 = aff - p0.astype(F32)
    p1 = r1.astype(BF16)
    p2 = (r1 - p1.astype(F32)).astype(BF16)
    pieces = jnp.where(lane < ne, p0, jnp.where(lane < 2 * ne, p1, p2))
    tok_ref[...] = jnp.concatenate([hi, pieces], axis=1)
    afft_ref[...] = aff.T[0:ne, :]


def _merge(x2, ya, ys, wgt, wglu, bglu, wab, wsb, wo, lng, lnb, l1g, l1b, wr, ne, tm):
    rows, d = x2.shape
    full = lambda a: pl.BlockSpec(a.shape, lambda i: (0,) * a.ndim)
    return pl.pallas_call(
        _merge_kernel,
        grid=(rows // tm,),
        in_specs=[
            pl.BlockSpec((tm, d), lambda i: (i, 0)),
            pl.BlockSpec((tm, ya.shape[1]), lambda i: (i, 0)),
            pl.BlockSpec((tm, ys.shape[1]), lambda i: (i, 0)),
            full(wgt), full(wglu), full(bglu), full(wab), full(wsb), full(wo),
            full(lng), full(lnb), full(l1g), full(l1b), full(wr),
        ],
        out_specs=[
            pl.BlockSpec((tm, d), lambda i: (i, 0)),
            pl.BlockSpec((tm, d + GATE_LANES), lambda i: (i, 0)),
            pl.BlockSpec((ne, tm), lambda i: (0, i)),
        ],
        out_shape=[
            jax.ShapeDtypeStruct((rows, d), F32),
            jax.ShapeDtypeStruct((rows, d + GATE_LANES), BF16),
            jax.ShapeDtypeStruct((ne, rows), F32),
        ],
        compiler_params=pltpu.CompilerParams(dimension_semantics=("arbitrary",),
                                             vmem_limit_bytes=VMEM_LIMIT),
        name="merge",
    )(x2, ya, ys, wgt, wglu, bglu, wab, wsb, wo, lng, lnb, l1g, l1b, wr)


def _route_kernel(afft_ref, tri_ref, scl_ref, lo_ref, *, cap):
    aff = afft_ref[...]
    ne, s = aff.shape
    capf = float(cap)

    def as_float(bits):
        return lax.bitcast_convert_type(bits, F32)

    def search(i, t):
        cand = t | jnp.left_shift(jnp.int32(1), 30 - i)
        cnt = jnp.sum(jnp.where(aff >= as_float(cand), 1.0, 0.0), axis=1, keepdims=True)
        return jnp.where(cnt >= capf, cand, t)

    thr_bits = lax.fori_loop(0, 31, search, jnp.zeros((ne, 1), jnp.int32))
    gt = aff >= as_float(thr_bits + 1)
    eq = (aff >= as_float(thr_bits)) & jnp.logical_not(gt)
    need = capf - jnp.sum(jnp.where(gt, 1.0, 0.0), axis=1, keepdims=True)
    tri = tri_ref[...]
    rt = ROUTE_TILE
    nt = s // rt
    col = lax.broadcasted_iota(jnp.int32, (ne, nt), 1)
    carry_eq = jnp.zeros((ne, 1), F32)
    carry_sel = jnp.zeros((ne, 1), F32)
    lo_val = jnp.zeros((ne, nt), F32)
    for t in range(nt):
        sl = slice(t * rt, (t + 1) * rt)
        eq_b = eq[:, sl]
        ceq = _dot(jnp.where(eq_b, 1.0, 0.0).astype(BF16), tri)
        sel_b = gt[:, sl] | (eq_b & ((ceq + carry_eq) <= need))
        carry_eq = carry_eq + ceq[:, rt - 1:rt]
        csel = _dot(jnp.where(sel_b, 1.0, 0.0).astype(BF16), tri)
        scl_ref[0, :, sl] = jnp.where(sel_b, csel, 0.0)
        lo_val = jnp.where(col == t, carry_sel, lo_val)
        carry_sel = carry_sel + csel[:, rt - 1:rt]
    lo_ref[0] = lo_val


def _route(afft, tri, nb, cap):
    ne, rows = afft.shape
    s = rows // nb
    nt = s // ROUTE_TILE
    return pl.pallas_call(
        functools.partial(_route_kernel, cap=cap),
        grid=(nb,),
        in_specs=[
            pl.BlockSpec((ne, s), lambda b: (0, b)),
            pl.BlockSpec(tri.shape, lambda b: (0, 0)),
        ],
        out_specs=[
            pl.BlockSpec((1, ne, s), lambda b: (b, 0, 0)),
            pl.BlockSpec((1, ne, nt), lambda b: (b, 0, 0)),
        ],
        out_shape=[
            jax.ShapeDtypeStruct((nb, ne, s), F32),
            jax.ShapeDtypeStruct((nb, ne, nt), F32),
        ],
        compiler_params=pltpu.CompilerParams(dimension_semantics=("arbitrary",)),
        name="route",
    )(afft, tri)


def _ffn_kernel(lo_ref, t_ref, scl_ref, wg_ref, wu_ref, wd_ref, ye_ref, xe_sc, xb_sc, gate_sc, acc_sc, *, cap):
    fc = pl.program_id(2)

    @pl.when(fc == 0)
    def _gather():
        _ffn_gather(lo_ref, t_ref, scl_ref, xe_sc, xb_sc, gate_sc, cap=cap)
        acc_sc[...] = jnp.zeros(acc_sc.shape, F32)

    xb = xb_sc[...]
    g = _dot(xb, wg_ref[0, 0].astype(BF16))
    u = _dot(xb, wu_ref[0, 0].astype(BF16))
    hh = (g * _sigmoid(g)) * u
    acc_sc[...] += _dot(hh.astype(BF16), wd_ref[0, 0].astype(BF16))

    @pl.when(fc == pl.num_programs(2) - 1)
    def _emit():
        ye_ref[0, 0] = (acc_sc[...] * gate_sc[...]).astype(BF16)


def _ffn_gather(lo_ref, t_ref, scl_ref, xe_sc, xb_sc, gate_sc, *, cap):
    b = pl.program_id(0)
    e = pl.program_id(1)
    ne = pl.num_programs(1)
    rt = ROUTE_TILE
    nt = t_ref.shape[1] // rt
    xe_sc[...] = jnp.zeros(xe_sc.shape, F32)
    r = lax.broadcasted_iota(jnp.int32, (PIECE, rt), 0).astype(F32)

    def tile_info(tau):
        base = (b * ne + e) * (nt + 1) + tau
        lo = lo_ref[base]
        n_pieces = jnp.right_shift((lo & 7) + (lo_ref[base + 1] - lo) + (PIECE - 1), PIECE_SHIFT)
        return lo, n_pieces

    def add_piece(tau, lo, p):
        off = lo & 7
        scl = scl_ref[0, 0, pl.ds(tau, 1), :]
        tok = t_ref[0, pl.ds(pl.multiple_of(tau * rt, rt), rt), :]
        shift = (off - 1 - p * PIECE).astype(F32)
        onehot = jnp.where((scl > 0.0) & ((scl + shift) == r), 1.0, 0.0).astype(BF16)
        xe_sc[pl.ds(pl.multiple_of(lo - off + p * PIECE, 8), PIECE), :] += _dot(onehot, tok)

    def first_piece(tau, carry):
        add_piece(tau, tile_info(tau)[0], 0)
        return carry

    def more_pieces(tau, carry):
        lo, n_pieces = tile_info(tau)
        lax.fori_loop(1, n_pieces, lambda p, c: (add_piece(tau, lo, p), c)[1], 0)
        return carry

    lax.fori_loop(0, nt, first_piece, 0, unroll=4)
    lax.fori_loop(0, nt, more_pieces, 0)
    d = xb_sc.shape[1]
    xb_sc[...] = xe_sc[0:cap, 0:d].astype(BF16)
    gl = xe_sc[0:cap, d:d + GATE_LANES]
    lane = lax.broadcasted_iota(jnp.int32, gl.shape, 1)
    mine = ((lane & (N_EXPERTS - 1)) == e) & (lane < 3 * N_EXPERTS)
    gate_sc[...] = jnp.sum(jnp.where(mine, gl, 0.0), axis=1, keepdims=True)


def _ffn(lo_i, t, scl4, wg, wu, wd, layer, cap, f_chunk):
    b, s, dx = t.shape
    _, ne, d, f = wg.shape
    nt = s // ROUTE_TILE
    grid_spec = pltpu.PrefetchScalarGridSpec(
        num_scalar_prefetch=1,
        grid=(b, ne, f // f_chunk),
        in_specs=[
            pl.BlockSpec((1, s, dx), lambda bi, ei, fi, lo: (bi, 0, 0), pipeline_mode=pl.Buffered(1)),
            pl.BlockSpec((1, 1, nt, ROUTE_TILE), lambda bi, ei, fi, lo: (bi, ei, 0, 0)),
            pl.BlockSpec((1, 1, d, f_chunk), lambda bi, ei, fi, lo: (layer, ei, 0, fi)),
            pl.BlockSpec((1, 1, d, f_chunk), lambda bi, ei, fi, lo: (layer, ei, 0, fi)),
            pl.BlockSpec((1, 1, f_chunk, d), lambda bi, ei, fi, lo: (layer, ei, fi, 0)),
        ],
        out_specs=pl.BlockSpec((1, 1, cap, d), lambda bi, ei, fi, lo: (bi, ei, 0, 0)),
        scratch_shapes=[pltpu.VMEM((cap + PIECE, dx), F32), pltpu.VMEM((cap, d), BF16),
                        pltpu.VMEM((cap, 1), F32), pltpu.VMEM((cap, d), F32)],
    )
    return pl.pallas_call(
        functools.partial(_ffn_kernel, cap=cap),
        grid_spec=grid_spec,
        out_shape=jax.ShapeDtypeStruct((b, ne, cap, d), BF16),
        compiler_params=pltpu.CompilerParams(dimension_semantics=("arbitrary", "arbitrary", "arbitrary"),
                                             vmem_limit_bytes=VMEM_LIMIT),
        name="ffn",
    )(lo_i, t, scl4, wg, wu, wd)


def _combine_kernel(lo_ref, h1_ref, sclt_ref, ye_hbm, g_ref, b_ref, o_ref, win, xwin, acc_sc, sem, xsem, *, cap):
    b = pl.program_id(0)
    tau = pl.program_id(1)
    nb = pl.num_programs(0)
    nt = pl.num_programs(1)
    ne = win.shape[1]
    d = win.shape[3]
    step = b * nt + tau
    slot = step & 1

    def pieces(bi, ti, e):
        base = (bi * ne + e) * (nt + 1) + ti
        lo = lo_ref[base]
        n_sel = lo_ref[base + 1] - lo
        n_pieces = jnp.maximum(jnp.right_shift((lo & 15) + n_sel + (PIECE - 1), PIECE_SHIFT), 1)
        first = jnp.minimum(lo - (lo & 15), cap - n_pieces * PIECE)
        return lo, first, n_pieces

    def copy(bi, e, first, p, buf, s):
        start = pl.multiple_of(first + p * PIECE, 16)
        return pltpu.make_async_copy(ye_hbm.at[bi, e, pl.ds(start, PIECE), :], buf, s)

    def first_pieces(bi, ti, sl):
        return [copy(bi, e, pieces(bi, ti, e)[1], 0, win.at[sl, e], sem.at[sl, e]) for e in range(ne)]

    @pl.when(step == 0)
    def _prime():
        for cp in first_pieces(b, tau, slot):
            cp.start()

    @pl.when(step + 1 < nb * nt)
    def _prefetch():
        wrap = tau + 1 == nt
        for cp in first_pieces(jnp.where(wrap, b + 1, b), jnp.where(wrap, 0, tau + 1), 1 - slot):
            cp.start()

    for cp in first_pieces(b, tau, slot):
        cp.wait()

    sclt = sclt_ref[0]
    col = lax.broadcasted_iota(jnp.int32, (1, ne * PIECE), 1)
    grp = jnp.right_shift(col, PIECE_SHIFT)
    expand = jnp.where(lax.broadcasted_iota(jnp.int32, (ne, ne * PIECE), 0) == grp, 1.0, 0.0).astype(BF16)
    scl = _dot(sclt.astype(BF16), expand)
    shift = jnp.zeros(col.shape, F32)
    for e in range(ne):
        lo, first, _ = pieces(b, tau, e)
        shift = jnp.where(grp == e, (lo - first - 1).astype(F32), shift)
    r = (col & (PIECE - 1)).astype(F32)
    onehot = jnp.where((scl > 0.0) & ((scl + shift) == r), 1.0, 0.0).astype(BF16)
    acc_sc[...] = DEEPNORM_ALPHA * h1_ref[0] + _dot(onehot, win[slot].reshape(ne * PIECE, d))

    for e in range(ne):
        lo, first, n_pieces = pieces(b, tau, e)

        def extra(p, c, e=e, lo=lo, first=first):
            cp = copy(b, e, first, p, xwin, xsem.at[0])
            cp.start()
            cp.wait()
            se = sclt_ref[0][:, e:e + 1]
            rr = lax.broadcasted_iota(jnp.int32, (se.shape[0], PIECE), 1).astype(F32)
            oh = jnp.where((se > 0.0) & ((se + (lo - first - p * PIECE - 1).astype(F32)) == rr), 1.0, 0.0)
            acc_sc[...] += _dot(oh.astype(BF16), xwin[...])
            return c

        lax.fori_loop(1, n_pieces, extra, 0)
    o_ref[0] = _ln(acc_sc[...], g_ref[...], b_ref[...])


def _combine(lo_i, h1, sclt, ye, g, bb, cap):
    b, s, d = h1.shape
    ne = sclt.shape[2]
    rt = ROUTE_TILE
    grid_spec = pltpu.PrefetchScalarGridSpec(
        num_scalar_prefetch=1,
        grid=(b, s // rt),
        in_specs=[
            pl.BlockSpec((1, rt, d), lambda bi, ti, lo: (bi, ti, 0)),
            pl.BlockSpec((1, rt, ne), lambda bi, ti, lo: (bi, ti, 0)),
            pl.BlockSpec(memory_space=pl.ANY),
            pl.BlockSpec(g.shape, lambda bi, ti, lo: (0, 0)),
            pl.BlockSpec(bb.shape, lambda bi, ti, lo: (0, 0)),
        ],
        out_specs=pl.BlockSpec((1, rt, d), lambda bi, ti, lo: (bi, ti, 0)),
        scratch_shapes=[pltpu.VMEM((2, ne, PIECE, d), BF16), pltpu.VMEM((PIECE, d), BF16),
                        pltpu.VMEM((rt, d), F32),
                        pltpu.SemaphoreType.DMA((2, ne)), pltpu.SemaphoreType.DMA((1,))],
    )
    return pl.pallas_call(
        functools.partial(_combine_kernel, cap=cap),
        grid_spec=grid_spec,
        out_shape=jax.ShapeDtypeStruct((b, s, d), F32),
        compiler_params=pltpu.CompilerParams(dimension_semantics=("arbitrary", "arbitrary"),
                                             vmem_limit_bytes=VMEM_LIMIT),
        name="combine",
    )(lo_i, h1, sclt, ye, g, bb)


def _rope_tables(n_tokens):
    half = HEAD_DIM // 2
    inv_freq = ROPE_THETA ** (-jnp.arange(0, half, 2, dtype=F32) / half)
    rows = n_tokens // GRID_W
    row = jnp.repeat(jnp.arange(rows, dtype=F32), GRID_W)
    colv = jnp.tile(jnp.arange(GRID_W, dtype=F32), rows)
    ang = jnp.concatenate([row[:, None] * inv_freq, colv[:, None] * inv_freq], axis=-1)
    return jnp.cos(ang), jnp.sin(ang)


def kernel(x, meta_tokens, ln_in_g, ln_in_b, w_in, q_norm_g, k_norm_g, ssm_a_re, ssm_a_im, ssm_log_dt,
           ssm_b_re, ssm_b_im, ssm_c_re, ssm_c_im, ssm_d, w_glu, b_glu, w_attn_br, w_ssm_br, w_o,
           ln1_g, ln1_b, w_router, w_gate_e, w_up_e, w_down_e, ln2_g, ln2_b):
    b, s, d = x.shape
    aw = N_HEADS * HEAD_DIM
    kw = N_KV_HEADS * HEAD_DIM
    sw = d // 2
    g = sw // SSM_GROUP
    half = HEAD_DIM // 2
    cap = CAPACITY_FACTOR * s // N_EXPERTS
    rows = b * s
    l = 0

    perm = np.concatenate([np.arange(0, HEAD_DIM, 2), np.arange(1, HEAD_DIM, 2)])
    qcols = np.concatenate([h * HEAD_DIM + perm for h in range(N_HEADS)])
    kcols = aw + np.concatenate([h * HEAD_DIM + perm for h in range(N_KV_HEADS)])
    wl = w_in[l]
    w_t = jnp.concatenate([wl[:, qcols], wl[:, aw + kw:aw + 2 * kw]], axis=1).T.astype(BF16)
    w_n = jnp.concatenate([wl[:, kcols], wl[:, aw + 2 * kw:aw + 2 * kw + sw]], axis=1).astype(BF16)
    w_gates = wl[:, aw + 2 * kw + sw:].astype(BF16)
    qg3 = (jnp.tile(q_norm_g[l][perm], N_HEADS) * (HEAD_DIM ** -0.5 * math.log2(math.e))).reshape(N_HEADS, HEAD_DIM, 1)
    kg = jnp.tile(k_norm_g[l][perm], N_KV_HEADS)[None, :]
    bd = jnp.asarray(np.kron(np.eye(N_KV_HEADS), np.full((HEAD_DIM, HEAD_DIM), 1.0 / HEAD_DIM)), BF16)
    lng = ln_in_g[None, :]
    lnb = ln_in_b[None, :]
    cos, sin = _rope_tables(s)
    cosr = jnp.tile(cos, (1, 2 * N_KV_HEADS))
    sinr = jnp.tile(jnp.concatenate([-sin, sin], axis=1), (1, N_KV_HEADS))
    cost = cos.T
    sint = sin.T

    x2 = x.reshape(rows, d)
    tm_in = 1024
    qt, kk, vt, u2 = _inproj(x2, cosr, sinr, cost, sint, w_t, w_n, lng, lnb, qg3, kg, bd, tm_in, s // tm_in)
    meta_p = jnp.pad(meta_tokens, ((0, META_PAD - N_META), (0, 0)))
    ones_r = jnp.ones((META_PAD, kw), F32)
    _, km, vtm, um = _inproj(meta_p, ones_r, jnp.zeros_like(ones_r), jnp.ones((half, META_PAD), F32),
                             jnp.zeros((half, META_PAD), F32), w_t, w_n, lng, lnb, qg3, kg, bd, META_PAD, 1)
    um = um[:N_META]
    km = km[:, :N_META, :]
    vtm = vtm[:, :, :N_META]

    ya = _attention(qt, kk, vt, km, vtm, b, 512, 512)

    t_chunk = SSM_CHUNK
    hgrp = SSM_GROUP
    n = t_chunk * hgrp
    arow = jnp.stack([ssm_a_re[l], ssm_a_im[l]], axis=2)
    acol = jnp.stack([ssm_a_re[l], ssm_a_im[l]], axis=3)
    ldt = ssm_log_dt[l][:, :, None, None]
    bt = jnp.stack([ssm_b_re[l], ssm_b_im[l]], axis=2)
    bt = jnp.swapaxes(bt, 3, 4)
    ct = jnp.stack([ssm_c_re[l], ssm_c_im[l]], axis=2)
    ct = jnp.swapaxes(ct, 3, 4)
    mm, ww, vv, at = _ssm_prep(arow, acol, ldt, bt, ct)
    nc = s // t_chunk
    um_g = um.reshape(N_META, g, hgrp).transpose(1, 0, 2).reshape(g, 1, N_META * hgrp)
    um_g = jnp.pad(um_g, ((0, 0), (0, 7), (n - N_META * hgrp, 0)))
    dvec = jnp.tile(ssm_d[l], (1, t_chunk))[:, None, :]
    ys = _ssm(u2.reshape(b, s, sw), um_g, mm, ww, vv, at, dvec).reshape(rows, sw)

    wr_hi = w_router[l].astype(BF16)
    wr_lo = (w_router[l] - wr_hi.astype(F32)).astype(BF16)
    lane_pad = ((0, 0), (0, GATE_LANES - 3 * N_EXPERTS))
    wr = jnp.concatenate([jnp.pad(jnp.tile(wr_hi, (1, 3)), lane_pad), jnp.pad(jnp.tile(wr_lo, (1, 3)), lane_pad)],
                         axis=1)
    h1, tok, afft = _merge(
        x2, ya, ys, w_gates, w_glu[l].astype(BF16), b_glu[l][None, :],
        w_attn_br[l].astype(BF16), w_ssm_br[l].astype(BF16), w_o[l].astype(BF16),
        lng, lnb, ln1_g[l][None, :], ln1_b[l][None, :], wr, N_EXPERTS, 512)

    tri = jnp.asarray(np.triu(np.ones((ROUTE_TILE, ROUTE_TILE), np.float32)), BF16)
    scl, lo_f = _route(afft, tri, b, cap)
    nt = s // ROUTE_TILE
    lo_i = jnp.pad(lo_f.astype(jnp.int32), ((0, 0), (0, 0), (0, 1)), constant_values=cap).reshape(-1)
    ye = _ffn(lo_i, tok.reshape(b, s, d + GATE_LANES), scl.reshape(b, N_EXPERTS, nt, ROUTE_TILE),
              w_gate_e, w_up_e, w_down_e, l, cap, 512)
    out = _combine(lo_i, h1.reshape(b, s, d), jnp.swapaxes(scl, 1, 2), ye,
                   ln2_g[l][None, :], ln2_b[l][None, :], cap)
    return out
```

```python
import functools
import math

import numpy as np
import jax
import jax.numpy as jnp
from jax import lax
from jax.experimental import pallas as pl
from jax.experimental.pallas import tpu as pltpu

F32 = jnp.float32
BF16 = jnp.bfloat16

N_META = 16
GRID_W = 64
N_HEADS = 8
N_KV_HEADS = 2
HEAD_DIM = 64
ROPE_THETA = 10000.0
SSM_GROUP = 16
SSM_STATE = 64
N_EXPERTS = 16
CAPACITY_FACTOR = 2
LN_EPS = 1e-5
QK_EPS = 1e-6
DEPTH = 1
DEEPNORM_ALPHA = (2.0 * DEPTH) ** 0.25

SSM_CHUNK = 32
SSM_GROUPS_TOGETHER = 4
ROUTE_TILE = 256
GATE_LANES = 128
PIECE_SHIFT = 6
PIECE = 1 << PIECE_SHIFT
META_PAD = 128
V_ROWS = HEAD_DIM + 16
VMEM_LIMIT = 56 * 1024 * 1024


def _ln(x, g, b):
    mu = jnp.mean(x, axis=-1, keepdims=True)
    xc = x - mu
    var = jnp.mean(xc * xc, axis=-1, keepdims=True)
    return xc * lax.rsqrt(var + LN_EPS) * g + b


def _sigmoid(x):
    return 1.0 / (1.0 + jnp.exp(-x))


def _split(t):
    hi = t.astype(BF16)
    lo = (t - hi.astype(F32)).astype(BF16)
    return hi, lo


def _dot(a, b):
    return jnp.dot(a, b, preferred_element_type=F32)


def _dot_nt(a, b):
    return lax.dot_general(a, b, (((1,), (1,)), ((), ())), preferred_element_type=F32)


def _inproj_kernel(x_ref, cosr_ref, sinr_ref, cost_ref, sint_ref, wt_ref, wn_ref, lng_ref, lnb_ref,
                   qg_ref, kg_ref, bd_ref, qt_ref, k_ref, vt_ref, u_ref):
    aw = N_HEADS * HEAD_DIM
    kw = N_KV_HEADS * HEAD_DIM
    half = HEAD_DIM // 2
    h = _ln(x_ref[...], lng_ref[...], lnb_ref[...])
    hb = h.astype(BF16)
    pt = _dot_nt(wt_ref[...], hb)
    pn = _dot(hb, wn_ref[...])
    tm = hb.shape[0]

    qt = pt[0:aw].reshape(N_HEADS, HEAD_DIM, tm)
    ms = jnp.mean(qt * qt, axis=1, keepdims=True)
    qn = qt * lax.rsqrt(ms + QK_EPS) * qg_ref[...]
    x0 = qn[:, 0:half, :]
    x1 = qn[:, half:, :]
    c = cost_ref[...][None]
    s = sint_ref[...][None]
    qr = jnp.concatenate([x0 * c - x1 * s, x0 * s + x1 * c], axis=1)
    qt_ref[...] = qr.reshape(aw, tm).astype(BF16)
    vrow = lax.broadcasted_iota(jnp.int32, (N_KV_HEADS, V_ROWS - HEAD_DIM, tm), 1)
    vt_ref[...] = jnp.concatenate([pt[aw:aw + kw].reshape(N_KV_HEADS, HEAD_DIM, tm),
                                   jnp.where(vrow == 0, 1.0, 0.0)], axis=1).astype(BF16)

    kk = pn[:, 0:kw]
    hi, lo = _split(kk * kk)
    bd = bd_ref[...]
    msk = _dot(hi, bd) + _dot(lo, bd)
    kn = kk * lax.rsqrt(msk + QK_EPS) * kg_ref[...]
    lane = lax.broadcasted_iota(jnp.int32, kn.shape, 1)
    first = (lane & (HEAD_DIM - 1)) < half
    partner = jnp.where(first, pltpu.roll(kn, kw - half, 1), pltpu.roll(kn, half, 1))
    kr = (kn * cosr_ref[...] + partner * sinr_ref[...]).astype(BF16)
    for g in range(N_KV_HEADS):
        k_ref[g] = kr[:, g * HEAD_DIM:(g + 1) * HEAD_DIM]
    u_ref[...] = pn[:, kw:]


def _inproj(x2, cosr, sinr, cost, sint, wt, wn, lng, lnb, qg3, kg, bd, tm, n_tab_blocks):
    rows, d = x2.shape
    aw = N_HEADS * HEAD_DIM
    kw = N_KV_HEADS * HEAD_DIM
    uw = wn.shape[1] - kw
    half = HEAD_DIM // 2
    full = lambda a: pl.BlockSpec(a.shape, lambda i: (0,) * a.ndim)
    return pl.pallas_call(
        _inproj_kernel,
        grid=(rows // tm,),
        in_specs=[
            pl.BlockSpec((tm, d), lambda i: (i, 0)),
            pl.BlockSpec((tm, kw), lambda i: (i % n_tab_blocks, 0)),
            pl.BlockSpec((tm, kw), lambda i: (i % n_tab_blocks, 0)),
            pl.BlockSpec((half, tm), lambda i: (0, i % n_tab_blocks)),
            pl.BlockSpec((half, tm), lambda i: (0, i % n_tab_blocks)),
            full(wt), full(wn), full(lng), full(lnb), full(qg3), full(kg), full(bd),
        ],
        out_specs=[
            pl.BlockSpec((aw, tm), lambda i: (0, i)),
            pl.BlockSpec((N_KV_HEADS, tm, HEAD_DIM), lambda i: (0, i, 0)),
            pl.BlockSpec((N_KV_HEADS, V_ROWS, tm), lambda i: (0, 0, i)),
            pl.BlockSpec((tm, uw), lambda i: (i, 0)),
        ],
        out_shape=[
            jax.ShapeDtypeStruct((aw, rows), BF16),
            jax.ShapeDtypeStruct((N_KV_HEADS, rows, HEAD_DIM), BF16),
            jax.ShapeDtypeStruct((N_KV_HEADS, V_ROWS, rows), BF16),
            jax.ShapeDtypeStruct((rows, uw), F32),
        ],
        compiler_params=pltpu.CompilerParams(dimension_semantics=("arbitrary",),
                                             vmem_limit_bytes=VMEM_LIMIT),
        name="inproj",
    )(x2, cosr, sinr, cost, sint, wt, wn, lng, lnb, qg3, kg, bd)


def _attn_kernel(qt_ref, k0_ref, k1_ref, vta_ref, vtb_ref, km_ref, vtm_ref, o_ref, m_sc, acc_sc, s_sc, mb_sc):
    j = pl.program_id(2)
    last = pl.num_programs(2) - 1
    grp = N_HEADS // N_KV_HEADS
    heads = lambda h: slice(h * HEAD_DIM, (h + 1) * HEAD_DIM)

    def score(k_ref, slot, h):
        s = _dot(k_ref[h // grp], qt_ref[heads(h), :])
        s_sc[slot, h] = s
        mb_sc[slot, h:h + 1, :] = jnp.max(s, axis=0, keepdims=True)

    def softmax_pv(h, s, m_blk, vt_g):
        m_prev = m_sc[h:h + 1, :]
        m_new = jnp.maximum(m_prev, m_blk)
        alpha = jnp.exp2(m_prev - m_new)
        p = jnp.exp2(s - m_new).astype(BF16)
        acc_sc[h] = alpha * acc_sc[h] + _dot(vt_g, p)
        m_sc[h:h + 1, :] = m_new

    def consume(vt_ref, slot, h):
        softmax_pv(h, s_sc[slot, h], mb_sc[slot, h:h + 1, :], vt_ref[h // grp])

    @pl.when(j == 0)
    def _first():
        m_sc[...] = jnp.full(m_sc.shape, -jnp.inf, F32)
        acc_sc[...] = jnp.zeros(acc_sc.shape, F32)
        for h in range(N_HEADS):
            s = _dot(km_ref[h // grp], qt_ref[heads(h), :])
            softmax_pv(h, s, jnp.max(s, axis=0, keepdims=True), vtm_ref[h // grp])
            score(k0_ref, 0, h)
        for h in range(N_HEADS):
            score(k1_ref, 1, h)
            consume(vtb_ref, 0, h)

    @pl.when((j > 0) & (j < last))
    def _middle():
        for h in range(N_HEADS):
            score(k0_ref, 0, h)
            consume(vta_ref, 1, h)
        for h in range(N_HEADS):
            score(k1_ref, 1, h)
            consume(vtb_ref, 0, h)

    @pl.when(j == last)
    def _last():
        for h in range(N_HEADS):
            consume(vta_ref, 1, h)
        acc = acc_sc[...]
        tq = acc.shape[2]
        out_t = (acc[:, 0:HEAD_DIM, :] / acc[:, HEAD_DIM:HEAD_DIM + 1, :]).reshape(N_HEADS * HEAD_DIM, tq)
        o_ref[...] = out_t.T.astype(BF16)


def _attention(qt, kk, vt, km, vtm, nb, tq, tk):
    aw, rows = qt.shape
    s = rows // nb
    nq = s // tq
    nkb = s // tk
    kblk = lambda f: pl.BlockSpec((N_KV_HEADS, tk, HEAD_DIM),
                                  lambda bi, qi, j: (0, bi * nkb + jnp.clip(f(j), 0, nkb - 1), 0))
    vblk = lambda f: pl.BlockSpec((N_KV_HEADS, V_ROWS, tk),
                                  lambda bi, qi, j: (0, 0, bi * nkb + jnp.clip(f(j), 0, nkb - 1)))
    return pl.pallas_call(
        _attn_kernel,
        grid=(nb, nq, nkb // 2 + 1),
        in_specs=[
            pl.BlockSpec((aw, tq), lambda bi, qi, j: (0, bi * nq + qi)),
            kblk(lambda j: 2 * j), kblk(lambda j: 2 * j + 1),
            vblk(lambda j: 2 * j - 1), vblk(lambda j: 2 * j),
            pl.BlockSpec(km.shape, lambda bi, qi, j: (0, 0, 0)),
            pl.BlockSpec(vtm.shape, lambda bi, qi, j: (0, 0, 0)),
        ],
        out_specs=pl.BlockSpec((tq, aw), lambda bi, qi, j: (bi * nq + qi, 0)),
        out_shape=jax.ShapeDtypeStruct((rows, aw), BF16),
        scratch_shapes=[
            pltpu.VMEM((N_HEADS, tq), F32),
            pltpu.VMEM((N_HEADS, V_ROWS, tq), F32),
            pltpu.VMEM((2, N_HEADS, tk, tq), F32),
            pltpu.VMEM((2, N_HEADS, tq), F32),
        ],
        compiler_params=pltpu.CompilerParams(
            dimension_semantics=("arbitrary", "arbitrary", "arbitrary"),
            vmem_limit_bytes=VMEM_LIMIT),
        name="attn",
    )(qt, kk, kk, vt, vt, km, vtm)


def _ssm_prep_kernel(*refs):
    for d in range(2):
        _ssm_prep_direction(d, *refs)


def _ssm_prep_direction(d, arow_ref, acol_ref, ldt_ref, bt_ref, ct_ref, m_ref, w_ref, v_ref, at_ref):
    t_chunk = SSM_CHUNK
    shift = int(math.log2(SSM_GROUP))
    df = float(d)
    dt = jnp.exp(ldt_ref[d, 0])

    def abar(ar, ai):
        mag = jnp.exp(ar * dt)
        ang = ai * dt
        return mag * jnp.cos(ang), mag * jnp.sin(ang)

    def cpow(ar, ai, e):
        mag = jnp.exp(ar * dt * e)
        ang = ai * dt * e
        return mag * jnp.cos(ang), mag * jnp.sin(ang)

    ar_r = arow_ref[d, 0, 0:1, :]
    ai_r = arow_ref[d, 0, 1:2, :]
    abr, abi = abar(ar_r, ai_r)
    nr = abr - 1.0
    ni = abi
    den = ar_r * ar_r + ai_r * ai_r
    cr = (nr * ar_r + ni * ai_r) / den
    ci = (ni * ar_r - nr * ai_r) / den
    nstate = bt_ref.shape[4]
    nrow = t_chunk * SSM_GROUP

    def tile_rows(x):
        return jnp.broadcast_to(x[None], (t_chunk, SSM_GROUP, nstate)).reshape(nrow, nstate)

    btr = tile_rows(bt_ref[d, 0, 0])
    bti = tile_rows(bt_ref[d, 0, 1])
    bbr = cr * btr - ci * bti
    bbi = cr * bti + ci * btr
    jr = (lax.broadcasted_iota(jnp.int32, (nrow, 1), 0) >> shift).astype(F32)
    lj = jr + df * ((t_chunk - 1) - 2.0 * jr)
    step = lax.broadcasted_iota(jnp.int32, (t_chunk, 1), 0).astype(F32)
    lstep = step + df * ((t_chunk - 1) - 2.0 * step)

    def rep_rows(x):
        return jnp.broadcast_to(x[:, None, :], (t_chunk, SSM_GROUP, nstate)).reshape(nrow, nstate)

    enr, eni = [rep_rows(x) for x in cpow(ar_r, ai_r, -lstep)]
    bmr = enr * bbr - eni * bbi
    bmi = enr * bbi + eni * bbr
    ewr, ewi = [rep_rows(x) for x in cpow(ar_r, ai_r, (t_chunk - 1) - lstep)]
    w_ref[d, 0] = jnp.concatenate([ewr * bbr - ewi * bbi, ewr * bbi + ewi * bbr], axis=1).astype(BF16)
    atr, ati = cpow(ar_r, ai_r, float(t_chunk))
    at_ref[d, 0] = jnp.concatenate([atr, ati], axis=1)

    ar_c = acol_ref[d, 0, :, 0:1]
    ai_c = acol_ref[d, 0, :, 1:2]
    ncol = t_chunk * SSM_GROUP
    lane = lax.broadcasted_iota(jnp.int32, (1, ncol), 1)
    tci = lane >> shift
    tile = jnp.where(lax.broadcasted_iota(jnp.int32, (SSM_GROUP, ncol), 0) == (lane & (SSM_GROUP - 1)),
                     1.0, 0.0).astype(BF16)

    def tile_cols(x):
        hi, lo = _split(x)
        lo2 = (x - hi.astype(F32) - lo.astype(F32)).astype(BF16)
        return _dot(hi, tile) + _dot(lo, tile) + _dot(lo2, tile)

    ctr = tile_cols(ct_ref[d, 0, 0])
    cti = tile_cols(ct_ref[d, 0, 1])
    tc = tci.astype(F32)
    lt = tc + df * ((t_chunk - 1) - 2.0 * tc)
    stepl = lax.broadcasted_iota(jnp.int32, (1, t_chunk), 1).astype(F32)
    lstepl = stepl + df * ((t_chunk - 1) - 2.0 * stepl)
    rep = jnp.where(lax.broadcasted_iota(jnp.int32, (t_chunk, ncol), 0) == tci, 1.0, 0.0).astype(BF16)

    def rep_cols(x):
        hi, lo = _split(x)
        lo2 = (x - hi.astype(F32) - lo.astype(F32)).astype(BF16)
        return _dot(hi, rep) + _dot(lo, rep) + _dot(lo2, rep)

    ecr, eci = [rep_cols(x) for x in cpow(ar_c, ai_c, lstepl)]
    cmr = ctr * ecr - cti * eci
    cmi = ctr * eci + cti * ecr
    lhs_hi, lhs_lo = _split(jnp.concatenate([bmr, -bmi], axis=1))
    rhs_hi, rhs_lo = _split(jnp.concatenate([cmr, cmi], axis=0))
    m = _dot(lhs_hi, rhs_hi) + _dot(lhs_hi, rhs_lo) + _dot(lhs_lo, rhs_hi)
    m_ref[d, 0] = jnp.where(lj <= lt, m, 0.0).astype(BF16)
    abr_c, abi_c = abar(ar_c, ai_c)
    c1r = cmr * abr_c - cmi * abi_c
    c1i = cmr * abi_c + cmi * abr_c
    v_ref[d, 0] = jnp.concatenate([c1r, -c1i], axis=0).astype(BF16)


def _ssm_prep(arow, acol, ldt, bt, ct):
    nd, g = arow.shape[0], arow.shape[1]
    p = SSM_STATE
    n = SSM_CHUNK * SSM_GROUP
    blk = lambda a: pl.BlockSpec((nd, 1) + a.shape[2:], lambda gi: (0, gi) + (0,) * (a.ndim - 2))
    return pl.pallas_call(
        _ssm_prep_kernel,
        grid=(g,),
        in_specs=[blk(arow), blk(acol), blk(ldt), blk(bt), blk(ct)],
        out_specs=[
            pl.BlockSpec((nd, 1, n, n), lambda gi: (0, gi, 0, 0)),
            pl.BlockSpec((nd, 1, n, 2 * p), lambda gi: (0, gi, 0, 0)),
            pl.BlockSpec((nd, 1, 2 * p, n), lambda gi: (0, gi, 0, 0)),
            pl.BlockSpec((nd, 1, 1, 2 * p), lambda gi: (0, gi, 0, 0)),
        ],
        out_shape=[
            jax.ShapeDtypeStruct((nd, g, n, n), BF16),
            jax.ShapeDtypeStruct((nd, g, n, 2 * p), BF16),
            jax.ShapeDtypeStruct((nd, g, 2 * p, n), BF16),
            jax.ShapeDtypeStruct((nd, g, 1, 2 * p), F32),
        ],
        compiler_params=pltpu.CompilerParams(dimension_semantics=("arbitrary",)),
        name="ssm_prep",
    )(arow, acol, ldt, bt, ct)


def _chunk_carry(chains):
    p = SSM_STATE
    z0, a0 = chains[0][0], chains[0][1]
    nc = z0.shape[0]
    row = lax.broadcasted_iota(jnp.int32, z0.shape, 0)
    is_re = lax.broadcasted_iota(jnp.int32, a0.shape, 1) < p
    sign = jnp.where(is_re, -1.0, 1.0)

    def parts(ap):
        sw = pltpu.roll(ap, p, 1)
        return jnp.where(is_re, ap, sw), sign * jnp.where(is_re, sw, ap)

    def cmul(x, ar_full, ai_sgn):
        return x * ar_full + pltpu.roll(x, p, 1) * ai_sgn

    coef = [parts(a) for _, a, _, _ in chains]
    firsts = [(nc - 1) if rev else 0 for _, _, _, rev in chains]
    es = [z + jnp.where(row == f, cmul(s0, *c), 0.0) for (z, _, s0, _), f, c in zip(chains, firsts, coef)]
    k = 1
    while k < nc:
        for i, (_, _, _, rev) in enumerate(chains):
            if rev:
                sh = jnp.where(row < nc - k, pltpu.roll(es[i], nc - k, 0), 0.0)
            else:
                sh = jnp.where(row >= k, pltpu.roll(es[i], k, 0), 0.0)
            es[i] = es[i] + cmul(sh, *coef[i])
        k *= 2
        if k < nc:
            coef = [parts(cmul(jnp.where(is_re, ar, sign * ai), ar, ai)) for ar, ai in coef]
    outs = []
    for e, (_, _, s0, rev), f in zip(es, chains, firsts):
        outs.append(jnp.where(row == f, s0, pltpu.roll(e, nc - 1 if rev else 1, 0)))
    return outs


def _ssm_kernel(u_ref, um_ref, m_ref, w_ref, v_ref, at_ref, dvec_ref, y_ref, ug_sc, yg_sc):
    p = SSM_STATE
    t_chunk = SSM_CHUNK
    hgrp = SSM_GROUP
    gb = u_ref.shape[2] // hgrp
    per_tile = u_ref.shape[2] // hgrp
    nc = u_ref.shape[1] // t_chunk
    lane_blk = lax.broadcasted_iota(jnp.int32, (nc, u_ref.shape[2]), 1) >> int(math.log2(hgrp))

    width = u_ref.shape[2]

    def block_transpose(arrs):
        arrs = list(arrs)
        d = len(arrs) // 2
        while d >= 1:
            low_bit = (lane_blk & d) == 0
            for i in range(len(arrs)):
                if i & d == 0:
                    lo, hi = arrs[i], arrs[i + d]
                    arrs[i] = jnp.where(low_bit, lo, pltpu.roll(hi, d * hgrp, 1))
                    arrs[i + d] = jnp.where(low_bit, pltpu.roll(lo, width - d * hgrp, 1), hi)
            d //= 2
        return arrs

    for q in range(t_chunk // per_tile):
        steps = [u_ref[0, pl.ds(q * per_tile + r, nc, stride=t_chunk), :] for r in range(per_tile)]
        for g, tile in enumerate(block_transpose(steps)):
            ug_sc[g, :, q * width:(q + 1) * width] = tile

    def groups(i, carry):
        gs = [i * SSM_GROUPS_TOGETHER + k for k in range(SSM_GROUPS_TOGETHER)]
        us = [ug_sc[g] for g in gs]
        ubs = [u.astype(BF16) for u in us]
        yin = [[_dot(ub, m_ref[d, g]) for d in range(2)] for g, ub in zip(gs, ubs)]
        chains = []
        for g, ub in zip(gs, ubs):
            for d in range(2):
                if d == 0:
                    s0 = _dot(um_ref[g].astype(BF16), w_ref[0, g])[0:1, :]
                else:
                    s0 = jnp.zeros((1, 2 * p), F32)
                chains.append((_dot(ub, w_ref[d, g]), at_ref[d, g], s0, d == 1))
        s_in = _chunk_carry(chains)
        for k, g in enumerate(gs):
            y = dvec_ref[g] * us[k] + yin[k][0] + yin[k][1]
            for d in range(2):
                y = y + _dot(s_in[2 * k + d].astype(BF16), v_ref[d, g])
            yg_sc[g] = y
        return carry

    lax.fori_loop(0, gb // SSM_GROUPS_TOGETHER, groups, 0)

    for q in range(t_chunk // per_tile):
        srcs = [yg_sc[g, :, q * width:(q + 1) * width] for g in range(gb)]
        for r, row in enumerate(block_transpose(srcs)):
            y_ref[0, pl.ds(q * per_tile + r, nc, stride=t_chunk), :] = row


def _ssm(u3, um2, m, w, v, at, dvec):
    b, s, sw = u3.shape
    g = sw // SSM_GROUP
    n = SSM_CHUNK * SSM_GROUP
    p = SSM_STATE
    lanes = 128
    gb = lanes // SSM_GROUP
    return pl.pallas_call(
        _ssm_kernel,
        grid=(b, sw // lanes),
        in_specs=[
            pl.BlockSpec((1, s, lanes), lambda bi, gi: (bi, 0, gi)),
            pl.BlockSpec((gb, 8, n), lambda bi, gi: (gi, 0, 0)),
            pl.BlockSpec((2, gb, n, n), lambda bi, gi: (0, gi, 0, 0)),
            pl.BlockSpec((2, gb, n, 2 * p), lambda bi, gi: (0, gi, 0, 0)),
            pl.BlockSpec((2, gb, 2 * p, n), lambda bi, gi: (0, gi, 0, 0)),
            pl.BlockSpec((2, gb, 1, 2 * p), lambda bi, gi: (0, gi, 0, 0)),
            pl.BlockSpec((gb, 1, n), lambda bi, gi: (gi, 0, 0)),
        ],
        out_specs=pl.BlockSpec((1, s, lanes), lambda bi, gi: (bi, 0, gi)),
        out_shape=jax.ShapeDtypeStruct((b, s, sw), F32),
        scratch_shapes=[pltpu.VMEM((gb, s // SSM_CHUNK, n), F32), pltpu.VMEM((gb, s // SSM_CHUNK, n), F32)],
        compiler_params=pltpu.CompilerParams(dimension_semantics=("arbitrary", "arbitrary"),
                                             vmem_limit_bytes=VMEM_LIMIT),
        name="ssm",
    )(u3, um2, m, w, v, at, dvec)


def _merge_kernel(x_ref, ya_ref, ys_ref, wgt_ref, wglu_ref, bglu_ref, wab_ref, wsb_ref, wo_ref,
                  lng_ref, lnb_ref, l1g_ref, l1b_ref, wr_ref,
                  h1_ref, tok_ref, afft_ref):
    d = x_ref.shape[1]
    h = _ln(x_ref[...], lng_ref[...], lnb_ref[...])
    gates = _dot(h.astype(BF16), wgt_ref[...])
    ga = _sigmoid(gates[:, 0:d])
    gs = _sigmoid(gates[:, d:2 * d])
    ys = ys_ref[...]
    y = 0.5 * ys * (1.0 + jnp.tanh(math.sqrt(2.0 / math.pi) * (ys + 0.044715 * (ys * ys * ys))))
    yg = y * _sigmoid(_dot(y.astype(BF16), wglu_ref[...]) + bglu_ref[...])
    merged = ga * _dot(ya_ref[...], wab_ref[...]) + gs * _dot(yg.astype(BF16), wsb_ref[...])
    h1 = _ln(DEEPNORM_ALPHA * h + _dot(merged.astype(BF16), wo_ref[...]), l1g_ref[...], l1b_ref[...])
    h1_ref[...] = h1
    hi, lo = _split(h1)
    ne = afft_ref.shape[0]
    cross = _dot(hi, wr_ref[...])
    logits = cross[:, 0:GATE_LANES] + cross[:, GATE_LANES:] + _dot(lo, wr_ref[:, 0:GATE_LANES])
    lane = lax.broadcasted_iota(jnp.int32, logits.shape, 1)
    logits = jnp.where(lane < 3 * ne, logits, -jnp.inf)
    ex = jnp.exp(logits - jnp.max(logits, axis=1, keepdims=True))
    aff = ex / jnp.sum(jnp.where(lane < ne, ex, 0.0), axis=1, keepdims=True)
    p0 = aff.astype(BF16)
    r1 = aff - p0.astype(F32)
    p1 = r1.astype(BF16)
    p2 = (r1 - p1.astype(F32)).astype(BF16)
    pieces = jnp.where(lane < ne, p0, jnp.where(lane < 2 * ne, p1, p2))
    tok_ref[...] = jnp.concatenate([hi, pieces], axis=1)
    afft_ref[...] = aff.T[0:ne, :]


def _merge(x2, ya, ys, wgt, wglu, bglu, wab, wsb, wo, lng, lnb, l1g, l1b, wr, ne, tm):
    rows, d = x2.shape
    full = lambda a: pl.BlockSpec(a.shape, lambda i: (0,) * a.ndim)
    return pl.pallas_call(
        _merge_kernel,
        grid=(rows // tm,),
        in_specs=[
            pl.BlockSpec((tm, d), lambda i: (i, 0)),
            pl.BlockSpec((tm, ya.shape[1]), lambda i: (i, 0)),
            pl.BlockSpec((tm, ys.shape[1]), lambda i: (i, 0)),
            full(wgt), full(wglu), full(bglu), full(wab), full(wsb), full(wo),
            full(lng), full(lnb), full(l1g), full(l1b), full(wr),
        ],
        out_specs=[
            pl.BlockSpec((tm, d), lambda i: (i, 0)),
            pl.BlockSpec((tm, d + GATE_LANES), lambda i: (i, 0)),
            pl.BlockSpec((ne, tm), lambda i: (0, i)),
        ],
        out_shape=[
            jax.ShapeDtypeStruct((rows, d), F32),
            jax.ShapeDtypeStruct((rows, d + GATE_LANES), BF16),
            jax.ShapeDtypeStruct((ne, rows), F32),
        ],
        compiler_params=pltpu.CompilerParams(dimension_semantics=("arbitrary",),
                                             vmem_limit_bytes=VMEM_LIMIT),
        name="merge",
    )(x2, ya, ys, wgt, wglu, bglu, wab, wsb, wo, lng, lnb, l1g, l1b, wr)


def _route_kernel(afft_ref, tri_ref, scl_ref, lo_ref, *, cap):
    aff = afft_ref[...]
    ne, s = aff.shape
    capf = float(cap)

    def as_float(bits):
        return lax.bitcast_convert_type(bits, F32)

    def search(i, t):
        cand = t | jnp.left_shift(jnp.int32(1), 30 - i)
        cnt = jnp.sum(jnp.where(aff >= as_float(cand), 1.0, 0.0), axis=1, keepdims=True)
        return jnp.where(cnt >= capf, cand, t)

    thr_bits = lax.fori_loop(0, 31, search, jnp.zeros((ne, 1), jnp.int32))
    gt = aff >= as_float(thr_bits + 1)
    eq = (aff >= as_float(thr_bits)) & jnp.logical_not(gt)
    need = capf - jnp.sum(jnp.where(gt, 1.0, 0.0), axis=1, keepdims=True)
    tri = tri_ref[...]
    rt = ROUTE_TILE
    nt = s // rt
    col = lax.broadcasted_iota(jnp.int32, (ne, nt), 1)
    carry_eq = jnp.zeros((ne, 1), F32)
    carry_sel = jnp.zeros((ne, 1), F32)
    lo_val = jnp.zeros((ne, nt), F32)
    for t in range(nt):
        sl = slice(t * rt, (t + 1) * rt)
        eq_b = eq[:, sl]
        ceq = _dot(jnp.where(eq_b, 1.0, 0.0).astype(BF16), tri)
        sel_b = gt[:, sl] | (eq_b & ((ceq + carry_eq) <= need))
        carry_eq = carry_eq + ceq[:, rt - 1:rt]
        csel = _dot(jnp.where(sel_b, 1.0, 0.0).astype(BF16), tri)
        scl_ref[0, :, sl] = jnp.where(sel_b, csel, 0.0)
        lo_val = jnp.where(col == t, carry_sel, lo_val)
        carry_sel = carry_sel + csel[:, rt - 1:rt]
    lo_ref[0] = lo_val


def _route(afft, tri, nb, cap):
    ne, rows = afft.shape
    s = rows // nb
    nt = s // ROUTE_TILE
    return pl.pallas_call(
        functools.partial(_route_kernel, cap=cap),
        grid=(nb,),
        in_specs=[
            pl.BlockSpec((ne, s), lambda b: (0, b)),
            pl.BlockSpec(tri.shape, lambda b: (0, 0)),
        ],
        out_specs=[
            pl.BlockSpec((1, ne, s), lambda b: (b, 0, 0)),
            pl.BlockSpec((1, ne, nt), lambda b: (b, 0, 0)),
        ],
        out_shape=[
            jax.ShapeDtypeStruct((nb, ne, s), F32),
            jax.ShapeDtypeStruct((nb, ne, nt), F32),
        ],
        compiler_params=pltpu.CompilerParams(dimension_semantics=("arbitrary",)),
        name="route",
    )(afft, tri)


def _ffn_kernel(lo_ref, t_ref, scl_ref, wg_ref, wu_ref, wd_ref, ye_ref, xe_sc, xb_sc, gate_sc, acc_sc, *, cap):
    fc = pl.program_id(2)

    @pl.when(fc == 0)
    def _gather():
        _ffn_gather(lo_ref, t_ref, scl_ref, xe_sc, xb_sc, gate_sc, cap=cap)
        acc_sc[...] = jnp.zeros(acc_sc.shape, F32)

    xb = xb_sc[...]
    g = _dot(xb, wg_ref[0, 0].astype(BF16))
    u = _dot(xb, wu_ref[0, 0].astype(BF16))
    hh = (g * _sigmoid(g)) * u
    acc_sc[...] += _dot(hh.astype(BF16), wd_ref[0, 0].astype(BF16))

    @pl.when(fc == pl.num_programs(2) - 1)
    def _emit():
        ye_ref[0, 0] = (acc_sc[...] * gate_sc[...]).astype(BF16)


def _ffn_gather(lo_ref, t_ref, scl_ref, xe_sc, xb_sc, gate_sc, *, cap):
    b = pl.program_id(0)
    e = pl.program_id(1)
    ne = pl.num_programs(1)
    rt = ROUTE_TILE
    nt = t_ref.shape[1] // rt
    xe_sc[...] = jnp.zeros(xe_sc.shape, F32)
    r = lax.broadcasted_iota(jnp.int32, (PIECE, rt), 0).astype(F32)

    def tile_info(tau):
        base = (b * ne + e) * (nt + 1) + tau
        lo = lo_ref[base]
        n_pieces = jnp.right_shift((lo & 7) + (lo_ref[base + 1] - lo) + (PIECE - 1), PIECE_SHIFT)
        return lo, n_pieces

    def add_piece(tau, lo, p):
        off = lo & 7
        scl = scl_ref[0, 0, pl.ds(tau, 1), :]
        tok = t_ref[0, pl.ds(pl.multiple_of(tau * rt, rt), rt), :]
        shift = (off - 1 - p * PIECE).astype(F32)
        onehot = jnp.where((scl > 0.0) & ((scl + shift) == r), 1.0, 0.0).astype(BF16)
        xe_sc[pl.ds(pl.multiple_of(lo - off + p * PIECE, 8), PIECE), :] += _dot(onehot, tok)

    def first_piece(tau, carry):
        add_piece(tau, tile_info(tau)[0], 0)
        return carry

    def more_pieces(tau, carry):
        lo, n_pieces = tile_info(tau)
        lax.fori_loop(1, n_pieces, lambda p, c: (add_piece(tau, lo, p), c)[1], 0)
        return carry

    lax.fori_loop(0, nt, first_piece, 0, unroll=4)
    lax.fori_loop(0, nt, more_pieces, 0)
    d = xb_sc.shape[1]
    xb_sc[...] = xe_sc[0:cap, 0:d].astype(BF16)
    gl = xe_sc[0:cap, d:d + GATE_LANES]
    lane = lax.broadcasted_iota(jnp.int32, gl.shape, 1)
    mine = ((lane & (N_EXPERTS - 1)) == e) & (lane < 3 * N_EXPERTS)
    gate_sc[...] = jnp.sum(jnp.where(mine, gl, 0.0), axis=1, keepdims=True)


def _ffn(lo_i, t, scl4, wg, wu, wd, layer, cap, f_chunk):
    b, s, dx = t.shape
    _, ne, d, f = wg.shape
    nt = s // ROUTE_TILE
    grid_spec = pltpu.PrefetchScalarGridSpec(
        num_scalar_prefetch=1,
        grid=(b, ne, f // f_chunk),
        in_specs=[
            pl.BlockSpec((1, s, dx), lambda bi, ei, fi, lo: (bi, 0, 0), pipeline_mode=pl.Buffered(1)),
            pl.BlockSpec((1, 1, nt, ROUTE_TILE), lambda bi, ei, fi, lo: (bi, ei, 0, 0)),
            pl.BlockSpec((1, 1, d, f_chunk), lambda bi, ei, fi, lo: (layer, ei, 0, fi)),
            pl.BlockSpec((1, 1, d, f_chunk), lambda bi, ei, fi, lo: (layer, ei, 0, fi)),
            pl.BlockSpec((1, 1, f_chunk, d), lambda bi, ei, fi, lo: (layer, ei, fi, 0)),
        ],
        out_specs=pl.BlockSpec((1, 1, cap, d), lambda bi, ei, fi, lo: (bi, ei, 0, 0)),
        scratch_shapes=[pltpu.VMEM((cap + PIECE, dx), F32), pltpu.VMEM((cap, d), BF16),
                        pltpu.VMEM((cap, 1), F32), pltpu.VMEM((cap, d), F32)],
    )
    return pl.pallas_call(
        functools.partial(_ffn_kernel, cap=cap),
        grid_spec=grid_spec,
        out_shape=jax.ShapeDtypeStruct((b, ne, cap, d), BF16),
        compiler_params=pltpu.CompilerParams(dimension_semantics=("arbitrary", "arbitrary", "arbitrary"),
                                             vmem_limit_bytes=VMEM_LIMIT),
        name="ffn",
    )(lo_i, t, scl4, wg, wu, wd)


def _combine_kernel(lo_ref, h1_ref, sclt_ref, ye_hbm, g_ref, b_ref, o_ref, win, xwin, acc_sc, sem, xsem, *, cap):
    b = pl.program_id(0)
    tau = pl.program_id(1)
    nb = pl.num_programs(0)
    nt = pl.num_programs(1)
    ne = win.shape[1]
    d = win.shape[3]
    step = b * nt + tau
    slot = step & 1

    def pieces(bi, ti, e):
        base = (bi * ne + e) * (nt + 1) + ti
        lo = lo_ref[base]
        n_sel = lo_ref[base + 1] - lo
        n_pieces = jnp.maximum(jnp.right_shift((lo & 15) + n_sel + (PIECE - 1), PIECE_SHIFT), 1)
        first = jnp.minimum(lo - (lo & 15), cap - n_pieces * PIECE)
        return lo, first, n_pieces

    def copy(bi, e, first, p, buf, s):
        start = pl.multiple_of(first + p * PIECE, 16)
        return pltpu.make_async_copy(ye_hbm.at[bi, e, pl.ds(start, PIECE), :], buf, s)

    def first_pieces(bi, ti, sl):
        return [copy(bi, e, pieces(bi, ti, e)[1], 0, win.at[sl, e], sem.at[sl, e]) for e in range(ne)]

    @pl.when(step == 0)
    def _prime():
        for cp in first_pieces(b, tau, slot):
            cp.start()

    @pl.when(step + 1 < nb * nt)
    def _prefetch():
        wrap = tau + 1 == nt
        for cp in first_pieces(jnp.where(wrap, b + 1, b), jnp.where(wrap, 0, tau + 1), 1 - slot):
            cp.start()

    for cp in first_pieces(b, tau, slot):
        cp.wait()

    sclt = sclt_ref[0]
    col = lax.broadcasted_iota(jnp.int32, (1, ne * PIECE), 1)
    grp = jnp.right_shift(col, PIECE_SHIFT)
    expand = jnp.where(lax.broadcasted_iota(jnp.int32, (ne, ne * PIECE), 0) == grp, 1.0, 0.0).astype(BF16)
    scl = _dot(sclt.astype(BF16), expand)
    shift = jnp.zeros(col.shape, F32)
    for e in range(ne):
        lo, first, _ = pieces(b, tau, e)
        shift = jnp.where(grp == e, (lo - first - 1).astype(F32), shift)
    r = (col & (PIECE - 1)).astype(F32)
    onehot = jnp.where((scl > 0.0) & ((scl + shift) == r), 1.0, 0.0).astype(BF16)
    acc_sc[...] = DEEPNORM_ALPHA * h1_ref[0] + _dot(onehot, win[slot].reshape(ne * PIECE, d))

    for e in range(ne):
        lo, first, n_pieces = pieces(b, tau, e)

        def extra(p, c, e=e, lo=lo, first=first):
            cp = copy(b, e, first, p, xwin, xsem.at[0])
            cp.start()
            cp.wait()
            se = sclt_ref[0][:, e:e + 1]
            rr = lax.broadcasted_iota(jnp.int32, (se.shape[0], PIECE), 1).astype(F32)
            oh = jnp.where((se > 0.0) & ((se + (lo - first - p * PIECE - 1).astype(F32)) == rr), 1.0, 0.0)
            acc_sc[...] += _dot(oh.astype(BF16), xwin[...])
            return c

        lax.fori_loop(1, n_pieces, extra, 0)
    o_ref[0] = _ln(acc_sc[...], g_ref[...], b_ref[...])


def _combine(lo_i, h1, sclt, ye, g, bb, cap):
    b, s, d = h1.shape
    ne = sclt.shape[2]
    rt = ROUTE_TILE
    grid_spec = pltpu.PrefetchScalarGridSpec(
        num_scalar_prefetch=1,
        grid=(b, s // rt),
        in_specs=[
            pl.BlockSpec((1, rt, d), lambda bi, ti, lo: (bi, ti, 0)),
            pl.BlockSpec((1, rt, ne), lambda bi, ti, lo: (bi, ti, 0)),
            pl.BlockSpec(memory_space=pl.ANY),
            pl.BlockSpec(g.shape, lambda bi, ti, lo: (0, 0)),
            pl.BlockSpec(bb.shape, lambda bi, ti, lo: (0, 0)),
        ],
        out_specs=pl.BlockSpec((1, rt, d), lambda bi, ti, lo: (bi, ti, 0)),
        scratch_shapes=[pltpu.VMEM((2, ne, PIECE, d), BF16), pltpu.VMEM((PIECE, d), BF16),
                        pltpu.VMEM((rt, d), F32),
                        pltpu.SemaphoreType.DMA((2, ne)), pltpu.SemaphoreType.DMA((1,))],
    )
    return pl.pallas_call(
        functools.partial(_combine_kernel, cap=cap),
        grid_spec=grid_spec,
        out_shape=jax.ShapeDtypeStruct((b, s, d), F32),
        compiler_params=pltpu.CompilerParams(dimension_semantics=("arbitrary", "arbitrary"),
                                             vmem_limit_bytes=VMEM_LIMIT),
        name="combine",
    )(lo_i, h1, sclt, ye, g, bb)


def _rope_tables(n_tokens):
    half = HEAD_DIM // 2
    inv_freq = ROPE_THETA ** (-jnp.arange(0, half, 2, dtype=F32) / half)
    rows = n_tokens // GRID_W
    row = jnp.repeat(jnp.arange(rows, dtype=F32), GRID_W)
    colv = jnp.tile(jnp.arange(GRID_W, dtype=F32), rows)
    ang = jnp.concatenate([row[:, None] * inv_freq, colv[:, None] * inv_freq], axis=-1)
    return jnp.cos(ang), jnp.sin(ang)


def kernel(x, meta_tokens, ln_in_g, ln_in_b, w_in, q_norm_g, k_norm_g, ssm_a_re, ssm_a_im, ssm_log_dt,
           ssm_b_re, ssm_b_im, ssm_c_re, ssm_c_im, ssm_d, w_glu, b_glu, w_attn_br, w_ssm_br, w_o,
           ln1_g, ln1_b, w_router, w_gate_e, w_up_e, w_down_e, ln2_g, ln2_b):
    b, s, d = x.shape
    aw = N_HEADS * HEAD_DIM
    kw = N_KV_HEADS * HEAD_DIM
    sw = d // 2
    g = sw // SSM_GROUP
    half = HEAD_DIM // 2
    cap = CAPACITY_FACTOR * s // N_EXPERTS
    rows = b * s
    l = 0

    perm = np.concatenate([np.arange(0, HEAD_DIM, 2), np.arange(1, HEAD_DIM, 2)])
    qcols = np.concatenate([h * HEAD_DIM + perm for h in range(N_HEADS)])
    kcols = aw + np.concatenate([h * HEAD_DIM + perm for h in range(N_KV_HEADS)])
    wl = w_in[l]
    w_t = jnp.concatenate([wl[:, qcols], wl[:, aw + kw:aw + 2 * kw]], axis=1).T.astype(BF16)
    w_n = jnp.concatenate([wl[:, kcols], wl[:, aw + 2 * kw:aw + 2 * kw + sw]], axis=1).astype(BF16)
    w_gates = wl[:, aw + 2 * kw + sw:].astype(BF16)
    qg3 = (jnp.tile(q_norm_g[l][perm], N_HEADS) * (HEAD_DIM ** -0.5 * math.log2(math.e))).reshape(N_HEADS, HEAD_DIM, 1)
    kg = jnp.tile(k_norm_g[l][perm], N_KV_HEADS)[None, :]
    bd = jnp.asarray(np.kron(np.eye(N_KV_HEADS), np.full((HEAD_DIM, HEAD_DIM), 1.0 / HEAD_DIM)), BF16)
    lng = ln_in_g[None, :]
    lnb = ln_in_b[None, :]
    cos, sin = _rope_tables(s)
    cosr = jnp.tile(cos, (1, 2 * N_KV_HEADS))
    sinr = jnp.tile(jnp.concatenate([-sin, sin], axis=1), (1, N_KV_HEADS))
    cost = cos.T
    sint = sin.T

    x2 = x.reshape(rows, d)
    tm_in = 1024
    qt, kk, vt, u2 = _inproj(x2, cosr, sinr, cost, sint, w_t, w_n, lng, lnb, qg3, kg, bd, tm_in, s // tm_in)
    meta_p = jnp.pad(meta_tokens, ((0, META_PAD - N_META), (0, 0)))
    ones_r = jnp.ones((META_PAD, kw), F32)
    _, km, vtm, um = _inproj(meta_p, ones_r, jnp.zeros_like(ones_r), jnp.ones((half, META_PAD), F32),
                             jnp.zeros((half, META_PAD), F32), w_t, w_n, lng, lnb, qg3, kg, bd, META_PAD, 1)
    um = um[:N_META]
    km = km[:, :N_META, :]
    vtm = vtm[:, :, :N_META]

    ya = _attention(qt, kk, vt, km, vtm, b, 512, 512)

    t_chunk = SSM_CHUNK
    hgrp = SSM_GROUP
    n = t_chunk * hgrp
    arow = jnp.stack([ssm_a_re[l], ssm_a_im[l]], axis=2)
    acol = jnp.stack([ssm_a_re[l], ssm_a_im[l]], axis=3)
    ldt = ssm_log_dt[l][:, :, None, None]
    bt = jnp.stack([ssm_b_re[l], ssm_b_im[l]], axis=2)
    bt = jnp.swapaxes(bt, 3, 4)
    ct = jnp.stack([ssm_c_re[l], ssm_c_im[l]], axis=2)
    ct = jnp.swapaxes(ct, 3, 4)
    mm, ww, vv, at = _ssm_prep(arow, acol, ldt, bt, ct)
    nc = s // t_chunk
    um_g = um.reshape(N_META, g, hgrp).transpose(1, 0, 2).reshape(g, 1, N_META * hgrp)
    um_g = jnp.pad(um_g, ((0, 0), (0, 7), (n - N_META * hgrp, 0)))
    dvec = jnp.tile(ssm_d[l], (1, t_chunk))[:, None, :]
    ys = _ssm(u2.reshape(b, s, sw), um_g, mm, ww, vv, at, dvec).reshape(rows, sw)

    wr_hi = w_router[l].astype(BF16)
    wr_lo = (w_router[l] - wr_hi.astype(F32)).astype(BF16)
    lane_pad = ((0, 0), (0, GATE_LANES - 3 * N_EXPERTS))
    wr = jnp.concatenate([jnp.pad(jnp.tile(wr_hi, (1, 3)), lane_pad), jnp.pad(jnp.tile(wr_lo, (1, 3)), lane_pad)],
                         axis=1)
    h1, tok, afft = _merge(
        x2, ya, ys, w_gates, w_glu[l].astype(BF16), b_glu[l][None, :],
        w_attn_br[l].astype(BF16), w_ssm_br[l].astype(BF16), w_o[l].astype(BF16),
        lng, lnb, ln1_g[l][None, :], ln1_b[l][None, :], wr, N_EXPERTS, 512)

    tri = jnp.asarray(np.triu(np.ones((ROUTE_TILE, ROUTE_TILE), np.float32)), BF16)
    scl, lo_f = _route(afft, tri, b, cap)
    nt = s // ROUTE_TILE
    lo_i = jnp.pad(lo_f.astype(jnp.int32), ((0, 0), (0, 0), (0, 1)), constant_values=cap).reshape(-1)
    ye = _ffn(lo_i, tok.reshape(b, s, d + GATE_LANES), scl.reshape(b, N_EXPERTS, nt, ROUTE_TILE),
              w_gate_e, w_up_e, w_down_e, l, cap, 512)
    out = _combine(lo_i, h1.reshape(b, s, d), jnp.swapaxes(scl, 1, 2), ye,
                   ln2_g[l][None, :], ln2_b[l][None, :], cap)
    return out
```

```python
import functools
import math

import numpy as np
import jax
import jax.numpy as jnp
from jax import lax
from jax.experimental import pallas as pl
from jax.experimental.pallas import tpu as pltpu

F32 = jnp.float32
BF16 = jnp.bfloat16

N_META = 16
GRID_W = 64
N_HEADS = 8
N_KV_HEADS = 2
HEAD_DIM = 64
ROPE_THETA = 10000.0
SSM_GROUP = 16
SSM_STATE = 64
N_EXPERTS = 16
CAPACITY_FACTOR = 2
LN_EPS = 1e-5
QK_EPS = 1e-6
DEPTH = 1
DEEPNORM_ALPHA = (2.0 * DEPTH) ** 0.25

SSM_CHUNK = 32
SSM_GROUPS_TOGETHER = 4
ROUTE_TILE = 256
GATE_LANES = 128
PIECE_SHIFT = 6
PIECE = 1 << PIECE_SHIFT
META_PAD = 128
V_ROWS = HEAD_DIM + 16
VMEM_LIMIT = 56 * 1024 * 1024


def _ln(x, g, b):
    mu = jnp.mean(x, axis=-1, keepdims=True)
    xc = x - mu
    var = jnp.mean(xc * xc, axis=-1, keepdims=True)
    return xc * lax.rsqrt(var + LN_EPS) * g + b


def _sigmoid(x):
    return 1.0 / (1.0 + jnp.exp(-x))


def _split(t):
    hi = t.astype(BF16)
    lo = (t - hi.astype(F32)).astype(BF16)
    return hi, lo


def _dot(a, b):
    return jnp.dot(a, b, preferred_element_type=F32)


def _dot_nt(a, b):
    return lax.dot_general(a, b, (((1,), (1,)), ((), ())), preferred_element_type=F32)


def _inproj_kernel(x_ref, cosr_ref, sinr_ref, cost_ref, sint_ref, wt_ref, wn_ref, lng_ref, lnb_ref,
                   qg_ref, kg_ref, bd_ref, qt_ref, k_ref, vt_ref, u_ref):
    aw = N_HEADS * HEAD_DIM
    kw = N_KV_HEADS * HEAD_DIM
    half = HEAD_DIM // 2
    h = _ln(x_ref[...], lng_ref[...], lnb_ref[...])
    hb = h.astype(BF16)
    pt = _dot_nt(wt_ref[...], hb)
    pn = _dot(hb, wn_ref[...])
    tm = hb.shape[0]

    qt = pt[0:aw].reshape(N_HEADS, HEAD_DIM, tm)
    ms = jnp.mean(qt * qt, axis=1, keepdims=True)
    qn = qt * lax.rsqrt(ms + QK_EPS) * qg_ref[...]
    x0 = qn[:, 0:half, :]
    x1 = qn[:, half:, :]
    c = cost_ref[...][None]
    s = sint_ref[...][None]
    qr = jnp.concatenate([x0 * c - x1 * s, x0 * s + x1 * c], axis=1)
    qt_ref[...] = qr.reshape(aw, tm).astype(BF16)
    vrow = lax.broadcasted_iota(jnp.int32, (N_KV_HEADS, V_ROWS - HEAD_DIM, tm), 1)
    vt_ref[...] = jnp.concatenate([pt[aw:aw + kw].reshape(N_KV_HEADS, HEAD_DIM, tm),
                                   jnp.where(vrow == 0, 1.0, 0.0)], axis=1).astype(BF16)

    kk = pn[:, 0:kw]
    hi, lo = _split(kk * kk)
    bd = bd_ref[...]
    msk = _dot(hi, bd) + _dot(lo, bd)
    kn = kk * lax.rsqrt(msk + QK_EPS) * kg_ref[...]
    lane = lax.broadcasted_iota(jnp.int32, kn.shape, 1)
    first = (lane & (HEAD_DIM - 1)) < half
    partner = jnp.where(first, pltpu.roll(kn, kw - half, 1), pltpu.roll(kn, half, 1))
    kr = (kn * cosr_ref[...] + partner * sinr_ref[...]).astype(BF16)
    for g in range(N_KV_HEADS):
        k_ref[g] = kr[:, g * HEAD_DIM:(g + 1) * HEAD_DIM]
    u_ref[...] = pn[:, kw:]


def _inproj(x2, cosr, sinr, cost, sint, wt, wn, lng, lnb, qg3, kg, bd, tm, n_tab_blocks):
    rows, d = x2.shape
    aw = N_HEADS * HEAD_DIM
    kw = N_KV_HEADS * HEAD_DIM
    uw = wn.shape[1] - kw
    half = HEAD_DIM // 2
    full = lambda a: pl.BlockSpec(a.shape, lambda i: (0,) * a.ndim)
    return pl.pallas_call(
        _inproj_kernel,
        grid=(rows // tm,),
        in_specs=[
            pl.BlockSpec((tm, d), lambda i: (i, 0)),
            pl.BlockSpec((tm, kw), lambda i: (i % n_tab_blocks, 0)),
            pl.BlockSpec((tm, kw), lambda i: (i % n_tab_blocks, 0)),
            pl.BlockSpec((half, tm), lambda i: (0, i % n_tab_blocks)),
            pl.BlockSpec((half, tm), lambda i: (0, i % n_tab_blocks)),
            full(wt), full(wn), full(lng), full(lnb), full(qg3), full(kg), full(bd),
        ],
        out_specs=[
            pl.BlockSpec((aw, tm), lambda i: (0, i)),
            pl.BlockSpec((N_KV_HEADS, tm, HEAD_DIM), lambda i: (0, i, 0)),
            pl.BlockSpec((N_KV_HEADS, V_ROWS, tm), lambda i: (0, 0, i)),
            pl.BlockSpec((tm, uw), lambda i: (i, 0)),
        ],
        out_shape=[
            jax.ShapeDtypeStruct((aw, rows), BF16),
            jax.ShapeDtypeStruct((N_KV_HEADS, rows, HEAD_DIM), BF16),
            jax.ShapeDtypeStruct((N_KV_HEADS, V_ROWS, rows), BF16),
            jax.ShapeDtypeStruct((rows, uw), F32),
        ],
        compiler_params=pltpu.CompilerParams(dimension_semantics=("arbitrary",),
                                             vmem_limit_bytes=VMEM_LIMIT),
        name="inproj",
    )(x2, cosr, sinr, cost, sint, wt, wn, lng, lnb, qg3, kg, bd)


def _attn_kernel(qt_ref, k0_ref, k1_ref, vta_ref, vtb_ref, km_ref, vtm_ref, o_ref, m_sc, acc_sc, s_sc, mb_sc):
    j = pl.program_id(2)
    last = pl.num_programs(2) - 1
    grp = N_HEADS // N_KV_HEADS
    heads = lambda h: slice(h * HEAD_DIM, (h + 1) * HEAD_DIM)

    def score(k_ref, slot, h):
        s = _dot(k_ref[h // grp], qt_ref[heads(h), :])
        s_sc[slot, h] = s
        mb_sc[slot, h:h + 1, :] = jnp.max(s, axis=0, keepdims=True)

    def softmax_pv(h, s, m_blk, vt_g):
        m_prev = m_sc[h:h + 1, :]
        m_new = jnp.maximum(m_prev, m_blk)
        alpha = jnp.exp2(m_prev - m_new)
        p = jnp.exp2(s - m_new).astype(BF16)
        acc_sc[h] = alpha * acc_sc[h] + _dot(vt_g, p)
        m_sc[h:h + 1, :] = m_new

    def consume(vt_ref, slot, h):
        softmax_pv(h, s_sc[slot, h], mb_sc[slot, h:h + 1, :], vt_ref[h // grp])

    @pl.when(j == 0)
    def _first():
        m_sc[...] = jnp.full(m_sc.shape, -jnp.inf, F32)
        acc_sc[...] = jnp.zeros(acc_sc.shape, F32)
        for h in range(N_HEADS):
            s = _dot(km_ref[h // grp], qt_ref[heads(h), :])
            softmax_pv(h, s, jnp.max(s, axis=0, keepdims=True), vtm_ref[h // grp])
            score(k0_ref, 0, h)
        for h in range(N_HEADS):
            score(k1_ref, 1, h)
            consume(vtb_ref, 0, h)

    @pl.when((j > 0) & (j < last))
    def _middle():
        for h in range(N_HEADS):
            score(k0_ref, 0, h)
            consume(vta_ref, 1, h)
        for h in range(N_HEADS):
            score(k1_ref, 1, h)
            consume(vtb_ref, 0, h)

    @pl.when(j == last)
    def _last():
        for h in range(N_HEADS):
            consume(vta_ref, 1, h)
        acc = acc_sc[...]
        tq = acc.shape[2]
        out_t = (acc[:, 0:HEAD_DIM, :] / acc[:, HEAD_DIM:HEAD_DIM + 1, :]).reshape(N_HEADS * HEAD_DIM, tq)
        o_ref[...] = out_t.T.astype(BF16)


def _attention(qt, kk, vt, km, vtm, nb, tq, tk):
    aw, rows = qt.shape
    s = rows // nb
    nq = s // tq
    nkb = s // tk
    kblk = lambda f: pl.BlockSpec((N_KV_HEADS, tk, HEAD_DIM),
                                  lambda bi, qi, j: (0, bi * nkb + jnp.clip(f(j), 0, nkb - 1), 0))
    vblk = lambda f: pl.BlockSpec((N_KV_HEADS, V_ROWS, tk),
                                  lambda bi, qi, j: (0, 0, bi * nkb + jnp.clip(f(j), 0, nkb - 1)))
    return pl.pallas_call(
        _attn_kernel,
        grid=(nb, nq, nkb // 2 + 1),
        in_specs=[
            pl.BlockSpec((aw, tq), lambda bi, qi, j: (0, bi * nq + qi)),
            kblk(lambda j: 2 * j), kblk(lambda j: 2 * j + 1),
            vblk(lambda j: 2 * j - 1), vblk(lambda j: 2 * j),
            pl.BlockSpec(km.shape, lambda bi, qi, j: (0, 0, 0)),
            pl.BlockSpec(vtm.shape, lambda bi, qi, j: (0, 0, 0)),
        ],
        out_specs=pl.BlockSpec((tq, aw), lambda bi, qi, j: (bi * nq + qi, 0)),
        out_shape=jax.ShapeDtypeStruct((rows, aw), BF16),
        scratch_shapes=[
            pltpu.VMEM((N_HEADS, tq), F32),
            pltpu.VMEM((N_HEADS, V_ROWS, tq), F32),
            pltpu.VMEM((2, N_HEADS, tk, tq), F32),
            pltpu.VMEM((2, N_HEADS, tq), F32),
        ],
        compiler_params=pltpu.CompilerParams(
            dimension_semantics=("arbitrary", "arbitrary", "arbitrary"),
            vmem_limit_bytes=VMEM_LIMIT),
        name="attn",
    )(qt, kk, kk, vt, vt, km, vtm)


def _ssm_prep_kernel(*refs):
    for d in range(2):
        _ssm_prep_direction(d, *refs)


def _ssm_prep_direction(d, arow_ref, acol_ref, ldt_ref, bt_ref, ct_ref, m_ref, w_ref, v_ref, at_ref):
    t_chunk = SSM_CHUNK
    shift = int(math.log2(SSM_GROUP))
    df = float(d)
    dt = jnp.exp(ldt_ref[d, 0])

    def abar(ar, ai):
        mag = jnp.exp(ar * dt)
        ang = ai * dt
        return mag * jnp.cos(ang), mag * jnp.sin(ang)

    def cpow(ar, ai, e):
        mag = jnp.exp(ar * dt * e)
        ang = ai * dt * e
        return mag * jnp.cos(ang), mag * jnp.sin(ang)

    ar_r = arow_ref[d, 0, 0:1, :]
    ai_r = arow_ref[d, 0, 1:2, :]
    abr, abi = abar(ar_r, ai_r)
    nr = abr - 1.0
    ni = abi
    den = ar_r * ar_r + ai_r * ai_r
    cr = (nr * ar_r + ni * ai_r) / den
    ci = (ni * ar_r - nr * ai_r) / den
    nstate = bt_ref.shape[4]
    nrow = t_chunk * SSM_GROUP

    def tile_rows(x):
        return jnp.broadcast_to(x[None], (t_chunk, SSM_GROUP, nstate)).reshape(nrow, nstate)

    btr = tile_rows(bt_ref[d, 0, 0])
    bti = tile_rows(bt_ref[d, 0, 1])
    bbr = cr * btr - ci * bti
    bbi = cr * bti + ci * btr
    jr = (lax.broadcasted_iota(jnp.int32, (nrow, 1), 0) >> shift).astype(F32)
    lj = jr + df * ((t_chunk - 1) - 2.0 * jr)
    step = lax.broadcasted_iota(jnp.int32, (t_chunk, 1), 0).astype(F32)
    lstep = step + df * ((t_chunk - 1) - 2.0 * step)

    def rep_rows(x):
        return jnp.broadcast_to(x[:, None, :], (t_chunk, SSM_GROUP, nstate)).reshape(nrow, nstate)

    enr, eni = [rep_rows(x) for x in cpow(ar_r, ai_r, -lstep)]
    bmr = enr * bbr - eni * bbi
    bmi = enr * bbi + eni * bbr
    ewr, ewi = [rep_rows(x) for x in cpow(ar_r, ai_r, (t_chunk - 1) - lstep)]
    w_ref[d, 0] = jnp.concatenate([ewr * bbr - ewi * bbi, ewr * bbi + ewi * bbr], axis=1).astype(BF16)
    atr, ati = cpow(ar_r, ai_r, float(t_chunk))
    at_ref[d, 0] = jnp.concatenate([atr, ati], axis=1)

    ar_c = acol_ref[d, 0, :, 0:1]
    ai_c = acol_ref[d, 0, :, 1:2]
    ncol = t_chunk * SSM_GROUP
    lane = lax.broadcasted_iota(jnp.int32, (1, ncol), 1)
    tci = lane >> shift
    tile = jnp.where(lax.broadcasted_iota(jnp.int32, (SSM_GROUP, ncol), 0) == (lane & (SSM_GROUP - 1)),
                     1.0, 0.0).astype(BF16)

    def tile_cols(x):
        hi, lo = _split(x)
        lo2 = (x - hi.astype(F32) - lo.astype(F32)).astype(BF16)
        return _dot(hi, tile) + _dot(lo, tile) + _dot(lo2, tile)

    ctr = tile_cols(ct_ref[d, 0, 0])
    cti = tile_cols(ct_ref[d, 0, 1])
    tc = tci.astype(F32)
    lt = tc + df * ((t_chunk - 1) - 2.0 * tc)
    stepl = lax.broadcasted_iota(jnp.int32, (1, t_chunk), 1).astype(F32)
    lstepl = stepl + df * ((t_chunk - 1) - 2.0 * stepl)
    rep = jnp.where(lax.broadcasted_iota(jnp.int32, (t_chunk, ncol), 0) == tci, 1.0, 0.0).astype(BF16)

    def rep_cols(x):
        hi, lo = _split(x)
        lo2 = (x - hi.astype(F32) - lo.astype(F32)).astype(BF16)
        return _dot(hi, rep) + _dot(lo, rep) + _dot(lo2, rep)

    ecr, eci = [rep_cols(x) for x in cpow(ar_c, ai_c, lstepl)]
    cmr = ctr * ecr - cti * eci
    cmi = ctr * eci + cti * ecr
    lhs_hi, lhs_lo = _split(jnp.concatenate([bmr, -bmi], axis=1))
    rhs_hi, rhs_lo = _split(jnp.concatenate([cmr, cmi], axis=0))
    m = _dot(lhs_hi, rhs_hi) + _dot(lhs_hi, rhs_lo) + _dot(lhs_lo, rhs_hi)
    m_ref[d, 0] = jnp.where(lj <= lt, m, 0.0).astype(BF16)
    abr_c, abi_c = abar(ar_c, ai_c)
    c1r = cmr * abr_c - cmi * abi_c
    c1i = cmr * abi_c + cmi * abr_c
    v_ref[d, 0] = jnp.concatenate([c1r, -c1i], axis=0).astype(BF16)


def _ssm_prep(arow, acol, ldt, bt, ct):
    nd, g = arow.shape[0], arow.shape[1]
    p = SSM_STATE
    n = SSM_CHUNK * SSM_GROUP
    blk = lambda a: pl.BlockSpec((nd, 1) + a.shape[2:], lambda gi: (0, gi) + (0,) * (a.ndim - 2))
    return pl.pallas_call(
        _ssm_prep_kernel,
        grid=(g,),
        in_specs=[blk(arow), blk(acol), blk(ldt), blk(bt), blk(ct)],
        out_specs=[
            pl.BlockSpec((nd, 1, n, n), lambda gi: (0, gi, 0, 0)),
            pl.BlockSpec((nd, 1, n, 2 * p), lambda gi: (0, gi, 0, 0)),
            pl.BlockSpec((nd, 1, 2 * p, n), lambda gi: (0, gi, 0, 0)),
            pl.BlockSpec((nd, 1, 1, 2 * p), lambda gi: (0, gi, 0, 0)),
        ],
        out_shape=[
            jax.ShapeDtypeStruct((nd, g, n, n), BF16),
            jax.ShapeDtypeStruct((nd, g, n, 2 * p), BF16),
            jax.ShapeDtypeStruct((nd, g, 2 * p, n), BF16),
            jax.ShapeDtypeStruct((nd, g, 1, 2 * p), F32),
        ],
        compiler_params=pltpu.CompilerParams(dimension_semantics=("arbitrary",)),
        name="ssm_prep",
    )(arow, acol, ldt, bt, ct)


def _chunk_carry(chains):
    p = SSM_STATE
    z0, a0 = chains[0][0], chains[0][1]
    nc = z0.shape[0]
    row = lax.broadcasted_iota(jnp.int32, z0.shape, 0)
    is_re = lax.broadcasted_iota(jnp.int32, a0.shape, 1) < p
    sign = jnp.where(is_re, -1.0, 1.0)

    def parts(ap):
        sw = pltpu.roll(ap, p, 1)
        return jnp.where(is_re, ap, sw), sign * jnp.where(is_re, sw, ap)

    def cmul(x, ar_full, ai_sgn):
        return x * ar_full + pltpu.roll(x, p, 1) * ai_sgn

    coef = [parts(a) for _, a, _, _ in chains]
    firsts = [(nc - 1) if rev else 0 for _, _, _, rev in chains]
    es = [z + jnp.where(row == f, cmul(s0, *c), 0.0) for (z, _, s0, _), f, c in zip(chains, firsts, coef)]
    k = 1
    while k < nc:
        for i, (_, _, _, rev) in enumerate(chains):
            if rev:
                sh = jnp.where(row < nc - k, pltpu.roll(es[i], nc - k, 0), 0.0)
            else:
                sh = jnp.where(row >= k, pltpu.roll(es[i], k, 0), 0.0)
            es[i] = es[i] + cmul(sh, *coef[i])
        k *= 2
        if k < nc:
            coef = [parts(cmul(jnp.where(is_re, ar, sign * ai), ar, ai)) for ar, ai in coef]
    outs = []
    for e, (_, _, s0, rev), f in zip(es, chains, firsts):
        outs.append(jnp.where(row == f, s0, pltpu.roll(e, nc - 1 if rev else 1, 0)))
    return outs


def _ssm_kernel(u_ref, um_ref, m_ref, w_ref, v_ref, at_ref, dvec_ref, y_ref, ug_sc, yg_sc):
    p = SSM_STATE
    t_chunk = SSM_CHUNK
    hgrp = SSM_GROUP
    gb = u_ref.shape[2] // hgrp
    per_tile = u_ref.shape[2] // hgrp
    nc = u_ref.shape[1] // t_chunk
    lane_blk = lax.broadcasted_iota(jnp.int32, (nc, u_ref.shape[2]), 1) >> int(math.log2(hgrp))

    width = u_ref.shape[2]

    def block_transpose(arrs):
        arrs = list(arrs)
        d = len(arrs) // 2
        while d >= 1:
            low_bit = (lane_blk & d) == 0
            for i in range(len(arrs)):
                if i & d == 0:
                    lo, hi = arrs[i], arrs[i + d]
                    arrs[i] = jnp.where(low_bit, lo, pltpu.roll(hi, d * hgrp, 1))
                    arrs[i + d] = jnp.where(low_bit, pltpu.roll(lo, width - d * hgrp, 1), hi)
            d //= 2
        return arrs

    for q in range(t_chunk // per_tile):
        steps = [u_ref[0, pl.ds(q * per_tile + r, nc, stride=t_chunk), :] for r in range(per_tile)]
        for g, tile in enumerate(block_transpose(steps)):
            ug_sc[g, :, q * width:(q + 1) * width] = tile

    def groups(i, carry):
        gs = [i * SSM_GROUPS_TOGETHER + k for k in range(SSM_GROUPS_TOGETHER)]
        us = [ug_sc[g] for g in gs]
        ubs = [u.astype(BF16) for u in us]
        yin = [[_dot(ub, m_ref[d, g]) for d in range(2)] for g, ub in zip(gs, ubs)]
        chains = []
        for g, ub in zip(gs, ubs):
            for d in range(2):
                if d == 0:
                    s0 = _dot(um_ref[g].astype(BF16), w_ref[0, g])[0:1, :]
                else:
                    s0 = jnp.zeros((1, 2 * p), F32)
                chains.append((_dot(ub, w_ref[d, g]), at_ref[d, g], s0, d == 1))
        s_in = _chunk_carry(chains)
        for k, g in enumerate(gs):
            y = dvec_ref[g] * us[k] + yin[k][0] + yin[k][1]
            for d in range(2):
                y = y + _dot(s_in[2 * k + d].astype(BF16), v_ref[d, g])
            yg_sc[g] = y
        return carry

    lax.fori_loop(0, gb // SSM_GROUPS_TOGETHER, groups, 0)

    for q in range(t_chunk // per_tile):
        srcs = [yg_sc[g, :, q * width:(q + 1) * width] for g in range(gb)]
        for r, row in enumerate(block_transpose(srcs)):
            y_ref[0, pl.ds(q * per_tile + r, nc, stride=t_chunk), :] = row


def _ssm(u3, um2, m, w, v, at, dvec):
    b, s, sw = u3.shape
    g = sw // SSM_GROUP
    n = SSM_CHUNK * SSM_GROUP
    p = SSM_STATE
    lanes = 128
    gb = lanes // SSM_GROUP
    return pl.pallas_call(
        _ssm_kernel,
        grid=(b, sw // lanes),
        in_specs=[
            pl.BlockSpec((1, s, lanes), lambda bi, gi: (bi, 0, gi)),
            pl.BlockSpec((gb, 8, n), lambda bi, gi: (gi, 0, 0)),
            pl.BlockSpec((2, gb, n, n), lambda bi, gi: (0, gi, 0, 0)),
            pl.BlockSpec((2, gb, n, 2 * p), lambda bi, gi: (0, gi, 0, 0)),
            pl.BlockSpec((2, gb, 2 * p, n), lambda bi, gi: (0, gi, 0, 0)),
            pl.BlockSpec((2, gb, 1, 2 * p), lambda bi, gi: (0, gi, 0, 0)),
            pl.BlockSpec((gb, 1, n), lambda bi, gi: (gi, 0, 0)),
        ],
        out_specs=pl.BlockSpec((1, s, lanes), lambda bi, gi: (bi, 0, gi)),
        out_shape=jax.ShapeDtypeStruct((b, s, sw), F32),
        scratch_shapes=[pltpu.VMEM((gb, s // SSM_CHUNK, n), F32), pltpu.VMEM((gb, s // SSM_CHUNK, n), F32)],
        compiler_params=pltpu.CompilerParams(dimension_semantics=("arbitrary", "arbitrary"),
                                             vmem_limit_bytes=VMEM_LIMIT),
        name="ssm",
    )(u3, um2, m, w, v, at, dvec)


def _merge_kernel(x_ref, ya_ref, ys_ref, wgt_ref, wglu_ref, bglu_ref, wab_ref, wsb_ref, wo_ref,
                  lng_ref, lnb_ref, l1g_ref, l1b_ref, wr_ref,
                  h1_ref, tok_ref, afft_ref):
    d = x_ref.shape[1]
    h = _ln(x_ref[...], lng_ref[...], lnb_ref[...])
    gates = _dot(h.astype(BF16), wgt_ref[...])
    ga = _sigmoid(gates[:, 0:d])
    gs = _sigmoid(gates[:, d:2 * d])
    ys = ys_ref[...]
    y = 0.5 * ys * (1.0 + jnp.tanh(math.sqrt(2.0 / math.pi) * (ys + 0.044715 * (ys * ys * ys))))
    yg = y * _sigmoid(_dot(y.astype(BF16), wglu_ref[...]) + bglu_ref[...])
    merged = ga * _dot(ya_ref[...], wab_ref[...]) + gs * _dot(yg.astype(BF16), wsb_ref[...])
    h1 = _ln(DEEPNORM_ALPHA * h + _dot(merged.astype(BF16), wo_ref[...]), l1g_ref[...], l1b_ref[...])
    h1_ref[...] = h1
    hi, lo = _split(h1)
    ne = afft_ref.shape[0]
    cross = _dot(hi, wr_ref[...])
    logits = cross[:, 0:GATE_LANES] + cross[:, GATE_LANES:] + _dot(lo, wr_ref[:, 0:GATE_LANES])
    lane = lax.broadcasted_iota(jnp.int32, logits.shape, 1)
    logits = jnp.where(lane < 3 * ne, logits, -jnp.inf)
    ex = jnp.exp(logits - jnp.max(logits, axis=1, keepdims=True))
    aff = ex / jnp.sum(jnp.where(lane < ne, ex, 0.0), axis=1, keepdims=True)
    p0 = aff.astype(BF16)
    r1 = aff - p0.astype(F32)
    p1 = r1.astype(BF16)
    p2 = (r1 - p1.astype(F32)).astype(BF16)
    pieces = jnp.where(lane < ne, p0, jnp.where(lane < 2 * ne, p1, p2))
    tok_ref[...] = jnp.concatenate([hi, pieces], axis=1)
    afft_ref[...] = aff.T[0:ne, :]


def _merge(x2, ya, ys, wgt, wglu, bglu, wab, wsb, wo, lng, lnb, l1g, l1b, wr, ne, tm):
    rows, d = x2.shape
    full = lambda a: pl.BlockSpec(a.shape, lambda i: (0,) * a.ndim)
    return pl.pallas_call(
        _merge_kernel,
        grid=(rows // tm,),
        in_specs=[
            pl.BlockSpec((tm, d), lambda i: (i, 0)),
            pl.BlockSpec((tm, ya.shape[1]), lambda i: (i, 0)),
            pl.BlockSpec((tm, ys.shape[1]), lambda i: (i, 0)),
            full(wgt), full(wglu), full(bglu), full(wab), full(wsb), full(wo),
            full(lng), full(lnb), full(l1g), full(l1b), full(wr),
        ],
        out_specs=[
            pl.BlockSpec((tm, d), lambda i: (i, 0)),
            pl.BlockSpec((tm, d + GATE_LANES), lambda i: (i, 0)),
            pl.BlockSpec((ne, tm), lambda i: (0, i)),
        ],
        out_shape=[
            jax.ShapeDtypeStruct((rows, d), F32),
            jax.ShapeDtypeStruct((rows, d + GATE_LANES), BF16),
            jax.ShapeDtypeStruct((ne, rows), F32),
        ],
        compiler_params=pltpu.CompilerParams(dimension_semantics=("arbitrary",),
                                             vmem_limit_bytes=VMEM_LIMIT),
        name="merge",
    )(x2, ya, ys, wgt, wglu, bglu, wab, wsb, wo, lng, lnb, l1g, l1b, wr)


def _route_kernel(afft_ref, tri_ref, scl_ref, lo_ref, *, cap):
    aff = afft_ref[...]
    ne, s = aff.shape
    capf = float(cap)

    def as_float(bits):
        return lax.bitcast_convert_type(bits, F32)

    def search(i, t):
        cand = t | jnp.left_shift(jnp.int32(1), 30 - i)
        cnt = jnp.sum(jnp.where(aff >= as_float(cand), 1.0, 0.0), axis=1, keepdims=True)
        return jnp.where(cnt >= capf, cand, t)

    thr_bits = lax.fori_loop(0, 31, search, jnp.zeros((ne, 1), jnp.int32))
    gt = aff >= as_float(thr_bits + 1)
    eq = (aff >= as_float(thr_bits)) & jnp.logical_not(gt)
    need = capf - jnp.sum(jnp.where(gt, 1.0, 0.0), axis=1, keepdims=True)
    tri = tri_ref[...]
    rt = ROUTE_TILE
    nt = s // rt
    col = lax.broadcasted_iota(jnp.int32, (ne, nt), 1)
    carry_eq = jnp.zeros((ne, 1), F32)
    carry_sel = jnp.zeros((ne, 1), F32)
    lo_val = jnp.zeros((ne, nt), F32)
    for t in range(nt):
        sl = slice(t * rt, (t + 1) * rt)
        eq_b = eq[:, sl]
        ceq = _dot(jnp.where(eq_b, 1.0, 0.0).astype(BF16), tri)
        sel_b = gt[:, sl] | (eq_b & ((ceq + carry_eq) <= need))
        carry_eq = carry_eq + ceq[:, rt - 1:rt]
        csel = _dot(jnp.where(sel_b, 1.0, 0.0).astype(BF16), tri)
        scl_ref[0, :, sl] = jnp.where(sel_b, csel, 0.0)
        lo_val = jnp.where(col == t, carry_sel, lo_val)
        carry_sel = carry_sel + csel[:, rt - 1:rt]
    lo_ref[0] = lo_val


def _route(afft, tri, nb, cap):
    ne, rows = afft.shape
    s = rows // nb
    nt = s // ROUTE_TILE
    return pl.pallas_call(
        functools.partial(_route_kernel, cap=cap),
        grid=(nb,),
        in_specs=[
            pl.BlockSpec((ne, s), lambda b: (0, b)),
            pl.BlockSpec(tri.shape, lambda b: (0, 0)),
        ],
        out_specs=[
            pl.BlockSpec((1, ne, s), lambda b: (b, 0, 0)),
            pl.BlockSpec((1, ne, nt), lambda b: (b, 0, 0)),
        ],
        out_shape=[
            jax.ShapeDtypeStruct((nb, ne, s), F32),
            jax.ShapeDtypeStruct((nb, ne, nt), F32),
        ],
        compiler_params=pltpu.CompilerParams(dimension_semantics=("arbitrary",)),
        name="route",
    )(afft, tri)


def _ffn_kernel(lo_ref, t_ref, scl_ref, wg_ref, wu_ref, wd_ref, ye_ref, xe_sc, xb_sc, gate_sc, acc_sc,
                *, cap, flag_off):
    fc = pl.program_id(2)

    @pl.when(fc == 0)
    def _gather():
        _ffn_gather(lo_ref, t_ref, scl_ref, xe_sc, xb_sc, gate_sc, cap=cap, flag_off=flag_off)
        acc_sc[...] = jnp.zeros(acc_sc.shape, F32)

    xb = xb_sc[...]
    g = _dot(xb, wg_ref[0, 0].astype(BF16))
    u = _dot(xb, wu_ref[0, 0].astype(BF16))
    hh = (g * _sigmoid(g)) * u
    acc_sc[...] += _dot(hh.astype(BF16), wd_ref[0, 0].astype(BF16))

    @pl.when(fc == pl.num_programs(2) - 1)
    def _emit():
        ye_ref[0, 0] = (acc_sc[...] * gate_sc[...]).astype(BF16)


def _ffn_gather(lo_ref, t_ref, scl_ref, xe_sc, xb_sc, gate_sc, *, cap, flag_off):
    b = pl.program_id(0)
    e = pl.program_id(1)
    ne = pl.num_programs(1)
    rt = ROUTE_TILE
    nt = t_ref.shape[1] // rt
    xe_sc[...] = jnp.zeros(xe_sc.shape, F32)
    r = lax.broadcasted_iota(jnp.int32, (PIECE, rt), 0).astype(F32)

    def tile_info(tau):
        base = (b * ne + e) * (nt + 1) + tau
        lo = lo_ref[base]
        n_pieces = jnp.right_shift((lo & 7) + (lo_ref[base + 1] - lo) + (PIECE - 1), PIECE_SHIFT)
        return lo, n_pieces

    def add_piece(tau, lo, p):
        off = lo & 7
        scl = scl_ref[0, 0, pl.ds(tau, 1), :]
        tok = t_ref[0, pl.ds(pl.multiple_of(tau * rt, rt), rt), :]
        shift = (off - 1 - p * PIECE).astype(F32)
        onehot = jnp.where((scl > 0.0) & ((scl + shift) == r), 1.0, 0.0).astype(BF16)
        xe_sc[pl.ds(pl.multiple_of(lo - off + p * PIECE, 8), PIECE), :] += _dot(onehot, tok)

    def first_piece(tau, carry):
        add_piece(tau, tile_info(tau)[0], 0)
        return carry

    def more_pieces(tau, carry):
        lo, n_pieces = tile_info(tau)
        lax.fori_loop(1, n_pieces, lambda p, c: (add_piece(tau, lo, p), c)[1], 0)
        return carry

    lax.fori_loop(0, nt, first_piece, 0, unroll=4)

    @pl.when(lo_ref[flag_off + b * ne + e] > 0)
    def _rare():
        lax.fori_loop(0, nt, more_pieces, 0)

    d = xb_sc.shape[1]
    xb_sc[...] = xe_sc[0:cap, 0:d].astype(BF16)
    gl = xe_sc[0:cap, d:d + GATE_LANES]
    lane = lax.broadcasted_iota(jnp.int32, gl.shape, 1)
    mine = ((lane & (N_EXPERTS - 1)) == e) & (lane < 3 * N_EXPERTS)
    gate_sc[...] = jnp.sum(jnp.where(mine, gl, 0.0), axis=1, keepdims=True)


def _ffn(lo_i, t, scl4, wg, wu, wd, layer, cap, f_chunk):
    b, s, dx = t.shape
    _, ne, d, f = wg.shape
    nt = s // ROUTE_TILE
    grid_spec = pltpu.PrefetchScalarGridSpec(
        num_scalar_prefetch=1,
        grid=(b, ne, f // f_chunk),
        in_specs=[
            pl.BlockSpec((1, s, dx), lambda bi, ei, fi, lo: (bi, 0, 0), pipeline_mode=pl.Buffered(1)),
            pl.BlockSpec((1, 1, nt, ROUTE_TILE), lambda bi, ei, fi, lo: (bi, ei, 0, 0)),
            pl.BlockSpec((1, 1, d, f_chunk), lambda bi, ei, fi, lo: (layer, ei, 0, fi)),
            pl.BlockSpec((1, 1, d, f_chunk), lambda bi, ei, fi, lo: (layer, ei, 0, fi)),
            pl.BlockSpec((1, 1, f_chunk, d), lambda bi, ei, fi, lo: (layer, ei, fi, 0)),
        ],
        out_specs=pl.BlockSpec((1, 1, cap, d), lambda bi, ei, fi, lo: (bi, ei, 0, 0)),
        scratch_shapes=[pltpu.VMEM((cap + PIECE, dx), F32), pltpu.VMEM((cap, d), BF16),
                        pltpu.VMEM((cap, 1), F32), pltpu.VMEM((cap, d), F32)],
    )
    return pl.pallas_call(
        functools.partial(_ffn_kernel, cap=cap, flag_off=b * ne * (nt + 1)),
        grid_spec=grid_spec,
        out_shape=jax.ShapeDtypeStruct((b, ne, cap, d), BF16),
        compiler_params=pltpu.CompilerParams(dimension_semantics=("arbitrary", "arbitrary", "arbitrary"),
                                             vmem_limit_bytes=VMEM_LIMIT),
        name="ffn",
    )(lo_i, t, scl4, wg, wu, wd)


def _combine_kernel(lo_ref, h1_ref, sclt_ref, ye_hbm, g_ref, b_ref, o_ref, win, xwin, acc_sc, sem, xsem,
                    *, cap, flag_off):
    b = pl.program_id(0)
    tau = pl.program_id(1)
    nb = pl.num_programs(0)
    nt = pl.num_programs(1)
    ne = win.shape[1]
    d = win.shape[3]
    step = b * nt + tau
    slot = step & 1

    def pieces(bi, ti, e):
        base = (bi * ne + e) * (nt + 1) + ti
        lo = lo_ref[base]
        n_sel = lo_ref[base + 1] - lo
        n_pieces = jnp.maximum(jnp.right_shift((lo & 15) + n_sel + (PIECE - 1), PIECE_SHIFT), 1)
        first = jnp.minimum(lo - (lo & 15), cap - n_pieces * PIECE)
        return lo, first, n_pieces

    def copy(bi, e, first, p, buf, s):
        start = pl.multiple_of(first + p * PIECE, 16)
        return pltpu.make_async_copy(ye_hbm.at[bi, e, pl.ds(start, PIECE), :], buf, s)

    def first_pieces(bi, ti, sl):
        return [copy(bi, e, pieces(bi, ti, e)[1], 0, win.at[sl, e], sem.at[sl, e]) for e in range(ne)]

    @pl.when(step == 0)
    def _prime():
        for cp in first_pieces(b, tau, slot):
            cp.start()

    @pl.when(step + 1 < nb * nt)
    def _prefetch():
        wrap = tau + 1 == nt
        for cp in first_pieces(jnp.where(wrap, b + 1, b), jnp.where(wrap, 0, tau + 1), 1 - slot):
            cp.start()

    for cp in first_pieces(b, tau, slot):
        cp.wait()

    sclt = sclt_ref[0]
    col = lax.broadcasted_iota(jnp.int32, (1, ne * PIECE), 1)
    grp = jnp.right_shift(col, PIECE_SHIFT)
    expand = jnp.where(lax.broadcasted_iota(jnp.int32, (ne, ne * PIECE), 0) == grp, 1.0, 0.0).astype(BF16)
    scl = _dot(sclt.astype(BF16), expand)
    shift = jnp.zeros(col.shape, F32)
    for e in range(ne):
        lo, first, _ = pieces(b, tau, e)
        shift = jnp.where(grp == e, (lo - first - 1).astype(F32), shift)
    r = (col & (PIECE - 1)).astype(F32)
    onehot = jnp.where((scl > 0.0) & ((scl + shift) == r), 1.0, 0.0).astype(BF16)
    acc_sc[...] = DEEPNORM_ALPHA * h1_ref[0] + _dot(onehot, win[slot].reshape(ne * PIECE, d))

    @pl.when(lo_ref[flag_off + step] > 0)
    def _rare():
        for e in range(ne):
            lo, first, n_pieces = pieces(b, tau, e)

            def extra(p, c, e=e, lo=lo, first=first):
                cp = copy(b, e, first, p, xwin, xsem.at[0])
                cp.start()
                cp.wait()
                se = sclt_ref[0][:, e:e + 1]
                rr = lax.broadcasted_iota(jnp.int32, (se.shape[0], PIECE), 1).astype(F32)
                oh = jnp.where((se > 0.0) & ((se + (lo - first - p * PIECE - 1).astype(F32)) == rr), 1.0, 0.0)
                acc_sc[...] += _dot(oh.astype(BF16), xwin[...])
                return c

            lax.fori_loop(1, n_pieces, extra, 0)

    o_ref[0] = _ln(acc_sc[...], g_ref[...], b_ref[...])


def _combine(lo_i, h1, sclt, ye, g, bb, cap):
    b, s, d = h1.shape
    ne = sclt.shape[2]
    rt = ROUTE_TILE
    grid_spec = pltpu.PrefetchScalarGridSpec(
        num_scalar_prefetch=1,
        grid=(b, s // rt),
        in_specs=[
            pl.BlockSpec((1, rt, d), lambda bi, ti, lo: (bi, ti, 0)),
            pl.BlockSpec((1, rt, ne), lambda bi, ti, lo: (bi, ti, 0)),
            pl.BlockSpec(memory_space=pl.ANY),
            pl.BlockSpec(g.shape, lambda bi, ti, lo: (0, 0)),
            pl.BlockSpec(bb.shape, lambda bi, ti, lo: (0, 0)),
        ],
        out_specs=pl.BlockSpec((1, rt, d), lambda bi, ti, lo: (bi, ti, 0)),
        scratch_shapes=[pltpu.VMEM((2, ne, PIECE, d), BF16), pltpu.VMEM((PIECE, d), BF16),
                        pltpu.VMEM((rt, d), F32),
                        pltpu.SemaphoreType.DMA((2, ne)), pltpu.SemaphoreType.DMA((1,))],
    )
    return pl.pallas_call(
        functools.partial(_combine_kernel, cap=cap, flag_off=b * ne * (s // rt + 1) + b * ne),
        grid_spec=grid_spec,
        out_shape=jax.ShapeDtypeStruct((b, s, d), F32),
        compiler_params=pltpu.CompilerParams(dimension_semantics=("arbitrary", "arbitrary"),
                                             vmem_limit_bytes=VMEM_LIMIT),
        name="combine",
    )(lo_i, h1, sclt, ye, g, bb)


def _rope_tables(n_tokens):
    half = HEAD_DIM // 2
    inv_freq = ROPE_THETA ** (-jnp.arange(0, half, 2, dtype=F32) / half)
    rows = n_tokens // GRID_W
    row = jnp.repeat(jnp.arange(rows, dtype=F32), GRID_W)
    colv = jnp.tile(jnp.arange(GRID_W, dtype=F32), rows)
    ang = jnp.concatenate([row[:, None] * inv_freq, colv[:, None] * inv_freq], axis=-1)
    return jnp.cos(ang), jnp.sin(ang)


def kernel(x, meta_tokens, ln_in_g, ln_in_b, w_in, q_norm_g, k_norm_g, ssm_a_re, ssm_a_im, ssm_log_dt,
           ssm_b_re, ssm_b_im, ssm_c_re, ssm_c_im, ssm_d, w_glu, b_glu, w_attn_br, w_ssm_br, w_o,
           ln1_g, ln1_b, w_router, w_gate_e, w_up_e, w_down_e, ln2_g, ln2_b):
    b, s, d = x.shape
    aw = N_HEADS * HEAD_DIM
    kw = N_KV_HEADS * HEAD_DIM
    sw = d // 2
    g = sw // SSM_GROUP
    half = HEAD_DIM // 2
    cap = CAPACITY_FACTOR * s // N_EXPERTS
    rows = b * s
    l = 0

    perm = np.concatenate([np.arange(0, HEAD_DIM, 2), np.arange(1, HEAD_DIM, 2)])
    qcols = np.concatenate([h * HEAD_DIM + perm for h in range(N_HEADS)])
    kcols = aw + np.concatenate([h * HEAD_DIM + perm for h in range(N_KV_HEADS)])
    wl = w_in[l]
    w_t = jnp.concatenate([wl[:, qcols], wl[:, aw + kw:aw + 2 * kw]], axis=1).T.astype(BF16)
    w_n = jnp.concatenate([wl[:, kcols], wl[:, aw + 2 * kw:aw + 2 * kw + sw]], axis=1).astype(BF16)
    w_gates = wl[:, aw + 2 * kw + sw:].astype(BF16)
    qg3 = (jnp.tile(q_norm_g[l][perm], N_HEADS) * (HEAD_DIM ** -0.5 * math.log2(math.e))).reshape(N_HEADS, HEAD_DIM, 1)
    kg = jnp.tile(k_norm_g[l][perm], N_KV_HEADS)[None, :]
    bd = jnp.asarray(np.kron(np.eye(N_KV_HEADS), np.full((HEAD_DIM, HEAD_DIM), 1.0 / HEAD_DIM)), BF16)
    lng = ln_in_g[None, :]
    lnb = ln_in_b[None, :]
    cos, sin = _rope_tables(s)
    cosr = jnp.tile(cos, (1, 2 * N_KV_HEADS))
    sinr = jnp.tile(jnp.concatenate([-sin, sin], axis=1), (1, N_KV_HEADS))
    cost = cos.T
    sint = sin.T

    x2 = x.reshape(rows, d)
    tm_in = 1024
    qt, kk, vt, u2 = _inproj(x2, cosr, sinr, cost, sint, w_t, w_n, lng, lnb, qg3, kg, bd, tm_in, s // tm_in)
    meta_p = jnp.pad(meta_tokens, ((0, META_PAD - N_META), (0, 0)))
    ones_r = jnp.ones((META_PAD, kw), F32)
    _, km, vtm, um = _inproj(meta_p, ones_r, jnp.zeros_like(ones_r), jnp.ones((half, META_PAD), F32),
                             jnp.zeros((half, META_PAD), F32), w_t, w_n, lng, lnb, qg3, kg, bd, META_PAD, 1)
    um = um[:N_META]
    km = km[:, :N_META, :]
    vtm = vtm[:, :, :N_META]

    ya = _attention(qt, kk, vt, km, vtm, b, 512, 512)

    t_chunk = SSM_CHUNK
    hgrp = SSM_GROUP
    n = t_chunk * hgrp
    arow = jnp.stack([ssm_a_re[l], ssm_a_im[l]], axis=2)
    acol = jnp.stack([ssm_a_re[l], ssm_a_im[l]], axis=3)
    ldt = ssm_log_dt[l][:, :, None, None]
    bt = jnp.stack([ssm_b_re[l], ssm_b_im[l]], axis=2)
    bt = jnp.swapaxes(bt, 3, 4)
    ct = jnp.stack([ssm_c_re[l], ssm_c_im[l]], axis=2)
    ct = jnp.swapaxes(ct, 3, 4)
    mm, ww, vv, at = _ssm_prep(arow, acol, ldt, bt, ct)
    nc = s // t_chunk
    um_g = um.reshape(N_META, g, hgrp).transpose(1, 0, 2).reshape(g, 1, N_META * hgrp)
    um_g = jnp.pad(um_g, ((0, 0), (0, 7), (n - N_META * hgrp, 0)))
    dvec = jnp.tile(ssm_d[l], (1, t_chunk))[:, None, :]
    ys = _ssm(u2.reshape(b, s, sw), um_g, mm, ww, vv, at, dvec).reshape(rows, sw)

    wr_hi = w_router[l].astype(BF16)
    wr_lo = (w_router[l] - wr_hi.astype(F32)).astype(BF16)
    lane_pad = ((0, 0), (0, GATE_LANES - 3 * N_EXPERTS))
    wr = jnp.concatenate([jnp.pad(jnp.tile(wr_hi, (1, 3)), lane_pad), jnp.pad(jnp.tile(wr_lo, (1, 3)), lane_pad)],
                         axis=1)
    h1, tok, afft = _merge(
        x2, ya, ys, w_gates, w_glu[l].astype(BF16), b_glu[l][None, :],
        w_attn_br[l].astype(BF16), w_ssm_br[l].astype(BF16), w_o[l].astype(BF16),
        lng, lnb, ln1_g[l][None, :], ln1_b[l][None, :], wr, N_EXPERTS, 512)

    tri = jnp.asarray(np.triu(np.ones((ROUTE_TILE, ROUTE_TILE), np.float32)), BF16)
    scl, lo_f = _route(afft, tri, b, cap)
    nt = s // ROUTE_TILE
    lo3 = jnp.pad(lo_f.astype(jnp.int32), ((0, 0), (0, 0), (0, 1)), constant_values=cap)
    seg = lo3[:, :, 1:] - lo3[:, :, :-1]
    multi = lambda align: ((lo3[:, :, :-1] & (align - 1)) + seg + (PIECE - 1)) // PIECE > 1
    lo_i = jnp.concatenate([lo3.reshape(-1),
                            jnp.any(multi(8), axis=2).astype(jnp.int32).reshape(-1),
                            jnp.any(multi(16), axis=1).astype(jnp.int32).reshape(-1)])
    ye = _ffn(lo_i, tok.reshape(b, s, d + GATE_LANES), scl.reshape(b, N_EXPERTS, nt, ROUTE_TILE),
              w_gate_e, w_up_e, w_down_e, l, cap, 512)
    out = _combine(lo_i, h1.reshape(b, s, d), jnp.swapaxes(scl, 1, 2), ye,
                   ln2_g[l][None, :], ln2_b[l][None, :], cap)
    return out
```

```python
import functools
import math

import numpy as np
import jax
import jax.numpy as jnp
from jax import lax
from jax.experimental import pallas as pl
from jax.experimental.pallas import tpu as pltpu

F32 = jnp.float32
BF16 = jnp.bfloat16

N_META = 16
GRID_W = 64
N_HEADS = 8
N_KV_HEADS = 2
HEAD_DIM = 64
ROPE_THETA = 10000.0
SSM_GROUP = 16
SSM_STATE = 64
N_EXPERTS = 16
CAPACITY_FACTOR = 2
LN_EPS = 1e-5
QK_EPS = 1e-6
DEPTH = 1
DEEPNORM_ALPHA = (2.0 * DEPTH) ** 0.25

SSM_CHUNK = 32
SSM_GROUPS_TOGETHER = 4
ROUTE_TILE = 256
GATE_LANES = 128
PIECE_SHIFT = 6
PIECE = 1 << PIECE_SHIFT
META_PAD = 128
V_ROWS = HEAD_DIM + 16
VMEM_LIMIT = 56 * 1024 * 1024


def _ln(x, g, b):
    mu = jnp.mean(x, axis=-1, keepdims=True)
    xc = x - mu
    var = jnp.mean(xc * xc, axis=-1, keepdims=True)
    return xc * lax.rsqrt(var + LN_EPS) * g + b


def _sigmoid(x):
    return 1.0 / (1.0 + jnp.exp(-x))


def _split(t):
    hi = t.astype(BF16)
    lo = (t - hi.astype(F32)).astype(BF16)
    return hi, lo


def _dot(a, b):
    return jnp.dot(a, b, preferred_element_type=F32)


def _dot_nt(a, b):
    return lax.dot_general(a, b, (((1,), (1,)), ((), ())), preferred_element_type=F32)


def _inproj_kernel(x_ref, cosr_ref, sinr_ref, cost_ref, sint_ref, wt_ref, wn_ref, lng_ref, lnb_ref,
                   qg_ref, kg_ref, bd_ref, qt_ref, k_ref, vt_ref, u_ref):
    aw = N_HEADS * HEAD_DIM
    kw = N_KV_HEADS * HEAD_DIM
    half = HEAD_DIM // 2
    h = _ln(x_ref[...], lng_ref[...], lnb_ref[...])
    hb = h.astype(BF16)
    pt = _dot_nt(wt_ref[...], hb)
    pn = _dot(hb, wn_ref[...])
    tm = hb.shape[0]

    qt = pt[0:aw].reshape(N_HEADS, HEAD_DIM, tm)
    ms = jnp.mean(qt * qt, axis=1, keepdims=True)
    qn = qt * lax.rsqrt(ms + QK_EPS) * qg_ref[...]
    x0 = qn[:, 0:half, :]
    x1 = qn[:, half:, :]
    c = cost_ref[...][None]
    s = sint_ref[...][None]
    qr = jnp.concatenate([x0 * c - x1 * s, x0 * s + x1 * c], axis=1)
    qt_ref[...] = qr.reshape(aw, tm).astype(BF16)
    vrow = lax.broadcasted_iota(jnp.int32, (N_KV_HEADS, V_ROWS - HEAD_DIM, tm), 1)
    vt_ref[...] = jnp.concatenate([pt[aw:aw + kw].reshape(N_KV_HEADS, HEAD_DIM, tm),
                                   jnp.where(vrow == 0, 1.0, 0.0)], axis=1).astype(BF16)

    kk = pn[:, 0:kw]
    hi, lo = _split(kk * kk)
    bd = bd_ref[...]
    msk = _dot(hi, bd) + _dot(lo, bd)
    kn = kk * lax.rsqrt(msk + QK_EPS) * kg_ref[...]
    lane = lax.broadcasted_iota(jnp.int32, kn.shape, 1)
    first = (lane & (HEAD_DIM - 1)) < half
    partner = jnp.where(first, pltpu.roll(kn, kw - half, 1), pltpu.roll(kn, half, 1))
    kr = (kn * cosr_ref[...] + partner * sinr_ref[...]).astype(BF16)
    for g in range(N_KV_HEADS):
        k_ref[g] = kr[:, g * HEAD_DIM:(g + 1) * HEAD_DIM]
    u_ref[...] = pn[:, kw:]


def _inproj(x2, cosr, sinr, cost, sint, wt, wn, lng, lnb, qg3, kg, bd, tm, n_tab_blocks):
    rows, d = x2.shape
    aw = N_HEADS * HEAD_DIM
    kw = N_KV_HEADS * HEAD_DIM
    uw = wn.shape[1] - kw
    half = HEAD_DIM // 2
    full = lambda a: pl.BlockSpec(a.shape, lambda i: (0,) * a.ndim)
    return pl.pallas_call(
        _inproj_kernel,
        grid=(rows // tm,),
        in_specs=[
            pl.BlockSpec((tm, d), lambda i: (i, 0)),
            pl.BlockSpec((tm, kw), lambda i: (i % n_tab_blocks, 0)),
            pl.BlockSpec((tm, kw), lambda i: (i % n_tab_blocks, 0)),
            pl.BlockSpec((half, tm), lambda i: (0, i % n_tab_blocks)),
            pl.BlockSpec((half, tm), lambda i: (0, i % n_tab_blocks)),
            full(wt), full(wn), full(lng), full(lnb), full(qg3), full(kg), full(bd),
        ],
        out_specs=[
            pl.BlockSpec((aw, tm), lambda i: (0, i)),
            pl.BlockSpec((N_KV_HEADS, tm, HEAD_DIM), lambda i: (0, i, 0)),
            pl.BlockSpec((N_KV_HEADS, V_ROWS, tm), lambda i: (0, 0, i)),
            pl.BlockSpec((tm, uw), lambda i: (i, 0)),
        ],
        out_shape=[
            jax.ShapeDtypeStruct((aw, rows), BF16),
            jax.ShapeDtypeStruct((N_KV_HEADS, rows, HEAD_DIM), BF16),
            jax.ShapeDtypeStruct((N_KV_HEADS, V_ROWS, rows), BF16),
            jax.ShapeDtypeStruct((rows, uw), F32),
        ],
        compiler_params=pltpu.CompilerParams(dimension_semantics=("arbitrary",),
                                             vmem_limit_bytes=VMEM_LIMIT),
        name="inproj",
    )(x2, cosr, sinr, cost, sint, wt, wn, lng, lnb, qg3, kg, bd)


def _attn_kernel(qt_ref, k0_ref, k1_ref, vta_ref, vtb_ref, km_ref, vtm_ref, o_ref, m_sc, acc_sc, s_sc, mb_sc):
    j = pl.program_id(2)
    last = pl.num_programs(2) - 1
    grp = N_HEADS // N_KV_HEADS
    heads = lambda h: slice(h * HEAD_DIM, (h + 1) * HEAD_DIM)

    def score(k_ref, slot, h):
        s = _dot(k_ref[h // grp], qt_ref[heads(h), :])
        s_sc[slot, h] = s
        mb_sc[slot, h:h + 1, :] = jnp.max(s, axis=0, keepdims=True)

    def softmax_pv(h, s, m_blk, vt_g):
        m_prev = m_sc[h:h + 1, :]
        m_new = jnp.maximum(m_prev, m_blk)
        alpha = jnp.exp2(m_prev - m_new)
        p = jnp.exp2(s - m_new).astype(BF16)
        acc_sc[h] = alpha * acc_sc[h] + _dot(vt_g, p)
        m_sc[h:h + 1, :] = m_new

    def consume(vt_ref, slot, h):
        softmax_pv(h, s_sc[slot, h], mb_sc[slot, h:h + 1, :], vt_ref[h // grp])

    @pl.when(j == 0)
    def _first():
        m_sc[...] = jnp.full(m_sc.shape, -jnp.inf, F32)
        acc_sc[...] = jnp.zeros(acc_sc.shape, F32)
        for h in range(N_HEADS):
            s = _dot(km_ref[h // grp], qt_ref[heads(h), :])
            softmax_pv(h, s, jnp.max(s, axis=0, keepdims=True), vtm_ref[h // grp])
            score(k0_ref, 0, h)
        for h in range(N_HEADS):
            score(k1_ref, 1, h)
            consume(vtb_ref, 0, h)

    @pl.when((j > 0) & (j < last))
    def _middle():
        for h in range(N_HEADS):
            score(k0_ref, 0, h)
            consume(vta_ref, 1, h)
        for h in range(N_HEADS):
            score(k1_ref, 1, h)
            consume(vtb_ref, 0, h)

    @pl.when(j == last)
    def _last():
        for h in range(N_HEADS):
            consume(vta_ref, 1, h)
        acc = acc_sc[...]
        tq = acc.shape[2]
        out_t = (acc[:, 0:HEAD_DIM, :] / acc[:, HEAD_DIM:HEAD_DIM + 1, :]).reshape(N_HEADS * HEAD_DIM, tq)
        o_ref[...] = out_t.T.astype(BF16)


def _attention(qt, kk, vt, km, vtm, nb, tq, tk):
    aw, rows = qt.shape
    s = rows // nb
    nq = s // tq
    nkb = s // tk
    kblk = lambda f: pl.BlockSpec((N_KV_HEADS, tk, HEAD_DIM),
                                  lambda bi, qi, j: (0, bi * nkb + jnp.clip(f(j), 0, nkb - 1), 0))
    vblk = lambda f: pl.BlockSpec((N_KV_HEADS, V_ROWS, tk),
                                  lambda bi, qi, j: (0, 0, bi * nkb + jnp.clip(f(j), 0, nkb - 1)))
    return pl.pallas_call(
        _attn_kernel,
        grid=(nb, nq, nkb // 2 + 1),
        in_specs=[
            pl.BlockSpec((aw, tq), lambda bi, qi, j: (0, bi * nq + qi)),
            kblk(lambda j: 2 * j), kblk(lambda j: 2 * j + 1),
            vblk(lambda j: 2 * j - 1), vblk(lambda j: 2 * j),
            pl.BlockSpec(km.shape, lambda bi, qi, j: (0, 0, 0)),
            pl.BlockSpec(vtm.shape, lambda bi, qi, j: (0, 0, 0)),
        ],
        out_specs=pl.BlockSpec((tq, aw), lambda bi, qi, j: (bi * nq + qi, 0)),
        out_shape=jax.ShapeDtypeStruct((rows, aw), BF16),
        scratch_shapes=[
            pltpu.VMEM((N_HEADS, tq), F32),
            pltpu.VMEM((N_HEADS, V_ROWS, tq), F32),
            pltpu.VMEM((2, N_HEADS, tk, tq), F32),
            pltpu.VMEM((2, N_HEADS, tq), F32),
        ],
        compiler_params=pltpu.CompilerParams(
            dimension_semantics=("arbitrary", "arbitrary", "arbitrary"),
            vmem_limit_bytes=VMEM_LIMIT),
        name="attn",
    )(qt, kk, kk, vt, vt, km, vtm)


def _ssm_prep_kernel(*refs):
    for d in range(2):
        _ssm_prep_direction(d, *refs)


def _ssm_prep_direction(d, arow_ref, ldt_ref, bt_ref, ct_ref, m_ref, w_ref, v_ref, at_ref):
    t_chunk = SSM_CHUNK
    shift = int(math.log2(SSM_GROUP))
    df = float(d)
    dt = jnp.exp(ldt_ref[d, 0])

    def abar(ar, ai):
        mag = jnp.exp(ar * dt)
        ang = ai * dt
        return mag * jnp.cos(ang), mag * jnp.sin(ang)

    nbits = int(math.log2(t_chunk))

    def ipow(br, bi, e):
        pr = pi = None
        for k in range(nbits):
            bit = ((e >> k) & 1) == 1
            if pr is None:
                pr, pi = jnp.where(bit, br, 1.0), jnp.where(bit, bi, 0.0)
            else:
                pr, pi = jnp.where(bit, pr * br - pi * bi, pr), jnp.where(bit, pr * bi + pi * br, pi)
            br, bi = br * br - bi * bi, 2.0 * br * bi
        return pr, pi, br, bi

    ar_r = arow_ref[d, 0, 0:1, :]
    ai_r = arow_ref[d, 0, 1:2, :]
    abr, abi = abar(ar_r, ai_r)
    nr = abr - 1.0
    ni = abi
    den = ar_r * ar_r + ai_r * ai_r
    cr = (nr * ar_r + ni * ai_r) / den
    ci = (ni * ar_r - nr * ai_r) / den
    nstate = bt_ref.shape[4]
    nrow = t_chunk * SSM_GROUP

    def tile_rows(x):
        return jnp.broadcast_to(x[None], (t_chunk, SSM_GROUP, nstate)).reshape(nrow, nstate)

    btr = tile_rows(bt_ref[d, 0, 0])
    bti = tile_rows(bt_ref[d, 0, 1])
    bbr = cr * btr - ci * bti
    bbi = cr * bti + ci * btr
    jr = (lax.broadcasted_iota(jnp.int32, (nrow, 1), 0) >> shift).astype(F32)
    lj = jr + df * ((t_chunk - 1) - 2.0 * jr)
    step = lax.broadcasted_iota(jnp.int32, (t_chunk, 1), 0)
    lstep = (t_chunk - 1) - step if d == 1 else step

    def rep_rows(x):
        return jnp.broadcast_to(x[:, None, :], (t_chunk, SSM_GROUP, nstate)).reshape(nrow, nstate)

    mag2 = abr * abr + abi * abi
    enr, eni, _, _ = ipow(abr / mag2, -abi / mag2, lstep)
    enr, eni = rep_rows(enr), rep_rows(eni)
    bmr = enr * bbr - eni * bbi
    bmi = enr * bbi + eni * bbr
    ewr, ewi, atr, ati = ipow(abr, abi, (t_chunk - 1) - lstep)
    ewr, ewi = rep_rows(ewr), rep_rows(ewi)
    w_ref[d, 0] = jnp.concatenate([ewr * bbr - ewi * bbi, ewr * bbi + ewi * bbr], axis=1).astype(BF16)
    at_ref[d, 0] = jnp.concatenate([atr, ati], axis=1)

    ncol = t_chunk * SSM_GROUP
    lane = lax.broadcasted_iota(jnp.int32, (1, ncol), 1)
    tci = lane >> shift
    tile = jnp.where(lax.broadcasted_iota(jnp.int32, (SSM_GROUP, ncol), 0) == (lane & (SSM_GROUP - 1)),
                     1.0, 0.0).astype(BF16)

    def tile_cols(x):
        hi, lo = _split(x)
        lo2 = (x - hi.astype(F32) - lo.astype(F32)).astype(BF16)
        return _dot(hi, tile) + _dot(lo, tile) + _dot(lo2, tile)

    ctr = tile_cols(ct_ref[d, 0, 0])
    cti = tile_cols(ct_ref[d, 0, 1])
    tc = tci.astype(F32)
    lt = tc + df * ((t_chunk - 1) - 2.0 * tc)
    stepl = lax.broadcasted_iota(jnp.int32, (1, t_chunk), 1)
    lstepl = (t_chunk - 1) - stepl if d == 1 else stepl
    diag = (lax.broadcasted_iota(jnp.int32, (nstate, nstate), 0)
            == lax.broadcasted_iota(jnp.int32, (nstate, nstate), 1))
    abr_c = jnp.sum(jnp.where(diag, abr, 0.0), axis=1, keepdims=True)
    abi_c = jnp.sum(jnp.where(diag, abi, 0.0), axis=1, keepdims=True)
    rep = jnp.where(lax.broadcasted_iota(jnp.int32, (t_chunk, ncol), 0) == tci, 1.0, 0.0).astype(BF16)

    def rep_cols(x):
        hi, lo = _split(x)
        lo2 = (x - hi.astype(F32) - lo.astype(F32)).astype(BF16)
        return _dot(hi, rep) + _dot(lo, rep) + _dot(lo2, rep)

    ecr, eci = [rep_cols(x) for x in ipow(abr_c, abi_c, lstepl)[:2]]
    cmr = ctr * ecr - cti * eci
    cmi = ctr * eci + cti * ecr
    lhs_hi, lhs_lo = _split(jnp.concatenate([bmr, -bmi], axis=1))
    rhs_hi, rhs_lo = _split(jnp.concatenate([cmr, cmi], axis=0))
    m = _dot(lhs_hi, rhs_hi) + _dot(lhs_hi, rhs_lo) + _dot(lhs_lo, rhs_hi)
    m_ref[d, 0] = jnp.where(lj <= lt, m, 0.0).astype(BF16)
    c1r = cmr * abr_c - cmi * abi_c
    c1i = cmr * abi_c + cmi * abr_c
    v_ref[d, 0] = jnp.concatenate([c1r, -c1i], axis=0).astype(BF16)


def _ssm_prep(arow, ldt, bt, ct):
    nd, g = arow.shape[0], arow.shape[1]
    p = SSM_STATE
    n = SSM_CHUNK * SSM_GROUP
    blk = lambda a: pl.BlockSpec((nd, 1) + a.shape[2:], lambda gi: (0, gi) + (0,) * (a.ndim - 2))
    return pl.pallas_call(
        _ssm_prep_kernel,
        grid=(g,),
        in_specs=[blk(arow), blk(ldt), blk(bt), blk(ct)],
        out_specs=[
            pl.BlockSpec((nd, 1, n, n), lambda gi: (0, gi, 0, 0)),
            pl.BlockSpec((nd, 1, n, 2 * p), lambda gi: (0, gi, 0, 0)),
            pl.BlockSpec((nd, 1, 2 * p, n), lambda gi: (0, gi, 0, 0)),
            pl.BlockSpec((nd, 1, 1, 2 * p), lambda gi: (0, gi, 0, 0)),
        ],
        out_shape=[
            jax.ShapeDtypeStruct((nd, g, n, n), BF16),
            jax.ShapeDtypeStruct((nd, g, n, 2 * p), BF16),
            jax.ShapeDtypeStruct((nd, g, 2 * p, n), BF16),
            jax.ShapeDtypeStruct((nd, g, 1, 2 * p), F32),
        ],
        compiler_params=pltpu.CompilerParams(dimension_semantics=("arbitrary",)),
        name="ssm_prep",
    )(arow, ldt, bt, ct)


def _chunk_carry(chains):
    p = SSM_STATE
    z0, a0 = chains[0][0], chains[0][1]
    nc = z0.shape[0]
    row = lax.broadcasted_iota(jnp.int32, z0.shape, 0)
    is_re = lax.broadcasted_iota(jnp.int32, a0.shape, 1) < p
    sign = jnp.where(is_re, -1.0, 1.0)

    def parts(ap):
        sw = pltpu.roll(ap, p, 1)
        return jnp.where(is_re, ap, sw), sign * jnp.where(is_re, sw, ap)

    def cmul(x, ar_full, ai_sgn):
        return x * ar_full + pltpu.roll(x, p, 1) * ai_sgn

    coef = [parts(a) for _, a, _, _ in chains]
    firsts = [(nc - 1) if rev else 0 for _, _, _, rev in chains]
    es = [z + jnp.where(row == f, cmul(s0, *c), 0.0) for (z, _, s0, _), f, c in zip(chains, firsts, coef)]
    k = 1
    while k < nc:
        for i, (_, _, _, rev) in enumerate(chains):
            if rev:
                sh = jnp.where(row < nc - k, pltpu.roll(es[i], nc - k, 0), 0.0)
            else:
                sh = jnp.where(row >= k, pltpu.roll(es[i], k, 0), 0.0)
            es[i] = es[i] + cmul(sh, *coef[i])
        k *= 2
        if k < nc:
            coef = [parts(cmul(jnp.where(is_re, ar, sign * ai), ar, ai)) for ar, ai in coef]
    outs = []
    for e, (_, _, s0, rev), f in zip(es, chains, firsts):
        outs.append(jnp.where(row == f, s0, pltpu.roll(e, nc - 1 if rev else 1, 0)))
    return outs


def _ssm_kernel(u_ref, um_ref, m_ref, w_ref, v_ref, at_ref, dvec_ref, y_ref, ug_sc, yg_sc):
    p = SSM_STATE
    t_chunk = SSM_CHUNK
    hgrp = SSM_GROUP
    gb = u_ref.shape[2] // hgrp
    per_tile = u_ref.shape[2] // hgrp
    nc = u_ref.shape[1] // t_chunk
    lane_blk = lax.broadcasted_iota(jnp.int32, (nc, u_ref.shape[2]), 1) >> int(math.log2(hgrp))

    width = u_ref.shape[2]

    def block_transpose(arrs):
        arrs = list(arrs)
        d = len(arrs) // 2
        while d >= 1:
            low_bit = (lane_blk & d) == 0
            for i in range(len(arrs)):
                if i & d == 0:
                    lo, hi = arrs[i], arrs[i + d]
                    arrs[i] = jnp.where(low_bit, lo, pltpu.roll(hi, d * hgrp, 1))
                    arrs[i + d] = jnp.where(low_bit, pltpu.roll(lo, width - d * hgrp, 1), hi)
            d //= 2
        return arrs

    for q in range(t_chunk // per_tile):
        steps = [u_ref[0, pl.ds(q * per_tile + r, nc, stride=t_chunk), :] for r in range(per_tile)]
        for g, tile in enumerate(block_transpose(steps)):
            ug_sc[g, :, q * width:(q + 1) * width] = tile

    def groups(i, carry):
        gs = [i * SSM_GROUPS_TOGETHER + k for k in range(SSM_GROUPS_TOGETHER)]
        us = [ug_sc[g] for g in gs]
        ubs = [u.astype(BF16) for u in us]
        yin = [[_dot(ub, m_ref[d, g]) for d in range(2)] for g, ub in zip(gs, ubs)]
        chains = []
        for g, ub in zip(gs, ubs):
            for d in range(2):
                if d == 0:
                    s0 = _dot(um_ref[g].astype(BF16), w_ref[0, g])[0:1, :]
                else:
                    s0 = jnp.zeros((1, 2 * p), F32)
                chains.append((_dot(ub, w_ref[d, g]), at_ref[d, g], s0, d == 1))
        s_in = _chunk_carry(chains)
        for k, g in enumerate(gs):
            y = dvec_ref[g] * us[k] + yin[k][0] + yin[k][1]
            for d in range(2):
                y = y + _dot(s_in[2 * k + d].astype(BF16), v_ref[d, g])
            yg_sc[g] = y
        return carry

    lax.fori_loop(0, gb // SSM_GROUPS_TOGETHER, groups, 0)

    for q in range(t_chunk // per_tile):
        srcs = [yg_sc[g, :, q * width:(q + 1) * width] for g in range(gb)]
        for r, row in enumerate(block_transpose(srcs)):
            y_ref[0, pl.ds(q * per_tile + r, nc, stride=t_chunk), :] = row


def _ssm(u3, um2, m, w, v, at, dvec):
    b, s, sw = u3.shape
    g = sw // SSM_GROUP
    n = SSM_CHUNK * SSM_GROUP
    p = SSM_STATE
    lanes = 128
    gb = lanes // SSM_GROUP
    return pl.pallas_call(
        _ssm_kernel,
        grid=(b, sw // lanes),
        in_specs=[
            pl.BlockSpec((1, s, lanes), lambda bi, gi: (bi, 0, gi)),
            pl.BlockSpec((gb, 8, n), lambda bi, gi: (gi, 0, 0)),
            pl.BlockSpec((2, gb, n, n), lambda bi, gi: (0, gi, 0, 0)),
            pl.BlockSpec((2, gb, n, 2 * p), lambda bi, gi: (0, gi, 0, 0)),
            pl.BlockSpec((2, gb, 2 * p, n), lambda bi, gi: (0, gi, 0, 0)),
            pl.BlockSpec((2, gb, 1, 2 * p), lambda bi, gi: (0, gi, 0, 0)),
            pl.BlockSpec((gb, 1, n), lambda bi, gi: (gi, 0, 0)),
        ],
        out_specs=pl.BlockSpec((1, s, lanes), lambda bi, gi: (bi, 0, gi)),
        out_shape=jax.ShapeDtypeStruct((b, s, sw), F32),
        scratch_shapes=[pltpu.VMEM((gb, s // SSM_CHUNK, n), F32), pltpu.VMEM((gb, s // SSM_CHUNK, n), F32)],
        compiler_params=pltpu.CompilerParams(dimension_semantics=("arbitrary", "arbitrary"),
                                             vmem_limit_bytes=VMEM_LIMIT),
        name="ssm",
    )(u3, um2, m, w, v, at, dvec)


def _merge_kernel(x_ref, ya_ref, ys_ref, wgt_ref, wglu_ref, bglu_ref, wab_ref, wsb_ref, wo_ref,
                  lng_ref, lnb_ref, l1g_ref, l1b_ref, wr_ref,
                  h1_ref, tok_ref, afft_ref):
    d = x_ref.shape[1]
    h = _ln(x_ref[...], lng_ref[...], lnb_ref[...])
    gates = _dot(h.astype(BF16), wgt_ref[...])
    ga = _sigmoid(gates[:, 0:d])
    gs = _sigmoid(gates[:, d:2 * d])
    ys = ys_ref[...]
    y = 0.5 * ys * (1.0 + jnp.tanh(math.sqrt(2.0 / math.pi) * (ys + 0.044715 * (ys * ys * ys))))
    yg = y * _sigmoid(_dot(y.astype(BF16), wglu_ref[...]) + bglu_ref[...])
    merged = ga * _dot(ya_ref[...], wab_ref[...]) + gs * _dot(yg.astype(BF16), wsb_ref[...])
    h1 = _ln(DEEPNORM_ALPHA * h + _dot(merged.astype(BF16), wo_ref[...]), l1g_ref[...], l1b_ref[...])
    h1_ref[...] = h1
    hi, lo = _split(h1)
    ne = afft_ref.shape[0]
    cross = _dot(hi, wr_ref[...])
    logits = cross[:, 0:GATE_LANES] + cross[:, GATE_LANES:] + _dot(lo, wr_ref[:, 0:GATE_LANES])
    lane = lax.broadcasted_iota(jnp.int32, logits.shape, 1)
    logits = jnp.where(lane < 3 * ne, logits, -jnp.inf)
    ex = jnp.exp(logits - jnp.max(logits, axis=1, keepdims=True))
    aff = ex / jnp.sum(jnp.where(lane < ne, ex, 0.0), axis=1, keepdims=True)
    p0 = aff.astype(BF16)
    r1 = aff - p0.astype(F32)
    p1 = r1.astype(BF16)
    p2 = (r1 - p1.astype(F32)).astype(BF16)
    pieces = jnp.where(lane < ne, p0, jnp.where(lane < 2 * ne, p1, p2))
    tok_ref[...] = jnp.concatenate([hi, pieces], axis=1)
    afft_ref[...] = aff.T[0:ne, :]


def _merge(x2, ya, ys, wgt, wglu, bglu, wab, wsb, wo, lng, lnb, l1g, l1b, wr, ne, tm):
    rows, d = x2.shape
    full = lambda a: pl.BlockSpec(a.shape, lambda i: (0,) * a.ndim)
    return pl.pallas_call(
        _merge_kernel,
        grid=(rows // tm,),
        in_specs=[
            pl.BlockSpec((tm, d), lambda i: (i, 0)),
            pl.BlockSpec((tm, ya.shape[1]), lambda i: (i, 0)),
            pl.BlockSpec((tm, ys.shape[1]), lambda i: (i, 0)),
            full(wgt), full(wglu), full(bglu), full(wab), full(wsb), full(wo),
            full(lng), full(lnb), full(l1g), full(l1b), full(wr),
        ],
        out_specs=[
            pl.BlockSpec((tm, d), lambda i: (i, 0)),
            pl.BlockSpec((tm, d + GATE_LANES), lambda i: (i, 0)),
            pl.BlockSpec((ne, tm), lambda i: (0, i)),
        ],
        out_shape=[
            jax.ShapeDtypeStruct((rows, d), F32),
            jax.ShapeDtypeStruct((rows, d + GATE_LANES), BF16),
            jax.ShapeDtypeStruct((ne, rows), F32),
        ],
        compiler_params=pltpu.CompilerParams(dimension_semantics=("arbitrary",),
                                             vmem_limit_bytes=VMEM_LIMIT),
        name="merge",
    )(x2, ya, ys, wgt, wglu, bglu, wab, wsb, wo, lng, lnb, l1g, l1b, wr)


def _route_kernel(afft_ref, tri_ref, scl_ref, lo_ref, *, cap):
    aff = afft_ref[...]
    ne, s = aff.shape
    capf = float(cap)

    def as_float(bits):
        return lax.bitcast_convert_type(bits, F32)

    def search(i, t):
        cand = t | jnp.left_shift(jnp.int32(1), 30 - i)
        cnt = jnp.sum(jnp.where(aff >= as_float(cand), 1.0, 0.0), axis=1, keepdims=True)
        return jnp.where(cnt >= capf, cand, t)

    thr_bits = lax.fori_loop(0, 31, search, jnp.zeros((ne, 1), jnp.int32))
    gt = aff >= as_float(thr_bits + 1)
    eq = (aff >= as_float(thr_bits)) & jnp.logical_not(gt)
    need = capf - jnp.sum(jnp.where(gt, 1.0, 0.0), axis=1, keepdims=True)
    tri = tri_ref[...]
    rt = ROUTE_TILE
    nt = s // rt
    col = lax.broadcasted_iota(jnp.int32, (ne, nt), 1)
    carry_eq = jnp.zeros((ne, 1), F32)
    carry_sel = jnp.zeros((ne, 1), F32)
    lo_val = jnp.zeros((ne, nt), F32)
    for t in range(nt):
        sl = slice(t * rt, (t + 1) * rt)
        eq_b = eq[:, sl]
        ceq = _dot(jnp.where(eq_b, 1.0, 0.0).astype(BF16), tri)
        sel_b = gt[:, sl] | (eq_b & ((ceq + carry_eq) <= need))
        carry_eq = carry_eq + ceq[:, rt - 1:rt]
        csel = _dot(jnp.where(sel_b, 1.0, 0.0).astype(BF16), tri)
        scl_ref[0, :, sl] = jnp.where(sel_b, csel, 0.0)
        lo_val = jnp.where(col == t, carry_sel, lo_val)
        carry_sel = carry_sel + csel[:, rt - 1:rt]
    lo_ref[0] = lo_val


def _route(afft, tri, nb, cap):
    ne, rows = afft.shape
    s = rows // nb
    nt = s // ROUTE_TILE
    return pl.pallas_call(
        functools.partial(_route_kernel, cap=cap),
        grid=(nb,),
        in_specs=[
            pl.BlockSpec((ne, s), lambda b: (0, b)),
            pl.BlockSpec(tri.shape, lambda b: (0, 0)),
        ],
        out_specs=[
            pl.BlockSpec((1, ne, s), lambda b: (b, 0, 0)),
            pl.BlockSpec((1, ne, nt), lambda b: (b, 0, 0)),
        ],
        out_shape=[
            jax.ShapeDtypeStruct((nb, ne, s), F32),
            jax.ShapeDtypeStruct((nb, ne, nt), F32),
        ],
        compiler_params=pltpu.CompilerParams(dimension_semantics=("arbitrary",)),
        name="route",
    )(afft, tri)


def _ffn_kernel(lo_ref, t_ref, scl_ref, wg_ref, wu_ref, wd_ref, ye_ref, xe_sc, xb_sc, gate_sc, acc_sc,
                *, cap, flag_off):
    fc = pl.program_id(2)

    @pl.when(fc == 0)
    def _gather():
        _ffn_gather(lo_ref, t_ref, scl_ref, xe_sc, xb_sc, gate_sc, cap=cap, flag_off=flag_off)
        acc_sc[...] = jnp.zeros(acc_sc.shape, F32)

    xb = xb_sc[...]
    g = _dot(xb, wg_ref[0, 0].astype(BF16))
    u = _dot(xb, wu_ref[0, 0].astype(BF16))
    hh = (g * _sigmoid(g)) * u
    acc_sc[...] += _dot(hh.astype(BF16), wd_ref[0, 0].astype(BF16))

    @pl.when(fc == pl.num_programs(2) - 1)
    def _emit():
        ye_ref[0, 0] = (acc_sc[...] * gate_sc[...]).astype(BF16)


def _ffn_gather(lo_ref, t_ref, scl_ref, xe_sc, xb_sc, gate_sc, *, cap, flag_off):
    b = pl.program_id(0)
    e = pl.program_id(1)
    ne = pl.num_programs(1)
    rt = ROUTE_TILE
    nt = t_ref.shape[1] // rt
    xe_sc[...] = jnp.zeros(xe_sc.shape, F32)
    r = lax.broadcasted_iota(jnp.int32, (PIECE, rt), 0).astype(F32)

    def tile_info(tau):
        base = (b * ne + e) * (nt + 1) + tau
        lo = lo_ref[base]
        n_pieces = jnp.right_shift((lo & 7) + (lo_ref[base + 1] - lo) + (PIECE - 1), PIECE_SHIFT)
        return lo, n_pieces

    def add_piece(tau, lo, p):
        off = lo & 7
        scl = scl_ref[0, 0, pl.ds(tau, 1), :]
        tok = t_ref[0, pl.ds(pl.multiple_of(tau * rt, rt), rt), :]
        shift = (off - 1 - p * PIECE).astype(F32)
        onehot = jnp.where((scl > 0.0) & ((scl + shift) == r), 1.0, 0.0).astype(BF16)
        xe_sc[pl.ds(pl.multiple_of(lo - off + p * PIECE, 8), PIECE), :] += _dot(onehot, tok)

    def first_piece(tau, carry):
        add_piece(tau, tile_info(tau)[0], 0)
        return carry

    def more_pieces(tau, carry):
        lo, n_pieces = tile_info(tau)
        lax.fori_loop(1, n_pieces, lambda p, c: (add_piece(tau, lo, p), c)[1], 0)
        return carry

    lax.fori_loop(0, nt, first_piece, 0, unroll=4)

    @pl.when(lo_ref[flag_off + b * ne + e] > 0)
    def _rare():
        lax.fori_loop(0, nt, more_pieces, 0)

    d = xb_sc.shape[1]
    xb_sc[...] = xe_sc[0:cap, 0:d].astype(BF16)
    gl = xe_sc[0:cap, d:d + GATE_LANES]
    lane = lax.broadcasted_iota(jnp.int32, gl.shape, 1)
    mine = ((lane & (N_EXPERTS - 1)) == e) & (lane < 3 * N_EXPERTS)
    gate_sc[...] = jnp.sum(jnp.where(mine, gl, 0.0), axis=1, keepdims=True)


def _ffn(lo_i, t, scl4, wg, wu, wd, layer, cap, f_chunk):
    b, s, dx = t.shape
    _, ne, d, f = wg.shape
    nt = s // ROUTE_TILE
    grid_spec = pltpu.PrefetchScalarGridSpec(
        num_scalar_prefetch=1,
        grid=(b, ne, f // f_chunk),
        in_specs=[
            pl.BlockSpec((1, s, dx), lambda bi, ei, fi, lo: (bi, 0, 0), pipeline_mode=pl.Buffered(1)),
            pl.BlockSpec((1, 1, nt, ROUTE_TILE), lambda bi, ei, fi, lo: (bi, ei, 0, 0)),
            pl.BlockSpec((1, 1, d, f_chunk), lambda bi, ei, fi, lo: (layer, ei, 0, fi)),
            pl.BlockSpec((1, 1, d, f_chunk), lambda bi, ei, fi, lo: (layer, ei, 0, fi)),
            pl.BlockSpec((1, 1, f_chunk, d), lambda bi, ei, fi, lo: (layer, ei, fi, 0)),
        ],
        out_specs=pl.BlockSpec((1, 1, cap, d), lambda bi, ei, fi, lo: (bi, ei, 0, 0)),
        scratch_shapes=[pltpu.VMEM((cap + PIECE, dx), F32), pltpu.VMEM((cap, d), BF16),
                        pltpu.VMEM((cap, 1), F32), pltpu.VMEM((cap, d), F32)],
    )
    return pl.pallas_call(
        functools.partial(_ffn_kernel, cap=cap, flag_off=b * ne * (nt + 1)),
        grid_spec=grid_spec,
        out_shape=jax.ShapeDtypeStruct((b, ne, cap, d), BF16),
        compiler_params=pltpu.CompilerParams(dimension_semantics=("arbitrary", "arbitrary", "arbitrary"),
                                             vmem_limit_bytes=VMEM_LIMIT),
        name="ffn",
    )(lo_i, t, scl4, wg, wu, wd)


def _combine_kernel(lo_ref, h1_ref, sclt_ref, ye_hbm, g_ref, b_ref, o_ref, win, xwin, acc_sc, sem, xsem,
                    *, cap, flag_off):
    b = pl.program_id(0)
    tau = pl.program_id(1)
    nb = pl.num_programs(0)
    nt = pl.num_programs(1)
    ne = win.shape[1]
    d = win.shape[3]
    step = b * nt + tau
    slot = step & 1

    def pieces(bi, ti, e):
        base = (bi * ne + e) * (nt + 1) + ti
        lo = lo_ref[base]
        n_sel = lo_ref[base + 1] - lo
        n_pieces = jnp.maximum(jnp.right_shift((lo & 15) + n_sel + (PIECE - 1), PIECE_SHIFT), 1)
        first = jnp.minimum(lo - (lo & 15), cap - n_pieces * PIECE)
        return lo, first, n_pieces

    def copy(bi, e, first, p, buf, s):
        start = pl.multiple_of(first + p * PIECE, 16)
        return pltpu.make_async_copy(ye_hbm.at[bi, e, pl.ds(start, PIECE), :], buf, s)

    def first_pieces(bi, ti, sl):
        return [copy(bi, e, pieces(bi, ti, e)[1], 0, win.at[sl, e], sem.at[sl, e]) for e in range(ne)]

    @pl.when(step == 0)
    def _prime():
        for cp in first_pieces(b, tau, slot):
            cp.start()

    @pl.when(step + 1 < nb * nt)
    def _prefetch():
        wrap = tau + 1 == nt
        for cp in first_pieces(jnp.where(wrap, b + 1, b), jnp.where(wrap, 0, tau + 1), 1 - slot):
            cp.start()

    for cp in first_pieces(b, tau, slot):
        cp.wait()

    sclt = sclt_ref[0]
    col = lax.broadcasted_iota(jnp.int32, (1, ne * PIECE), 1)
    grp = jnp.right_shift(col, PIECE_SHIFT)
    expand = jnp.where(lax.broadcasted_iota(jnp.int32, (ne, ne * PIECE), 0) == grp, 1.0, 0.0).astype(BF16)
    scl = _dot(sclt.astype(BF16), expand)
    shift = jnp.zeros(col.shape, F32)
    for e in range(ne):
        lo, first, _ = pieces(b, tau, e)
        shift = jnp.where(grp == e, (lo - first - 1).astype(F32), shift)
    r = (col & (PIECE - 1)).astype(F32)
    onehot = jnp.where((scl > 0.0) & ((scl + shift) == r), 1.0, 0.0).astype(BF16)
    acc_sc[...] = DEEPNORM_ALPHA * h1_ref[0] + _dot(onehot, win[slot].reshape(ne * PIECE, d))

    @pl.when(lo_ref[flag_off + step] > 0)
    def _rare():
        for e in range(ne):
            lo, first, n_pieces = pieces(b, tau, e)

            def extra(p, c, e=e, lo=lo, first=first):
                cp = copy(b, e, first, p, xwin, xsem.at[0])
                cp.start()
                cp.wait()
                se = sclt_ref[0][:, e:e + 1]
                rr = lax.broadcasted_iota(jnp.int32, (se.shape[0], PIECE), 1).astype(F32)
                oh = jnp.where((se > 0.0) & ((se + (lo - first - p * PIECE - 1).astype(F32)) == rr), 1.0, 0.0)
                acc_sc[...] += _dot(oh.astype(BF16), xwin[...])
                return c

            lax.fori_loop(1, n_pieces, extra, 0)

    o_ref[0] = _ln(acc_sc[...], g_ref[...], b_ref[...])


def _combine(lo_i, h1, sclt, ye, g, bb, cap):
    b, s, d = h1.shape
    ne = sclt.shape[2]
    rt = ROUTE_TILE
    grid_spec = pltpu.PrefetchScalarGridSpec(
        num_scalar_prefetch=1,
        grid=(b, s // rt),
        in_specs=[
            pl.BlockSpec((1, rt, d), lambda bi, ti, lo: (bi, ti, 0)),
            pl.BlockSpec((1, rt, ne), lambda bi, ti, lo: (bi, ti, 0)),
            pl.BlockSpec(memory_space=pl.ANY),
            pl.BlockSpec(g.shape, lambda bi, ti, lo: (0, 0)),
            pl.BlockSpec(bb.shape, lambda bi, ti, lo: (0, 0)),
        ],
        out_specs=pl.BlockSpec((1, rt, d), lambda bi, ti, lo: (bi, ti, 0)),
        scratch_shapes=[pltpu.VMEM((2, ne, PIECE, d), BF16), pltpu.VMEM((PIECE, d), BF16),
                        pltpu.VMEM((rt, d), F32),
                        pltpu.SemaphoreType.DMA((2, ne)), pltpu.SemaphoreType.DMA((1,))],
    )
    return pl.pallas_call(
        functools.partial(_combine_kernel, cap=cap, flag_off=b * ne * (s // rt + 1) + b * ne),
        grid_spec=grid_spec,
        out_shape=jax.ShapeDtypeStruct((b, s, d), F32),
        compiler_params=pltpu.CompilerParams(dimension_semantics=("arbitrary", "arbitrary"),
                                             vmem_limit_bytes=VMEM_LIMIT),
        name="combine",
    )(lo_i, h1, sclt, ye, g, bb)


def _rope_tables(n_tokens):
    half = HEAD_DIM // 2
    inv_freq = ROPE_THETA ** (-jnp.arange(0, half, 2, dtype=F32) / half)
    rows = n_tokens // GRID_W
    row = jnp.repeat(jnp.arange(rows, dtype=F32), GRID_W)
    colv = jnp.tile(jnp.arange(GRID_W, dtype=F32), rows)
    ang = jnp.concatenate([row[:, None] * inv_freq, colv[:, None] * inv_freq], axis=-1)
    return jnp.cos(ang), jnp.sin(ang)


def kernel(x, meta_tokens, ln_in_g, ln_in_b, w_in, q_norm_g, k_norm_g, ssm_a_re, ssm_a_im, ssm_log_dt,
           ssm_b_re, ssm_b_im, ssm_c_re, ssm_c_im, ssm_d, w_glu, b_glu, w_attn_br, w_ssm_br, w_o,
           ln1_g, ln1_b, w_router, w_gate_e, w_up_e, w_down_e, ln2_g, ln2_b):
    b, s, d = x.shape
    aw = N_HEADS * HEAD_DIM
    kw = N_KV_HEADS * HEAD_DIM
    sw = d // 2
    g = sw // SSM_GROUP
    half = HEAD_DIM // 2
    cap = CAPACITY_FACTOR * s // N_EXPERTS
    rows = b * s
    l = 0

    perm = np.concatenate([np.arange(0, HEAD_DIM, 2), np.arange(1, HEAD_DIM, 2)])
    qcols = np.concatenate([h * HEAD_DIM + perm for h in range(N_HEADS)])
    kcols = aw + np.concatenate([h * HEAD_DIM + perm for h in range(N_KV_HEADS)])
    wl = w_in[l]
    w_t = jnp.concatenate([wl[:, qcols], wl[:, aw + kw:aw + 2 * kw]], axis=1).T.astype(BF16)
    w_n = jnp.concatenate([wl[:, kcols], wl[:, aw + 2 * kw:aw + 2 * kw + sw]], axis=1).astype(BF16)
    w_gates = wl[:, aw + 2 * kw + sw:].astype(BF16)
    qg3 = (jnp.tile(q_norm_g[l][perm], N_HEADS) * (HEAD_DIM ** -0.5 * math.log2(math.e))).reshape(N_HEADS, HEAD_DIM, 1)
    kg = jnp.tile(k_norm_g[l][perm], N_KV_HEADS)[None, :]
    bd = jnp.asarray(np.kron(np.eye(N_KV_HEADS), np.full((HEAD_DIM, HEAD_DIM), 1.0 / HEAD_DIM)), BF16)
    lng = ln_in_g[None, :]
    lnb = ln_in_b[None, :]
    cos, sin = _rope_tables(s)
    cosr = jnp.tile(cos, (1, 2 * N_KV_HEADS))
    sinr = jnp.tile(jnp.concatenate([-sin, sin], axis=1), (1, N_KV_HEADS))
    cost = cos.T
    sint = sin.T

    x2 = x.reshape(rows, d)
    tm_in = 1024
    qt, kk, vt, u2 = _inproj(x2, cosr, sinr, cost, sint, w_t, w_n, lng, lnb, qg3, kg, bd, tm_in, s // tm_in)
    meta_p = jnp.pad(meta_tokens, ((0, META_PAD - N_META), (0, 0)))
    ones_r = jnp.ones((META_PAD, kw), F32)
    _, km, vtm, um = _inproj(meta_p, ones_r, jnp.zeros_like(ones_r), jnp.ones((half, META_PAD), F32),
                             jnp.zeros((half, META_PAD), F32), w_t, w_n, lng, lnb, qg3, kg, bd, META_PAD, 1)
    um = um[:N_META]
    km = km[:, :N_META, :]
    vtm = vtm[:, :, :N_META]

    ya = _attention(qt, kk, vt, km, vtm, b, 512, 512)

    t_chunk = SSM_CHUNK
    hgrp = SSM_GROUP
    n = t_chunk * hgrp
    arow = jnp.stack([ssm_a_re[l], ssm_a_im[l]], axis=2)
    ldt = ssm_log_dt[l][:, :, None, None]
    bt = jnp.stack([ssm_b_re[l], ssm_b_im[l]], axis=2)
    bt = jnp.swapaxes(bt, 3, 4)
    ct = jnp.stack([ssm_c_re[l], ssm_c_im[l]], axis=2)
    ct = jnp.swapaxes(ct, 3, 4)
    mm, ww, vv, at = _ssm_prep(arow, ldt, bt, ct)
    nc = s // t_chunk
    um_g = um.reshape(N_META, g, hgrp).transpose(1, 0, 2).reshape(g, 1, N_META * hgrp)
    um_g = jnp.pad(um_g, ((0, 0), (0, 7), (n - N_META * hgrp, 0)))
    dvec = jnp.tile(ssm_d[l], (1, t_chunk))[:, None, :]
    ys = _ssm(u2.reshape(b, s, sw), um_g, mm, ww, vv, at, dvec).reshape(rows, sw)

    wr_hi = w_router[l].astype(BF16)
    wr_lo = (w_router[l] - wr_hi.astype(F32)).astype(BF16)
    lane_pad = ((0, 0), (0, GATE_LANES - 3 * N_EXPERTS))
    wr = jnp.concatenate([jnp.pad(jnp.tile(wr_hi, (1, 3)), lane_pad), jnp.pad(jnp.tile(wr_lo, (1, 3)), lane_pad)],
                         axis=1)
    h1, tok, afft = _merge(
        x2, ya, ys, w_gates, w_glu[l].astype(BF16), b_glu[l][None, :],
        w_attn_br[l].astype(BF16), w_ssm_br[l].astype(BF16), w_o[l].astype(BF16),
        lng, lnb, ln1_g[l][None, :], ln1_b[l][None, :], wr, N_EXPERTS, 512)

    tri = jnp.asarray(np.triu(np.ones((ROUTE_TILE, ROUTE_TILE), np.float32)), BF16)
    scl, lo_f = _route(afft, tri, b, cap)
    nt = s // ROUTE_TILE
    lo3 = jnp.pad(lo_f.astype(jnp.int32), ((0, 0), (0, 0), (0, 1)), constant_values=cap)
    seg = lo3[:, :, 1:] - lo3[:, :, :-1]
    multi = lambda align: ((lo3[:, :, :-1] & (align - 1)) + seg + (PIECE - 1)) // PIECE > 1
    lo_i = jnp.concatenate([lo3.reshape(-1),
                            jnp.any(multi(8), axis=2).astype(jnp.int32).reshape(-1),
                            jnp.any(multi(16), axis=1).astype(jnp.int32).reshape(-1)])
    ye = _ffn(lo_i, tok.reshape(b, s, d + GATE_LANES), scl.reshape(b, N_EXPERTS, nt, ROUTE_TILE),
              w_gate_e, w_up_e, w_down_e, l, cap, 512)
    out = _combine(lo_i, h1.reshape(b, s, d), jnp.swapaxes(scl, 1, 2), ye,
                   ln2_g[l][None, :], ln2_b[l][None, :], cap)
    return out
```

```python
import functools
import math

import numpy as np
import jax
import jax.numpy as jnp
from jax import lax
from jax.experimental import pallas as pl
from jax.experimental.pallas import tpu as pltpu

F32 = jnp.float32
BF16 = jnp.bfloat16

N_META = 16
GRID_W = 64
N_HEADS = 8
N_KV_HEADS = 2
HEAD_DIM = 64
ROPE_THETA = 10000.0
SSM_GROUP = 16
SSM_STATE = 64
N_EXPERTS = 16
CAPACITY_FACTOR = 2
LN_EPS = 1e-5
QK_EPS = 1e-6
DEPTH = 1
DEEPNORM_ALPHA = (2.0 * DEPTH) ** 0.25

SSM_CHUNK = 32
SSM_GROUPS_TOGETHER = 4
ROUTE_TILE = 256
GATE_LANES = 128
PIECE_SHIFT = 6
PIECE = 1 << PIECE_SHIFT
META_PAD = 128
V_ROWS = HEAD_DIM + 16
VMEM_LIMIT = 56 * 1024 * 1024


def _ln(x, g, b):
    mu = jnp.mean(x, axis=-1, keepdims=True)
    xc = x - mu
    var = jnp.mean(xc * xc, axis=-1, keepdims=True)
    return xc * lax.rsqrt(var + LN_EPS) * g + b


def _sigmoid(x):
    return 1.0 / (1.0 + jnp.exp(-x))


def _split(t):
    hi = t.astype(BF16)
    lo = (t - hi.astype(F32)).astype(BF16)
    return hi, lo


def _dot(a, b):
    return jnp.dot(a, b, preferred_element_type=F32)


def _dot_nt(a, b):
    return lax.dot_general(a, b, (((1,), (1,)), ((), ())), preferred_element_type=F32)


def _inproj_kernel(x_ref, cosr_ref, sinr_ref, cost_ref, sint_ref, wt_ref, wn_ref, lng_ref, lnb_ref,
                   qg_ref, kg_ref, bd_ref, qt_ref, k_ref, vt_ref, u_ref):
    aw = N_HEADS * HEAD_DIM
    kw = N_KV_HEADS * HEAD_DIM
    half = HEAD_DIM // 2
    h = _ln(x_ref[...], lng_ref[...], lnb_ref[...])
    hb = h.astype(BF16)
    pt = _dot_nt(wt_ref[...], hb)
    pn = _dot(hb, wn_ref[...])
    tm = hb.shape[0]

    qt = pt[0:aw].reshape(N_HEADS, HEAD_DIM, tm)
    ms = jnp.mean(qt * qt, axis=1, keepdims=True)
    qn = qt * lax.rsqrt(ms + QK_EPS) * qg_ref[...]
    x0 = qn[:, 0:half, :]
    x1 = qn[:, half:, :]
    c = cost_ref[...][None]
    s = sint_ref[...][None]
    qr = jnp.concatenate([x0 * c - x1 * s, x0 * s + x1 * c], axis=1)
    qt_ref[...] = qr.reshape(aw, tm).astype(BF16)
    vrow = lax.broadcasted_iota(jnp.int32, (N_KV_HEADS, V_ROWS - HEAD_DIM, tm), 1)
    vt_ref[...] = jnp.concatenate([pt[aw:aw + kw].reshape(N_KV_HEADS, HEAD_DIM, tm),
                                   jnp.where(vrow == 0, 1.0, 0.0)], axis=1).astype(BF16)

    kk = pn[:, 0:kw]
    hi, lo = _split(kk * kk)
    bd = bd_ref[...]
    msk = _dot(hi, bd) + _dot(lo, bd)
    kn = kk * lax.rsqrt(msk + QK_EPS) * kg_ref[...]
    lane = lax.broadcasted_iota(jnp.int32, kn.shape, 1)
    first = (lane & (HEAD_DIM - 1)) < half
    partner = jnp.where(first, pltpu.roll(kn, kw - half, 1), pltpu.roll(kn, half, 1))
    kr = (kn * cosr_ref[...] + partner * sinr_ref[...]).astype(BF16)
    for g in range(N_KV_HEADS):
        k_ref[g] = kr[:, g * HEAD_DIM:(g + 1) * HEAD_DIM]
    u_ref[...] = pn[:, kw:]


def _inproj(x2, cosr, sinr, cost, sint, wt, wn, lng, lnb, qg3, kg, bd, tm, n_tab_blocks):
    rows, d = x2.shape
    aw = N_HEADS * HEAD_DIM
    kw = N_KV_HEADS * HEAD_DIM
    uw = wn.shape[1] - kw
    half = HEAD_DIM // 2
    full = lambda a: pl.BlockSpec(a.shape, lambda i: (0,) * a.ndim)
    return pl.pallas_call(
        _inproj_kernel,
        grid=(rows // tm,),
        in_specs=[
            pl.BlockSpec((tm, d), lambda i: (i, 0)),
            pl.BlockSpec((tm, kw), lambda i: (i % n_tab_blocks, 0)),
            pl.BlockSpec((tm, kw), lambda i: (i % n_tab_blocks, 0)),
            pl.BlockSpec((half, tm), lambda i: (0, i % n_tab_blocks)),
            pl.BlockSpec((half, tm), lambda i: (0, i % n_tab_blocks)),
            full(wt), full(wn), full(lng), full(lnb), full(qg3), full(kg), full(bd),
        ],
        out_specs=[
            pl.BlockSpec((aw, tm), lambda i: (0, i)),
            pl.BlockSpec((N_KV_HEADS, tm, HEAD_DIM), lambda i: (0, i, 0)),
            pl.BlockSpec((N_KV_HEADS, V_ROWS, tm), lambda i: (0, 0, i)),
            pl.BlockSpec((tm, uw), lambda i: (i, 0)),
        ],
        out_shape=[
            jax.ShapeDtypeStruct((aw, rows), BF16),
            jax.ShapeDtypeStruct((N_KV_HEADS, rows, HEAD_DIM), BF16),
            jax.ShapeDtypeStruct((N_KV_HEADS, V_ROWS, rows), BF16),
            jax.ShapeDtypeStruct((rows, uw), F32),
        ],
        compiler_params=pltpu.CompilerParams(dimension_semantics=("arbitrary",),
                                             vmem_limit_bytes=VMEM_LIMIT),
        name="inproj",
    )(x2, cosr, sinr, cost, sint, wt, wn, lng, lnb, qg3, kg, bd)


def _attn_kernel(qt_ref, k0_ref, k1_ref, vta_ref, vtb_ref, km_ref, vtm_ref, o_ref, m_sc, acc_sc, s_sc, mb_sc):
    j = pl.program_id(2)
    last = pl.num_programs(2) - 1
    grp = N_HEADS // N_KV_HEADS
    heads = lambda h: slice(h * HEAD_DIM, (h + 1) * HEAD_DIM)

    def score(k_ref, slot, h):
        s = _dot(k_ref[h // grp], qt_ref[heads(h), :])
        s_sc[slot, h] = s
        mb_sc[slot, h:h + 1, :] = jnp.max(s, axis=0, keepdims=True)

    def softmax_pv(h, s, m_blk, vt_g):
        m_prev = m_sc[h:h + 1, :]
        m_new = jnp.maximum(m_prev, m_blk)
        alpha = jnp.exp2(m_prev - m_new)
        p = jnp.exp2(s - m_new).astype(BF16)
        acc_sc[h] = alpha * acc_sc[h] + _dot(vt_g, p)
        m_sc[h:h + 1, :] = m_new

    def consume(vt_ref, slot, h):
        softmax_pv(h, s_sc[slot, h], mb_sc[slot, h:h + 1, :], vt_ref[h // grp])

    @pl.when(j == 0)
    def _first():
        m_sc[...] = jnp.full(m_sc.shape, -jnp.inf, F32)
        acc_sc[...] = jnp.zeros(acc_sc.shape, F32)
        meta = [_dot(km_ref[h // grp], qt_ref[heads(h), :]) for h in range(N_HEADS)]
        for h in range(N_HEADS):
            score(k0_ref, 0, h)
            softmax_pv(h, meta[h], jnp.max(meta[h], axis=0, keepdims=True), vtm_ref[h // grp])
        for h in range(N_HEADS):
            score(k1_ref, 1, h)
            consume(vtb_ref, 0, h)

    @pl.when((j > 0) & (j < last))
    def _middle():
        for h in range(N_HEADS):
            score(k0_ref, 0, h)
            consume(vta_ref, 1, h)
        for h in range(N_HEADS):
            score(k1_ref, 1, h)
            consume(vtb_ref, 0, h)

    @pl.when(j == last)
    def _last():
        for h in range(N_HEADS):
            consume(vta_ref, 1, h)
        acc = acc_sc[...]
        tq = acc.shape[2]
        out_t = (acc[:, 0:HEAD_DIM, :] / acc[:, HEAD_DIM:HEAD_DIM + 1, :]).reshape(N_HEADS * HEAD_DIM, tq)
        o_ref[...] = out_t.T.astype(BF16)


def _attention(qt, kk, vt, km, vtm, nb, tq, tk):
    aw, rows = qt.shape
    s = rows // nb
    nq = s // tq
    nkb = s // tk
    kblk = lambda f: pl.BlockSpec((N_KV_HEADS, tk, HEAD_DIM),
                                  lambda bi, qi, j: (0, bi * nkb + jnp.clip(f(j), 0, nkb - 1), 0))
    vblk = lambda f: pl.BlockSpec((N_KV_HEADS, V_ROWS, tk),
                                  lambda bi, qi, j: (0, 0, bi * nkb + jnp.clip(f(j), 0, nkb - 1)))
    return pl.pallas_call(
        _attn_kernel,
        grid=(nb, nq, nkb // 2 + 1),
        in_specs=[
            pl.BlockSpec((aw, tq), lambda bi, qi, j: (0, bi * nq + qi)),
            kblk(lambda j: 2 * j), kblk(lambda j: 2 * j + 1),
            vblk(lambda j: 2 * j - 1), vblk(lambda j: 2 * j),
            pl.BlockSpec(km.shape, lambda bi, qi, j: (0, 0, 0)),
            pl.BlockSpec(vtm.shape, lambda bi, qi, j: (0, 0, 0)),
        ],
        out_specs=pl.BlockSpec((tq, aw), lambda bi, qi, j: (bi * nq + qi, 0)),
        out_shape=jax.ShapeDtypeStruct((rows, aw), BF16),
        scratch_shapes=[
            pltpu.VMEM((N_HEADS, tq), F32),
            pltpu.VMEM((N_HEADS, V_ROWS, tq), F32),
            pltpu.VMEM((2, N_HEADS, tk, tq), F32),
            pltpu.VMEM((2, N_HEADS, tq), F32),
        ],
        compiler_params=pltpu.CompilerParams(
            dimension_semantics=("arbitrary", "arbitrary", "arbitrary"),
            vmem_limit_bytes=VMEM_LIMIT),
        name="attn",
    )(qt, kk, kk, vt, vt, km, vtm)


def _ssm_prep_kernel(*refs):
    for d in range(2):
        _ssm_prep_direction(d, *refs)


def _ssm_prep_direction(d, arow_ref, ldt_ref, bt_ref, ct_ref, m_ref, w_ref, v_ref, at_ref):
    t_chunk = SSM_CHUNK
    shift = int(math.log2(SSM_GROUP))
    df = float(d)
    dt = jnp.exp(ldt_ref[d, 0])

    def abar(ar, ai):
        mag = jnp.exp(ar * dt)
        ang = ai * dt
        return mag * jnp.cos(ang), mag * jnp.sin(ang)

    nbits = int(math.log2(t_chunk))

    def ipow(br, bi, e):
        pr = pi = None
        for k in range(nbits):
            bit = ((e >> k) & 1) == 1
            if pr is None:
                pr, pi = jnp.where(bit, br, 1.0), jnp.where(bit, bi, 0.0)
            else:
                pr, pi = jnp.where(bit, pr * br - pi * bi, pr), jnp.where(bit, pr * bi + pi * br, pi)
            br, bi = br * br - bi * bi, 2.0 * br * bi
        return pr, pi, br, bi

    ar_r = arow_ref[d, 0, 0:1, :]
    ai_r = arow_ref[d, 0, 1:2, :]
    abr, abi = abar(ar_r, ai_r)
    nr = abr - 1.0
    ni = abi
    den = ar_r * ar_r + ai_r * ai_r
    cr = (nr * ar_r + ni * ai_r) / den
    ci = (ni * ar_r - nr * ai_r) / den
    nstate = bt_ref.shape[4]
    nrow = t_chunk * SSM_GROUP

    def tile_rows(x):
        return jnp.broadcast_to(x[None], (t_chunk, SSM_GROUP, nstate)).reshape(nrow, nstate)

    btr = tile_rows(bt_ref[d, 0, 0])
    bti = tile_rows(bt_ref[d, 0, 1])
    bbr = cr * btr - ci * bti
    bbi = cr * bti + ci * btr
    jr = (lax.broadcasted_iota(jnp.int32, (nrow, 1), 0) >> shift).astype(F32)
    lj = jr + df * ((t_chunk - 1) - 2.0 * jr)
    step = lax.broadcasted_iota(jnp.int32, (t_chunk, 1), 0)
    lstep = (t_chunk - 1) - step if d == 1 else step

    def rep_rows(x):
        return jnp.broadcast_to(x[:, None, :], (t_chunk, SSM_GROUP, nstate)).reshape(nrow, nstate)

    mag2 = abr * abr + abi * abi
    enr, eni, _, _ = ipow(abr / mag2, -abi / mag2, lstep)
    enr, eni = rep_rows(enr), rep_rows(eni)
    bmr = enr * bbr - eni * bbi
    bmi = enr * bbi + eni * bbr
    ewr, ewi, atr, ati = ipow(abr, abi, (t_chunk - 1) - lstep)
    ewr, ewi = rep_rows(ewr), rep_rows(ewi)
    w_ref[d, 0] = jnp.concatenate([ewr * bbr - ewi * bbi, ewr * bbi + ewi * bbr], axis=1).astype(BF16)
    at_ref[d, 0] = jnp.concatenate([atr, ati], axis=1)

    ncol = t_chunk * SSM_GROUP
    lane = lax.broadcasted_iota(jnp.int32, (1, ncol), 1)
    tci = lane >> shift
    tile = jnp.where(lax.broadcasted_iota(jnp.int32, (SSM_GROUP, ncol), 0) == (lane & (SSM_GROUP - 1)),
                     1.0, 0.0).astype(BF16)

    def tile_cols(x):
        hi, lo = _split(x)
        lo2 = (x - hi.astype(F32) - lo.astype(F32)).astype(BF16)
        return _dot(hi, tile) + _dot(lo, tile) + _dot(lo2, tile)

    ctr = tile_cols(ct_ref[d, 0, 0])
    cti = tile_cols(ct_ref[d, 0, 1])
    tc = tci.astype(F32)
    lt = tc + df * ((t_chunk - 1) - 2.0 * tc)
    stepl = lax.broadcasted_iota(jnp.int32, (1, t_chunk), 1)
    lstepl = (t_chunk - 1) - stepl if d == 1 else stepl
    diag = (lax.broadcasted_iota(jnp.int32, (nstate, nstate), 0)
            == lax.broadcasted_iota(jnp.int32, (nstate, nstate), 1))
    abr_c = jnp.sum(jnp.where(diag, abr, 0.0), axis=1, keepdims=True)
    abi_c = jnp.sum(jnp.where(diag, abi, 0.0), axis=1, keepdims=True)
    rep = jnp.where(lax.broadcasted_iota(jnp.int32, (t_chunk, ncol), 0) == tci, 1.0, 0.0).astype(BF16)

    def rep_cols(x):
        hi, lo = _split(x)
        lo2 = (x - hi.astype(F32) - lo.astype(F32)).astype(BF16)
        return _dot(hi, rep) + _dot(lo, rep) + _dot(lo2, rep)

    ecr, eci = [rep_cols(x) for x in ipow(abr_c, abi_c, lstepl)[:2]]
    cmr = ctr * ecr - cti * eci
    cmi = ctr * eci + cti * ecr
    lhs_hi, lhs_lo = _split(jnp.concatenate([bmr, -bmi], axis=1))
    rhs_hi, rhs_lo = _split(jnp.concatenate([cmr, cmi], axis=0))
    m = _dot(lhs_hi, rhs_hi) + _dot(lhs_hi, rhs_lo) + _dot(lhs_lo, rhs_hi)
    m_ref[d, 0] = jnp.where(lj <= lt, m, 0.0).astype(BF16)
    c1r = cmr * abr_c - cmi * abi_c
    c1i = cmr * abi_c + cmi * abr_c
    v_ref[d, 0] = jnp.concatenate([c1r, -c1i], axis=0).astype(BF16)


def _ssm_prep(arow, ldt, bt, ct):
    nd, g = arow.shape[0], arow.shape[1]
    p = SSM_STATE
    n = SSM_CHUNK * SSM_GROUP
    blk = lambda a: pl.BlockSpec((nd, 1) + a.shape[2:], lambda gi: (0, gi) + (0,) * (a.ndim - 2))
    return pl.pallas_call(
        _ssm_prep_kernel,
        grid=(g,),
        in_specs=[blk(arow), blk(ldt), blk(bt), blk(ct)],
        out_specs=[
            pl.BlockSpec((nd, 1, n, n), lambda gi: (0, gi, 0, 0)),
            pl.BlockSpec((nd, 1, n, 2 * p), lambda gi: (0, gi, 0, 0)),
            pl.BlockSpec((nd, 1, 2 * p, n), lambda gi: (0, gi, 0, 0)),
            pl.BlockSpec((nd, 1, 1, 2 * p), lambda gi: (0, gi, 0, 0)),
        ],
        out_shape=[
            jax.ShapeDtypeStruct((nd, g, n, n), BF16),
            jax.ShapeDtypeStruct((nd, g, n, 2 * p), BF16),
            jax.ShapeDtypeStruct((nd, g, 2 * p, n), BF16),
            jax.ShapeDtypeStruct((nd, g, 1, 2 * p), F32),
        ],
        compiler_params=pltpu.CompilerParams(dimension_semantics=("arbitrary",)),
        name="ssm_prep",
    )(arow, ldt, bt, ct)


def _chunk_carry(chains):
    p = SSM_STATE
    z0, a0 = chains[0][0], chains[0][1]
    nc = z0.shape[0]
    row = lax.broadcasted_iota(jnp.int32, z0.shape, 0)
    is_re = lax.broadcasted_iota(jnp.int32, a0.shape, 1) < p
    sign = jnp.where(is_re, -1.0, 1.0)

    def parts(ap):
        sw = pltpu.roll(ap, p, 1)
        return jnp.where(is_re, ap, sw), sign * jnp.where(is_re, sw, ap)

    def cmul(x, ar_full, ai_sgn):
        return x * ar_full + pltpu.roll(x, p, 1) * ai_sgn

    coef = [parts(a) for _, a, _, _ in chains]
    firsts = [(nc - 1) if rev else 0 for _, _, _, rev in chains]
    es = [z + jnp.where(row == f, cmul(s0, *c), 0.0) for (z, _, s0, _), f, c in zip(chains, firsts, coef)]
    k = 1
    while k < nc:
        for i, (_, _, _, rev) in enumerate(chains):
            if rev:
                sh = jnp.where(row < nc - k, pltpu.roll(es[i], nc - k, 0), 0.0)
            else:
                sh = jnp.where(row >= k, pltpu.roll(es[i], k, 0), 0.0)
            es[i] = es[i] + cmul(sh, *coef[i])
        k *= 2
        if k < nc:
            coef = [parts(cmul(jnp.where(is_re, ar, sign * ai), ar, ai)) for ar, ai in coef]
    outs = []
    for e, (_, _, s0, rev), f in zip(es, chains, firsts):
        outs.append(jnp.where(row == f, s0, pltpu.roll(e, nc - 1 if rev else 1, 0)))
    return outs


def _ssm_kernel(u_ref, um_ref, m_ref, w_ref, v_ref, at_ref, dvec_ref, y_ref, ug_sc, yg_sc):
    p = SSM_STATE
    t_chunk = SSM_CHUNK
    hgrp = SSM_GROUP
    gb = u_ref.shape[2] // hgrp
    per_tile = u_ref.shape[2] // hgrp
    nc = u_ref.shape[1] // t_chunk
    lane_blk = lax.broadcasted_iota(jnp.int32, (nc, u_ref.shape[2]), 1) >> int(math.log2(hgrp))

    width = u_ref.shape[2]

    def block_transpose(arrs):
        arrs = list(arrs)
        d = len(arrs) // 2
        while d >= 1:
            low_bit = (lane_blk & d) == 0
            for i in range(len(arrs)):
                if i & d == 0:
                    lo, hi = arrs[i], arrs[i + d]
                    arrs[i] = jnp.where(low_bit, lo, pltpu.roll(hi, d * hgrp, 1))
                    arrs[i + d] = jnp.where(low_bit, pltpu.roll(lo, width - d * hgrp, 1), hi)
            d //= 2
        return arrs

    for q in range(t_chunk // per_tile):
        steps = [u_ref[0, pl.ds(q * per_tile + r, nc, stride=t_chunk), :] for r in range(per_tile)]
        for g, tile in enumerate(block_transpose(steps)):
            ug_sc[g, :, q * width:(q + 1) * width] = tile

    def groups(i, carry):
        gs = [i * SSM_GROUPS_TOGETHER + k for k in range(SSM_GROUPS_TOGETHER)]
        us = [ug_sc[g] for g in gs]
        ubs = [u.astype(BF16) for u in us]
        yin = [[_dot(ub, m_ref[d, g]) for d in range(2)] for g, ub in zip(gs, ubs)]
        chains = []
        for g, ub in zip(gs, ubs):
            for d in range(2):
                if d == 0:
                    s0 = _dot(um_ref[g].astype(BF16), w_ref[0, g])[0:1, :]
                else:
                    s0 = jnp.zeros((1, 2 * p), F32)
                chains.append((_dot(ub, w_ref[d, g]), at_ref[d, g], s0, d == 1))
        s_in = _chunk_carry(chains)
        for k, g in enumerate(gs):
            y = dvec_ref[g] * us[k] + yin[k][0] + yin[k][1]
            for d in range(2):
                y = y + _dot(s_in[2 * k + d].astype(BF16), v_ref[d, g])
            yg_sc[g] = y
        return carry

    lax.fori_loop(0, gb // SSM_GROUPS_TOGETHER, groups, 0)

    for q in range(t_chunk // per_tile):
        srcs = [yg_sc[g, :, q * width:(q + 1) * width] for g in range(gb)]
        for r, row in enumerate(block_transpose(srcs)):
            y_ref[0, pl.ds(q * per_tile + r, nc, stride=t_chunk), :] = row


def _ssm(u3, um2, m, w, v, at, dvec):
    b, s, sw = u3.shape
    g = sw // SSM_GROUP
    n = SSM_CHUNK * SSM_GROUP
    p = SSM_STATE
    lanes = 128
    gb = lanes // SSM_GROUP
    return pl.pallas_call(
        _ssm_kernel,
        grid=(b, sw // lanes),
        in_specs=[
            pl.BlockSpec((1, s, lanes), lambda bi, gi: (bi, 0, gi)),
            pl.BlockSpec((gb, 8, n), lambda bi, gi: (gi, 0, 0)),
            pl.BlockSpec((2, gb, n, n), lambda bi, gi: (0, gi, 0, 0)),
            pl.BlockSpec((2, gb, n, 2 * p), lambda bi, gi: (0, gi, 0, 0)),
            pl.BlockSpec((2, gb, 2 * p, n), lambda bi, gi: (0, gi, 0, 0)),
            pl.BlockSpec((2, gb, 1, 2 * p), lambda bi, gi: (0, gi, 0, 0)),
            pl.BlockSpec((gb, 1, n), lambda bi, gi: (gi, 0, 0)),
        ],
        out_specs=pl.BlockSpec((1, s, lanes), lambda bi, gi: (bi, 0, gi)),
        out_shape=jax.ShapeDtypeStruct((b, s, sw), F32),
        scratch_shapes=[pltpu.VMEM((gb, s // SSM_CHUNK, n), F32), pltpu.VMEM((gb, s // SSM_CHUNK, n), F32)],
        compiler_params=pltpu.CompilerParams(dimension_semantics=("arbitrary", "arbitrary"),
                                             vmem_limit_bytes=VMEM_LIMIT),
        name="ssm",
    )(u3, um2, m, w, v, at, dvec)


def _merge_kernel(x_ref, ya_ref, ys_ref, wgt_ref, wglu_ref, bglu_ref, wab_ref, wsb_ref, wo_ref,
                  lng_ref, lnb_ref, l1g_ref, l1b_ref, wr_ref,
                  h1_ref, tok_ref, afft_ref):
    d = x_ref.shape[1]
    h = _ln(x_ref[...], lng_ref[...], lnb_ref[...])
    gates = _dot(h.astype(BF16), wgt_ref[...])
    ga = _sigmoid(gates[:, 0:d])
    gs = _sigmoid(gates[:, d:2 * d])
    ys = ys_ref[...]
    y = 0.5 * ys * (1.0 + jnp.tanh(math.sqrt(2.0 / math.pi) * (ys + 0.044715 * (ys * ys * ys))))
    yg = y * _sigmoid(_dot(y.astype(BF16), wglu_ref[...]) + bglu_ref[...])
    merged = ga * _dot(ya_ref[...], wab_ref[...]) + gs * _dot(yg.astype(BF16), wsb_ref[...])
    h1 = _ln(DEEPNORM_ALPHA * h + _dot(merged.astype(BF16), wo_ref[...]), l1g_ref[...], l1b_ref[...])
    h1_ref[...] = h1
    hi, lo = _split(h1)
    ne = afft_ref.shape[0]
    cross = _dot(hi, wr_ref[...])
    logits = cross[:, 0:GATE_LANES] + cross[:, GATE_LANES:] + _dot(lo, wr_ref[:, 0:GATE_LANES])
    lane = lax.broadcasted_iota(jnp.int32, logits.shape, 1)
    logits = jnp.where(lane < 3 * ne, logits, -jnp.inf)
    ex = jnp.exp(logits - jnp.max(logits, axis=1, keepdims=True))
    aff = ex / jnp.sum(jnp.where(lane < ne, ex, 0.0), axis=1, keepdims=True)
    p0 = aff.astype(BF16)
    r1 = aff - p0.astype(F32)
    p1 = r1.astype(BF16)
    p2 = (r1 - p1.astype(F32)).astype(BF16)
    pieces = jnp.where(lane < ne, p0, jnp.where(lane < 2 * ne, p1, p2))
    tok_ref[...] = jnp.concatenate([hi, pieces], axis=1)
    afft_ref[...] = aff.T[0:ne, :]


def _merge(x2, ya, ys, wgt, wglu, bglu, wab, wsb, wo, lng, lnb, l1g, l1b, wr, ne, tm):
    rows, d = x2.shape
    full = lambda a: pl.BlockSpec(a.shape, lambda i: (0,) * a.ndim)
    return pl.pallas_call(
        _merge_kernel,
        grid=(rows // tm,),
        in_specs=[
            pl.BlockSpec((tm, d), lambda i: (i, 0)),
            pl.BlockSpec((tm, ya.shape[1]), lambda i: (i, 0)),
            pl.BlockSpec((tm, ys.shape[1]), lambda i: (i, 0)),
            full(wgt), full(wglu), full(bglu), full(wab), full(wsb), full(wo),
            full(lng), full(lnb), full(l1g), full(l1b), full(wr),
        ],
        out_specs=[
            pl.BlockSpec((tm, d), lambda i: (i, 0)),
            pl.BlockSpec((tm, d + GATE_LANES), lambda i: (i, 0)),
            pl.BlockSpec((ne, tm), lambda i: (0, i)),
        ],
        out_shape=[
            jax.ShapeDtypeStruct((rows, d), F32),
            jax.ShapeDtypeStruct((rows, d + GATE_LANES), BF16),
            jax.ShapeDtypeStruct((ne, rows), F32),
        ],
        compiler_params=pltpu.CompilerParams(dimension_semantics=("arbitrary",),
                                             vmem_limit_bytes=VMEM_LIMIT),
        name="merge",
    )(x2, ya, ys, wgt, wglu, bglu, wab, wsb, wo, lng, lnb, l1g, l1b, wr)


def _route_kernel(afft_ref, tri_ref, scl_ref, lo_ref, *, cap):
    aff = afft_ref[...]
    ne, s = aff.shape
    capf = float(cap)

    def as_float(bits):
        return lax.bitcast_convert_type(bits, F32)

    def search(i, t):
        cand = t | jnp.left_shift(jnp.int32(1), 30 - i)
        cnt = jnp.sum(jnp.where(aff >= as_float(cand), 1.0, 0.0), axis=1, keepdims=True)
        return jnp.where(cnt >= capf, cand, t)

    thr_bits = lax.fori_loop(0, 31, search, jnp.zeros((ne, 1), jnp.int32))
    gt = aff >= as_float(thr_bits + 1)
    eq = (aff >= as_float(thr_bits)) & jnp.logical_not(gt)
    need = capf - jnp.sum(jnp.where(gt, 1.0, 0.0), axis=1, keepdims=True)
    tri = tri_ref[...]
    rt = ROUTE_TILE
    nt = s // rt
    col = lax.broadcasted_iota(jnp.int32, (ne, nt), 1)
    carry_eq = jnp.zeros((ne, 1), F32)
    carry_sel = jnp.zeros((ne, 1), F32)
    lo_val = jnp.zeros((ne, nt), F32)
    for t in range(nt):
        sl = slice(t * rt, (t + 1) * rt)
        eq_b = eq[:, sl]
        ceq = _dot(jnp.where(eq_b, 1.0, 0.0).astype(BF16), tri)
        sel_b = gt[:, sl] | (eq_b & ((ceq + carry_eq) <= need))
        carry_eq = carry_eq + ceq[:, rt - 1:rt]
        csel = _dot(jnp.where(sel_b, 1.0, 0.0).astype(BF16), tri)
        scl_ref[0, :, sl] = jnp.where(sel_b, csel, 0.0)
        lo_val = jnp.where(col == t, carry_sel, lo_val)
        carry_sel = carry_sel + csel[:, rt - 1:rt]
    lo_ref[0] = lo_val


def _route(afft, tri, nb, cap):
    ne, rows = afft.shape
    s = rows // nb
    nt = s // ROUTE_TILE
    return pl.pallas_call(
        functools.partial(_route_kernel, cap=cap),
        grid=(nb,),
        in_specs=[
            pl.BlockSpec((ne, s), lambda b: (0, b)),
            pl.BlockSpec(tri.shape, lambda b: (0, 0)),
        ],
        out_specs=[
            pl.BlockSpec((1, ne, s), lambda b: (b, 0, 0)),
            pl.BlockSpec((1, ne, nt), lambda b: (b, 0, 0)),
        ],
        out_shape=[
            jax.ShapeDtypeStruct((nb, ne, s), F32),
            jax.ShapeDtypeStruct((nb, ne, nt), F32),
        ],
        compiler_params=pltpu.CompilerParams(dimension_semantics=("arbitrary",)),
        name="route",
    )(afft, tri)


def _ffn_kernel(lo_ref, t_ref, scl_ref, wg_ref, wu_ref, wd_ref, ye_ref, xe_sc, xb_sc, gate_sc, acc_sc,
                *, cap, flag_off):
    fc = pl.program_id(2)

    @pl.when(fc == 0)
    def _gather():
        _ffn_gather(lo_ref, t_ref, scl_ref, xe_sc, xb_sc, gate_sc, cap=cap, flag_off=flag_off)
        acc_sc[...] = jnp.zeros(acc_sc.shape, F32)

    xb = xb_sc[...]
    g = _dot(xb, wg_ref[0, 0].astype(BF16))
    u = _dot(xb, wu_ref[0, 0].astype(BF16))
    hh = (g * _sigmoid(g)) * u
    acc_sc[...] += _dot(hh.astype(BF16), wd_ref[0, 0].astype(BF16))

    @pl.when(fc == pl.num_programs(2) - 1)
    def _emit():
        ye_ref[0, 0] = (acc_sc[...] * gate_sc[...]).astype(BF16)


def _ffn_gather(lo_ref, t_ref, scl_ref, xe_sc, xb_sc, gate_sc, *, cap, flag_off):
    b = pl.program_id(0)
    e = pl.program_id(1)
    ne = pl.num_programs(1)
    rt = ROUTE_TILE
    nt = t_ref.shape[1] // rt
    xe_sc[...] = jnp.zeros(xe_sc.shape, F32)
    r = lax.broadcasted_iota(jnp.int32, (PIECE, rt), 0).astype(F32)

    def tile_info(tau):
        base = (b * ne + e) * (nt + 1) + tau
        lo = lo_ref[base]
        n_pieces = jnp.right_shift((lo & 7) + (lo_ref[base + 1] - lo) + (PIECE - 1), PIECE_SHIFT)
        return lo, n_pieces

    def add_piece(tau, lo, p):
        off = lo & 7
        scl = scl_ref[0, 0, pl.ds(tau, 1), :]
        tok = t_ref[0, pl.ds(pl.multiple_of(tau * rt, rt), rt), :]
        shift = (off - 1 - p * PIECE).astype(F32)
        onehot = jnp.where((scl > 0.0) & ((scl + shift) == r), 1.0, 0.0).astype(BF16)
        xe_sc[pl.ds(pl.multiple_of(lo - off + p * PIECE, 8), PIECE), :] += _dot(onehot, tok)

    def first_piece(tau, carry):
        add_piece(tau, tile_info(tau)[0], 0)
        return carry

    def more_pieces(tau, carry):
        lo, n_pieces = tile_info(tau)
        lax.fori_loop(1, n_pieces, lambda p, c: (add_piece(tau, lo, p), c)[1], 0)
        return carry

    lax.fori_loop(0, nt, first_piece, 0, unroll=4)

    @pl.when(lo_ref[flag_off + b * ne + e] > 0)
    def _rare():
        lax.fori_loop(0, nt, more_pieces, 0)

    d = xb_sc.shape[1]
    xb_sc[...] = xe_sc[0:cap, 0:d].astype(BF16)
    gl = xe_sc[0:cap, d:d + GATE_LANES]
    lane = lax.broadcasted_iota(jnp.int32, gl.shape, 1)
    mine = ((lane & (N_EXPERTS - 1)) == e) & (lane < 3 * N_EXPERTS)
    gate_sc[...] = jnp.sum(jnp.where(mine, gl, 0.0), axis=1, keepdims=True)


def _ffn(lo_i, t, scl4, wg, wu, wd, layer, cap, f_chunk):
    b, s, dx = t.shape
    _, ne, d, f = wg.shape
    nt = s // ROUTE_TILE
    grid_spec = pltpu.PrefetchScalarGridSpec(
        num_scalar_prefetch=1,
        grid=(b, ne, f // f_chunk),
        in_specs=[
            pl.BlockSpec((1, s, dx), lambda bi, ei, fi, lo: (bi, 0, 0), pipeline_mode=pl.Buffered(1)),
            pl.BlockSpec((1, 1, nt, ROUTE_TILE), lambda bi, ei, fi, lo: (bi, ei, 0, 0)),
            pl.BlockSpec((1, 1, d, f_chunk), lambda bi, ei, fi, lo: (layer, ei, 0, fi)),
            pl.BlockSpec((1, 1, d, f_chunk), lambda bi, ei, fi, lo: (layer, ei, 0, fi)),
            pl.BlockSpec((1, 1, f_chunk, d), lambda bi, ei, fi, lo: (layer, ei, fi, 0)),
        ],
        out_specs=pl.BlockSpec((1, 1, cap, d), lambda bi, ei, fi, lo: (bi, ei, 0, 0)),
        scratch_shapes=[pltpu.VMEM((cap + PIECE, dx), F32), pltpu.VMEM((cap, d), BF16),
                        pltpu.VMEM((cap, 1), F32), pltpu.VMEM((cap, d), F32)],
    )
    return pl.pallas_call(
        functools.partial(_ffn_kernel, cap=cap, flag_off=b * ne * (nt + 1)),
        grid_spec=grid_spec,
        out_shape=jax.ShapeDtypeStruct((b, ne, cap, d), BF16),
        compiler_params=pltpu.CompilerParams(dimension_semantics=("arbitrary", "arbitrary", "arbitrary"),
                                             vmem_limit_bytes=VMEM_LIMIT),
        name="ffn",
    )(lo_i, t, scl4, wg, wu, wd)


def _combine_kernel(lo_ref, h1_ref, sclt_ref, ye_hbm, g_ref, b_ref, o_ref, win, xwin, acc_sc, sem, xsem,
                    *, cap, flag_off):
    b = pl.program_id(0)
    tau = pl.program_id(1)
    nb = pl.num_programs(0)
    nt = pl.num_programs(1)
    ne = win.shape[1]
    d = win.shape[3]
    step = b * nt + tau
    slot = step & 1

    def pieces(bi, ti, e):
        base = (bi * ne + e) * (nt + 1) + ti
        lo = lo_ref[base]
        n_sel = lo_ref[base + 1] - lo
        n_pieces = jnp.maximum(jnp.right_shift((lo & 15) + n_sel + (PIECE - 1), PIECE_SHIFT), 1)
        first = jnp.minimum(lo - (lo & 15), cap - n_pieces * PIECE)
        return lo, first, n_pieces

    def copy(bi, e, first, p, buf, s):
        start = pl.multiple_of(first + p * PIECE, 16)
        return pltpu.make_async_copy(ye_hbm.at[bi, e, pl.ds(start, PIECE), :], buf, s)

    def first_pieces(bi, ti, sl):
        return [copy(bi, e, pieces(bi, ti, e)[1], 0, win.at[sl, e], sem.at[sl, e]) for e in range(ne)]

    @pl.when(step == 0)
    def _prime():
        for cp in first_pieces(b, tau, slot):
            cp.start()

    @pl.when(step + 1 < nb * nt)
    def _prefetch():
        wrap = tau + 1 == nt
        for cp in first_pieces(jnp.where(wrap, b + 1, b), jnp.where(wrap, 0, tau + 1), 1 - slot):
            cp.start()

    for cp in first_pieces(b, tau, slot):
        cp.wait()

    sclt = sclt_ref[0]
    col = lax.broadcasted_iota(jnp.int32, (1, ne * PIECE), 1)
    grp = jnp.right_shift(col, PIECE_SHIFT)
    expand = jnp.where(lax.broadcasted_iota(jnp.int32, (ne, ne * PIECE), 0) == grp, 1.0, 0.0).astype(BF16)
    scl = _dot(sclt.astype(BF16), expand)
    shift = jnp.zeros(col.shape, F32)
    for e in range(ne):
        lo, first, _ = pieces(b, tau, e)
        shift = jnp.where(grp == e, (lo - first - 1).astype(F32), shift)
    r = (col & (PIECE - 1)).astype(F32)
    onehot = jnp.where((scl > 0.0) & ((scl + shift) == r), 1.0, 0.0).astype(BF16)
    acc_sc[...] = DEEPNORM_ALPHA * h1_ref[0] + _dot(onehot, win[slot].reshape(ne * PIECE, d))

    @pl.when(lo_ref[flag_off + step] > 0)
    def _rare():
        for e in range(ne):
            lo, first, n_pieces = pieces(b, tau, e)

            def extra(p, c, e=e, lo=lo, first=first):
                cp = copy(b, e, first, p, xwin, xsem.at[0])
                cp.start()
                cp.wait()
                se = sclt_ref[0][:, e:e + 1]
                rr = lax.broadcasted_iota(jnp.int32, (se.shape[0], PIECE), 1).astype(F32)
                oh = jnp.where((se > 0.0) & ((se + (lo - first - p * PIECE - 1).astype(F32)) == rr), 1.0, 0.0)
                acc_sc[...] += _dot(oh.astype(BF16), xwin[...])
                return c

            lax.fori_loop(1, n_pieces, extra, 0)

    o_ref[0] = _ln(acc_sc[...], g_ref[...], b_ref[...])


def _combine(lo_i, h1, sclt, ye, g, bb, cap):
    b, s, d = h1.shape
    ne = sclt.shape[2]
    rt = ROUTE_TILE
    grid_spec = pltpu.PrefetchScalarGridSpec(
        num_scalar_prefetch=1,
        grid=(b, s // rt),
        in_specs=[
            pl.BlockSpec((1, rt, d), lambda bi, ti, lo: (bi, ti, 0)),
            pl.BlockSpec((1, rt, ne), lambda bi, ti, lo: (bi, ti, 0)),
            pl.BlockSpec(memory_space=pl.ANY),
            pl.BlockSpec(g.shape, lambda bi, ti, lo: (0, 0)),
            pl.BlockSpec(bb.shape, lambda bi, ti, lo: (0, 0)),
        ],
        out_specs=pl.BlockSpec((1, rt, d), lambda bi, ti, lo: (bi, ti, 0)),
        scratch_shapes=[pltpu.VMEM((2, ne, PIECE, d), BF16), pltpu.VMEM((PIECE, d), BF16),
                        pltpu.VMEM((rt, d), F32),
                        pltpu.SemaphoreType.DMA((2, ne)), pltpu.SemaphoreType.DMA((1,))],
    )
    return pl.pallas_call(
        functools.partial(_combine_kernel, cap=cap, flag_off=b * ne * (s // rt + 1) + b * ne),
        grid_spec=grid_spec,
        out_shape=jax.ShapeDtypeStruct((b, s, d), F32),
        compiler_params=pltpu.CompilerParams(dimension_semantics=("arbitrary", "arbitrary"),
                                             vmem_limit_bytes=VMEM_LIMIT),
        name="combine",
    )(lo_i, h1, sclt, ye, g, bb)


def _rope_tables(n_tokens):
    half = HEAD_DIM // 2
    inv_freq = ROPE_THETA ** (-jnp.arange(0, half, 2, dtype=F32) / half)
    rows = n_tokens // GRID_W
    row = jnp.repeat(jnp.arange(rows, dtype=F32), GRID_W)
    colv = jnp.tile(jnp.arange(GRID_W, dtype=F32), rows)
    ang = jnp.concatenate([row[:, None] * inv_freq, colv[:, None] * inv_freq], axis=-1)
    return jnp.cos(ang), jnp.sin(ang)


def kernel(x, meta_tokens, ln_in_g, ln_in_b, w_in, q_norm_g, k_norm_g, ssm_a_re, ssm_a_im, ssm_log_dt,
           ssm_b_re, ssm_b_im, ssm_c_re, ssm_c_im, ssm_d, w_glu, b_glu, w_attn_br, w_ssm_br, w_o,
           ln1_g, ln1_b, w_router, w_gate_e, w_up_e, w_down_e, ln2_g, ln2_b):
    b, s, d = x.shape
    aw = N_HEADS * HEAD_DIM
    kw = N_KV_HEADS * HEAD_DIM
    sw = d // 2
    g = sw // SSM_GROUP
    half = HEAD_DIM // 2
    cap = CAPACITY_FACTOR * s // N_EXPERTS
    rows = b * s
    l = 0

    perm = np.concatenate([np.arange(0, HEAD_DIM, 2), np.arange(1, HEAD_DIM, 2)])
    qcols = np.concatenate([h * HEAD_DIM + perm for h in range(N_HEADS)])
    kcols = aw + np.concatenate([h * HEAD_DIM + perm for h in range(N_KV_HEADS)])
    wl = w_in[l]
    w_t = jnp.concatenate([wl[:, qcols], wl[:, aw + kw:aw + 2 * kw]], axis=1).T.astype(BF16)
    w_n = jnp.concatenate([wl[:, kcols], wl[:, aw + 2 * kw:aw + 2 * kw + sw]], axis=1).astype(BF16)
    w_gates = wl[:, aw + 2 * kw + sw:].astype(BF16)
    qg3 = (jnp.tile(q_norm_g[l][perm], N_HEADS) * (HEAD_DIM ** -0.5 * math.log2(math.e))).reshape(N_HEADS, HEAD_DIM, 1)
    kg = jnp.tile(k_norm_g[l][perm], N_KV_HEADS)[None, :]
    bd = jnp.asarray(np.kron(np.eye(N_KV_HEADS), np.full((HEAD_DIM, HEAD_DIM), 1.0 / HEAD_DIM)), BF16)
    lng = ln_in_g[None, :]
    lnb = ln_in_b[None, :]
    cos, sin = _rope_tables(s)
    cosr = jnp.tile(cos, (1, 2 * N_KV_HEADS))
    sinr = jnp.tile(jnp.concatenate([-sin, sin], axis=1), (1, N_KV_HEADS))
    cost = cos.T
    sint = sin.T

    x2 = x.reshape(rows, d)
    tm_in = 1024
    qt, kk, vt, u2 = _inproj(x2, cosr, sinr, cost, sint, w_t, w_n, lng, lnb, qg3, kg, bd, tm_in, s // tm_in)
    meta_p = jnp.pad(meta_tokens, ((0, META_PAD - N_META), (0, 0)))
    ones_r = jnp.ones((META_PAD, kw), F32)
    _, km, vtm, um = _inproj(meta_p, ones_r, jnp.zeros_like(ones_r), jnp.ones((half, META_PAD), F32),
                             jnp.zeros((half, META_PAD), F32), w_t, w_n, lng, lnb, qg3, kg, bd, META_PAD, 1)
    um = um[:N_META]
    km = km[:, :N_META, :]
    vtm = vtm[:, :, :N_META]

    ya = _attention(qt, kk, vt, km, vtm, b, 512, 512)

    t_chunk = SSM_CHUNK
    hgrp = SSM_GROUP
    n = t_chunk * hgrp
    arow = jnp.stack([ssm_a_re[l], ssm_a_im[l]], axis=2)
    ldt = ssm_log_dt[l][:, :, None, None]
    bt = jnp.stack([ssm_b_re[l], ssm_b_im[l]], axis=2)
    bt = jnp.swapaxes(bt, 3, 4)
    ct = jnp.stack([ssm_c_re[l], ssm_c_im[l]], axis=2)
    ct = jnp.swapaxes(ct, 3, 4)
    mm, ww, vv, at = _ssm_prep(arow, ldt, bt, ct)
    nc = s // t_chunk
    um_g = um.reshape(N_META, g, hgrp).transpose(1, 0, 2).reshape(g, 1, N_META * hgrp)
    um_g = jnp.pad(um_g, ((0, 0), (0, 7), (n - N_META * hgrp, 0)))
    dvec = jnp.tile(ssm_d[l], (1, t_chunk))[:, None, :]
    ys = _ssm(u2.reshape(b, s, sw), um_g, mm, ww, vv, at, dvec).reshape(rows, sw)

    wr_hi = w_router[l].astype(BF16)
    wr_lo = (w_router[l] - wr_hi.astype(F32)).astype(BF16)
    lane_pad = ((0, 0), (0, GATE_LANES - 3 * N_EXPERTS))
    wr = jnp.concatenate([jnp.pad(jnp.tile(wr_hi, (1, 3)), lane_pad), jnp.pad(jnp.tile(wr_lo, (1, 3)), lane_pad)],
                         axis=1)
    h1, tok, afft = _merge(
        x2, ya, ys, w_gates, w_glu[l].astype(BF16), b_glu[l][None, :],
        w_attn_br[l].astype(BF16), w_ssm_br[l].astype(BF16), w_o[l].astype(BF16),
        lng, lnb, ln1_g[l][None, :], ln1_b[l][None, :], wr, N_EXPERTS, 512)

    tri = jnp.asarray(np.triu(np.ones((ROUTE_TILE, ROUTE_TILE), np.float32)), BF16)
    scl, lo_f = _route(afft, tri, b, cap)
    nt = s // ROUTE_TILE
    lo3 = jnp.pad(lo_f.astype(jnp.int32), ((0, 0), (0, 0), (0, 1)), constant_values=cap)
    seg = lo3[:, :, 1:] - lo3[:, :, :-1]
    multi = lambda align: ((lo3[:, :, :-1] & (align - 1)) + seg + (PIECE - 1)) // PIECE > 1
    lo_i = jnp.concatenate([lo3.reshape(-1),
                            jnp.any(multi(8), axis=2).astype(jnp.int32).reshape(-1),
                            jnp.any(multi(16), axis=1).astype(jnp.int32).reshape(-1)])
    ye = _ffn(lo_i, tok.reshape(b, s, d + GATE_LANES), scl.reshape(b, N_EXPERTS, nt, ROUTE_TILE),
              w_gate_e, w_up_e, w_down_e, l, cap, 512)
    out = _combine(lo_i, h1.reshape(b, s, d), jnp.swapaxes(scl, 1, 2), ye,
                   ln2_g[l][None, :], ln2_b[l][None, :], cap)
    return out
```

```python
import functools
import math

import numpy as np
import jax
import jax.numpy as jnp
from jax import lax
from jax.experimental import pallas as pl
from jax.experimental.pallas import tpu as pltpu

F32 = jnp.float32
BF16 = jnp.bfloat16

N_META = 16
GRID_W = 64
N_HEADS = 8
N_KV_HEADS = 2
HEAD_DIM = 64
ROPE_THETA = 10000.0
SSM_GROUP = 16
SSM_STATE = 64
N_EXPERTS = 16
CAPACITY_FACTOR = 2
LN_EPS = 1e-5
QK_EPS = 1e-6
DEPTH = 1
DEEPNORM_ALPHA = (2.0 * DEPTH) ** 0.25

SSM_CHUNK = 32
SSM_GROUPS_TOGETHER = 4
ROUTE_TILE = 256
ROUTER_LANES = 128
PIECE_SHIFT = 6
PIECE = 1 << PIECE_SHIFT
META_PAD = 128
V_ROWS = HEAD_DIM + 16
VMEM_LIMIT = 56 * 1024 * 1024


def _ln(x, g, b):
    mu = jnp.mean(x, axis=-1, keepdims=True)
    xc = x - mu
    var = jnp.mean(xc * xc, axis=-1, keepdims=True)
    return xc * lax.rsqrt(var + LN_EPS) * g + b


def _sigmoid(x):
    return 1.0 / (1.0 + jnp.exp(-x))


def _split(t):
    hi = t.astype(BF16)
    lo = (t - hi.astype(F32)).astype(BF16)
    return hi, lo


def _dot(a, b):
    return jnp.dot(a, b, preferred_element_type=F32)


def _dot_nt(a, b):
    return lax.dot_general(a, b, (((1,), (1,)), ((), ())), preferred_element_type=F32)


def _inproj_kernel(x_ref, cosr_ref, sinr_ref, cost_ref, sint_ref, wt_ref, wn_ref, lng_ref, lnb_ref,
                   qg_ref, kg_ref, bd_ref, qt_ref, k_ref, vt_ref, u_ref):
    aw = N_HEADS * HEAD_DIM
    kw = N_KV_HEADS * HEAD_DIM
    half = HEAD_DIM // 2
    h = _ln(x_ref[...], lng_ref[...], lnb_ref[...])
    hb = h.astype(BF16)
    pt = _dot_nt(wt_ref[...], hb)
    pn = _dot(hb, wn_ref[...])
    tm = hb.shape[0]

    qt = pt[0:aw].reshape(N_HEADS, HEAD_DIM, tm)
    ms = jnp.mean(qt * qt, axis=1, keepdims=True)
    qn = qt * lax.rsqrt(ms + QK_EPS) * qg_ref[...]
    x0 = qn[:, 0:half, :]
    x1 = qn[:, half:, :]
    c = cost_ref[...][None]
    s = sint_ref[...][None]
    qr = jnp.concatenate([x0 * c - x1 * s, x0 * s + x1 * c], axis=1)
    qt_ref[...] = qr.reshape(aw, tm).astype(BF16)
    vrow = lax.broadcasted_iota(jnp.int32, (N_KV_HEADS, V_ROWS - HEAD_DIM, tm), 1)
    vt_ref[...] = jnp.concatenate([pt[aw:aw + kw].reshape(N_KV_HEADS, HEAD_DIM, tm),
                                   jnp.where(vrow == 0, 1.0, 0.0)], axis=1).astype(BF16)

    kk = pn[:, 0:kw]
    hi, lo = _split(kk * kk)
    bd = bd_ref[...]
    msk = _dot(hi, bd) + _dot(lo, bd)
    kn = kk * lax.rsqrt(msk + QK_EPS) * kg_ref[...]
    lane = lax.broadcasted_iota(jnp.int32, kn.shape, 1)
    first = (lane & (HEAD_DIM - 1)) < half
    partner = jnp.where(first, pltpu.roll(kn, kw - half, 1), pltpu.roll(kn, half, 1))
    kr = (kn * cosr_ref[...] + partner * sinr_ref[...]).astype(BF16)
    for g in range(N_KV_HEADS):
        k_ref[g] = kr[:, g * HEAD_DIM:(g + 1) * HEAD_DIM]
    u_ref[...] = pn[:, kw:]


def _inproj(x2, cosr, sinr, cost, sint, wt, wn, lng, lnb, qg3, kg, bd, tm, n_tab_blocks):
    rows, d = x2.shape
    aw = N_HEADS * HEAD_DIM
    kw = N_KV_HEADS * HEAD_DIM
    uw = wn.shape[1] - kw
    half = HEAD_DIM // 2
    full = lambda a: pl.BlockSpec(a.shape, lambda i: (0,) * a.ndim)
    return pl.pallas_call(
        _inproj_kernel,
        grid=(rows // tm,),
        in_specs=[
            pl.BlockSpec((tm, d), lambda i: (i, 0)),
            pl.BlockSpec((tm, kw), lambda i: (i % n_tab_blocks, 0)),
            pl.BlockSpec((tm, kw), lambda i: (i % n_tab_blocks, 0)),
            pl.BlockSpec((half, tm), lambda i: (0, i % n_tab_blocks)),
            pl.BlockSpec((half, tm), lambda i: (0, i % n_tab_blocks)),
            full(wt), full(wn), full(lng), full(lnb), full(qg3), full(kg), full(bd),
        ],
        out_specs=[
            pl.BlockSpec((aw, tm), lambda i: (0, i)),
            pl.BlockSpec((N_KV_HEADS, tm, HEAD_DIM), lambda i: (0, i, 0)),
            pl.BlockSpec((N_KV_HEADS, V_ROWS, tm), lambda i: (0, 0, i)),
            pl.BlockSpec((tm, uw), lambda i: (i, 0)),
        ],
        out_shape=[
            jax.ShapeDtypeStruct((aw, rows), BF16),
            jax.ShapeDtypeStruct((N_KV_HEADS, rows, HEAD_DIM), BF16),
            jax.ShapeDtypeStruct((N_KV_HEADS, V_ROWS, rows), BF16),
            jax.ShapeDtypeStruct((rows, uw), F32),
        ],
        compiler_params=pltpu.CompilerParams(dimension_semantics=("arbitrary",),
                                             vmem_limit_bytes=VMEM_LIMIT),
        name="inproj",
    )(x2, cosr, sinr, cost, sint, wt, wn, lng, lnb, qg3, kg, bd)


def _attn_kernel(qt_ref, k0_ref, k1_ref, vta_ref, vtb_ref, km_ref, vtm_ref, o_ref, m_sc, acc_sc, s_sc, mb_sc):
    j = pl.program_id(2)
    last = pl.num_programs(2) - 1
    grp = N_HEADS // N_KV_HEADS
    heads = lambda h: slice(h * HEAD_DIM, (h + 1) * HEAD_DIM)

    def score(k_ref, slot, h):
        s = _dot(k_ref[h // grp], qt_ref[heads(h), :])
        s_sc[slot, h] = s
        mb_sc[slot, h:h + 1, :] = jnp.max(s, axis=0, keepdims=True)

    def softmax_pv(h, s, m_blk, vt_g):
        m_prev = m_sc[h:h + 1, :]
        m_new = jnp.maximum(m_prev, m_blk)
        alpha = jnp.exp2(m_prev - m_new)
        p = jnp.exp2(s - m_new).astype(BF16)
        acc_sc[h] = alpha * acc_sc[h] + _dot(vt_g, p)
        m_sc[h:h + 1, :] = m_new

    def consume(vt_ref, slot, h):
        softmax_pv(h, s_sc[slot, h], mb_sc[slot, h:h + 1, :], vt_ref[h // grp])

    @pl.when(j == 0)
    def _first():
        m_sc[...] = jnp.full(m_sc.shape, -jnp.inf, F32)
        acc_sc[...] = jnp.zeros(acc_sc.shape, F32)
        meta = [_dot(km_ref[h // grp], qt_ref[heads(h), :]) for h in range(N_HEADS)]
        for h in range(N_HEADS):
            score(k0_ref, 0, h)
            softmax_pv(h, meta[h], jnp.max(meta[h], axis=0, keepdims=True), vtm_ref[h // grp])
        for h in range(N_HEADS):
            score(k1_ref, 1, h)
            consume(vtb_ref, 0, h)

    @pl.when((j > 0) & (j < last))
    def _middle():
        for h in range(N_HEADS):
            score(k0_ref, 0, h)
            consume(vta_ref, 1, h)
        for h in range(N_HEADS):
            score(k1_ref, 1, h)
            consume(vtb_ref, 0, h)

    @pl.when(j == last)
    def _last():
        for h in range(N_HEADS):
            consume(vta_ref, 1, h)
        acc = acc_sc[...]
        tq = acc.shape[2]
        out_t = (acc[:, 0:HEAD_DIM, :] / acc[:, HEAD_DIM:HEAD_DIM + 1, :]).reshape(N_HEADS * HEAD_DIM, tq)
        o_ref[...] = out_t.T.astype(BF16)


def _attention(qt, kk, vt, km, vtm, nb, tq, tk):
    aw, rows = qt.shape
    s = rows // nb
    nq = s // tq
    nkb = s // tk
    kblk = lambda f: pl.BlockSpec((N_KV_HEADS, tk, HEAD_DIM),
                                  lambda bi, qi, j: (0, bi * nkb + jnp.clip(f(j), 0, nkb - 1), 0))
    vblk = lambda f: pl.BlockSpec((N_KV_HEADS, V_ROWS, tk),
                                  lambda bi, qi, j: (0, 0, bi * nkb + jnp.clip(f(j), 0, nkb - 1)))
    return pl.pallas_call(
        _attn_kernel,
        grid=(nb, nq, nkb // 2 + 1),
        in_specs=[
            pl.BlockSpec((aw, tq), lambda bi, qi, j: (0, bi * nq + qi)),
            kblk(lambda j: 2 * j), kblk(lambda j: 2 * j + 1),
            vblk(lambda j: 2 * j - 1), vblk(lambda j: 2 * j),
            pl.BlockSpec(km.shape, lambda bi, qi, j: (0, 0, 0)),
            pl.BlockSpec(vtm.shape, lambda bi, qi, j: (0, 0, 0)),
        ],
        out_specs=pl.BlockSpec((tq, aw), lambda bi, qi, j: (bi * nq + qi, 0)),
        out_shape=jax.ShapeDtypeStruct((rows, aw), BF16),
        scratch_shapes=[
            pltpu.VMEM((N_HEADS, tq), F32),
            pltpu.VMEM((N_HEADS, V_ROWS, tq), F32),
            pltpu.VMEM((2, N_HEADS, tk, tq), F32),
            pltpu.VMEM((2, N_HEADS, tq), F32),
        ],
        compiler_params=pltpu.CompilerParams(
            dimension_semantics=("arbitrary", "arbitrary", "arbitrary"),
            vmem_limit_bytes=VMEM_LIMIT),
        name="attn",
    )(qt, kk, kk, vt, vt, km, vtm)


def _ssm_prep_kernel(*refs):
    for d in range(2):
        _ssm_prep_direction(d, *refs)


def _ssm_prep_direction(d, arow_ref, ldt_ref, bt_ref, ct_ref, m_ref, w_ref, v_ref, at_ref):
    t_chunk = SSM_CHUNK
    shift = int(math.log2(SSM_GROUP))
    df = float(d)
    dt = jnp.exp(ldt_ref[d, 0])

    def abar(ar, ai):
        mag = jnp.exp(ar * dt)
        ang = ai * dt
        return mag * jnp.cos(ang), mag * jnp.sin(ang)

    nbits = int(math.log2(t_chunk))

    def ipow(br, bi, e):
        pr = pi = None
        for k in range(nbits):
            bit = ((e >> k) & 1) == 1
            if pr is None:
                pr, pi = jnp.where(bit, br, 1.0), jnp.where(bit, bi, 0.0)
            else:
                pr, pi = jnp.where(bit, pr * br - pi * bi, pr), jnp.where(bit, pr * bi + pi * br, pi)
            br, bi = br * br - bi * bi, 2.0 * br * bi
        return pr, pi, br, bi

    ar_r = arow_ref[d, 0, 0:1, :]
    ai_r = arow_ref[d, 0, 1:2, :]
    abr, abi = abar(ar_r, ai_r)
    nr = abr - 1.0
    ni = abi
    den = ar_r * ar_r + ai_r * ai_r
    cr = (nr * ar_r + ni * ai_r) / den
    ci = (ni * ar_r - nr * ai_r) / den
    nstate = bt_ref.shape[4]
    nrow = t_chunk * SSM_GROUP

    def tile_rows(x):
        return jnp.broadcast_to(x[None], (t_chunk, SSM_GROUP, nstate)).reshape(nrow, nstate)

    btr = tile_rows(bt_ref[d, 0, 0])
    bti = tile_rows(bt_ref[d, 0, 1])
    bbr = cr * btr - ci * bti
    bbi = cr * bti + ci * btr
    jr = (lax.broadcasted_iota(jnp.int32, (nrow, 1), 0) >> shift).astype(F32)
    lj = jr + df * ((t_chunk - 1) - 2.0 * jr)
    step = lax.broadcasted_iota(jnp.int32, (t_chunk, 1), 0)
    lstep = (t_chunk - 1) - step if d == 1 else step

    def rep_rows(x):
        return jnp.broadcast_to(x[:, None, :], (t_chunk, SSM_GROUP, nstate)).reshape(nrow, nstate)

    mag2 = abr * abr + abi * abi
    enr, eni, _, _ = ipow(abr / mag2, -abi / mag2, lstep)
    enr, eni = rep_rows(enr), rep_rows(eni)
    bmr = enr * bbr - eni * bbi
    bmi = enr * bbi + eni * bbr
    ewr, ewi, atr, ati = ipow(abr, abi, (t_chunk - 1) - lstep)
    ewr, ewi = rep_rows(ewr), rep_rows(ewi)
    w_ref[d, 0] = jnp.concatenate([ewr * bbr - ewi * bbi, ewr * bbi + ewi * bbr], axis=1).astype(BF16)
    at_ref[d, 0] = jnp.concatenate([atr, ati], axis=1)

    ncol = t_chunk * SSM_GROUP
    lane = lax.broadcasted_iota(jnp.int32, (1, ncol), 1)
    tci = lane >> shift
    tile = jnp.where(lax.broadcasted_iota(jnp.int32, (SSM_GROUP, ncol), 0) == (lane & (SSM_GROUP - 1)),
                     1.0, 0.0).astype(BF16)

    def tile_cols(x):
        hi, lo = _split(x)
        lo2 = (x - hi.astype(F32) - lo.astype(F32)).astype(BF16)
        return _dot(hi, tile) + _dot(lo, tile) + _dot(lo2, tile)

    ctr = tile_cols(ct_ref[d, 0, 0])
    cti = tile_cols(ct_ref[d, 0, 1])
    tc = tci.astype(F32)
    lt = tc + df * ((t_chunk - 1) - 2.0 * tc)
    stepl = lax.broadcasted_iota(jnp.int32, (1, t_chunk), 1)
    lstepl = (t_chunk - 1) - stepl if d == 1 else stepl
    diag = (lax.broadcasted_iota(jnp.int32, (nstate, nstate), 0)
            == lax.broadcasted_iota(jnp.int32, (nstate, nstate), 1))
    abr_c = jnp.sum(jnp.where(diag, abr, 0.0), axis=1, keepdims=True)
    abi_c = jnp.sum(jnp.where(diag, abi, 0.0), axis=1, keepdims=True)
    rep = jnp.where(lax.broadcasted_iota(jnp.int32, (t_chunk, ncol), 0) == tci, 1.0, 0.0).astype(BF16)

    def rep_cols(x):
        hi, lo = _split(x)
        lo2 = (x - hi.astype(F32) - lo.astype(F32)).astype(BF16)
        return _dot(hi, rep) + _dot(lo, rep) + _dot(lo2, rep)

    ecr, eci = [rep_cols(x) for x in ipow(abr_c, abi_c, lstepl)[:2]]
    cmr = ctr * ecr - cti * eci
    cmi = ctr * eci + cti * ecr
    lhs_hi, lhs_lo = _split(jnp.concatenate([bmr, -bmi], axis=1))
    rhs_hi, rhs_lo = _split(jnp.concatenate([cmr, cmi], axis=0))
    m = _dot(lhs_hi, rhs_hi) + _dot(lhs_hi, rhs_lo) + _dot(lhs_lo, rhs_hi)
    m_ref[d, 0] = jnp.where(lj <= lt, m, 0.0).astype(BF16)
    c1r = cmr * abr_c - cmi * abi_c
    c1i = cmr * abi_c + cmi * abr_c
    v_ref[d, 0] = jnp.concatenate([c1r, -c1i], axis=0).astype(BF16)


def _ssm_prep(arow, ldt, bt, ct):
    nd, g = arow.shape[0], arow.shape[1]
    p = SSM_STATE
    n = SSM_CHUNK * SSM_GROUP
    blk = lambda a: pl.BlockSpec((nd, 1) + a.shape[2:], lambda gi: (0, gi) + (0,) * (a.ndim - 2))
    return pl.pallas_call(
        _ssm_prep_kernel,
        grid=(g,),
        in_specs=[blk(arow), blk(ldt), blk(bt), blk(ct)],
        out_specs=[
            pl.BlockSpec((nd, 1, n, n), lambda gi: (0, gi, 0, 0)),
            pl.BlockSpec((nd, 1, n, 2 * p), lambda gi: (0, gi, 0, 0)),
            pl.BlockSpec((nd, 1, 2 * p, n), lambda gi: (0, gi, 0, 0)),
            pl.BlockSpec((nd, 1, 1, 2 * p), lambda gi: (0, gi, 0, 0)),
        ],
        out_shape=[
            jax.ShapeDtypeStruct((nd, g, n, n), BF16),
            jax.ShapeDtypeStruct((nd, g, n, 2 * p), BF16),
            jax.ShapeDtypeStruct((nd, g, 2 * p, n), BF16),
            jax.ShapeDtypeStruct((nd, g, 1, 2 * p), F32),
        ],
        compiler_params=pltpu.CompilerParams(dimension_semantics=("arbitrary",)),
        name="ssm_prep",
    )(arow, ldt, bt, ct)


def _chunk_carry(chains):
    p = SSM_STATE
    z0, a0 = chains[0][0], chains[0][1]
    nc = z0.shape[0]
    row = lax.broadcasted_iota(jnp.int32, z0.shape, 0)
    is_re = lax.broadcasted_iota(jnp.int32, a0.shape, 1) < p
    sign = jnp.where(is_re, -1.0, 1.0)

    def parts(ap):
        sw = pltpu.roll(ap, p, 1)
        return jnp.where(is_re, ap, sw), sign * jnp.where(is_re, sw, ap)

    def cmul(x, ar_full, ai_sgn):
        return x * ar_full + pltpu.roll(x, p, 1) * ai_sgn

    coef = [parts(a) for _, a, _, _ in chains]
    firsts = [(nc - 1) if rev else 0 for _, _, _, rev in chains]
    es = [z + jnp.where(row == f, cmul(s0, *c), 0.0) for (z, _, s0, _), f, c in zip(chains, firsts, coef)]
    k = 1
    while k < nc:
        for i, (_, _, _, rev) in enumerate(chains):
            if rev:
                sh = jnp.where(row < nc - k, pltpu.roll(es[i], nc - k, 0), 0.0)
            else:
                sh = jnp.where(row >= k, pltpu.roll(es[i], k, 0), 0.0)
            es[i] = es[i] + cmul(sh, *coef[i])
        k *= 2
        if k < nc:
            coef = [parts(cmul(jnp.where(is_re, ar, sign * ai), ar, ai)) for ar, ai in coef]
    outs = []
    for e, (_, _, s0, rev), f in zip(es, chains, firsts):
        outs.append(jnp.where(row == f, s0, pltpu.roll(e, nc - 1 if rev else 1, 0)))
    return outs


def _ssm_kernel(u_ref, um_ref, m_ref, w_ref, v_ref, at_ref, dvec_ref, y_ref, ug_sc, yg_sc):
    p = SSM_STATE
    t_chunk = SSM_CHUNK
    hgrp = SSM_GROUP
    gb = u_ref.shape[2] // hgrp
    per_tile = u_ref.shape[2] // hgrp
    nc = u_ref.shape[1] // t_chunk
    lane_blk = lax.broadcasted_iota(jnp.int32, (nc, u_ref.shape[2]), 1) >> int(math.log2(hgrp))

    width = u_ref.shape[2]

    def block_transpose(arrs):
        arrs = list(arrs)
        d = len(arrs) // 2
        while d >= 1:
            low_bit = (lane_blk & d) == 0
            for i in range(len(arrs)):
                if i & d == 0:
                    lo, hi = arrs[i], arrs[i + d]
                    arrs[i] = jnp.where(low_bit, lo, pltpu.roll(hi, d * hgrp, 1))
                    arrs[i + d] = jnp.where(low_bit, pltpu.roll(lo, width - d * hgrp, 1), hi)
            d //= 2
        return arrs

    for q in range(t_chunk // per_tile):
        steps = [u_ref[0, pl.ds(q * per_tile + r, nc, stride=t_chunk), :] for r in range(per_tile)]
        for g, tile in enumerate(block_transpose(steps)):
            ug_sc[g, :, q * width:(q + 1) * width] = tile

    def groups(i, carry):
        gs = [i * SSM_GROUPS_TOGETHER + k for k in range(SSM_GROUPS_TOGETHER)]
        us = [ug_sc[g] for g in gs]
        ubs = [u.astype(BF16) for u in us]
        yin = [[_dot(ub, m_ref[d, g]) for d in range(2)] for g, ub in zip(gs, ubs)]
        chains = []
        for g, ub in zip(gs, ubs):
            for d in range(2):
                if d == 0:
                    s0 = _dot(um_ref[g].astype(BF16), w_ref[0, g])[0:1, :]
                else:
                    s0 = jnp.zeros((1, 2 * p), F32)
                chains.append((_dot(ub, w_ref[d, g]), at_ref[d, g], s0, d == 1))
        s_in = _chunk_carry(chains)
        for k, g in enumerate(gs):
            y = dvec_ref[g] * us[k] + yin[k][0] + yin[k][1]
            for d in range(2):
                y = y + _dot(s_in[2 * k + d].astype(BF16), v_ref[d, g])
            yg_sc[g] = y
        return carry

    lax.fori_loop(0, gb // SSM_GROUPS_TOGETHER, groups, 0)

    for q in range(t_chunk // per_tile):
        srcs = [yg_sc[g, :, q * width:(q + 1) * width] for g in range(gb)]
        for r, row in enumerate(block_transpose(srcs)):
            y_ref[0, pl.ds(q * per_tile + r, nc, stride=t_chunk), :] = row


def _ssm(u3, um2, m, w, v, at, dvec):
    b, s, sw = u3.shape
    g = sw // SSM_GROUP
    n = SSM_CHUNK * SSM_GROUP
    p = SSM_STATE
    lanes = 128
    gb = lanes // SSM_GROUP
    return pl.pallas_call(
        _ssm_kernel,
        grid=(b, sw // lanes),
        in_specs=[
            pl.BlockSpec((1, s, lanes), lambda bi, gi: (bi, 0, gi)),
            pl.BlockSpec((gb, 8, n), lambda bi, gi: (gi, 0, 0)),
            pl.BlockSpec((2, gb, n, n), lambda bi, gi: (0, gi, 0, 0)),
            pl.BlockSpec((2, gb, n, 2 * p), lambda bi, gi: (0, gi, 0, 0)),
            pl.BlockSpec((2, gb, 2 * p, n), lambda bi, gi: (0, gi, 0, 0)),
            pl.BlockSpec((2, gb, 1, 2 * p), lambda bi, gi: (0, gi, 0, 0)),
            pl.BlockSpec((gb, 1, n), lambda bi, gi: (gi, 0, 0)),
        ],
        out_specs=pl.BlockSpec((1, s, lanes), lambda bi, gi: (bi, 0, gi)),
        out_shape=jax.ShapeDtypeStruct((b, s, sw), F32),
        scratch_shapes=[pltpu.VMEM((gb, s // SSM_CHUNK, n), F32), pltpu.VMEM((gb, s // SSM_CHUNK, n), F32)],
        compiler_params=pltpu.CompilerParams(dimension_semantics=("arbitrary", "arbitrary"),
                                             vmem_limit_bytes=VMEM_LIMIT),
        name="ssm",
    )(u3, um2, m, w, v, at, dvec)


def _merge_kernel(x_ref, ya_ref, ys_ref, wgt_ref, wglu_ref, bglu_ref, wab_ref, wsb_ref, wo_ref,
                  lng_ref, lnb_ref, l1g_ref, l1b_ref, wr_ref,
                  h1_ref, tok_ref, afft_ref):
    d = x_ref.shape[1]
    h = _ln(x_ref[...], lng_ref[...], lnb_ref[...])
    gates = _dot(h.astype(BF16), wgt_ref[...])
    ga = _sigmoid(gates[:, 0:d])
    gs = _sigmoid(gates[:, d:2 * d])
    ys = ys_ref[...]
    y = 0.5 * ys * (1.0 + jnp.tanh(math.sqrt(2.0 / math.pi) * (ys + 0.044715 * (ys * ys * ys))))
    yg = y * _sigmoid(_dot(y.astype(BF16), wglu_ref[...]) + bglu_ref[...])
    merged = ga * _dot(ya_ref[...], wab_ref[...]) + gs * _dot(yg.astype(BF16), wsb_ref[...])
    h1 = _ln(DEEPNORM_ALPHA * h + _dot(merged.astype(BF16), wo_ref[...]), l1g_ref[...], l1b_ref[...])
    h1_ref[...] = h1
    hi, lo = _split(h1)
    tok_ref[...] = hi
    ne = afft_ref.shape[0]
    cross = _dot(hi, wr_ref[...])
    logits = cross[:, 0:ROUTER_LANES] + cross[:, ROUTER_LANES:] + _dot(lo, wr_ref[:, 0:ROUTER_LANES])
    lane = lax.broadcasted_iota(jnp.int32, logits.shape, 1)
    logits = jnp.where(lane < ne, logits, -jnp.inf)
    ex = jnp.exp(logits - jnp.max(logits, axis=1, keepdims=True))
    aff = ex / jnp.sum(ex, axis=1, keepdims=True)
    afft_ref[...] = aff.T[0:ne, :]


def _merge(x2, ya, ys, wgt, wglu, bglu, wab, wsb, wo, lng, lnb, l1g, l1b, wr, ne, tm):
    rows, d = x2.shape
    full = lambda a: pl.BlockSpec(a.shape, lambda i: (0,) * a.ndim)
    return pl.pallas_call(
        _merge_kernel,
        grid=(rows // tm,),
        in_specs=[
            pl.BlockSpec((tm, d), lambda i: (i, 0)),
            pl.BlockSpec((tm, ya.shape[1]), lambda i: (i, 0)),
            pl.BlockSpec((tm, ys.shape[1]), lambda i: (i, 0)),
            full(wgt), full(wglu), full(bglu), full(wab), full(wsb), full(wo),
            full(lng), full(lnb), full(l1g), full(l1b), full(wr),
        ],
        out_specs=[
            pl.BlockSpec((tm, d), lambda i: (i, 0)),
            pl.BlockSpec((tm, d), lambda i: (i, 0)),
            pl.BlockSpec((ne, tm), lambda i: (0, i)),
        ],
        out_shape=[
            jax.ShapeDtypeStruct((rows, d), F32),
            jax.ShapeDtypeStruct((rows, d), BF16),
            jax.ShapeDtypeStruct((ne, rows), F32),
        ],
        compiler_params=pltpu.CompilerParams(dimension_semantics=("arbitrary",),
                                             vmem_limit_bytes=VMEM_LIMIT),
        name="merge",
    )(x2, ya, ys, wgt, wglu, bglu, wab, wsb, wo, lng, lnb, l1g, l1b, wr)


def _route_kernel(afft_ref, tri_ref, scl_ref, lo_ref, *, cap):
    aff = afft_ref[...]
    ne, s = aff.shape
    capf = float(cap)

    def as_float(bits):
        return lax.bitcast_convert_type(bits, F32)

    def search(i, t):
        cand = t | jnp.left_shift(jnp.int32(1), 30 - i)
        cnt = jnp.sum(jnp.where(aff >= as_float(cand), 1.0, 0.0), axis=1, keepdims=True)
        return jnp.where(cnt >= capf, cand, t)

    thr_bits = lax.fori_loop(0, 31, search, jnp.zeros((ne, 1), jnp.int32))
    gt = aff >= as_float(thr_bits + 1)
    eq = (aff >= as_float(thr_bits)) & jnp.logical_not(gt)
    need = capf - jnp.sum(jnp.where(gt, 1.0, 0.0), axis=1, keepdims=True)
    tri = tri_ref[...]
    rt = ROUTE_TILE
    nt = s // rt
    col = lax.broadcasted_iota(jnp.int32, (ne, nt), 1)
    carry_eq = jnp.zeros((ne, 1), F32)
    carry_sel = jnp.zeros((ne, 1), F32)
    lo_val = jnp.zeros((ne, nt), F32)
    for t in range(nt):
        sl = slice(t * rt, (t + 1) * rt)
        eq_b = eq[:, sl]
        ceq = _dot(jnp.where(eq_b, 1.0, 0.0).astype(BF16), tri)
        sel_b = gt[:, sl] | (eq_b & ((ceq + carry_eq) <= need))
        carry_eq = carry_eq + ceq[:, rt - 1:rt]
        csel = _dot(jnp.where(sel_b, 1.0, 0.0).astype(BF16), tri)
        scl_ref[0, :, sl] = jnp.where(sel_b, csel, 0.0)
        lo_val = jnp.where(col == t, carry_sel, lo_val)
        carry_sel = carry_sel + csel[:, rt - 1:rt]
    lo_ref[0] = lo_val


def _route(afft, tri, nb, cap):
    ne, rows = afft.shape
    s = rows // nb
    nt = s // ROUTE_TILE
    return pl.pallas_call(
        functools.partial(_route_kernel, cap=cap),
        grid=(nb,),
        in_specs=[
            pl.BlockSpec((ne, s), lambda b: (0, b)),
            pl.BlockSpec(tri.shape, lambda b: (0, 0)),
        ],
        out_specs=[
            pl.BlockSpec((1, ne, s), lambda b: (b, 0, 0)),
            pl.BlockSpec((1, ne, nt), lambda b: (b, 0, 0)),
        ],
        out_shape=[
            jax.ShapeDtypeStruct((nb, ne, s), F32),
            jax.ShapeDtypeStruct((nb, ne, nt), F32),
        ],
        compiler_params=pltpu.CompilerParams(dimension_semantics=("arbitrary",)),
        name="route",
    )(afft, tri)


def _ffn_kernel(lo_ref, t_ref, scl_ref, aff_ref, wg_ref, wu_ref, wd_ref, ye_ref, xe_sc, xb_sc, gate_sc, acc_sc,
                *, cap, flag_off):
    fc = pl.program_id(2)

    @pl.when(fc == 0)
    def _gather():
        _ffn_gather(lo_ref, t_ref, scl_ref, aff_ref, xe_sc, xb_sc, gate_sc, cap=cap, flag_off=flag_off)
        acc_sc[...] = jnp.zeros(acc_sc.shape, F32)

    xb = xb_sc[...]
    g = _dot(xb, wg_ref[0, 0].astype(BF16))
    u = _dot(xb, wu_ref[0, 0].astype(BF16))
    hh = (g * _sigmoid(g)) * u
    acc_sc[...] += _dot(hh.astype(BF16), wd_ref[0, 0].astype(BF16))

    @pl.when(fc == pl.num_programs(2) - 1)
    def _emit():
        ye_ref[0, 0] = (acc_sc[...] * gate_sc[0:cap, :]).astype(BF16)


def _ffn_gather(lo_ref, t_ref, scl_ref, aff_ref, xe_sc, xb_sc, gate_sc, *, cap, flag_off):
    b = pl.program_id(0)
    e = pl.program_id(1)
    ne = pl.num_programs(1)
    rt = ROUTE_TILE
    nt = t_ref.shape[1] // rt
    xe_sc[...] = jnp.zeros(xe_sc.shape, F32)
    gate_sc[...] = jnp.zeros(gate_sc.shape, F32)
    r = lax.broadcasted_iota(jnp.int32, (PIECE, rt), 0).astype(F32)

    def tile_info(tau):
        base = (b * ne + e) * (nt + 1) + tau
        lo = lo_ref[base]
        n_pieces = jnp.right_shift((lo & 7) + (lo_ref[base + 1] - lo) + (PIECE - 1), PIECE_SHIFT)
        return lo, n_pieces

    def add_piece(tau, lo, p):
        off = lo & 7
        scl = scl_ref[0, 0, pl.ds(tau, 1), :]
        tok = t_ref[0, pl.ds(pl.multiple_of(tau * rt, rt), rt), :]
        shift = (off - 1 - p * PIECE).astype(F32)
        chosen = (scl > 0.0) & ((scl + shift) == r)
        rows = pl.ds(pl.multiple_of(lo - off + p * PIECE, 8), PIECE)
        xe_sc[rows, :] += _dot(jnp.where(chosen, 1.0, 0.0).astype(BF16), tok)
        gate_sc[rows, :] += jnp.sum(jnp.where(chosen, aff_ref[0, 0, pl.ds(tau, 1), :], 0.0), axis=1, keepdims=True)

    def first_piece(tau, carry):
        add_piece(tau, tile_info(tau)[0], 0)
        return carry

    def more_pieces(tau, carry):
        lo, n_pieces = tile_info(tau)
        lax.fori_loop(1, n_pieces, lambda p, c: (add_piece(tau, lo, p), c)[1], 0)
        return carry

    lax.fori_loop(0, nt, first_piece, 0, unroll=4)

    @pl.when(lo_ref[flag_off + b * ne + e] > 0)
    def _rare():
        lax.fori_loop(0, nt, more_pieces, 0)

    xb_sc[...] = xe_sc[0:cap, :].astype(BF16)


def _ffn(lo_i, t, scl4, aff4, wg, wu, wd, layer, cap, f_chunk):
    b, s, d = t.shape
    _, ne, _, f = wg.shape
    nt = s // ROUTE_TILE
    grid_spec = pltpu.PrefetchScalarGridSpec(
        num_scalar_prefetch=1,
        grid=(b, ne, f // f_chunk),
        in_specs=[
            pl.BlockSpec((1, s, d), lambda bi, ei, fi, lo: (bi, 0, 0), pipeline_mode=pl.Buffered(1)),
            pl.BlockSpec((1, 1, nt, ROUTE_TILE), lambda bi, ei, fi, lo: (bi, ei, 0, 0)),
            pl.BlockSpec((1, 1, nt, ROUTE_TILE), lambda bi, ei, fi, lo: (ei, bi, 0, 0)),
            pl.BlockSpec((1, 1, d, f_chunk), lambda bi, ei, fi, lo: (layer, ei, 0, fi)),
            pl.BlockSpec((1, 1, d, f_chunk), lambda bi, ei, fi, lo: (layer, ei, 0, fi)),
            pl.BlockSpec((1, 1, f_chunk, d), lambda bi, ei, fi, lo: (layer, ei, fi, 0)),
        ],
        out_specs=pl.BlockSpec((1, 1, cap, d), lambda bi, ei, fi, lo: (bi, ei, 0, 0)),
        scratch_shapes=[pltpu.VMEM((cap + PIECE, d), F32), pltpu.VMEM((cap, d), BF16),
                        pltpu.VMEM((cap + PIECE, 1), F32), pltpu.VMEM((cap, d), F32)],
    )
    return pl.pallas_call(
        functools.partial(_ffn_kernel, cap=cap, flag_off=b * ne * (nt + 1)),
        grid_spec=grid_spec,
        out_shape=jax.ShapeDtypeStruct((b, ne, cap, d), BF16),
        compiler_params=pltpu.CompilerParams(dimension_semantics=("arbitrary", "arbitrary", "arbitrary"),
                                             vmem_limit_bytes=VMEM_LIMIT),
        name="ffn",
    )(lo_i, t, scl4, aff4, wg, wu, wd)


def _combine_kernel(lo_ref, h1_ref, sclt_ref, ye_hbm, g_ref, b_ref, o_ref, win, xwin, acc_sc, sem, xsem,
                    *, cap, flag_off):
    b = pl.program_id(0)
    tau = pl.program_id(1)
    nb = pl.num_programs(0)
    nt = pl.num_programs(1)
    ne = win.shape[1]
    d = win.shape[3]
    step = b * nt + tau
    slot = step & 1

    def pieces(bi, ti, e):
        base = (bi * ne + e) * (nt + 1) + ti
        lo = lo_ref[base]
        n_sel = lo_ref[base + 1] - lo
        n_pieces = jnp.maximum(jnp.right_shift((lo & 15) + n_sel + (PIECE - 1), PIECE_SHIFT), 1)
        first = jnp.minimum(lo - (lo & 15), cap - n_pieces * PIECE)
        return lo, first, n_pieces

    def copy(bi, e, first, p, buf, s):
        start = pl.multiple_of(first + p * PIECE, 16)
        return pltpu.make_async_copy(ye_hbm.at[bi, e, pl.ds(start, PIECE), :], buf, s)

    def first_pieces(bi, ti, sl):
        return [copy(bi, e, pieces(bi, ti, e)[1], 0, win.at[sl, e], sem.at[sl, e]) for e in range(ne)]

    @pl.when(step == 0)
    def _prime():
        for cp in first_pieces(b, tau, slot):
            cp.start()

    @pl.when(step + 1 < nb * nt)
    def _prefetch():
        wrap = tau + 1 == nt
        for cp in first_pieces(jnp.where(wrap, b + 1, b), jnp.where(wrap, 0, tau + 1), 1 - slot):
            cp.start()

    for cp in first_pieces(b, tau, slot):
        cp.wait()

    sclt = sclt_ref[0]
    col = lax.broadcasted_iota(jnp.int32, (1, ne * PIECE), 1)
    grp = jnp.right_shift(col, PIECE_SHIFT)
    expand = jnp.where(lax.broadcasted_iota(jnp.int32, (ne, ne * PIECE), 0) == grp, 1.0, 0.0).astype(BF16)
    scl = _dot(sclt.astype(BF16), expand)
    shift = jnp.zeros(col.shape, F32)
    for e in range(ne):
        lo, first, _ = pieces(b, tau, e)
        shift = jnp.where(grp == e, (lo - first - 1).astype(F32), shift)
    r = (col & (PIECE - 1)).astype(F32)
    onehot = jnp.where((scl > 0.0) & ((scl + shift) == r), 1.0, 0.0).astype(BF16)
    acc_sc[...] = DEEPNORM_ALPHA * h1_ref[0] + _dot(onehot, win[slot].reshape(ne * PIECE, d))

    @pl.when(lo_ref[flag_off + step] > 0)
    def _rare():
        for e in range(ne):
            lo, first, n_pieces = pieces(b, tau, e)

            def extra(p, c, e=e, lo=lo, first=first):
                cp = copy(b, e, first, p, xwin, xsem.at[0])
                cp.start()
                cp.wait()
                se = sclt_ref[0][:, e:e + 1]
                rr = lax.broadcasted_iota(jnp.int32, (se.shape[0], PIECE), 1).astype(F32)
                oh = jnp.where((se > 0.0) & ((se + (lo - first - p * PIECE - 1).astype(F32)) == rr), 1.0, 0.0)
                acc_sc[...] += _dot(oh.astype(BF16), xwin[...])
                return c

            lax.fori_loop(1, n_pieces, extra, 0)

    o_ref[0] = _ln(acc_sc[...], g_ref[...], b_ref[...])


def _combine(lo_i, h1, sclt, ye, g, bb, cap):
    b, s, d = h1.shape
    ne = sclt.shape[2]
    rt = ROUTE_TILE
    grid_spec = pltpu.PrefetchScalarGridSpec(
        num_scalar_prefetch=1,
        grid=(b, s // rt),
        in_specs=[
            pl.BlockSpec((1, rt, d), lambda bi, ti, lo: (bi, ti, 0)),
            pl.BlockSpec((1, rt, ne), lambda bi, ti, lo: (bi, ti, 0)),
            pl.BlockSpec(memory_space=pl.ANY),
            pl.BlockSpec(g.shape, lambda bi, ti, lo: (0, 0)),
            pl.BlockSpec(bb.shape, lambda bi, ti, lo: (0, 0)),
        ],
        out_specs=pl.BlockSpec((1, rt, d), lambda bi, ti, lo: (bi, ti, 0)),
        scratch_shapes=[pltpu.VMEM((2, ne, PIECE, d), BF16), pltpu.VMEM((PIECE, d), BF16),
                        pltpu.VMEM((rt, d), F32),
                        pltpu.SemaphoreType.DMA((2, ne)), pltpu.SemaphoreType.DMA((1,))],
    )
    return pl.pallas_call(
        functools.partial(_combine_kernel, cap=cap, flag_off=b * ne * (s // rt + 1) + b * ne),
        grid_spec=grid_spec,
        out_shape=jax.ShapeDtypeStruct((b, s, d), F32),
        compiler_params=pltpu.CompilerParams(dimension_semantics=("arbitrary", "arbitrary"),
                                             vmem_limit_bytes=VMEM_LIMIT),
        name="combine",
    )(lo_i, h1, sclt, ye, g, bb)


def _rope_tables(n_tokens):
    half = HEAD_DIM // 2
    inv_freq = ROPE_THETA ** (-jnp.arange(0, half, 2, dtype=F32) / half)
    rows = n_tokens // GRID_W
    row = jnp.repeat(jnp.arange(rows, dtype=F32), GRID_W)
    colv = jnp.tile(jnp.arange(GRID_W, dtype=F32), rows)
    ang = jnp.concatenate([row[:, None] * inv_freq, colv[:, None] * inv_freq], axis=-1)
    return jnp.cos(ang), jnp.sin(ang)


def kernel(x, meta_tokens, ln_in_g, ln_in_b, w_in, q_norm_g, k_norm_g, ssm_a_re, ssm_a_im, ssm_log_dt,
           ssm_b_re, ssm_b_im, ssm_c_re, ssm_c_im, ssm_d, w_glu, b_glu, w_attn_br, w_ssm_br, w_o,
           ln1_g, ln1_b, w_router, w_gate_e, w_up_e, w_down_e, ln2_g, ln2_b):
    b, s, d = x.shape
    aw = N_HEADS * HEAD_DIM
    kw = N_KV_HEADS * HEAD_DIM
    sw = d // 2
    g = sw // SSM_GROUP
    half = HEAD_DIM // 2
    cap = CAPACITY_FACTOR * s // N_EXPERTS
    rows = b * s
    l = 0

    perm = np.concatenate([np.arange(0, HEAD_DIM, 2), np.arange(1, HEAD_DIM, 2)])
    qcols = np.concatenate([h * HEAD_DIM + perm for h in range(N_HEADS)])
    kcols = aw + np.concatenate([h * HEAD_DIM + perm for h in range(N_KV_HEADS)])
    wl = w_in[l]
    w_t = jnp.concatenate([wl[:, qcols], wl[:, aw + kw:aw + 2 * kw]], axis=1).T.astype(BF16)
    w_n = jnp.concatenate([wl[:, kcols], wl[:, aw + 2 * kw:aw + 2 * kw + sw]], axis=1).astype(BF16)
    w_gates = wl[:, aw + 2 * kw + sw:].astype(BF16)
    qg3 = (jnp.tile(q_norm_g[l][perm], N_HEADS) * (HEAD_DIM ** -0.5 * math.log2(math.e))).reshape(N_HEADS, HEAD_DIM, 1)
    kg = jnp.tile(k_norm_g[l][perm], N_KV_HEADS)[None, :]
    bd = jnp.asarray(np.kron(np.eye(N_KV_HEADS), np.full((HEAD_DIM, HEAD_DIM), 1.0 / HEAD_DIM)), BF16)
    lng = ln_in_g[None, :]
    lnb = ln_in_b[None, :]
    cos, sin = _rope_tables(s)
    cosr = jnp.tile(cos, (1, 2 * N_KV_HEADS))
    sinr = jnp.tile(jnp.concatenate([-sin, sin], axis=1), (1, N_KV_HEADS))
    cost = cos.T
    sint = sin.T

    x2 = x.reshape(rows, d)
    tm_in = 1024
    qt, kk, vt, u2 = _inproj(x2, cosr, sinr, cost, sint, w_t, w_n, lng, lnb, qg3, kg, bd, tm_in, s // tm_in)
    meta_p = jnp.pad(meta_tokens, ((0, META_PAD - N_META), (0, 0)))
    ones_r = jnp.ones((META_PAD, kw), F32)
    _, km, vtm, um = _inproj(meta_p, ones_r, jnp.zeros_like(ones_r), jnp.ones((half, META_PAD), F32),
                             jnp.zeros((half, META_PAD), F32), w_t, w_n, lng, lnb, qg3, kg, bd, META_PAD, 1)
    um = um[:N_META]
    km = km[:, :N_META, :]
    vtm = vtm[:, :, :N_META]

    ya = _attention(qt, kk, vt, km, vtm, b, 512, 512)

    t_chunk = SSM_CHUNK
    hgrp = SSM_GROUP
    n = t_chunk * hgrp
    arow = jnp.stack([ssm_a_re[l], ssm_a_im[l]], axis=2)
    ldt = ssm_log_dt[l][:, :, None, None]
    bt = jnp.stack([ssm_b_re[l], ssm_b_im[l]], axis=2)
    bt = jnp.swapaxes(bt, 3, 4)
    ct = jnp.stack([ssm_c_re[l], ssm_c_im[l]], axis=2)
    ct = jnp.swapaxes(ct, 3, 4)
    mm, ww, vv, at = _ssm_prep(arow, ldt, bt, ct)
    nc = s // t_chunk
    um_g = um.reshape(N_META, g, hgrp).transpose(1, 0, 2).reshape(g, 1, N_META * hgrp)
    um_g = jnp.pad(um_g, ((0, 0), (0, 7), (n - N_META * hgrp, 0)))
    dvec = jnp.tile(ssm_d[l], (1, t_chunk))[:, None, :]
    ys = _ssm(u2.reshape(b, s, sw), um_g, mm, ww, vv, at, dvec).reshape(rows, sw)

    wr_hi = w_router[l].astype(BF16)
    wr_lo = (w_router[l] - wr_hi.astype(F32)).astype(BF16)
    lane_pad = ((0, 0), (0, ROUTER_LANES - N_EXPERTS))
    wr = jnp.concatenate([jnp.pad(wr_hi, lane_pad), jnp.pad(wr_lo, lane_pad)], axis=1)
    h1, tok, afft = _merge(
        x2, ya, ys, w_gates, w_glu[l].astype(BF16), b_glu[l][None, :],
        w_attn_br[l].astype(BF16), w_ssm_br[l].astype(BF16), w_o[l].astype(BF16),
        lng, lnb, ln1_g[l][None, :], ln1_b[l][None, :], wr, N_EXPERTS, 512)

    tri = jnp.asarray(np.triu(np.ones((ROUTE_TILE, ROUTE_TILE), np.float32)), BF16)
    scl, lo_f = _route(afft, tri, b, cap)
    nt = s // ROUTE_TILE
    lo3 = jnp.pad(lo_f.astype(jnp.int32), ((0, 0), (0, 0), (0, 1)), constant_values=cap)
    seg = lo3[:, :, 1:] - lo3[:, :, :-1]
    multi = lambda align: ((lo3[:, :, :-1] & (align - 1)) + seg + (PIECE - 1)) // PIECE > 1
    lo_i = jnp.concatenate([lo3.reshape(-1),
                            jnp.any(multi(8), axis=2).astype(jnp.int32).reshape(-1),
                            jnp.any(multi(16), axis=1).astype(jnp.int32).reshape(-1)])
    ye = _ffn(lo_i, tok.reshape(b, s, d), scl.reshape(b, N_EXPERTS, nt, ROUTE_TILE),
              afft.reshape(N_EXPERTS, b, nt, ROUTE_TILE), w_gate_e, w_up_e, w_down_e, l, cap, 512)
    out = _combine(lo_i, h1.reshape(b, s, d), jnp.swapaxes(scl, 1, 2), ye,
                   ln2_g[l][None, :], ln2_b[l][None, :], cap)
    return out
```

```python
import functools
import math

import numpy as np
import jax
import jax.numpy as jnp
from jax import lax
from jax.experimental import pallas as pl
from jax.experimental.pallas import tpu as pltpu

F32 = jnp.float32
BF16 = jnp.bfloat16

N_META = 16
GRID_W = 64
N_HEADS = 8
N_KV_HEADS = 2
HEAD_DIM = 64
ROPE_THETA = 10000.0
SSM_GROUP = 16
SSM_STATE = 64
N_EXPERTS = 16
CAPACITY_FACTOR = 2
LN_EPS = 1e-5
QK_EPS = 1e-6
DEPTH = 1
DEEPNORM_ALPHA = (2.0 * DEPTH) ** 0.25

SSM_CHUNK = 32
SSM_GROUPS_TOGETHER = 4
ROUTE_TILE = 256
ROUTER_LANES = 128
PIECE_SHIFT = 6
PIECE = 1 << PIECE_SHIFT
META_PAD = 128
V_ROWS = HEAD_DIM + 16
VMEM_LIMIT = 56 * 1024 * 1024


def _ln(x, g, b):
    mu = jnp.mean(x, axis=-1, keepdims=True)
    xc = x - mu
    var = jnp.mean(xc * xc, axis=-1, keepdims=True)
    return xc * lax.rsqrt(var + LN_EPS) * g + b


def _sigmoid(x):
    return 1.0 / (1.0 + jnp.exp(-x))


def _split(t):
    hi = t.astype(BF16)
    lo = (t - hi.astype(F32)).astype(BF16)
    return hi, lo


def _dot(a, b):
    return jnp.dot(a, b, preferred_element_type=F32)


def _dot_nt(a, b):
    return lax.dot_general(a, b, (((1,), (1,)), ((), ())), preferred_element_type=F32)


def _inproj_kernel(x_ref, cosr_ref, sinr_ref, cost_ref, sint_ref, wt_ref, wn_ref, lng_ref, lnb_ref,
                   qg_ref, kg_ref, bd_ref, qt_ref, k_ref, vt_ref, u_ref):
    aw = N_HEADS * HEAD_DIM
    kw = N_KV_HEADS * HEAD_DIM
    half = HEAD_DIM // 2
    h = _ln(x_ref[...], lng_ref[...], lnb_ref[...])
    hb = h.astype(BF16)
    pt = _dot_nt(wt_ref[...], hb)
    pn = _dot(hb, wn_ref[...])
    tm = hb.shape[0]

    qt = pt[0:aw].reshape(N_HEADS, HEAD_DIM, tm)
    ms = jnp.mean(qt * qt, axis=1, keepdims=True)
    qn = qt * lax.rsqrt(ms + QK_EPS) * qg_ref[...]
    x0 = qn[:, 0:half, :]
    x1 = qn[:, half:, :]
    c = cost_ref[...][None]
    s = sint_ref[...][None]
    qr = jnp.concatenate([x0 * c - x1 * s, x0 * s + x1 * c], axis=1)
    qt_ref[...] = qr.reshape(aw, tm).astype(BF16)
    vrow = lax.broadcasted_iota(jnp.int32, (N_KV_HEADS, V_ROWS - HEAD_DIM, tm), 1)
    vt_ref[...] = jnp.concatenate([pt[aw:aw + kw].reshape(N_KV_HEADS, HEAD_DIM, tm),
                                   jnp.where(vrow == 0, 1.0, 0.0)], axis=1).astype(BF16)

    kk = pn[:, 0:kw]
    hi, lo = _split(kk * kk)
    bd = bd_ref[...]
    msk = _dot(hi, bd) + _dot(lo, bd)
    kn = kk * lax.rsqrt(msk + QK_EPS) * kg_ref[...]
    lane = lax.broadcasted_iota(jnp.int32, kn.shape, 1)
    first = (lane & (HEAD_DIM - 1)) < half
    partner = jnp.where(first, pltpu.roll(kn, kw - half, 1), pltpu.roll(kn, half, 1))
    kr = (kn * cosr_ref[...] + partner * sinr_ref[...]).astype(BF16)
    for g in range(N_KV_HEADS):
        k_ref[g] = kr[:, g * HEAD_DIM:(g + 1) * HEAD_DIM]
    u_ref[...] = pn[:, kw:]


def _inproj(x2, cosr, sinr, cost, sint, wt, wn, lng, lnb, qg3, kg, bd, tm, n_tab_blocks):
    rows, d = x2.shape
    aw = N_HEADS * HEAD_DIM
    kw = N_KV_HEADS * HEAD_DIM
    uw = wn.shape[1] - kw
    half = HEAD_DIM // 2
    full = lambda a: pl.BlockSpec(a.shape, lambda i: (0,) * a.ndim)
    return pl.pallas_call(
        _inproj_kernel,
        grid=(rows // tm,),
        in_specs=[
            pl.BlockSpec((tm, d), lambda i: (i, 0)),
            pl.BlockSpec((tm, kw), lambda i: (i % n_tab_blocks, 0)),
            pl.BlockSpec((tm, kw), lambda i: (i % n_tab_blocks, 0)),
            pl.BlockSpec((half, tm), lambda i: (0, i % n_tab_blocks)),
            pl.BlockSpec((half, tm), lambda i: (0, i % n_tab_blocks)),
            full(wt), full(wn), full(lng), full(lnb), full(qg3), full(kg), full(bd),
        ],
        out_specs=[
            pl.BlockSpec((aw, tm), lambda i: (0, i)),
            pl.BlockSpec((N_KV_HEADS, tm, HEAD_DIM), lambda i: (0, i, 0)),
            pl.BlockSpec((N_KV_HEADS, V_ROWS, tm), lambda i: (0, 0, i)),
            pl.BlockSpec((tm, uw), lambda i: (i, 0)),
        ],
        out_shape=[
            jax.ShapeDtypeStruct((aw, rows), BF16),
            jax.ShapeDtypeStruct((N_KV_HEADS, rows, HEAD_DIM), BF16),
            jax.ShapeDtypeStruct((N_KV_HEADS, V_ROWS, rows), BF16),
            jax.ShapeDtypeStruct((rows, uw), F32),
        ],
        compiler_params=pltpu.CompilerParams(dimension_semantics=("arbitrary",),
                                             vmem_limit_bytes=VMEM_LIMIT),
        name="inproj",
    )(x2, cosr, sinr, cost, sint, wt, wn, lng, lnb, qg3, kg, bd)


def _attn_kernel(qt_ref, k0_ref, k1_ref, vta_ref, vtb_ref, km_ref, vtm_ref, o_ref, m_sc, acc_sc, s_sc, mb_sc):
    j = pl.program_id(2)
    last = pl.num_programs(2) - 1
    grp = N_HEADS // N_KV_HEADS
    heads = lambda h: slice(h * HEAD_DIM, (h + 1) * HEAD_DIM)

    def score(k_ref, slot, h):
        s = _dot(k_ref[h // grp], qt_ref[heads(h), :])
        s_sc[slot, h] = s
        mb_sc[slot, h:h + 1, :] = jnp.max(s, axis=0, keepdims=True)

    def softmax_pv(h, s, m_blk, vt_g):
        m_prev = m_sc[h:h + 1, :]
        m_new = jnp.maximum(m_prev, m_blk)
        alpha = jnp.exp2(m_prev - m_new)
        p = jnp.exp2(s - m_new).astype(BF16)
        acc_sc[h] = alpha * acc_sc[h] + _dot(vt_g, p)
        m_sc[h:h + 1, :] = m_new

    def consume(vt_ref, slot, h):
        softmax_pv(h, s_sc[slot, h], mb_sc[slot, h:h + 1, :], vt_ref[h // grp])

    @pl.when(j == 0)
    def _first():
        m_sc[...] = jnp.full(m_sc.shape, -jnp.inf, F32)
        acc_sc[...] = jnp.zeros(acc_sc.shape, F32)
        meta = [_dot(km_ref[h // grp], qt_ref[heads(h), :]) for h in range(N_HEADS)]
        for h in range(N_HEADS):
            score(k0_ref, 0, h)
            softmax_pv(h, meta[h], jnp.max(meta[h], axis=0, keepdims=True), vtm_ref[h // grp])
        for h in range(N_HEADS):
            score(k1_ref, 1, h)
            consume(vtb_ref, 0, h)

    @pl.when((j > 0) & (j < last))
    def _middle():
        for h in range(N_HEADS):
            score(k0_ref, 0, h)
            consume(vta_ref, 1, h)
        for h in range(N_HEADS):
            score(k1_ref, 1, h)
            consume(vtb_ref, 0, h)

    @pl.when(j == last)
    def _last():
        for h in range(N_HEADS):
            consume(vta_ref, 1, h)
        acc = acc_sc[...]
        tq = acc.shape[2]
        out_t = (acc[:, 0:HEAD_DIM, :] / acc[:, HEAD_DIM:HEAD_DIM + 1, :]).reshape(N_HEADS * HEAD_DIM, tq)
        o_ref[...] = out_t.T.astype(BF16)


def _attention(qt, kk, vt, km, vtm, nb, tq, tk):
    aw, rows = qt.shape
    s = rows // nb
    nq = s // tq
    nkb = s // tk
    kblk = lambda f: pl.BlockSpec((N_KV_HEADS, tk, HEAD_DIM),
                                  lambda bi, qi, j: (0, bi * nkb + jnp.clip(f(j), 0, nkb - 1), 0))
    vblk = lambda f: pl.BlockSpec((N_KV_HEADS, V_ROWS, tk),
                                  lambda bi, qi, j: (0, 0, bi * nkb + jnp.clip(f(j), 0, nkb - 1)))
    return pl.pallas_call(
        _attn_kernel,
        grid=(nb, nq, nkb // 2 + 1),
        in_specs=[
            pl.BlockSpec((aw, tq), lambda bi, qi, j: (0, bi * nq + qi)),
            kblk(lambda j: 2 * j), kblk(lambda j: 2 * j + 1),
            vblk(lambda j: 2 * j - 1), vblk(lambda j: 2 * j),
            pl.BlockSpec(km.shape, lambda bi, qi, j: (0, 0, 0)),
            pl.BlockSpec(vtm.shape, lambda bi, qi, j: (0, 0, 0)),
        ],
        out_specs=pl.BlockSpec((tq, aw), lambda bi, qi, j: (bi * nq + qi, 0)),
        out_shape=jax.ShapeDtypeStruct((rows, aw), BF16),
        scratch_shapes=[
            pltpu.VMEM((N_HEADS, tq), F32),
            pltpu.VMEM((N_HEADS, V_ROWS, tq), F32),
            pltpu.VMEM((2, N_HEADS, tk, tq), F32),
            pltpu.VMEM((2, N_HEADS, tq), F32),
        ],
        compiler_params=pltpu.CompilerParams(
            dimension_semantics=("arbitrary", "arbitrary", "arbitrary"),
            vmem_limit_bytes=VMEM_LIMIT),
        name="attn",
    )(qt, kk, kk, vt, vt, km, vtm)


def _ssm_prep_kernel(*refs):
    for d in range(2):
        _ssm_prep_direction(d, *refs)


def _ssm_prep_direction(d, arow_ref, ldt_ref, bt_ref, ct_ref, m_ref, w_ref, v_ref, at_ref):
    t_chunk = SSM_CHUNK
    shift = int(math.log2(SSM_GROUP))
    df = float(d)
    dt = jnp.exp(ldt_ref[d, 0])

    def abar(ar, ai):
        mag = jnp.exp(ar * dt)
        ang = ai * dt
        return mag * jnp.cos(ang), mag * jnp.sin(ang)

    nbits = int(math.log2(t_chunk))

    def ipow(br, bi, e):
        pr = pi = None
        for k in range(nbits):
            bit = ((e >> k) & 1) == 1
            if pr is None:
                pr, pi = jnp.where(bit, br, 1.0), jnp.where(bit, bi, 0.0)
            else:
                pr, pi = jnp.where(bit, pr * br - pi * bi, pr), jnp.where(bit, pr * bi + pi * br, pi)
            br, bi = br * br - bi * bi, 2.0 * br * bi
        return pr, pi, br, bi

    ar_r = arow_ref[d, 0, 0:1, :]
    ai_r = arow_ref[d, 0, 1:2, :]
    abr, abi = abar(ar_r, ai_r)
    nr = abr - 1.0
    ni = abi
    den = ar_r * ar_r + ai_r * ai_r
    cr = (nr * ar_r + ni * ai_r) / den
    ci = (ni * ar_r - nr * ai_r) / den
    nstate = bt_ref.shape[4]
    nrow = t_chunk * SSM_GROUP

    def tile_rows(x):
        return jnp.broadcast_to(x[None], (t_chunk, SSM_GROUP, nstate)).reshape(nrow, nstate)

    btr = tile_rows(bt_ref[d, 0, 0])
    bti = tile_rows(bt_ref[d, 0, 1])
    bbr = cr * btr - ci * bti
    bbi = cr * bti + ci * btr
    jr = (lax.broadcasted_iota(jnp.int32, (nrow, 1), 0) >> shift).astype(F32)
    lj = jr + df * ((t_chunk - 1) - 2.0 * jr)
    step = lax.broadcasted_iota(jnp.int32, (t_chunk, 1), 0)
    lstep = (t_chunk - 1) - step if d == 1 else step

    def rep_rows(x):
        return jnp.broadcast_to(x[:, None, :], (t_chunk, SSM_GROUP, nstate)).reshape(nrow, nstate)

    mag2 = abr * abr + abi * abi
    enr, eni, _, _ = ipow(abr / mag2, -abi / mag2, lstep)
    enr, eni = rep_rows(enr), rep_rows(eni)
    bmr = enr * bbr - eni * bbi
    bmi = enr * bbi + eni * bbr
    ewr, ewi, atr, ati = ipow(abr, abi, (t_chunk - 1) - lstep)
    ewr, ewi = rep_rows(ewr), rep_rows(ewi)
    w_ref[d, 0] = jnp.concatenate([ewr * bbr - ewi * bbi, ewr * bbi + ewi * bbr], axis=1).astype(BF16)
    at_ref[d, 0] = jnp.concatenate([atr, ati], axis=1)

    ncol = t_chunk * SSM_GROUP
    lane = lax.broadcasted_iota(jnp.int32, (1, ncol), 1)
    tci = lane >> shift
    tile = jnp.where(lax.broadcasted_iota(jnp.int32, (SSM_GROUP, ncol), 0) == (lane & (SSM_GROUP - 1)),
                     1.0, 0.0).astype(BF16)

    def tile_cols(x):
        hi, lo = _split(x)
        lo2 = (x - hi.astype(F32) - lo.astype(F32)).astype(BF16)
        return _dot(hi, tile) + _dot(lo, tile) + _dot(lo2, tile)

    ctr = tile_cols(ct_ref[d, 0, 0])
    cti = tile_cols(ct_ref[d, 0, 1])
    tc = tci.astype(F32)
    lt = tc + df * ((t_chunk - 1) - 2.0 * tc)
    stepl = lax.broadcasted_iota(jnp.int32, (1, t_chunk), 1)
    lstepl = (t_chunk - 1) - stepl if d == 1 else stepl
    diag = (lax.broadcasted_iota(jnp.int32, (nstate, nstate), 0)
            == lax.broadcasted_iota(jnp.int32, (nstate, nstate), 1))
    abr_c = jnp.sum(jnp.where(diag, abr, 0.0), axis=1, keepdims=True)
    abi_c = jnp.sum(jnp.where(diag, abi, 0.0), axis=1, keepdims=True)
    rep = jnp.where(lax.broadcasted_iota(jnp.int32, (t_chunk, ncol), 0) == tci, 1.0, 0.0).astype(BF16)

    def rep_cols(x):
        hi, lo = _split(x)
        lo2 = (x - hi.astype(F32) - lo.astype(F32)).astype(BF16)
        return _dot(hi, rep) + _dot(lo, rep) + _dot(lo2, rep)

    ecr, eci = [rep_cols(x) for x in ipow(abr_c, abi_c, lstepl)[:2]]
    cmr = ctr * ecr - cti * eci
    cmi = ctr * eci + cti * ecr
    lhs_hi, lhs_lo = _split(jnp.concatenate([bmr, -bmi], axis=1))
    rhs_hi, rhs_lo = _split(jnp.concatenate([cmr, cmi], axis=0))
    m = _dot(lhs_hi, rhs_hi) + _dot(lhs_hi, rhs_lo) + _dot(lhs_lo, rhs_hi)
    m_ref[d, 0] = jnp.where(lj <= lt, m, 0.0).astype(BF16)
    c1r = cmr * abr_c - cmi * abi_c
    c1i = cmr * abi_c + cmi * abr_c
    v_ref[d, 0] = jnp.concatenate([c1r, -c1i], axis=0).astype(BF16)


def _ssm_prep(arow, ldt, bt, ct):
    nd, g = arow.shape[0], arow.shape[1]
    p = SSM_STATE
    n = SSM_CHUNK * SSM_GROUP
    blk = lambda a: pl.BlockSpec((nd, 1) + a.shape[2:], lambda gi: (0, gi) + (0,) * (a.ndim - 2))
    return pl.pallas_call(
        _ssm_prep_kernel,
        grid=(g,),
        in_specs=[blk(arow), blk(ldt), blk(bt), blk(ct)],
        out_specs=[
            pl.BlockSpec((nd, 1, n, n), lambda gi: (0, gi, 0, 0)),
            pl.BlockSpec((nd, 1, n, 2 * p), lambda gi: (0, gi, 0, 0)),
            pl.BlockSpec((nd, 1, 2 * p, n), lambda gi: (0, gi, 0, 0)),
            pl.BlockSpec((nd, 1, 1, 2 * p), lambda gi: (0, gi, 0, 0)),
        ],
        out_shape=[
            jax.ShapeDtypeStruct((nd, g, n, n), BF16),
            jax.ShapeDtypeStruct((nd, g, n, 2 * p), BF16),
            jax.ShapeDtypeStruct((nd, g, 2 * p, n), BF16),
            jax.ShapeDtypeStruct((nd, g, 1, 2 * p), F32),
        ],
        compiler_params=pltpu.CompilerParams(dimension_semantics=("arbitrary",)),
        name="ssm_prep",
    )(arow, ldt, bt, ct)


def _chunk_carry(chains):
    p = SSM_STATE
    z0, a0 = chains[0][0], chains[0][1]
    nc = z0.shape[0]
    row = lax.broadcasted_iota(jnp.int32, z0.shape, 0)
    is_re = lax.broadcasted_iota(jnp.int32, a0.shape, 1) < p
    sign = jnp.where(is_re, -1.0, 1.0)

    def parts(ap):
        sw = pltpu.roll(ap, p, 1)
        return jnp.where(is_re, ap, sw), sign * jnp.where(is_re, sw, ap)

    def cmul(x, ar_full, ai_sgn):
        return x * ar_full + pltpu.roll(x, p, 1) * ai_sgn

    coef = [parts(a) for _, a, _, _ in chains]
    firsts = [(nc - 1) if rev else 0 for _, _, _, rev in chains]
    es = [z + jnp.where(row == f, cmul(s0, *c), 0.0) for (z, _, s0, _), f, c in zip(chains, firsts, coef)]
    k = 1
    while k < nc:
        for i, (_, _, _, rev) in enumerate(chains):
            if rev:
                sh = jnp.where(row < nc - k, pltpu.roll(es[i], nc - k, 0), 0.0)
            else:
                sh = jnp.where(row >= k, pltpu.roll(es[i], k, 0), 0.0)
            es[i] = es[i] + cmul(sh, *coef[i])
        k *= 2
        if k < nc:
            coef = [parts(cmul(jnp.where(is_re, ar, sign * ai), ar, ai)) for ar, ai in coef]
    outs = []
    for e, (_, _, s0, rev), f in zip(es, chains, firsts):
        outs.append(jnp.where(row == f, s0, pltpu.roll(e, nc - 1 if rev else 1, 0)))
    return outs


def _ssm_kernel(u_ref, um_ref, m_ref, w_ref, v_ref, at_ref, dvec_ref, y_ref, ug_sc, yg_sc):
    p = SSM_STATE
    t_chunk = SSM_CHUNK
    hgrp = SSM_GROUP
    gb = u_ref.shape[2] // hgrp
    per_tile = u_ref.shape[2] // hgrp
    nc = u_ref.shape[1] // t_chunk
    lane_blk = lax.broadcasted_iota(jnp.int32, (nc, u_ref.shape[2]), 1) >> int(math.log2(hgrp))

    width = u_ref.shape[2]

    def block_transpose(arrs):
        arrs = list(arrs)
        d = len(arrs) // 2
        while d >= 1:
            low_bit = (lane_blk & d) == 0
            for i in range(len(arrs)):
                if i & d == 0:
                    lo, hi = arrs[i], arrs[i + d]
                    arrs[i] = jnp.where(low_bit, lo, pltpu.roll(hi, d * hgrp, 1))
                    arrs[i + d] = jnp.where(low_bit, pltpu.roll(lo, width - d * hgrp, 1), hi)
            d //= 2
        return arrs

    for q in range(t_chunk // per_tile):
        steps = [u_ref[0, pl.ds(q * per_tile + r, nc, stride=t_chunk), :] for r in range(per_tile)]
        for g, tile in enumerate(block_transpose(steps)):
            ug_sc[g, :, q * width:(q + 1) * width] = tile

    def groups(i, carry):
        gs = [i * SSM_GROUPS_TOGETHER + k for k in range(SSM_GROUPS_TOGETHER)]
        us = [ug_sc[g] for g in gs]
        ubs = [u.astype(BF16) for u in us]
        yin = [[_dot(ub, m_ref[d, g]) for d in range(2)] for g, ub in zip(gs, ubs)]
        chains = []
        for g, ub in zip(gs, ubs):
            for d in range(2):
                if d == 0:
                    s0 = _dot(um_ref[g].astype(BF16), w_ref[0, g])[0:1, :]
                else:
                    s0 = jnp.zeros((1, 2 * p), F32)
                chains.append((_dot(ub, w_ref[d, g]), at_ref[d, g], s0, d == 1))
        s_in = _chunk_carry(chains)
        for k, g in enumerate(gs):
            y = dvec_ref[g] * us[k] + yin[k][0] + yin[k][1]
            for d in range(2):
                y = y + _dot(s_in[2 * k + d].astype(BF16), v_ref[d, g])
            yg_sc[g] = y
        return carry

    lax.fori_loop(0, gb // SSM_GROUPS_TOGETHER, groups, 0)

    for q in range(t_chunk // per_tile):
        srcs = [yg_sc[g, :, q * width:(q + 1) * width] for g in range(gb)]
        for r, row in enumerate(block_transpose(srcs)):
            y_ref[0, pl.ds(q * per_tile + r, nc, stride=t_chunk), :] = row


def _ssm(u3, um2, m, w, v, at, dvec):
    b, s, sw = u3.shape
    g = sw // SSM_GROUP
    n = SSM_CHUNK * SSM_GROUP
    p = SSM_STATE
    lanes = 128
    gb = lanes // SSM_GROUP
    return pl.pallas_call(
        _ssm_kernel,
        grid=(b, sw // lanes),
        in_specs=[
            pl.BlockSpec((1, s, lanes), lambda bi, gi: (bi, 0, gi)),
            pl.BlockSpec((gb, 8, n), lambda bi, gi: (gi, 0, 0)),
            pl.BlockSpec((2, gb, n, n), lambda bi, gi: (0, gi, 0, 0)),
            pl.BlockSpec((2, gb, n, 2 * p), lambda bi, gi: (0, gi, 0, 0)),
            pl.BlockSpec((2, gb, 2 * p, n), lambda bi, gi: (0, gi, 0, 0)),
            pl.BlockSpec((2, gb, 1, 2 * p), lambda bi, gi: (0, gi, 0, 0)),
            pl.BlockSpec((gb, 1, n), lambda bi, gi: (gi, 0, 0)),
        ],
        out_specs=pl.BlockSpec((1, s, lanes), lambda bi, gi: (bi, 0, gi)),
        out_shape=jax.ShapeDtypeStruct((b, s, sw), F32),
        scratch_shapes=[pltpu.VMEM((gb, s // SSM_CHUNK, n), F32), pltpu.VMEM((gb, s // SSM_CHUNK, n), F32)],
        compiler_params=pltpu.CompilerParams(dimension_semantics=("arbitrary", "arbitrary"),
                                             vmem_limit_bytes=VMEM_LIMIT),
        name="ssm",
    )(u3, um2, m, w, v, at, dvec)


def _merge_kernel(x_ref, ya_ref, ys_ref, wgt_ref, wglu_ref, bglu_ref, wab_ref, wsb_ref, wo_ref,
                  lng_ref, lnb_ref, l1g_ref, l1b_ref, wr_ref,
                  h1_ref, tok_ref, afft_ref):
    d = x_ref.shape[1]
    h = _ln(x_ref[...], lng_ref[...], lnb_ref[...])
    gates = _dot(h.astype(BF16), wgt_ref[...])
    ga = _sigmoid(gates[:, 0:d])
    gs = _sigmoid(gates[:, d:2 * d])
    ys = ys_ref[...]
    y = 0.5 * ys * (1.0 + jnp.tanh(math.sqrt(2.0 / math.pi) * (ys + 0.044715 * (ys * ys * ys))))
    yg = y * _sigmoid(_dot(y.astype(BF16), wglu_ref[...]) + bglu_ref[...])
    merged = ga * _dot(ya_ref[...], wab_ref[...]) + gs * _dot(yg.astype(BF16), wsb_ref[...])
    h1 = _ln(DEEPNORM_ALPHA * h + _dot(merged.astype(BF16), wo_ref[...]), l1g_ref[...], l1b_ref[...])
    h1_ref[...] = h1
    hi, lo = _split(h1)
    tok_ref[...] = hi
    ne = afft_ref.shape[0]
    cross = _dot(hi, wr_ref[...])
    logits = cross[:, 0:ROUTER_LANES] + cross[:, ROUTER_LANES:] + _dot(lo, wr_ref[:, 0:ROUTER_LANES])
    lane = lax.broadcasted_iota(jnp.int32, logits.shape, 1)
    logits = jnp.where(lane < ne, logits, -jnp.inf)
    ex = jnp.exp(logits - jnp.max(logits, axis=1, keepdims=True))
    aff = ex / jnp.sum(ex, axis=1, keepdims=True)
    afft_ref[...] = aff.T[0:ne, :]


def _merge(x2, ya, ys, wgt, wglu, bglu, wab, wsb, wo, lng, lnb, l1g, l1b, wr, ne, tm):
    rows, d = x2.shape
    full = lambda a: pl.BlockSpec(a.shape, lambda i: (0,) * a.ndim)
    return pl.pallas_call(
        _merge_kernel,
        grid=(rows // tm,),
        in_specs=[
            pl.BlockSpec((tm, d), lambda i: (i, 0)),
            pl.BlockSpec((tm, ya.shape[1]), lambda i: (i, 0)),
            pl.BlockSpec((tm, ys.shape[1]), lambda i: (i, 0)),
            full(wgt), full(wglu), full(bglu), full(wab), full(wsb), full(wo),
            full(lng), full(lnb), full(l1g), full(l1b), full(wr),
        ],
        out_specs=[
            pl.BlockSpec((tm, d), lambda i: (i, 0)),
            pl.BlockSpec((tm, d), lambda i: (i, 0)),
            pl.BlockSpec((ne, tm), lambda i: (0, i)),
        ],
        out_shape=[
            jax.ShapeDtypeStruct((rows, d), F32),
            jax.ShapeDtypeStruct((rows, d), BF16),
            jax.ShapeDtypeStruct((ne, rows), F32),
        ],
        compiler_params=pltpu.CompilerParams(dimension_semantics=("arbitrary",),
                                             vmem_limit_bytes=VMEM_LIMIT),
        name="merge",
    )(x2, ya, ys, wgt, wglu, bglu, wab, wsb, wo, lng, lnb, l1g, l1b, wr)


def _route_kernel(afft_ref, tri_ref, scl_ref, lo_ref, *, cap):
    aff = afft_ref[...]
    ne, s = aff.shape
    capf = float(cap)

    def as_float(bits):
        return lax.bitcast_convert_type(bits, F32)

    def search(i, t):
        cand = t | jnp.left_shift(jnp.int32(1), 30 - i)
        cnt = jnp.sum(jnp.where(aff >= as_float(cand), 1.0, 0.0), axis=1, keepdims=True)
        return jnp.where(cnt >= capf, cand, t)

    thr_bits = lax.fori_loop(0, 31, search, jnp.zeros((ne, 1), jnp.int32))
    gt = aff >= as_float(thr_bits + 1)
    eq = (aff >= as_float(thr_bits)) & jnp.logical_not(gt)
    need = capf - jnp.sum(jnp.where(gt, 1.0, 0.0), axis=1, keepdims=True)
    tri = tri_ref[...]
    rt = ROUTE_TILE
    nt = s // rt
    col = lax.broadcasted_iota(jnp.int32, (ne, nt), 1)
    carry_eq = jnp.zeros((ne, 1), F32)
    carry_sel = jnp.zeros((ne, 1), F32)
    lo_val = jnp.zeros((ne, nt), F32)
    for t in range(nt):
        sl = slice(t * rt, (t + 1) * rt)
        eq_b = eq[:, sl]
        ceq = _dot(jnp.where(eq_b, 1.0, 0.0).astype(BF16), tri)
        sel_b = gt[:, sl] | (eq_b & ((ceq + carry_eq) <= need))
        carry_eq = carry_eq + ceq[:, rt - 1:rt]
        csel = _dot(jnp.where(sel_b, 1.0, 0.0).astype(BF16), tri)
        scl_ref[0, :, sl] = jnp.where(sel_b, csel, 0.0)
        lo_val = jnp.where(col == t, carry_sel, lo_val)
        carry_sel = carry_sel + csel[:, rt - 1:rt]
    lo_ref[0] = lo_val


def _route(afft, tri, nb, cap):
    ne, rows = afft.shape
    s = rows // nb
    nt = s // ROUTE_TILE
    return pl.pallas_call(
        functools.partial(_route_kernel, cap=cap),
        grid=(nb,),
        in_specs=[
            pl.BlockSpec((ne, s), lambda b: (0, b)),
            pl.BlockSpec(tri.shape, lambda b: (0, 0)),
        ],
        out_specs=[
            pl.BlockSpec((1, ne, s), lambda b: (b, 0, 0)),
            pl.BlockSpec((1, ne, nt), lambda b: (b, 0, 0)),
        ],
        out_shape=[
            jax.ShapeDtypeStruct((nb, ne, s), F32),
            jax.ShapeDtypeStruct((nb, ne, nt), F32),
        ],
        compiler_params=pltpu.CompilerParams(dimension_semantics=("arbitrary",)),
        name="route",
    )(afft, tri)


def _ffn_kernel(lo_ref, t_ref, scl_ref, aff_ref, wg_ref, wu_ref, wd_ref, ye_ref, xe_sc, xb_sc, gate_sc, acc_sc,
                *, cap, flag_off):
    fc = pl.program_id(2)

    @pl.when(fc == 0)
    def _gather():
        _ffn_gather(lo_ref, t_ref, scl_ref, aff_ref, xe_sc, xb_sc, gate_sc, cap=cap, flag_off=flag_off)
        acc_sc[...] = jnp.zeros(acc_sc.shape, F32)

    xb = xb_sc[...]
    g = _dot(xb, wg_ref[0, 0].astype(BF16))
    u = _dot(xb, wu_ref[0, 0].astype(BF16))
    hh = (g * _sigmoid(g)) * u
    acc_sc[...] += _dot(hh.astype(BF16), wd_ref[0, 0].astype(BF16))

    @pl.when(fc == pl.num_programs(2) - 1)
    def _emit():
        ye_ref[0, 0] = (acc_sc[...] * gate_sc[0:cap, :]).astype(BF16)


def _ffn_gather(lo_ref, t_ref, scl_ref, aff_ref, xe_sc, xb_sc, gate_sc, *, cap, flag_off):
    b = pl.program_id(0)
    e = pl.program_id(1)
    ne = pl.num_programs(1)
    rt = ROUTE_TILE
    nt = t_ref.shape[1] // rt
    xe_sc[...] = jnp.zeros(xe_sc.shape, F32)
    gate_sc[...] = jnp.zeros(gate_sc.shape, F32)
    r = lax.broadcasted_iota(jnp.int32, (PIECE, rt), 0).astype(F32)

    def tile_info(tau):
        base = (b * ne + e) * (nt + 1) + tau
        lo = lo_ref[base]
        n_pieces = jnp.right_shift((lo & 7) + (lo_ref[base + 1] - lo) + (PIECE - 1), PIECE_SHIFT)
        return lo, n_pieces

    def add_piece(tau, lo, p):
        off = lo & 7
        scl = scl_ref[0, 0, pl.ds(tau, 1), :]
        tok = t_ref[0, pl.ds(pl.multiple_of(tau * rt, rt), rt), :]
        shift = (off - 1 - p * PIECE).astype(F32)
        chosen = (scl > 0.0) & ((scl + shift) == r)
        rows = pl.ds(pl.multiple_of(lo - off + p * PIECE, 8), PIECE)
        xe_sc[rows, :] += _dot(jnp.where(chosen, 1.0, 0.0).astype(BF16), tok)
        gate_sc[rows, :] += jnp.sum(jnp.where(chosen, aff_ref[0, 0, pl.ds(tau, 1), :], 0.0), axis=1, keepdims=True)

    def first_piece(tau, carry):
        add_piece(tau, tile_info(tau)[0], 0)
        return carry

    def more_pieces(tau, carry):
        lo, n_pieces = tile_info(tau)
        lax.fori_loop(1, n_pieces, lambda p, c: (add_piece(tau, lo, p), c)[1], 0)
        return carry

    lax.fori_loop(0, nt, first_piece, 0, unroll=4)

    @pl.when(lo_ref[flag_off + b * ne + e] > 0)
    def _rare():
        lax.fori_loop(0, nt, more_pieces, 0)

    xb_sc[...] = xe_sc[0:cap, :].astype(BF16)


def _ffn(lo_i, t, scl4, aff4, wg, wu, wd, layer, cap, f_chunk):
    b, s, d = t.shape
    _, ne, _, f = wg.shape
    nt = s // ROUTE_TILE
    grid_spec = pltpu.PrefetchScalarGridSpec(
        num_scalar_prefetch=1,
        grid=(b, ne, f // f_chunk),
        in_specs=[
            pl.BlockSpec((1, s, d), lambda bi, ei, fi, lo: (bi, 0, 0), pipeline_mode=pl.Buffered(1)),
            pl.BlockSpec((1, 1, nt, ROUTE_TILE), lambda bi, ei, fi, lo: (bi, ei, 0, 0)),
            pl.BlockSpec((1, 1, nt, ROUTE_TILE), lambda bi, ei, fi, lo: (ei, bi, 0, 0)),
            pl.BlockSpec((1, 1, d, f_chunk), lambda bi, ei, fi, lo: (layer, ei, 0, fi)),
            pl.BlockSpec((1, 1, d, f_chunk), lambda bi, ei, fi, lo: (layer, ei, 0, fi)),
            pl.BlockSpec((1, 1, f_chunk, d), lambda bi, ei, fi, lo: (layer, ei, fi, 0)),
        ],
        out_specs=pl.BlockSpec((1, 1, cap, d), lambda bi, ei, fi, lo: (bi, ei, 0, 0)),
        scratch_shapes=[pltpu.VMEM((cap + PIECE, d), F32), pltpu.VMEM((cap, d), BF16),
                        pltpu.VMEM((cap + PIECE, 1), F32), pltpu.VMEM((cap, d), F32)],
    )
    return pl.pallas_call(
        functools.partial(_ffn_kernel, cap=cap, flag_off=b * ne * (nt + 1)),
        grid_spec=grid_spec,
        out_shape=jax.ShapeDtypeStruct((b, ne, cap, d), BF16),
        compiler_params=pltpu.CompilerParams(dimension_semantics=("arbitrary", "arbitrary", "arbitrary"),
                                             vmem_limit_bytes=VMEM_LIMIT),
        name="ffn",
    )(lo_i, t, scl4, aff4, wg, wu, wd)


def _combine_kernel(lo_ref, h1_ref, sclt_ref, ye_hbm, g_ref, b_ref, o_ref, win, xwin, acc_sc, sem, xsem,
                    *, cap, flag_off):
    b = pl.program_id(0)
    tau = pl.program_id(1)
    nb = pl.num_programs(0)
    nt = pl.num_programs(1)
    ne = win.shape[1]
    d = win.shape[3]
    step = b * nt + tau
    slot = step & 1

    def pieces(bi, ti, e):
        base = (bi * ne + e) * (nt + 1) + ti
        lo = lo_ref[base]
        n_sel = lo_ref[base + 1] - lo
        n_pieces = jnp.maximum(jnp.right_shift((lo & 15) + n_sel + (PIECE - 1), PIECE_SHIFT), 1)
        first = jnp.minimum(lo - (lo & 15), cap - n_pieces * PIECE)
        return lo, first, n_pieces

    def copy(bi, e, first, p, buf, s):
        start = pl.multiple_of(first + p * PIECE, 16)
        return pltpu.make_async_copy(ye_hbm.at[bi, e, pl.ds(start, PIECE), :], buf, s)

    def first_pieces(bi, ti, sl):
        return [copy(bi, e, pieces(bi, ti, e)[1], 0, win.at[sl, e], sem.at[sl, e]) for e in range(ne)]

    @pl.when(step == 0)
    def _prime():
        for cp in first_pieces(b, tau, slot):
            cp.start()

    @pl.when(step + 1 < nb * nt)
    def _prefetch():
        wrap = tau + 1 == nt
        for cp in first_pieces(jnp.where(wrap, b + 1, b), jnp.where(wrap, 0, tau + 1), 1 - slot):
            cp.start()

    for cp in first_pieces(b, tau, slot):
        cp.wait()

    sclt = sclt_ref[0]
    col = lax.broadcasted_iota(jnp.int32, (1, ne * PIECE), 1)
    grp = jnp.right_shift(col, PIECE_SHIFT)
    expand = jnp.where(lax.broadcasted_iota(jnp.int32, (ne, ne * PIECE), 0) == grp, 1.0, 0.0).astype(BF16)
    scl = _dot(sclt.astype(BF16), expand)
    shift = jnp.zeros(col.shape, F32)
    for e in range(ne):
        lo, first, _ = pieces(b, tau, e)
        shift = jnp.where(grp == e, (lo - first - 1).astype(F32), shift)
    r = (col & (PIECE - 1)).astype(F32)
    onehot = jnp.where((scl > 0.0) & ((scl + shift) == r), 1.0, 0.0).astype(BF16)
    acc_sc[...] = DEEPNORM_ALPHA * h1_ref[0] + _dot(onehot, win[slot].reshape(ne * PIECE, d))

    @pl.when(lo_ref[flag_off + step] > 0)
    def _rare():
        for e in range(ne):
            lo, first, n_pieces = pieces(b, tau, e)

            def extra(p, c, e=e, lo=lo, first=first):
                cp = copy(b, e, first, p, xwin, xsem.at[0])
                cp.start()
                cp.wait()
                se = sclt_ref[0][:, e:e + 1]
                rr = lax.broadcasted_iota(jnp.int32, (se.shape[0], PIECE), 1).astype(F32)
                oh = jnp.where((se > 0.0) & ((se + (lo - first - p * PIECE - 1).astype(F32)) == rr), 1.0, 0.0)
                acc_sc[...] += _dot(oh.astype(BF16), xwin[...])
                return c

            lax.fori_loop(1, n_pieces, extra, 0)

    o_ref[0] = _ln(acc_sc[...], g_ref[...], b_ref[...])


def _combine(lo_i, h1, sclt, ye, g, bb, cap):
    b, s, d = h1.shape
    ne = sclt.shape[2]
    rt = ROUTE_TILE
    grid_spec = pltpu.PrefetchScalarGridSpec(
        num_scalar_prefetch=1,
        grid=(b, s // rt),
        in_specs=[
            pl.BlockSpec((1, rt, d), lambda bi, ti, lo: (bi, ti, 0)),
            pl.BlockSpec((1, rt, ne), lambda bi, ti, lo: (bi, ti, 0)),
            pl.BlockSpec(memory_space=pl.ANY),
            pl.BlockSpec(g.shape, lambda bi, ti, lo: (0, 0)),
            pl.BlockSpec(bb.shape, lambda bi, ti, lo: (0, 0)),
        ],
        out_specs=pl.BlockSpec((1, rt, d), lambda bi, ti, lo: (bi, ti, 0)),
        scratch_shapes=[pltpu.VMEM((2, ne, PIECE, d), BF16), pltpu.VMEM((PIECE, d), BF16),
                        pltpu.VMEM((rt, d), F32),
                        pltpu.SemaphoreType.DMA((2, ne)), pltpu.SemaphoreType.DMA((1,))],
    )
    return pl.pallas_call(
        functools.partial(_combine_kernel, cap=cap, flag_off=b * ne * (s // rt + 1) + b * ne),
        grid_spec=grid_spec,
        out_shape=jax.ShapeDtypeStruct((b, s, d), F32),
        compiler_params=pltpu.CompilerParams(dimension_semantics=("arbitrary", "arbitrary"),
                                             vmem_limit_bytes=VMEM_LIMIT),
        name="combine",
    )(lo_i, h1, sclt, ye, g, bb)


def _rope_tables(n_tokens):
    half = HEAD_DIM // 2
    inv_freq = ROPE_THETA ** (-jnp.arange(0, half, 2, dtype=F32) / half)
    rows = n_tokens // GRID_W
    row = jnp.repeat(jnp.arange(rows, dtype=F32), GRID_W)
    colv = jnp.tile(jnp.arange(GRID_W, dtype=F32), rows)
    ang = jnp.concatenate([row[:, None] * inv_freq, colv[:, None] * inv_freq], axis=-1)
    return jnp.cos(ang), jnp.sin(ang)


def kernel(x, meta_tokens, ln_in_g, ln_in_b, w_in, q_norm_g, k_norm_g, ssm_a_re, ssm_a_im, ssm_log_dt,
           ssm_b_re, ssm_b_im, ssm_c_re, ssm_c_im, ssm_d, w_glu, b_glu, w_attn_br, w_ssm_br, w_o,
           ln1_g, ln1_b, w_router, w_gate_e, w_up_e, w_down_e, ln2_g, ln2_b):
    b, s, d = x.shape
    aw = N_HEADS * HEAD_DIM
    kw = N_KV_HEADS * HEAD_DIM
    sw = d // 2
    g = sw // SSM_GROUP
    half = HEAD_DIM // 2
    cap = CAPACITY_FACTOR * s // N_EXPERTS
    rows = b * s
    assert w_in.shape[0] == DEPTH == 1 and s % (2 * 512) == 0 and d == 2 * aw, (w_in.shape, x.shape)
    l = 0

    perm = np.concatenate([np.arange(0, HEAD_DIM, 2), np.arange(1, HEAD_DIM, 2)])
    qcols = np.concatenate([h * HEAD_DIM + perm for h in range(N_HEADS)])
    kcols = aw + np.concatenate([h * HEAD_DIM + perm for h in range(N_KV_HEADS)])
    wl = w_in[l]
    w_t = jnp.concatenate([wl[:, qcols], wl[:, aw + kw:aw + 2 * kw]], axis=1).T.astype(BF16)
    w_n = jnp.concatenate([wl[:, kcols], wl[:, aw + 2 * kw:aw + 2 * kw + sw]], axis=1).astype(BF16)
    w_gates = wl[:, aw + 2 * kw + sw:].astype(BF16)
    qg3 = (jnp.tile(q_norm_g[l][perm], N_HEADS) * (HEAD_DIM ** -0.5 * math.log2(math.e))).reshape(N_HEADS, HEAD_DIM, 1)
    kg = jnp.tile(k_norm_g[l][perm], N_KV_HEADS)[None, :]
    bd = jnp.asarray(np.kron(np.eye(N_KV_HEADS), np.full((HEAD_DIM, HEAD_DIM), 1.0 / HEAD_DIM)), BF16)
    lng = ln_in_g[None, :]
    lnb = ln_in_b[None, :]
    cos, sin = _rope_tables(s)
    cosr = jnp.tile(cos, (1, 2 * N_KV_HEADS))
    sinr = jnp.tile(jnp.concatenate([-sin, sin], axis=1), (1, N_KV_HEADS))
    cost = cos.T
    sint = sin.T

    x2 = x.reshape(rows, d)
    tm_in = 1024
    qt, kk, vt, u2 = _inproj(x2, cosr, sinr, cost, sint, w_t, w_n, lng, lnb, qg3, kg, bd, tm_in, s // tm_in)
    meta_p = jnp.pad(meta_tokens, ((0, META_PAD - N_META), (0, 0)))
    ones_r = jnp.ones((META_PAD, kw), F32)
    _, km, vtm, um = _inproj(meta_p, ones_r, jnp.zeros_like(ones_r), jnp.ones((half, META_PAD), F32),
                             jnp.zeros((half, META_PAD), F32), w_t, w_n, lng, lnb, qg3, kg, bd, META_PAD, 1)
    um = um[:N_META]
    km = km[:, :N_META, :]
    vtm = vtm[:, :, :N_META]

    ya = _attention(qt, kk, vt, km, vtm, b, 512, 512)

    t_chunk = SSM_CHUNK
    hgrp = SSM_GROUP
    n = t_chunk * hgrp
    arow = jnp.stack([ssm_a_re[l], ssm_a_im[l]], axis=2)
    ldt = ssm_log_dt[l][:, :, None, None]
    bt = jnp.stack([ssm_b_re[l], ssm_b_im[l]], axis=2)
    bt = jnp.swapaxes(bt, 3, 4)
    ct = jnp.stack([ssm_c_re[l], ssm_c_im[l]], axis=2)
    ct = jnp.swapaxes(ct, 3, 4)
    mm, ww, vv, at = _ssm_prep(arow, ldt, bt, ct)
    nc = s // t_chunk
    um_g = um.reshape(N_META, g, hgrp).transpose(1, 0, 2).reshape(g, 1, N_META * hgrp)
    um_g = jnp.pad(um_g, ((0, 0), (0, 7), (n - N_META * hgrp, 0)))
    dvec = jnp.tile(ssm_d[l], (1, t_chunk))[:, None, :]
    ys = _ssm(u2.reshape(b, s, sw), um_g, mm, ww, vv, at, dvec).reshape(rows, sw)

    wr_hi = w_router[l].astype(BF16)
    wr_lo = (w_router[l] - wr_hi.astype(F32)).astype(BF16)
    lane_pad = ((0, 0), (0, ROUTER_LANES - N_EXPERTS))
    wr = jnp.concatenate([jnp.pad(wr_hi, lane_pad), jnp.pad(wr_lo, lane_pad)], axis=1)
    h1, tok, afft = _merge(
        x2, ya, ys, w_gates, w_glu[l].astype(BF16), b_glu[l][None, :],
        w_attn_br[l].astype(BF16), w_ssm_br[l].astype(BF16), w_o[l].astype(BF16),
        lng, lnb, ln1_g[l][None, :], ln1_b[l][None, :], wr, N_EXPERTS, 512)

    tri = jnp.asarray(np.triu(np.ones((ROUTE_TILE, ROUTE_TILE), np.float32)), BF16)
    scl, lo_f = _route(afft, tri, b, cap)
    nt = s // ROUTE_TILE
    lo3 = jnp.pad(lo_f.astype(jnp.int32), ((0, 0), (0, 0), (0, 1)), constant_values=cap)
    seg = lo3[:, :, 1:] - lo3[:, :, :-1]
    multi = lambda align: ((lo3[:, :, :-1] & (align - 1)) + seg + (PIECE - 1)) // PIECE > 1
    lo_i = jnp.concatenate([lo3.reshape(-1),
                            jnp.any(multi(8), axis=2).astype(jnp.int32).reshape(-1),
                            jnp.any(multi(16), axis=1).astype(jnp.int32).reshape(-1)])
    ye = _ffn(lo_i, tok.reshape(b, s, d), scl.reshape(b, N_EXPERTS, nt, ROUTE_TILE),
              afft.reshape(N_EXPERTS, b, nt, ROUTE_TILE), w_gate_e, w_up_e, w_down_e, l, cap, 512)
    out = _combine(lo_i, h1.reshape(b, s, d), jnp.swapaxes(scl, 1, 2), ye,
                   ln2_g[l][None, :], ln2_b[l][None, :], cap)
    return out
```

```python
import functools
import math

import numpy as np
import jax
import jax.numpy as jnp
from jax import lax
from jax.experimental import pallas as pl
from jax.experimental.pallas import tpu as pltpu

F32 = jnp.float32
BF16 = jnp.bfloat16

N_META = 16
GRID_W = 64
N_HEADS = 8
N_KV_HEADS = 2
HEAD_DIM = 64
ROPE_THETA = 10000.0
SSM_GROUP = 16
SSM_STATE = 64
N_EXPERTS = 16
CAPACITY_FACTOR = 2
LN_EPS = 1e-5
QK_EPS = 1e-6
DEPTH = 1
DEEPNORM_ALPHA = (2.0 * DEPTH) ** 0.25

SSM_CHUNK = 32
SSM_GROUPS_TOGETHER = 4
ROUTE_TILE = 256
ROUTER_LANES = 128
PIECE_SHIFT = 6
PIECE = 1 << PIECE_SHIFT
META_PAD = 128
V_ROWS = HEAD_DIM + 16
VMEM_LIMIT = 56 * 1024 * 1024


def _ln(x, g, b):
    mu = jnp.mean(x, axis=-1, keepdims=True)
    xc = x - mu
    var = jnp.mean(xc * xc, axis=-1, keepdims=True)
    return xc * lax.rsqrt(var + LN_EPS) * g + b


def _sigmoid(x):
    return 1.0 / (1.0 + jnp.exp(-x))


def _split(t):
    hi = t.astype(BF16)
    lo = (t - hi.astype(F32)).astype(BF16)
    return hi, lo


def _dot(a, b):
    return jnp.dot(a, b, preferred_element_type=F32)


def _dot_nt(a, b):
    return lax.dot_general(a, b, (((1,), (1,)), ((), ())), preferred_element_type=F32)


def _inproj_kernel(x_ref, cosr_ref, sinr_ref, cost_ref, sint_ref, wt_ref, wn_ref, lng_ref, lnb_ref,
                   qg_ref, kg_ref, bd_ref, qt_ref, k_ref, vt_ref, u_ref):
    aw = N_HEADS * HEAD_DIM
    kw = N_KV_HEADS * HEAD_DIM
    half = HEAD_DIM // 2
    h = _ln(x_ref[...], lng_ref[...], lnb_ref[...])
    hb = h.astype(BF16)
    pt = _dot_nt(wt_ref[...], hb)
    pn = _dot(hb, wn_ref[...])
    tm = hb.shape[0]

    qt = pt[0:aw].reshape(N_HEADS, HEAD_DIM, tm)
    ms = jnp.mean(qt * qt, axis=1, keepdims=True)
    qn = qt * lax.rsqrt(ms + QK_EPS) * qg_ref[...]
    x0 = qn[:, 0:half, :]
    x1 = qn[:, half:, :]
    c = cost_ref[...][None]
    s = sint_ref[...][None]
    qr = jnp.concatenate([x0 * c - x1 * s, x0 * s + x1 * c], axis=1)
    qt_ref[...] = qr.reshape(aw, tm).astype(BF16)
    vrow = lax.broadcasted_iota(jnp.int32, (N_KV_HEADS, V_ROWS - HEAD_DIM, tm), 1)
    vt_ref[...] = jnp.concatenate([pt[aw:aw + kw].reshape(N_KV_HEADS, HEAD_DIM, tm),
                                   jnp.where(vrow == 0, 1.0, 0.0)], axis=1).astype(BF16)

    kk = pn[:, 0:kw]
    hi, lo = _split(kk * kk)
    bd = bd_ref[...]
    msk = _dot(hi, bd) + _dot(lo, bd)
    kn = kk * lax.rsqrt(msk + QK_EPS) * kg_ref[...]
    lane = lax.broadcasted_iota(jnp.int32, kn.shape, 1)
    first = (lane & (HEAD_DIM - 1)) < half
    partner = jnp.where(first, pltpu.roll(kn, kw - half, 1), pltpu.roll(kn, half, 1))
    kr = (kn * cosr_ref[...] + partner * sinr_ref[...]).astype(BF16)
    for g in range(N_KV_HEADS):
        k_ref[g] = kr[:, g * HEAD_DIM:(g + 1) * HEAD_DIM]
    u_ref[...] = pn[:, kw:]


def _inproj(x2, cosr, sinr, cost, sint, wt, wn, lng, lnb, qg3, kg, bd, tm, n_tab_blocks):
    rows, d = x2.shape
    aw = N_HEADS * HEAD_DIM
    kw = N_KV_HEADS * HEAD_DIM
    uw = wn.shape[1] - kw
    half = HEAD_DIM // 2
    full = lambda a: pl.BlockSpec(a.shape, lambda i: (0,) * a.ndim)
    return pl.pallas_call(
        _inproj_kernel,
        grid=(rows // tm,),
        in_specs=[
            pl.BlockSpec((tm, d), lambda i: (i, 0)),
            pl.BlockSpec((tm, kw), lambda i: (i % n_tab_blocks, 0)),
            pl.BlockSpec((tm, kw), lambda i: (i % n_tab_blocks, 0)),
            pl.BlockSpec((half, tm), lambda i: (0, i % n_tab_blocks)),
            pl.BlockSpec((half, tm), lambda i: (0, i % n_tab_blocks)),
            full(wt), full(wn), full(lng), full(lnb), full(qg3), full(kg), full(bd),
        ],
        out_specs=[
            pl.BlockSpec((aw, tm), lambda i: (0, i)),
            pl.BlockSpec((N_KV_HEADS, tm, HEAD_DIM), lambda i: (0, i, 0)),
            pl.BlockSpec((N_KV_HEADS, V_ROWS, tm), lambda i: (0, 0, i)),
            pl.BlockSpec((tm, uw), lambda i: (i, 0)),
        ],
        out_shape=[
            jax.ShapeDtypeStruct((aw, rows), BF16),
            jax.ShapeDtypeStruct((N_KV_HEADS, rows, HEAD_DIM), BF16),
            jax.ShapeDtypeStruct((N_KV_HEADS, V_ROWS, rows), BF16),
            jax.ShapeDtypeStruct((rows, uw), F32),
        ],
        compiler_params=pltpu.CompilerParams(dimension_semantics=("arbitrary",),
                                             vmem_limit_bytes=VMEM_LIMIT),
        name="inproj",
    )(x2, cosr, sinr, cost, sint, wt, wn, lng, lnb, qg3, kg, bd)


def _attn_kernel(qt_ref, k0_ref, k1_ref, vta_ref, vtb_ref, km_ref, vtm_ref, o_ref, m_sc, acc_sc, s_sc, mb_sc):
    j = pl.program_id(2)
    last = pl.num_programs(2) - 1
    grp = N_HEADS // N_KV_HEADS
    heads = lambda h: slice(h * HEAD_DIM, (h + 1) * HEAD_DIM)

    def score(k_ref, slot, h):
        s = _dot(k_ref[h // grp], qt_ref[heads(h), :])
        s_sc[slot, h] = s
        mb_sc[slot, h:h + 1, :] = jnp.max(s, axis=0, keepdims=True)

    def softmax_pv(h, s, m_blk, vt_g):
        m_prev = m_sc[h:h + 1, :]
        m_new = jnp.maximum(m_prev, m_blk)
        alpha = jnp.exp2(m_prev - m_new)
        p = jnp.exp2(s - m_new).astype(BF16)
        acc_sc[h] = alpha * acc_sc[h] + _dot(vt_g, p)
        m_sc[h:h + 1, :] = m_new

    def consume(vt_ref, slot, h):
        softmax_pv(h, s_sc[slot, h], mb_sc[slot, h:h + 1, :], vt_ref[h // grp])

    @pl.when(j == 0)
    def _first():
        m_sc[...] = jnp.full(m_sc.shape, -jnp.inf, F32)
        acc_sc[...] = jnp.zeros(acc_sc.shape, F32)
        meta = [_dot(km_ref[h // grp], qt_ref[heads(h), :]) for h in range(N_HEADS)]
        for h in range(N_HEADS):
            score(k0_ref, 0, h)
            softmax_pv(h, meta[h], jnp.max(meta[h], axis=0, keepdims=True), vtm_ref[h // grp])
        for h in range(N_HEADS):
            score(k1_ref, 1, h)
            consume(vtb_ref, 0, h)

    @pl.when((j > 0) & (j < last))
    def _middle():
        for h in range(N_HEADS):
            score(k0_ref, 0, h)
            consume(vta_ref, 1, h)
        for h in range(N_HEADS):
            score(k1_ref, 1, h)
            consume(vtb_ref, 0, h)

    @pl.when(j == last)
    def _last():
        for h in range(N_HEADS):
            consume(vta_ref, 1, h)
        acc = acc_sc[...]
        tq = acc.shape[2]
        out_t = (acc[:, 0:HEAD_DIM, :] / acc[:, HEAD_DIM:HEAD_DIM + 1, :]).reshape(N_HEADS * HEAD_DIM, tq)
        o_ref[...] = out_t.T.astype(BF16)


def _attention(qt, kk, vt, km, vtm, nb, tq, tk):
    aw, rows = qt.shape
    s = rows // nb
    nq = s // tq
    nkb = s // tk
    kblk = lambda f: pl.BlockSpec((N_KV_HEADS, tk, HEAD_DIM),
                                  lambda bi, qi, j: (0, bi * nkb + jnp.clip(f(j), 0, nkb - 1), 0))
    vblk = lambda f: pl.BlockSpec((N_KV_HEADS, V_ROWS, tk),
                                  lambda bi, qi, j: (0, 0, bi * nkb + jnp.clip(f(j), 0, nkb - 1)))
    return pl.pallas_call(
        _attn_kernel,
        grid=(nb, nq, nkb // 2 + 1),
        in_specs=[
            pl.BlockSpec((aw, tq), lambda bi, qi, j: (0, bi * nq + qi)),
            kblk(lambda j: 2 * j), kblk(lambda j: 2 * j + 1),
            vblk(lambda j: 2 * j - 1), vblk(lambda j: 2 * j),
            pl.BlockSpec(km.shape, lambda bi, qi, j: (0, 0, 0)),
            pl.BlockSpec(vtm.shape, lambda bi, qi, j: (0, 0, 0)),
        ],
        out_specs=pl.BlockSpec((tq, aw), lambda bi, qi, j: (bi * nq + qi, 0)),
        out_shape=jax.ShapeDtypeStruct((rows, aw), BF16),
        scratch_shapes=[
            pltpu.VMEM((N_HEADS, tq), F32),
            pltpu.VMEM((N_HEADS, V_ROWS, tq), F32),
            pltpu.VMEM((2, N_HEADS, tk, tq), F32),
            pltpu.VMEM((2, N_HEADS, tq), F32),
        ],
        compiler_params=pltpu.CompilerParams(
            dimension_semantics=("arbitrary", "arbitrary", "arbitrary"),
            vmem_limit_bytes=VMEM_LIMIT),
        name="attn",
    )(qt, kk, kk, vt, vt, km, vtm)


def _ssm_prep_kernel(*refs):
    for d in range(2):
        _ssm_prep_direction(d, *refs)


def _ssm_prep_direction(d, arow_ref, ldt_ref, bt_ref, ct_ref, m_ref, w_ref, v_ref, at_ref):
    t_chunk = SSM_CHUNK
    shift = int(math.log2(SSM_GROUP))
    df = float(d)
    dt = jnp.exp(ldt_ref[d, 0])

    def abar(ar, ai):
        mag = jnp.exp(ar * dt)
        ang = ai * dt
        return mag * jnp.cos(ang), mag * jnp.sin(ang)

    nbits = int(math.log2(t_chunk))

    def ipow(br, bi, e):
        pr = pi = None
        for k in range(nbits):
            bit = ((e >> k) & 1) == 1
            if pr is None:
                pr, pi = jnp.where(bit, br, 1.0), jnp.where(bit, bi, 0.0)
            else:
                pr, pi = jnp.where(bit, pr * br - pi * bi, pr), jnp.where(bit, pr * bi + pi * br, pi)
            br, bi = br * br - bi * bi, 2.0 * br * bi
        return pr, pi, br, bi

    ar_r = arow_ref[d, 0, 0:1, :]
    ai_r = arow_ref[d, 0, 1:2, :]
    abr, abi = abar(ar_r, ai_r)
    nr = abr - 1.0
    ni = abi
    den = ar_r * ar_r + ai_r * ai_r
    cr = (nr * ar_r + ni * ai_r) / den
    ci = (ni * ar_r - nr * ai_r) / den
    nstate = bt_ref.shape[4]
    nrow = t_chunk * SSM_GROUP

    def tile_rows(x):
        return jnp.broadcast_to(x[None], (t_chunk, SSM_GROUP, nstate)).reshape(nrow, nstate)

    btr = tile_rows(bt_ref[d, 0, 0])
    bti = tile_rows(bt_ref[d, 0, 1])
    bbr = cr * btr - ci * bti
    bbi = cr * bti + ci * btr
    jr = (lax.broadcasted_iota(jnp.int32, (nrow, 1), 0) >> shift).astype(F32)
    lj = jr + df * ((t_chunk - 1) - 2.0 * jr)
    step = lax.broadcasted_iota(jnp.int32, (t_chunk, 1), 0)
    lstep = (t_chunk - 1) - step if d == 1 else step

    def rep_rows(x):
        return jnp.broadcast_to(x[:, None, :], (t_chunk, SSM_GROUP, nstate)).reshape(nrow, nstate)

    mag2 = abr * abr + abi * abi
    enr, eni, _, _ = ipow(abr / mag2, -abi / mag2, lstep)
    enr, eni = rep_rows(enr), rep_rows(eni)
    bmr = enr * bbr - eni * bbi
    bmi = enr * bbi + eni * bbr
    ewr, ewi, atr, ati = ipow(abr, abi, (t_chunk - 1) - lstep)
    ewr, ewi = rep_rows(ewr), rep_rows(ewi)
    w_ref[d, 0] = jnp.concatenate([ewr * bbr - ewi * bbi, ewr * bbi + ewi * bbr], axis=1).astype(BF16)
    at_ref[d, 0] = jnp.concatenate([atr, ati], axis=1)

    ncol = t_chunk * SSM_GROUP
    lane = lax.broadcasted_iota(jnp.int32, (1, ncol), 1)
    tci = lane >> shift
    tile = jnp.where(lax.broadcasted_iota(jnp.int32, (SSM_GROUP, ncol), 0) == (lane & (SSM_GROUP - 1)),
                     1.0, 0.0).astype(BF16)

    def tile_cols(x):
        hi, lo = _split(x)
        lo2 = (x - hi.astype(F32) - lo.astype(F32)).astype(BF16)
        return _dot(hi, tile) + _dot(lo, tile) + _dot(lo2, tile)

    ctr = tile_cols(ct_ref[d, 0, 0])
    cti = tile_cols(ct_ref[d, 0, 1])
    tc = tci.astype(F32)
    lt = tc + df * ((t_chunk - 1) - 2.0 * tc)
    stepl = lax.broadcasted_iota(jnp.int32, (1, t_chunk), 1)
    lstepl = (t_chunk - 1) - stepl if d == 1 else stepl
    diag = (lax.broadcasted_iota(jnp.int32, (nstate, nstate), 0)
            == lax.broadcasted_iota(jnp.int32, (nstate, nstate), 1))
    abr_c = jnp.sum(jnp.where(diag, abr, 0.0), axis=1, keepdims=True)
    abi_c = jnp.sum(jnp.where(diag, abi, 0.0), axis=1, keepdims=True)
    rep = jnp.where(lax.broadcasted_iota(jnp.int32, (t_chunk, ncol), 0) == tci, 1.0, 0.0).astype(BF16)

    def rep_cols(x):
        hi, lo = _split(x)
        lo2 = (x - hi.astype(F32) - lo.astype(F32)).astype(BF16)
        return _dot(hi, rep) + _dot(lo, rep) + _dot(lo2, rep)

    ecr, eci = [rep_cols(x) for x in ipow(abr_c, abi_c, lstepl)[:2]]
    cmr = ctr * ecr - cti * eci
    cmi = ctr * eci + cti * ecr
    lhs_hi, lhs_lo = _split(jnp.concatenate([bmr, -bmi], axis=1))
    rhs_hi, rhs_lo = _split(jnp.concatenate([cmr, cmi], axis=0))
    m = _dot(lhs_hi, rhs_hi) + _dot(lhs_hi, rhs_lo) + _dot(lhs_lo, rhs_hi)
    m_ref[d, 0] = jnp.where(lj <= lt, m, 0.0).astype(BF16)
    c1r = cmr * abr_c - cmi * abi_c
    c1i = cmr * abi_c + cmi * abr_c
    v_ref[d, 0] = jnp.concatenate([c1r, -c1i], axis=0).astype(BF16)


def _ssm_prep(arow, ldt, bt, ct):
    nd, g = arow.shape[0], arow.shape[1]
    p = SSM_STATE
    n = SSM_CHUNK * SSM_GROUP
    blk = lambda a: pl.BlockSpec((nd, 1) + a.shape[2:], lambda gi: (0, gi) + (0,) * (a.ndim - 2))
    return pl.pallas_call(
        _ssm_prep_kernel,
        grid=(g,),
        in_specs=[blk(arow), blk(ldt), blk(bt), blk(ct)],
        out_specs=[
            pl.BlockSpec((nd, 1, n, n), lambda gi: (0, gi, 0, 0)),
            pl.BlockSpec((nd, 1, n, 2 * p), lambda gi: (0, gi, 0, 0)),
            pl.BlockSpec((nd, 1, 2 * p, n), lambda gi: (0, gi, 0, 0)),
            pl.BlockSpec((nd, 1, 1, 2 * p), lambda gi: (0, gi, 0, 0)),
        ],
        out_shape=[
            jax.ShapeDtypeStruct((nd, g, n, n), BF16),
            jax.ShapeDtypeStruct((nd, g, n, 2 * p), BF16),
            jax.ShapeDtypeStruct((nd, g, 2 * p, n), BF16),
            jax.ShapeDtypeStruct((nd, g, 1, 2 * p), F32),
        ],
        compiler_params=pltpu.CompilerParams(dimension_semantics=("arbitrary",)),
        name="ssm_prep",
    )(arow, ldt, bt, ct)


def _chunk_carry(chains):
    p = SSM_STATE
    z0, a0 = chains[0][0], chains[0][1]
    nc = z0.shape[0]
    row = lax.broadcasted_iota(jnp.int32, z0.shape, 0)
    is_re = lax.broadcasted_iota(jnp.int32, a0.shape, 1) < p
    sign = jnp.where(is_re, -1.0, 1.0)

    def parts(ap):
        sw = pltpu.roll(ap, p, 1)
        return jnp.where(is_re, ap, sw), sign * jnp.where(is_re, sw, ap)

    def cmul(x, ar_full, ai_sgn):
        return x * ar_full + pltpu.roll(x, p, 1) * ai_sgn

    coef = [parts(a) for _, a, _, _ in chains]
    firsts = [(nc - 1) if rev else 0 for _, _, _, rev in chains]
    es = [z + jnp.where(row == f, cmul(s0, *c), 0.0) for (z, _, s0, _), f, c in zip(chains, firsts, coef)]
    k = 1
    while k < nc:
        for i, (_, _, _, rev) in enumerate(chains):
            if rev:
                sh = jnp.where(row < nc - k, pltpu.roll(es[i], nc - k, 0), 0.0)
            else:
                sh = jnp.where(row >= k, pltpu.roll(es[i], k, 0), 0.0)
            es[i] = es[i] + cmul(sh, *coef[i])
        k *= 2
        if k < nc:
            coef = [parts(cmul(jnp.where(is_re, ar, sign * ai), ar, ai)) for ar, ai in coef]
    outs = []
    for e, (_, _, s0, rev), f in zip(es, chains, firsts):
        outs.append(jnp.where(row == f, s0, pltpu.roll(e, nc - 1 if rev else 1, 0)))
    return outs


def _ssm_kernel(u_ref, um_ref, m_ref, w_ref, v_ref, at_ref, dvec_ref, y_ref, ug_sc, yg_sc):
    p = SSM_STATE
    t_chunk = SSM_CHUNK
    hgrp = SSM_GROUP
    gb = u_ref.shape[2] // hgrp
    per_tile = u_ref.shape[2] // hgrp
    nc = u_ref.shape[1] // t_chunk
    lane_blk = lax.broadcasted_iota(jnp.int32, (nc, u_ref.shape[2]), 1) >> int(math.log2(hgrp))

    width = u_ref.shape[2]

    def block_transpose(arrs):
        arrs = list(arrs)
        d = len(arrs) // 2
        while d >= 1:
            low_bit = (lane_blk & d) == 0
            for i in range(len(arrs)):
                if i & d == 0:
                    lo, hi = arrs[i], arrs[i + d]
                    arrs[i] = jnp.where(low_bit, lo, pltpu.roll(hi, d * hgrp, 1))
                    arrs[i + d] = jnp.where(low_bit, pltpu.roll(lo, width - d * hgrp, 1), hi)
            d //= 2
        return arrs

    for q in range(t_chunk // per_tile):
        steps = [u_ref[0, pl.ds(q * per_tile + r, nc, stride=t_chunk), :] for r in range(per_tile)]
        for g, tile in enumerate(block_transpose(steps)):
            ug_sc[g, :, q * width:(q + 1) * width] = tile

    def groups(i, carry):
        gs = [i * SSM_GROUPS_TOGETHER + k for k in range(SSM_GROUPS_TOGETHER)]
        us = [ug_sc[g] for g in gs]
        ubs = [u.astype(BF16) for u in us]
        yin = [[_dot(ub, m_ref[d, g]) for d in range(2)] for g, ub in zip(gs, ubs)]
        chains = []
        for g, ub in zip(gs, ubs):
            for d in range(2):
                if d == 0:
                    s0 = _dot(um_ref[g].astype(BF16), w_ref[0, g])[0:1, :]
                else:
                    s0 = jnp.zeros((1, 2 * p), F32)
                chains.append((_dot(ub, w_ref[d, g]), at_ref[d, g], s0, d == 1))
        s_in = _chunk_carry(chains)
        for k, g in enumerate(gs):
            y = dvec_ref[g] * us[k] + yin[k][0] + yin[k][1]
            for d in range(2):
                y = y + _dot(s_in[2 * k + d].astype(BF16), v_ref[d, g])
            yg_sc[g] = y
        return carry

    lax.fori_loop(0, gb // SSM_GROUPS_TOGETHER, groups, 0)

    for q in range(t_chunk // per_tile):
        srcs = [yg_sc[g, :, q * width:(q + 1) * width] for g in range(gb)]
        for r, row in enumerate(block_transpose(srcs)):
            y_ref[0, pl.ds(q * per_tile + r, nc, stride=t_chunk), :] = row


def _ssm(u3, um2, m, w, v, at, dvec):
    b, s, sw = u3.shape
    g = sw // SSM_GROUP
    n = SSM_CHUNK * SSM_GROUP
    p = SSM_STATE
    lanes = 128
    gb = lanes // SSM_GROUP
    return pl.pallas_call(
        _ssm_kernel,
        grid=(sw // lanes, b),
        in_specs=[
            pl.BlockSpec((1, s, lanes), lambda gi, bi: (bi, 0, gi)),
            pl.BlockSpec((gb, 8, n), lambda gi, bi: (gi, 0, 0)),
            pl.BlockSpec((2, gb, n, n), lambda gi, bi: (0, gi, 0, 0)),
            pl.BlockSpec((2, gb, n, 2 * p), lambda gi, bi: (0, gi, 0, 0)),
            pl.BlockSpec((2, gb, 2 * p, n), lambda gi, bi: (0, gi, 0, 0)),
            pl.BlockSpec((2, gb, 1, 2 * p), lambda gi, bi: (0, gi, 0, 0)),
            pl.BlockSpec((gb, 1, n), lambda gi, bi: (gi, 0, 0)),
        ],
        out_specs=pl.BlockSpec((1, s, lanes), lambda gi, bi: (bi, 0, gi)),
        out_shape=jax.ShapeDtypeStruct((b, s, sw), F32),
        scratch_shapes=[pltpu.VMEM((gb, s // SSM_CHUNK, n), F32), pltpu.VMEM((gb, s // SSM_CHUNK, n), F32)],
        compiler_params=pltpu.CompilerParams(dimension_semantics=("arbitrary", "arbitrary"),
                                             vmem_limit_bytes=VMEM_LIMIT),
        name="ssm",
    )(u3, um2, m, w, v, at, dvec)


def _merge_kernel(x_ref, ya_ref, ys_ref, wgt_ref, wglu_ref, bglu_ref, wab_ref, wsb_ref, wo_ref,
                  lng_ref, lnb_ref, l1g_ref, l1b_ref, wr_ref,
                  h1_ref, tok_ref, afft_ref):
    d = x_ref.shape[1]
    h = _ln(x_ref[...], lng_ref[...], lnb_ref[...])
    gates = _dot(h.astype(BF16), wgt_ref[...])
    ga = _sigmoid(gates[:, 0:d])
    gs = _sigmoid(gates[:, d:2 * d])
    ys = ys_ref[...]
    y = 0.5 * ys * (1.0 + jnp.tanh(math.sqrt(2.0 / math.pi) * (ys + 0.044715 * (ys * ys * ys))))
    yg = y * _sigmoid(_dot(y.astype(BF16), wglu_ref[...]) + bglu_ref[...])
    merged = ga * _dot(ya_ref[...], wab_ref[...]) + gs * _dot(yg.astype(BF16), wsb_ref[...])
    h1 = _ln(DEEPNORM_ALPHA * h + _dot(merged.astype(BF16), wo_ref[...]), l1g_ref[...], l1b_ref[...])
    h1_ref[...] = h1
    hi, lo = _split(h1)
    tok_ref[...] = hi
    ne = afft_ref.shape[0]
    cross = _dot(hi, wr_ref[...])
    logits = cross[:, 0:ROUTER_LANES] + cross[:, ROUTER_LANES:] + _dot(lo, wr_ref[:, 0:ROUTER_LANES])
    lane = lax.broadcasted_iota(jnp.int32, logits.shape, 1)
    logits = jnp.where(lane < ne, logits, -jnp.inf)
    ex = jnp.exp(logits - jnp.max(logits, axis=1, keepdims=True))
    aff = ex / jnp.sum(ex, axis=1, keepdims=True)
    afft_ref[...] = aff.T[0:ne, :]


def _merge(x2, ya, ys, wgt, wglu, bglu, wab, wsb, wo, lng, lnb, l1g, l1b, wr, ne, tm):
    rows, d = x2.shape
    full = lambda a: pl.BlockSpec(a.shape, lambda i: (0,) * a.ndim)
    return pl.pallas_call(
        _merge_kernel,
        grid=(rows // tm,),
        in_specs=[
            pl.BlockSpec((tm, d), lambda i: (i, 0)),
            pl.BlockSpec((tm, ya.shape[1]), lambda i: (i, 0)),
            pl.BlockSpec((tm, ys.shape[1]), lambda i: (i, 0)),
            full(wgt), full(wglu), full(bglu), full(wab), full(wsb), full(wo),
            full(lng), full(lnb), full(l1g), full(l1b), full(wr),
        ],
        out_specs=[
            pl.BlockSpec((tm, d), lambda i: (i, 0)),
            pl.BlockSpec((tm, d), lambda i: (i, 0)),
            pl.BlockSpec((ne, tm), lambda i: (0, i)),
        ],
        out_shape=[
            jax.ShapeDtypeStruct((rows, d), F32),
            jax.ShapeDtypeStruct((rows, d), BF16),
            jax.ShapeDtypeStruct((ne, rows), F32),
        ],
        compiler_params=pltpu.CompilerParams(dimension_semantics=("arbitrary",),
                                             vmem_limit_bytes=VMEM_LIMIT),
        name="merge",
    )(x2, ya, ys, wgt, wglu, bglu, wab, wsb, wo, lng, lnb, l1g, l1b, wr)


def _route_kernel(afft_ref, tri_ref, scl_ref, lo_ref, *, cap):
    aff = afft_ref[...]
    ne, s = aff.shape
    capf = float(cap)

    def as_float(bits):
        return lax.bitcast_convert_type(bits, F32)

    def search(i, t):
        cand = t | jnp.left_shift(jnp.int32(1), 30 - i)
        cnt = jnp.sum(jnp.where(aff >= as_float(cand), 1.0, 0.0), axis=1, keepdims=True)
        return jnp.where(cnt >= capf, cand, t)

    thr_bits = lax.fori_loop(0, 31, search, jnp.zeros((ne, 1), jnp.int32))
    gt = aff >= as_float(thr_bits + 1)
    eq = (aff >= as_float(thr_bits)) & jnp.logical_not(gt)
    need = capf - jnp.sum(jnp.where(gt, 1.0, 0.0), axis=1, keepdims=True)
    tri = tri_ref[...]
    rt = ROUTE_TILE
    nt = s // rt
    col = lax.broadcasted_iota(jnp.int32, (ne, nt), 1)
    carry_eq = jnp.zeros((ne, 1), F32)
    carry_sel = jnp.zeros((ne, 1), F32)
    lo_val = jnp.zeros((ne, nt), F32)
    for t in range(nt):
        sl = slice(t * rt, (t + 1) * rt)
        eq_b = eq[:, sl]
        ceq = _dot(jnp.where(eq_b, 1.0, 0.0).astype(BF16), tri)
        sel_b = gt[:, sl] | (eq_b & ((ceq + carry_eq) <= need))
        carry_eq = carry_eq + ceq[:, rt - 1:rt]
        csel = _dot(jnp.where(sel_b, 1.0, 0.0).astype(BF16), tri)
        scl_ref[0, :, sl] = jnp.where(sel_b, csel, 0.0)
        lo_val = jnp.where(col == t, carry_sel, lo_val)
        carry_sel = carry_sel + csel[:, rt - 1:rt]
    lo_ref[0] = lo_val


def _route(afft, tri, nb, cap):
    ne, rows = afft.shape
    s = rows // nb
    nt = s // ROUTE_TILE
    return pl.pallas_call(
        functools.partial(_route_kernel, cap=cap),
        grid=(nb,),
        in_specs=[
            pl.BlockSpec((ne, s), lambda b: (0, b)),
            pl.BlockSpec(tri.shape, lambda b: (0, 0)),
        ],
        out_specs=[
            pl.BlockSpec((1, ne, s), lambda b: (b, 0, 0)),
            pl.BlockSpec((1, ne, nt), lambda b: (b, 0, 0)),
        ],
        out_shape=[
            jax.ShapeDtypeStruct((nb, ne, s), F32),
            jax.ShapeDtypeStruct((nb, ne, nt), F32),
        ],
        compiler_params=pltpu.CompilerParams(dimension_semantics=("arbitrary",)),
        name="route",
    )(afft, tri)


def _ffn_kernel(lo_ref, t_ref, scl_ref, aff_ref, wg_ref, wu_ref, wd_ref, ye_ref, xe_sc, xb_sc, gate_sc, acc_sc,
                *, cap, flag_off):
    fc = pl.program_id(2)

    @pl.when(fc == 0)
    def _gather():
        _ffn_gather(lo_ref, t_ref, scl_ref, aff_ref, xe_sc, xb_sc, gate_sc, cap=cap, flag_off=flag_off)
        acc_sc[...] = jnp.zeros(acc_sc.shape, F32)

    xb = xb_sc[...]
    g = _dot(xb, wg_ref[0, 0].astype(BF16))
    u = _dot(xb, wu_ref[0, 0].astype(BF16))
    hh = (g * _sigmoid(g)) * u
    acc_sc[...] += _dot(hh.astype(BF16), wd_ref[0, 0].astype(BF16))

    @pl.when(fc == pl.num_programs(2) - 1)
    def _emit():
        ye_ref[0, 0] = (acc_sc[...] * gate_sc[0:cap, :]).astype(BF16)


def _ffn_gather(lo_ref, t_ref, scl_ref, aff_ref, xe_sc, xb_sc, gate_sc, *, cap, flag_off):
    b = pl.program_id(0)
    e = pl.program_id(1)
    ne = pl.num_programs(1)
    rt = ROUTE_TILE
    nt = t_ref.shape[1] // rt
    xe_sc[...] = jnp.zeros(xe_sc.shape, F32)
    gate_sc[...] = jnp.zeros(gate_sc.shape, F32)
    r = lax.broadcasted_iota(jnp.int32, (PIECE, rt), 0).astype(F32)

    def tile_info(tau):
        base = (b * ne + e) * (nt + 1) + tau
        lo = lo_ref[base]
        n_pieces = jnp.right_shift((lo & 7) + (lo_ref[base + 1] - lo) + (PIECE - 1), PIECE_SHIFT)
        return lo, n_pieces

    def add_piece(tau, lo, p):
        off = lo & 7
        scl = scl_ref[0, 0, pl.ds(tau, 1), :]
        tok = t_ref[0, pl.ds(pl.multiple_of(tau * rt, rt), rt), :]
        shift = (off - 1 - p * PIECE).astype(F32)
        chosen = (scl > 0.0) & ((scl + shift) == r)
        rows = pl.ds(pl.multiple_of(lo - off + p * PIECE, 8), PIECE)
        xe_sc[rows, :] += _dot(jnp.where(chosen, 1.0, 0.0).astype(BF16), tok)
        gate_sc[rows, :] += jnp.sum(jnp.where(chosen, aff_ref[0, 0, pl.ds(tau, 1), :], 0.0), axis=1, keepdims=True)

    def first_piece(tau, carry):
        add_piece(tau, tile_info(tau)[0], 0)
        return carry

    def more_pieces(tau, carry):
        lo, n_pieces = tile_info(tau)
        lax.fori_loop(1, n_pieces, lambda p, c: (add_piece(tau, lo, p), c)[1], 0)
        return carry

    lax.fori_loop(0, nt, first_piece, 0, unroll=4)

    @pl.when(lo_ref[flag_off + b * ne + e] > 0)
    def _rare():
        lax.fori_loop(0, nt, more_pieces, 0)

    xb_sc[...] = xe_sc[0:cap, :].astype(BF16)


def _ffn(lo_i, t, scl4, aff4, wg, wu, wd, layer, cap, f_chunk):
    b, s, d = t.shape
    _, ne, _, f = wg.shape
    nt = s // ROUTE_TILE
    grid_spec = pltpu.PrefetchScalarGridSpec(
        num_scalar_prefetch=1,
        grid=(b, ne, f // f_chunk),
        in_specs=[
            pl.BlockSpec((1, s, d), lambda bi, ei, fi, lo: (bi, 0, 0), pipeline_mode=pl.Buffered(1)),
            pl.BlockSpec((1, 1, nt, ROUTE_TILE), lambda bi, ei, fi, lo: (bi, ei, 0, 0)),
            pl.BlockSpec((1, 1, nt, ROUTE_TILE), lambda bi, ei, fi, lo: (ei, bi, 0, 0)),
            pl.BlockSpec((1, 1, d, f_chunk), lambda bi, ei, fi, lo: (layer, ei, 0, fi)),
            pl.BlockSpec((1, 1, d, f_chunk), lambda bi, ei, fi, lo: (layer, ei, 0, fi)),
            pl.BlockSpec((1, 1, f_chunk, d), lambda bi, ei, fi, lo: (layer, ei, fi, 0)),
        ],
        out_specs=pl.BlockSpec((1, 1, cap, d), lambda bi, ei, fi, lo: (bi, ei, 0, 0)),
        scratch_shapes=[pltpu.VMEM((cap + PIECE, d), F32), pltpu.VMEM((cap, d), BF16),
                        pltpu.VMEM((cap + PIECE, 1), F32), pltpu.VMEM((cap, d), F32)],
    )
    return pl.pallas_call(
        functools.partial(_ffn_kernel, cap=cap, flag_off=b * ne * (nt + 1)),
        grid_spec=grid_spec,
        out_shape=jax.ShapeDtypeStruct((b, ne, cap, d), BF16),
        compiler_params=pltpu.CompilerParams(dimension_semantics=("arbitrary", "arbitrary", "arbitrary"),
                                             vmem_limit_bytes=VMEM_LIMIT),
        name="ffn",
    )(lo_i, t, scl4, aff4, wg, wu, wd)


def _combine_kernel(lo_ref, h1_ref, sclt_ref, ye_hbm, g_ref, b_ref, o_ref, win, xwin, acc_sc, sem, xsem,
                    *, cap, flag_off):
    b = pl.program_id(0)
    tau = pl.program_id(1)
    nb = pl.num_programs(0)
    nt = pl.num_programs(1)
    ne = win.shape[1]
    d = win.shape[3]
    step = b * nt + tau
    slot = step & 1

    def pieces(bi, ti, e):
        base = (bi * ne + e) * (nt + 1) + ti
        lo = lo_ref[base]
        n_sel = lo_ref[base + 1] - lo
        n_pieces = jnp.maximum(jnp.right_shift((lo & 15) + n_sel + (PIECE - 1), PIECE_SHIFT), 1)
        first = jnp.minimum(lo - (lo & 15), cap - n_pieces * PIECE)
        return lo, first, n_pieces

    def copy(bi, e, first, p, buf, s):
        start = pl.multiple_of(first + p * PIECE, 16)
        return pltpu.make_async_copy(ye_hbm.at[bi, e, pl.ds(start, PIECE), :], buf, s)

    def first_pieces(bi, ti, sl):
        return [copy(bi, e, pieces(bi, ti, e)[1], 0, win.at[sl, e], sem.at[sl, e]) for e in range(ne)]

    @pl.when(step == 0)
    def _prime():
        for cp in first_pieces(b, tau, slot):
            cp.start()

    @pl.when(step + 1 < nb * nt)
    def _prefetch():
        wrap = tau + 1 == nt
        for cp in first_pieces(jnp.where(wrap, b + 1, b), jnp.where(wrap, 0, tau + 1), 1 - slot):
            cp.start()

    for cp in first_pieces(b, tau, slot):
        cp.wait()

    sclt = sclt_ref[0]
    col = lax.broadcasted_iota(jnp.int32, (1, ne * PIECE), 1)
    grp = jnp.right_shift(col, PIECE_SHIFT)
    expand = jnp.where(lax.broadcasted_iota(jnp.int32, (ne, ne * PIECE), 0) == grp, 1.0, 0.0).astype(BF16)
    scl = _dot(sclt.astype(BF16), expand)
    shift = jnp.zeros(col.shape, F32)
    for e in range(ne):
        lo, first, _ = pieces(b, tau, e)
        shift = jnp.where(grp == e, (lo - first - 1).astype(F32), shift)
    r = (col & (PIECE - 1)).astype(F32)
    onehot = jnp.where((scl > 0.0) & ((scl + shift) == r), 1.0, 0.0).astype(BF16)
    acc_sc[...] = DEEPNORM_ALPHA * h1_ref[0] + _dot(onehot, win[slot].reshape(ne * PIECE, d))

    @pl.when(lo_ref[flag_off + step] > 0)
    def _rare():
        for e in range(ne):
            lo, first, n_pieces = pieces(b, tau, e)

            def extra(p, c, e=e, lo=lo, first=first):
                cp = copy(b, e, first, p, xwin, xsem.at[0])
                cp.start()
                cp.wait()
                se = sclt_ref[0][:, e:e + 1]
                rr = lax.broadcasted_iota(jnp.int32, (se.shape[0], PIECE), 1).astype(F32)
                oh = jnp.where((se > 0.0) & ((se + (lo - first - p * PIECE - 1).astype(F32)) == rr), 1.0, 0.0)
                acc_sc[...] += _dot(oh.astype(BF16), xwin[...])
                return c

            lax.fori_loop(1, n_pieces, extra, 0)

    o_ref[0] = _ln(acc_sc[...], g_ref[...], b_ref[...])


def _combine(lo_i, h1, sclt, ye, g, bb, cap):
    b, s, d = h1.shape
    ne = sclt.shape[2]
    rt = ROUTE_TILE
    grid_spec = pltpu.PrefetchScalarGridSpec(
        num_scalar_prefetch=1,
        grid=(b, s // rt),
        in_specs=[
            pl.BlockSpec((1, rt, d), lambda bi, ti, lo: (bi, ti, 0)),
            pl.BlockSpec((1, rt, ne), lambda bi, ti, lo: (bi, ti, 0)),
            pl.BlockSpec(memory_space=pl.ANY),
            pl.BlockSpec(g.shape, lambda bi, ti, lo: (0, 0)),
            pl.BlockSpec(bb.shape, lambda bi, ti, lo: (0, 0)),
        ],
        out_specs=pl.BlockSpec((1, rt, d), lambda bi, ti, lo: (bi, ti, 0)),
        scratch_shapes=[pltpu.VMEM((2, ne, PIECE, d), BF16), pltpu.VMEM((PIECE, d), BF16),
                        pltpu.VMEM((rt, d), F32),
                        pltpu.SemaphoreType.DMA((2, ne)), pltpu.SemaphoreType.DMA((1,))],
    )
    return pl.pallas_call(
        functools.partial(_combine_kernel, cap=cap, flag_off=b * ne * (s // rt + 1) + b * ne),
        grid_spec=grid_spec,
        out_shape=jax.ShapeDtypeStruct((b, s, d), F32),
        compiler_params=pltpu.CompilerParams(dimension_semantics=("arbitrary", "arbitrary"),
                                             vmem_limit_bytes=VMEM_LIMIT),
        name="combine",
    )(lo_i, h1, sclt, ye, g, bb)


def _rope_tables(n_tokens):
    half = HEAD_DIM // 2
    inv_freq = ROPE_THETA ** (-jnp.arange(0, half, 2, dtype=F32) / half)
    rows = n_tokens // GRID_W
    row = jnp.repeat(jnp.arange(rows, dtype=F32), GRID_W)
    colv = jnp.tile(jnp.arange(GRID_W, dtype=F32), rows)
    ang = jnp.concatenate([row[:, None] * inv_freq, colv[:, None] * inv_freq], axis=-1)
    return jnp.cos(ang), jnp.sin(ang)


def kernel(x, meta_tokens, ln_in_g, ln_in_b, w_in, q_norm_g, k_norm_g, ssm_a_re, ssm_a_im, ssm_log_dt,
           ssm_b_re, ssm_b_im, ssm_c_re, ssm_c_im, ssm_d, w_glu, b_glu, w_attn_br, w_ssm_br, w_o,
           ln1_g, ln1_b, w_router, w_gate_e, w_up_e, w_down_e, ln2_g, ln2_b):
    b, s, d = x.shape
    aw = N_HEADS * HEAD_DIM
    kw = N_KV_HEADS * HEAD_DIM
    sw = d // 2
    g = sw // SSM_GROUP
    half = HEAD_DIM // 2
    cap = CAPACITY_FACTOR * s // N_EXPERTS
    rows = b * s
    assert w_in.shape[0] == DEPTH == 1 and s % (2 * 512) == 0 and d == 2 * aw, (w_in.shape, x.shape)
    l = 0

    perm = np.concatenate([np.arange(0, HEAD_DIM, 2), np.arange(1, HEAD_DIM, 2)])
    qcols = np.concatenate([h * HEAD_DIM + perm for h in range(N_HEADS)])
    kcols = aw + np.concatenate([h * HEAD_DIM + perm for h in range(N_KV_HEADS)])
    wl = w_in[l]
    w_t = jnp.concatenate([wl[:, qcols], wl[:, aw + kw:aw + 2 * kw]], axis=1).T.astype(BF16)
    w_n = jnp.concatenate([wl[:, kcols], wl[:, aw + 2 * kw:aw + 2 * kw + sw]], axis=1).astype(BF16)
    w_gates = wl[:, aw + 2 * kw + sw:].astype(BF16)
    qg3 = (jnp.tile(q_norm_g[l][perm], N_HEADS) * (HEAD_DIM ** -0.5 * math.log2(math.e))).reshape(N_HEADS, HEAD_DIM, 1)
    kg = jnp.tile(k_norm_g[l][perm], N_KV_HEADS)[None, :]
    bd = jnp.asarray(np.kron(np.eye(N_KV_HEADS), np.full((HEAD_DIM, HEAD_DIM), 1.0 / HEAD_DIM)), BF16)
    lng = ln_in_g[None, :]
    lnb = ln_in_b[None, :]
    cos, sin = _rope_tables(s)
    cosr = jnp.tile(cos, (1, 2 * N_KV_HEADS))
    sinr = jnp.tile(jnp.concatenate([-sin, sin], axis=1), (1, N_KV_HEADS))
    cost = cos.T
    sint = sin.T

    x2 = x.reshape(rows, d)
    tm_in = 1024
    qt, kk, vt, u2 = _inproj(x2, cosr, sinr, cost, sint, w_t, w_n, lng, lnb, qg3, kg, bd, tm_in, s // tm_in)
    meta_p = jnp.pad(meta_tokens, ((0, META_PAD - N_META), (0, 0)))
    ones_r = jnp.ones((META_PAD, kw), F32)
    _, km, vtm, um = _inproj(meta_p, ones_r, jnp.zeros_like(ones_r), jnp.ones((half, META_PAD), F32),
                             jnp.zeros((half, META_PAD), F32), w_t, w_n, lng, lnb, qg3, kg, bd, META_PAD, 1)
    um = um[:N_META]
    km = km[:, :N_META, :]
    vtm = vtm[:, :, :N_META]

    ya = _attention(qt, kk, vt, km, vtm, b, 512, 512)

    t_chunk = SSM_CHUNK
    hgrp = SSM_GROUP
    n = t_chunk * hgrp
    arow = jnp.stack([ssm_a_re[l], ssm_a_im[l]], axis=2)
    ldt = ssm_log_dt[l][:, :, None, None]
    bt = jnp.stack([ssm_b_re[l], ssm_b_im[l]], axis=2)
    bt = jnp.swapaxes(bt, 3, 4)
    ct = jnp.stack([ssm_c_re[l], ssm_c_im[l]], axis=2)
    ct = jnp.swapaxes(ct, 3, 4)
    mm, ww, vv, at = _ssm_prep(arow, ldt, bt, ct)
    nc = s // t_chunk
    um_g = um.reshape(N_META, g, hgrp).transpose(1, 0, 2).reshape(g, 1, N_META * hgrp)
    um_g = jnp.pad(um_g, ((0, 0), (0, 7), (n - N_META * hgrp, 0)))
    dvec = jnp.tile(ssm_d[l], (1, t_chunk))[:, None, :]
    ys = _ssm(u2.reshape(b, s, sw), um_g, mm, ww, vv, at, dvec).reshape(rows, sw)

    wr_hi = w_router[l].astype(BF16)
    wr_lo = (w_router[l] - wr_hi.astype(F32)).astype(BF16)
    lane_pad = ((0, 0), (0, ROUTER_LANES - N_EXPERTS))
    wr = jnp.concatenate([jnp.pad(wr_hi, lane_pad), jnp.pad(wr_lo, lane_pad)], axis=1)
    h1, tok, afft = _merge(
        x2, ya, ys, w_gates, w_glu[l].astype(BF16), b_glu[l][None, :],
        w_attn_br[l].astype(BF16), w_ssm_br[l].astype(BF16), w_o[l].astype(BF16),
        lng, lnb, ln1_g[l][None, :], ln1_b[l][None, :], wr, N_EXPERTS, 512)

    tri = jnp.asarray(np.triu(np.ones((ROUTE_TILE, ROUTE_TILE), np.float32)), BF16)
    scl, lo_f = _route(afft, tri, b, cap)
    nt = s // ROUTE_TILE
    lo3 = jnp.pad(lo_f.astype(jnp.int32), ((0, 0), (0, 0), (0, 1)), constant_values=cap)
    seg = lo3[:, :, 1:] - lo3[:, :, :-1]
    multi = lambda align: ((lo3[:, :, :-1] & (align - 1)) + seg + (PIECE - 1)) // PIECE > 1
    lo_i = jnp.concatenate([lo3.reshape(-1),
                            jnp.any(multi(8), axis=2).astype(jnp.int32).reshape(-1),
                            jnp.any(multi(16), axis=1).astype(jnp.int32).reshape(-1)])
    ye = _ffn(lo_i, tok.reshape(b, s, d), scl.reshape(b, N_EXPERTS, nt, ROUTE_TILE),
              afft.reshape(N_EXPERTS, b, nt, ROUTE_TILE), w_gate_e, w_up_e, w_down_e, l, cap, 512)
    out = _combine(lo_i, h1.reshape(b, s, d), jnp.swapaxes(scl, 1, 2), ye,
                   ln2_g[l][None, :], ln2_b[l][None, :], cap)
    return out
```

```python
import functools
import math

import numpy as np
import jax
import jax.numpy as jnp
from jax import lax
from jax.experimental import pallas as pl
from jax.experimental.pallas import tpu as pltpu

F32 = jnp.float32
BF16 = jnp.bfloat16

N_META = 16
GRID_W = 64
N_HEADS = 8
N_KV_HEADS = 2
HEAD_DIM = 64
ROPE_THETA = 10000.0
SSM_GROUP = 16
SSM_STATE = 64
N_EXPERTS = 16
CAPACITY_FACTOR = 2
LN_EPS = 1e-5
QK_EPS = 1e-6
DEPTH = 1
DEEPNORM_ALPHA = (2.0 * DEPTH) ** 0.25

SSM_CHUNK = 32
SSM_GROUPS_TOGETHER = 4
ROUTE_TILE = 256
ROUTER_LANES = 128
PIECE_SHIFT = 6
PIECE = 1 << PIECE_SHIFT
META_PAD = 128
V_ROWS = HEAD_DIM + 16
VMEM_LIMIT = 56 * 1024 * 1024


def _ln(x, g, b):
    mu = jnp.mean(x, axis=-1, keepdims=True)
    xc = x - mu
    var = jnp.mean(xc * xc, axis=-1, keepdims=True)
    return xc * lax.rsqrt(var + LN_EPS) * g + b


def _sigmoid(x):
    return 1.0 / (1.0 + jnp.exp(-x))


def _split(t):
    hi = t.astype(BF16)
    lo = (t - hi.astype(F32)).astype(BF16)
    return hi, lo


def _dot(a, b):
    return jnp.dot(a, b, preferred_element_type=F32)


def _dot_nt(a, b):
    return lax.dot_general(a, b, (((1,), (1,)), ((), ())), preferred_element_type=F32)


def _inproj_kernel(x_ref, cosr_ref, sinr_ref, cost_ref, sint_ref, wt_ref, wn_ref, lng_ref, lnb_ref,
                   qg_ref, kg_ref, bd_ref, qt_ref, k_ref, vt_ref, u_ref):
    aw = N_HEADS * HEAD_DIM
    kw = N_KV_HEADS * HEAD_DIM
    half = HEAD_DIM // 2
    h = _ln(x_ref[...], lng_ref[...], lnb_ref[...])
    hb = h.astype(BF16)
    pt = _dot_nt(wt_ref[...], hb)
    pn = _dot(hb, wn_ref[...])
    tm = hb.shape[0]

    qt = pt[0:aw].reshape(N_HEADS, HEAD_DIM, tm)
    ms = jnp.mean(qt * qt, axis=1, keepdims=True)
    qn = qt * lax.rsqrt(ms + QK_EPS) * qg_ref[...]
    x0 = qn[:, 0:half, :]
    x1 = qn[:, half:, :]
    c = cost_ref[...][None]
    s = sint_ref[...][None]
    qr = jnp.concatenate([x0 * c - x1 * s, x0 * s + x1 * c], axis=1)
    qt_ref[...] = qr.reshape(aw, tm).astype(BF16)
    vrow = lax.broadcasted_iota(jnp.int32, (N_KV_HEADS, V_ROWS - HEAD_DIM, tm), 1)
    vt_ref[...] = jnp.concatenate([pt[aw:aw + kw].reshape(N_KV_HEADS, HEAD_DIM, tm),
                                   jnp.where(vrow == 0, 1.0, 0.0)], axis=1).astype(BF16)

    kk = pn[:, 0:kw]
    hi, lo = _split(kk * kk)
    bd = bd_ref[...]
    msk = _dot(hi, bd) + _dot(lo, bd)
    kn = kk * lax.rsqrt(msk + QK_EPS) * kg_ref[...]
    lane = lax.broadcasted_iota(jnp.int32, kn.shape, 1)
    first = (lane & (HEAD_DIM - 1)) < half
    partner = jnp.where(first, pltpu.roll(kn, kw - half, 1), pltpu.roll(kn, half, 1))
    kr = (kn * cosr_ref[...] + partner * sinr_ref[...]).astype(BF16)
    for g in range(N_KV_HEADS):
        k_ref[g] = kr[:, g * HEAD_DIM:(g + 1) * HEAD_DIM]
    u_ref[...] = pn[:, kw:]


def _inproj(x2, cosr, sinr, cost, sint, wt, wn, lng, lnb, qg3, kg, bd, tm, n_tab_blocks):
    rows, d = x2.shape
    aw = N_HEADS * HEAD_DIM
    kw = N_KV_HEADS * HEAD_DIM
    uw = wn.shape[1] - kw
    half = HEAD_DIM // 2
    full = lambda a: pl.BlockSpec(a.shape, lambda i: (0,) * a.ndim)
    return pl.pallas_call(
        _inproj_kernel,
        grid=(rows // tm,),
        in_specs=[
            pl.BlockSpec((tm, d), lambda i: (i, 0)),
            pl.BlockSpec((tm, kw), lambda i: (i % n_tab_blocks, 0)),
            pl.BlockSpec((tm, kw), lambda i: (i % n_tab_blocks, 0)),
            pl.BlockSpec((half, tm), lambda i: (0, i % n_tab_blocks)),
            pl.BlockSpec((half, tm), lambda i: (0, i % n_tab_blocks)),
            full(wt), full(wn), full(lng), full(lnb), full(qg3), full(kg), full(bd),
        ],
        out_specs=[
            pl.BlockSpec((aw, tm), lambda i: (0, i)),
            pl.BlockSpec((N_KV_HEADS, tm, HEAD_DIM), lambda i: (0, i, 0)),
            pl.BlockSpec((N_KV_HEADS, V_ROWS, tm), lambda i: (0, 0, i)),
            pl.BlockSpec((tm, uw), lambda i: (i, 0)),
        ],
        out_shape=[
            jax.ShapeDtypeStruct((aw, rows), BF16),
            jax.ShapeDtypeStruct((N_KV_HEADS, rows, HEAD_DIM), BF16),
            jax.ShapeDtypeStruct((N_KV_HEADS, V_ROWS, rows), BF16),
            jax.ShapeDtypeStruct((rows, uw), F32),
        ],
        compiler_params=pltpu.CompilerParams(dimension_semantics=("arbitrary",),
                                             vmem_limit_bytes=VMEM_LIMIT),
        name="inproj",
    )(x2, cosr, sinr, cost, sint, wt, wn, lng, lnb, qg3, kg, bd)


def _attn_kernel(qt_ref, k0_ref, k1_ref, vta_ref, vtb_ref, km_ref, vtm_ref, o_ref, m_sc, acc_sc, s_sc, mb_sc):
    j = pl.program_id(2)
    last = pl.num_programs(2) - 1
    grp = N_HEADS // N_KV_HEADS
    heads = lambda h: slice(h * HEAD_DIM, (h + 1) * HEAD_DIM)

    def score(k_ref, slot, h):
        s = _dot(k_ref[h // grp], qt_ref[heads(h), :])
        s_sc[slot, h] = s
        mb_sc[slot, h:h + 1, :] = jnp.max(s, axis=0, keepdims=True)

    def softmax_pv(h, s, m_blk, vt_g):
        m_prev = m_sc[h:h + 1, :]
        m_new = jnp.maximum(m_prev, m_blk)
        alpha = jnp.exp2(m_prev - m_new)
        p = jnp.exp2(s - m_new).astype(BF16)
        acc_sc[h] = alpha * acc_sc[h] + _dot(vt_g, p)
        m_sc[h:h + 1, :] = m_new

    def consume(vt_ref, slot, h):
        softmax_pv(h, s_sc[slot, h], mb_sc[slot, h:h + 1, :], vt_ref[h // grp])

    @pl.when(j == 0)
    def _first():
        m_sc[...] = jnp.full(m_sc.shape, -jnp.inf, F32)
        acc_sc[...] = jnp.zeros(acc_sc.shape, F32)
        meta = [_dot(km_ref[h // grp], qt_ref[heads(h), :]) for h in range(N_HEADS)]
        for h in range(N_HEADS):
            score(k0_ref, 0, h)
            softmax_pv(h, meta[h], jnp.max(meta[h], axis=0, keepdims=True), vtm_ref[h // grp])
        for h in range(N_HEADS):
            score(k1_ref, 1, h)
            consume(vtb_ref, 0, h)

    @pl.when((j > 0) & (j < last))
    def _middle():
        for h in range(N_HEADS):
            score(k0_ref, 0, h)
            consume(vta_ref, 1, h)
        for h in range(N_HEADS):
            score(k1_ref, 1, h)
            consume(vtb_ref, 0, h)

    @pl.when(j == last)
    def _last():
        for h in range(N_HEADS):
            consume(vta_ref, 1, h)
        acc = acc_sc[...]
        tq = acc.shape[2]
        out_t = (acc[:, 0:HEAD_DIM, :] / acc[:, HEAD_DIM:HEAD_DIM + 1, :]).reshape(N_HEADS * HEAD_DIM, tq)
        o_ref[...] = out_t.T.astype(BF16)


def _attention(qt, kk, vt, km, vtm, nb, tq, tk):
    aw, rows = qt.shape
    s = rows // nb
    nq = s // tq
    nkb = s // tk
    kblk = lambda f: pl.BlockSpec((N_KV_HEADS, tk, HEAD_DIM),
                                  lambda bi, qi, j: (0, bi * nkb + jnp.clip(f(j), 0, nkb - 1), 0))
    vblk = lambda f: pl.BlockSpec((N_KV_HEADS, V_ROWS, tk),
                                  lambda bi, qi, j: (0, 0, bi * nkb + jnp.clip(f(j), 0, nkb - 1)))
    return pl.pallas_call(
        _attn_kernel,
        grid=(nb, nq, nkb // 2 + 1),
        in_specs=[
            pl.BlockSpec((aw, tq), lambda bi, qi, j: (0, bi * nq + qi)),
            kblk(lambda j: 2 * j), kblk(lambda j: 2 * j + 1),
            vblk(lambda j: 2 * j - 1), vblk(lambda j: 2 * j),
            pl.BlockSpec(km.shape, lambda bi, qi, j: (0, 0, 0)),
            pl.BlockSpec(vtm.shape, lambda bi, qi, j: (0, 0, 0)),
        ],
        out_specs=pl.BlockSpec((tq, aw), lambda bi, qi, j: (bi * nq + qi, 0)),
        out_shape=jax.ShapeDtypeStruct((rows, aw), BF16),
        scratch_shapes=[
            pltpu.VMEM((N_HEADS, tq), F32),
            pltpu.VMEM((N_HEADS, V_ROWS, tq), F32),
            pltpu.VMEM((2, N_HEADS, tk, tq), F32),
            pltpu.VMEM((2, N_HEADS, tq), F32),
        ],
        compiler_params=pltpu.CompilerParams(
            dimension_semantics=("arbitrary", "arbitrary", "arbitrary"),
            vmem_limit_bytes=VMEM_LIMIT),
        name="attn",
    )(qt, kk, kk, vt, vt, km, vtm)


def _ssm_prep_kernel(*refs):
    for d in range(2):
        _ssm_prep_direction(d, *refs)


def _ssm_prep_direction(d, arow_ref, ldt_ref, bt_ref, ct_ref, m_ref, w_ref, v_ref, at_ref):
    t_chunk = SSM_CHUNK
    shift = int(math.log2(SSM_GROUP))
    df = float(d)
    dt = jnp.exp(ldt_ref[d, 0])

    def abar(ar, ai):
        mag = jnp.exp(ar * dt)
        ang = ai * dt
        return mag * jnp.cos(ang), mag * jnp.sin(ang)

    nbits = int(math.log2(t_chunk))

    def ipow(br, bi, e):
        pr = pi = None
        for k in range(nbits):
            bit = ((e >> k) & 1) == 1
            if pr is None:
                pr, pi = jnp.where(bit, br, 1.0), jnp.where(bit, bi, 0.0)
            else:
                pr, pi = jnp.where(bit, pr * br - pi * bi, pr), jnp.where(bit, pr * bi + pi * br, pi)
            br, bi = br * br - bi * bi, 2.0 * br * bi
        return pr, pi, br, bi

    ar_r = arow_ref[d, 0, 0:1, :]
    ai_r = arow_ref[d, 0, 1:2, :]
    abr, abi = abar(ar_r, ai_r)
    nr = abr - 1.0
    ni = abi
    den = ar_r * ar_r + ai_r * ai_r
    cr = (nr * ar_r + ni * ai_r) / den
    ci = (ni * ar_r - nr * ai_r) / den
    nstate = bt_ref.shape[4]
    nrow = t_chunk * SSM_GROUP

    def tile_rows(x):
        return jnp.broadcast_to(x[None], (t_chunk, SSM_GROUP, nstate)).reshape(nrow, nstate)

    btr = tile_rows(bt_ref[d, 0, 0])
    bti = tile_rows(bt_ref[d, 0, 1])
    bbr = cr * btr - ci * bti
    bbi = cr * bti + ci * btr
    jr = (lax.broadcasted_iota(jnp.int32, (nrow, 1), 0) >> shift).astype(F32)
    lj = jr + df * ((t_chunk - 1) - 2.0 * jr)
    step = lax.broadcasted_iota(jnp.int32, (t_chunk, 1), 0)
    lstep = (t_chunk - 1) - step if d == 1 else step

    def rep_rows(x):
        return jnp.broadcast_to(x[:, None, :], (t_chunk, SSM_GROUP, nstate)).reshape(nrow, nstate)

    mag2 = abr * abr + abi * abi
    enr, eni, _, _ = ipow(abr / mag2, -abi / mag2, lstep)
    enr, eni = rep_rows(enr), rep_rows(eni)
    bmr = enr * bbr - eni * bbi
    bmi = enr * bbi + eni * bbr
    ewr, ewi, atr, ati = ipow(abr, abi, (t_chunk - 1) - lstep)
    ewr, ewi = rep_rows(ewr), rep_rows(ewi)
    w_ref[d, 0] = jnp.concatenate([ewr * bbr - ewi * bbi, ewr * bbi + ewi * bbr], axis=1).astype(BF16)
    at_ref[d, 0] = jnp.concatenate([atr, ati], axis=1)

    ncol = t_chunk * SSM_GROUP
    lane = lax.broadcasted_iota(jnp.int32, (1, ncol), 1)
    tci = lane >> shift
    tile = jnp.where(lax.broadcasted_iota(jnp.int32, (SSM_GROUP, ncol), 0) == (lane & (SSM_GROUP - 1)),
                     1.0, 0.0).astype(BF16)

    def tile_cols(x):
        hi, lo = _split(x)
        lo2 = (x - hi.astype(F32) - lo.astype(F32)).astype(BF16)
        return _dot(hi, tile) + _dot(lo, tile) + _dot(lo2, tile)

    ctr = tile_cols(ct_ref[d, 0, 0])
    cti = tile_cols(ct_ref[d, 0, 1])
    tc = tci.astype(F32)
    lt = tc + df * ((t_chunk - 1) - 2.0 * tc)
    stepl = lax.broadcasted_iota(jnp.int32, (1, t_chunk), 1)
    lstepl = (t_chunk - 1) - stepl if d == 1 else stepl
    diag = (lax.broadcasted_iota(jnp.int32, (nstate, nstate), 0)
            == lax.broadcasted_iota(jnp.int32, (nstate, nstate), 1))
    abr_c = jnp.sum(jnp.where(diag, abr, 0.0), axis=1, keepdims=True)
    abi_c = jnp.sum(jnp.where(diag, abi, 0.0), axis=1, keepdims=True)
    rep = jnp.where(lax.broadcasted_iota(jnp.int32, (t_chunk, ncol), 0) == tci, 1.0, 0.0).astype(BF16)

    def rep_cols(x):
        hi, lo = _split(x)
        lo2 = (x - hi.astype(F32) - lo.astype(F32)).astype(BF16)
        return _dot(hi, rep) + _dot(lo, rep) + _dot(lo2, rep)

    ecr, eci = [rep_cols(x) for x in ipow(abr_c, abi_c, lstepl)[:2]]
    cmr = ctr * ecr - cti * eci
    cmi = ctr * eci + cti * ecr
    lhs_hi, lhs_lo = _split(jnp.concatenate([bmr, -bmi], axis=1))
    rhs_hi, rhs_lo = _split(jnp.concatenate([cmr, cmi], axis=0))
    m = _dot(lhs_hi, rhs_hi) + _dot(lhs_hi, rhs_lo) + _dot(lhs_lo, rhs_hi)
    m_ref[d, 0] = jnp.where(lj <= lt, m, 0.0).astype(BF16)
    c1r = cmr * abr_c - cmi * abi_c
    c1i = cmr * abi_c + cmi * abr_c
    v_ref[d, 0] = jnp.concatenate([c1r, -c1i], axis=0).astype(BF16)


def _ssm_prep(arow, ldt, bt, ct):
    nd, g = arow.shape[0], arow.shape[1]
    p = SSM_STATE
    n = SSM_CHUNK * SSM_GROUP
    blk = lambda a: pl.BlockSpec((nd, 1) + a.shape[2:], lambda gi: (0, gi) + (0,) * (a.ndim - 2))
    return pl.pallas_call(
        _ssm_prep_kernel,
        grid=(g,),
        in_specs=[blk(arow), blk(ldt), blk(bt), blk(ct)],
        out_specs=[
            pl.BlockSpec((nd, 1, n, n), lambda gi: (0, gi, 0, 0)),
            pl.BlockSpec((nd, 1, n, 2 * p), lambda gi: (0, gi, 0, 0)),
            pl.BlockSpec((nd, 1, 2 * p, n), lambda gi: (0, gi, 0, 0)),
            pl.BlockSpec((nd, 1, 1, 2 * p), lambda gi: (0, gi, 0, 0)),
        ],
        out_shape=[
            jax.ShapeDtypeStruct((nd, g, n, n), BF16),
            jax.ShapeDtypeStruct((nd, g, n, 2 * p), BF16),
            jax.ShapeDtypeStruct((nd, g, 2 * p, n), BF16),
            jax.ShapeDtypeStruct((nd, g, 1, 2 * p), F32),
        ],
        compiler_params=pltpu.CompilerParams(dimension_semantics=("arbitrary",)),
        name="ssm_prep",
    )(arow, ldt, bt, ct)


def _chunk_carry(chains):
    p = SSM_STATE
    z0, a0 = chains[0][0], chains[0][1]
    nc = z0.shape[0]
    row = lax.broadcasted_iota(jnp.int32, z0.shape, 0)
    is_re = lax.broadcasted_iota(jnp.int32, a0.shape, 1) < p
    sign = jnp.where(is_re, -1.0, 1.0)

    def parts(ap):
        sw = pltpu.roll(ap, p, 1)
        return jnp.where(is_re, ap, sw), sign * jnp.where(is_re, sw, ap)

    def cmul(x, ar_full, ai_sgn):
        return x * ar_full + pltpu.roll(x, p, 1) * ai_sgn

    coef = [parts(a) for _, a, _, _ in chains]
    firsts = [(nc - 1) if rev else 0 for _, _, _, rev in chains]
    es = [z + jnp.where(row == f, cmul(s0, *c), 0.0) for (z, _, s0, _), f, c in zip(chains, firsts, coef)]
    k = 1
    while k < nc:
        for i, (_, _, _, rev) in enumerate(chains):
            if rev:
                sh = jnp.where(row < nc - k, pltpu.roll(es[i], nc - k, 0), 0.0)
            else:
                sh = jnp.where(row >= k, pltpu.roll(es[i], k, 0), 0.0)
            es[i] = es[i] + cmul(sh, *coef[i])
        k *= 2
        if k < nc:
            coef = [parts(cmul(jnp.where(is_re, ar, sign * ai), ar, ai)) for ar, ai in coef]
    outs = []
    for e, (_, _, s0, rev), f in zip(es, chains, firsts):
        outs.append(jnp.where(row == f, s0, pltpu.roll(e, nc - 1 if rev else 1, 0)))
    return outs


def _ssm_kernel(u_ref, um_ref, m_ref, w_ref, v_ref, at_ref, dvec_ref, y_ref, ug_sc, yg_sc):
    p = SSM_STATE
    t_chunk = SSM_CHUNK
    hgrp = SSM_GROUP
    gb = u_ref.shape[2] // hgrp
    per_tile = u_ref.shape[2] // hgrp
    nc = u_ref.shape[1] // t_chunk
    lane_blk = lax.broadcasted_iota(jnp.int32, (nc, u_ref.shape[2]), 1) >> int(math.log2(hgrp))

    width = u_ref.shape[2]

    def block_transpose(arrs):
        arrs = list(arrs)
        d = len(arrs) // 2
        while d >= 1:
            low_bit = (lane_blk & d) == 0
            for i in range(len(arrs)):
                if i & d == 0:
                    lo, hi = arrs[i], arrs[i + d]
                    arrs[i] = jnp.where(low_bit, lo, pltpu.roll(hi, d * hgrp, 1))
                    arrs[i + d] = jnp.where(low_bit, pltpu.roll(lo, width - d * hgrp, 1), hi)
            d //= 2
        return arrs

    for q in range(t_chunk // per_tile):
        steps = [u_ref[0, pl.ds(q * per_tile + r, nc, stride=t_chunk), :] for r in range(per_tile)]
        for g, tile in enumerate(block_transpose(steps)):
            ug_sc[g, :, q * width:(q + 1) * width] = tile

    def groups(i, carry):
        gs = [i * SSM_GROUPS_TOGETHER + k for k in range(SSM_GROUPS_TOGETHER)]
        us = [ug_sc[g] for g in gs]
        ubs = [u.astype(BF16) for u in us]
        yin = [[_dot(ub, m_ref[d, g]) for d in range(2)] for g, ub in zip(gs, ubs)]
        chains = []
        for g, ub in zip(gs, ubs):
            for d in range(2):
                if d == 0:
                    s0 = _dot(um_ref[g].astype(BF16), w_ref[0, g])[0:1, :]
                else:
                    s0 = jnp.zeros((1, 2 * p), F32)
                chains.append((_dot(ub, w_ref[d, g]), at_ref[d, g], s0, d == 1))
        s_in = _chunk_carry(chains)
        for k, g in enumerate(gs):
            y = dvec_ref[g] * us[k] + yin[k][0] + yin[k][1]
            for d in range(2):
                y = y + _dot(s_in[2 * k + d].astype(BF16), v_ref[d, g])
            yg_sc[g] = y
        return carry

    lax.fori_loop(0, gb // SSM_GROUPS_TOGETHER, groups, 0)

    for q in range(t_chunk // per_tile):
        srcs = [yg_sc[g, :, q * width:(q + 1) * width] for g in range(gb)]
        for r, row in enumerate(block_transpose(srcs)):
            y_ref[0, pl.ds(q * per_tile + r, nc, stride=t_chunk), :] = row


def _ssm(u3, um2, m, w, v, at, dvec):
    b, s, sw = u3.shape
    g = sw // SSM_GROUP
    n = SSM_CHUNK * SSM_GROUP
    p = SSM_STATE
    lanes = 128
    gb = lanes // SSM_GROUP
    return pl.pallas_call(
        _ssm_kernel,
        grid=(b, sw // lanes),
        in_specs=[
            pl.BlockSpec((1, s, lanes), lambda bi, gi: (bi, 0, gi)),
            pl.BlockSpec((gb, 8, n), lambda bi, gi: (gi, 0, 0)),
            pl.BlockSpec((2, gb, n, n), lambda bi, gi: (0, gi, 0, 0)),
            pl.BlockSpec((2, gb, n, 2 * p), lambda bi, gi: (0, gi, 0, 0)),
            pl.BlockSpec((2, gb, 2 * p, n), lambda bi, gi: (0, gi, 0, 0)),
            pl.BlockSpec((2, gb, 1, 2 * p), lambda bi, gi: (0, gi, 0, 0)),
            pl.BlockSpec((gb, 1, n), lambda bi, gi: (gi, 0, 0)),
        ],
        out_specs=pl.BlockSpec((1, s, lanes), lambda bi, gi: (bi, 0, gi)),
        out_shape=jax.ShapeDtypeStruct((b, s, sw), F32),
        scratch_shapes=[pltpu.VMEM((gb, s // SSM_CHUNK, n), F32), pltpu.VMEM((gb, s // SSM_CHUNK, n), F32)],
        compiler_params=pltpu.CompilerParams(dimension_semantics=("arbitrary", "arbitrary"),
                                             vmem_limit_bytes=VMEM_LIMIT),
        name="ssm",
    )(u3, um2, m, w, v, at, dvec)


def _merge_kernel(x_ref, ya_ref, ys_ref, wgt_ref, wglu_ref, bglu_ref, wab_ref, wsb_ref, wo_ref,
                  lng_ref, lnb_ref, l1g_ref, l1b_ref, wr_ref,
                  h1_ref, tok_ref, afft_ref):
    d = x_ref.shape[1]
    h = _ln(x_ref[...], lng_ref[...], lnb_ref[...])
    gates = _dot(h.astype(BF16), wgt_ref[...])
    ga = _sigmoid(gates[:, 0:d])
    gs = _sigmoid(gates[:, d:2 * d])
    ys = ys_ref[...]
    y = 0.5 * ys * (1.0 + jnp.tanh(math.sqrt(2.0 / math.pi) * (ys + 0.044715 * (ys * ys * ys))))
    yg = y * _sigmoid(_dot(y.astype(BF16), wglu_ref[...]) + bglu_ref[...])
    merged = ga * _dot(ya_ref[...], wab_ref[...]) + gs * _dot(yg.astype(BF16), wsb_ref[...])
    h1 = _ln(DEEPNORM_ALPHA * h + _dot(merged.astype(BF16), wo_ref[...]), l1g_ref[...], l1b_ref[...])
    h1_ref[...] = h1
    hi, lo = _split(h1)
    tok_ref[...] = hi
    ne = afft_ref.shape[0]
    cross = _dot(hi, wr_ref[...])
    logits = cross[:, 0:ROUTER_LANES] + cross[:, ROUTER_LANES:] + _dot(lo, wr_ref[:, 0:ROUTER_LANES])
    lane = lax.broadcasted_iota(jnp.int32, logits.shape, 1)
    logits = jnp.where(lane < ne, logits, -jnp.inf)
    ex = jnp.exp(logits - jnp.max(logits, axis=1, keepdims=True))
    aff = ex / jnp.sum(ex, axis=1, keepdims=True)
    afft_ref[...] = aff.T[0:ne, :]


def _merge(x2, ya, ys, wgt, wglu, bglu, wab, wsb, wo, lng, lnb, l1g, l1b, wr, ne, tm):
    rows, d = x2.shape
    full = lambda a: pl.BlockSpec(a.shape, lambda i: (0,) * a.ndim)
    return pl.pallas_call(
        _merge_kernel,
        grid=(rows // tm,),
        in_specs=[
            pl.BlockSpec((tm, d), lambda i: (i, 0)),
            pl.BlockSpec((tm, ya.shape[1]), lambda i: (i, 0)),
            pl.BlockSpec((tm, ys.shape[1]), lambda i: (i, 0)),
            full(wgt), full(wglu), full(bglu), full(wab), full(wsb), full(wo),
            full(lng), full(lnb), full(l1g), full(l1b), full(wr),
        ],
        out_specs=[
            pl.BlockSpec((tm, d), lambda i: (i, 0)),
            pl.BlockSpec((tm, d), lambda i: (i, 0)),
            pl.BlockSpec((ne, tm), lambda i: (0, i)),
        ],
        out_shape=[
            jax.ShapeDtypeStruct((rows, d), F32),
            jax.ShapeDtypeStruct((rows, d), BF16),
            jax.ShapeDtypeStruct((ne, rows), F32),
        ],
        compiler_params=pltpu.CompilerParams(dimension_semantics=("arbitrary",),
                                             vmem_limit_bytes=VMEM_LIMIT),
        name="merge",
    )(x2, ya, ys, wgt, wglu, bglu, wab, wsb, wo, lng, lnb, l1g, l1b, wr)


def _route_kernel(afft_ref, tri_ref, scl_ref, lo_ref, *, cap):
    aff = afft_ref[...]
    ne, s = aff.shape
    capf = float(cap)

    def as_float(bits):
        return lax.bitcast_convert_type(bits, F32)

    def search(i, t):
        cand = t | jnp.left_shift(jnp.int32(1), 30 - i)
        cnt = jnp.sum(jnp.where(aff >= as_float(cand), 1.0, 0.0), axis=1, keepdims=True)
        return jnp.where(cnt >= capf, cand, t)

    thr_bits = lax.fori_loop(0, 31, search, jnp.zeros((ne, 1), jnp.int32))
    gt = aff >= as_float(thr_bits + 1)
    eq = (aff >= as_float(thr_bits)) & jnp.logical_not(gt)
    need = capf - jnp.sum(jnp.where(gt, 1.0, 0.0), axis=1, keepdims=True)
    tri = tri_ref[...]
    rt = ROUTE_TILE
    nt = s // rt
    col = lax.broadcasted_iota(jnp.int32, (ne, nt), 1)
    carry_eq = jnp.zeros((ne, 1), F32)
    carry_sel = jnp.zeros((ne, 1), F32)
    lo_val = jnp.zeros((ne, nt), F32)
    for t in range(nt):
        sl = slice(t * rt, (t + 1) * rt)
        eq_b = eq[:, sl]
        ceq = _dot(jnp.where(eq_b, 1.0, 0.0).astype(BF16), tri)
        sel_b = gt[:, sl] | (eq_b & ((ceq + carry_eq) <= need))
        carry_eq = carry_eq + ceq[:, rt - 1:rt]
        csel = _dot(jnp.where(sel_b, 1.0, 0.0).astype(BF16), tri)
        scl_ref[0, :, sl] = jnp.where(sel_b, csel, 0.0)
        lo_val = jnp.where(col == t, carry_sel, lo_val)
        carry_sel = carry_sel + csel[:, rt - 1:rt]
    lo_ref[0] = lo_val


def _route(afft, tri, nb, cap):
    ne, rows = afft.shape
    s = rows // nb
    nt = s // ROUTE_TILE
    return pl.pallas_call(
        functools.partial(_route_kernel, cap=cap),
        grid=(nb,),
        in_specs=[
            pl.BlockSpec((ne, s), lambda b: (0, b)),
            pl.BlockSpec(tri.shape, lambda b: (0, 0)),
        ],
        out_specs=[
            pl.BlockSpec((1, ne, s), lambda b: (b, 0, 0)),
            pl.BlockSpec((1, ne, nt), lambda b: (b, 0, 0)),
        ],
        out_shape=[
            jax.ShapeDtypeStruct((nb, ne, s), F32),
            jax.ShapeDtypeStruct((nb, ne, nt), F32),
        ],
        compiler_params=pltpu.CompilerParams(dimension_semantics=("arbitrary",)),
        name="route",
    )(afft, tri)


def _ffn_kernel(lo_ref, t_ref, scl_ref, aff_ref, wg_ref, wu_ref, wd_ref, ye_ref, xe_sc, xb_sc, gate_sc, acc_sc,
                *, cap, flag_off):
    fc = pl.program_id(2)

    @pl.when(fc == 0)
    def _gather():
        _ffn_gather(lo_ref, t_ref, scl_ref, aff_ref, xe_sc, xb_sc, gate_sc, cap=cap, flag_off=flag_off)
        acc_sc[...] = jnp.zeros(acc_sc.shape, F32)

    xb = xb_sc[...]
    g = _dot(xb, wg_ref[0, 0].astype(BF16))
    u = _dot(xb, wu_ref[0, 0].astype(BF16))
    hh = (g * _sigmoid(g)) * u
    acc_sc[...] += _dot(hh.astype(BF16), wd_ref[0, 0].astype(BF16))

    @pl.when(fc == pl.num_programs(2) - 1)
    def _emit():
        ye_ref[0, 0] = (acc_sc[...] * gate_sc[0:cap, :]).astype(BF16)


def _ffn_gather(lo_ref, t_ref, scl_ref, aff_ref, xe_sc, xb_sc, gate_sc, *, cap, flag_off):
    b = pl.program_id(0)
    e = pl.program_id(1)
    ne = pl.num_programs(1)
    rt = ROUTE_TILE
    nt = t_ref.shape[1] // rt
    xe_sc[...] = jnp.zeros(xe_sc.shape, F32)
    gate_sc[...] = jnp.zeros(gate_sc.shape, F32)
    r = lax.broadcasted_iota(jnp.int32, (PIECE, rt), 0).astype(F32)

    def tile_info(tau):
        base = (b * ne + e) * (nt + 1) + tau
        lo = lo_ref[base]
        n_pieces = jnp.right_shift((lo & 7) + (lo_ref[base + 1] - lo) + (PIECE - 1), PIECE_SHIFT)
        return lo, n_pieces

    def add_piece(tau, lo, p):
        off = lo & 7
        scl = scl_ref[0, 0, pl.ds(tau, 1), :]
        tok = t_ref[0, pl.ds(pl.multiple_of(tau * rt, rt), rt), :]
        shift = (off - 1 - p * PIECE).astype(F32)
        chosen = (scl > 0.0) & ((scl + shift) == r)
        rows = pl.ds(pl.multiple_of(lo - off + p * PIECE, 8), PIECE)
        xe_sc[rows, :] += _dot(jnp.where(chosen, 1.0, 0.0).astype(BF16), tok)
        gate_sc[rows, :] += jnp.sum(jnp.where(chosen, aff_ref[0, 0, pl.ds(tau, 1), :], 0.0), axis=1, keepdims=True)

    def first_piece(tau, carry):
        add_piece(tau, tile_info(tau)[0], 0)
        return carry

    def more_pieces(tau, carry):
        lo, n_pieces = tile_info(tau)
        lax.fori_loop(1, n_pieces, lambda p, c: (add_piece(tau, lo, p), c)[1], 0)
        return carry

    lax.fori_loop(0, nt, first_piece, 0, unroll=4)

    @pl.when(lo_ref[flag_off + b * ne + e] > 0)
    def _rare():
        lax.fori_loop(0, nt, more_pieces, 0)

    xb_sc[...] = xe_sc[0:cap, :].astype(BF16)


def _ffn(lo_i, t, scl4, aff4, wg, wu, wd, layer, cap, f_chunk):
    b, s, d = t.shape
    _, ne, _, f = wg.shape
    nt = s // ROUTE_TILE
    grid_spec = pltpu.PrefetchScalarGridSpec(
        num_scalar_prefetch=1,
        grid=(b, ne, f // f_chunk),
        in_specs=[
            pl.BlockSpec((1, s, d), lambda bi, ei, fi, lo: (bi, 0, 0), pipeline_mode=pl.Buffered(1)),
            pl.BlockSpec((1, 1, nt, ROUTE_TILE), lambda bi, ei, fi, lo: (bi, ei, 0, 0)),
            pl.BlockSpec((1, 1, nt, ROUTE_TILE), lambda bi, ei, fi, lo: (ei, bi, 0, 0)),
            pl.BlockSpec((1, 1, d, f_chunk), lambda bi, ei, fi, lo: (layer, ei, 0, fi)),
            pl.BlockSpec((1, 1, d, f_chunk), lambda bi, ei, fi, lo: (layer, ei, 0, fi)),
            pl.BlockSpec((1, 1, f_chunk, d), lambda bi, ei, fi, lo: (layer, ei, fi, 0)),
        ],
        out_specs=pl.BlockSpec((1, 1, cap, d), lambda bi, ei, fi, lo: (bi, ei, 0, 0)),
        scratch_shapes=[pltpu.VMEM((cap + PIECE, d), F32), pltpu.VMEM((cap, d), BF16),
                        pltpu.VMEM((cap + PIECE, 1), F32), pltpu.VMEM((cap, d), F32)],
    )
    return pl.pallas_call(
        functools.partial(_ffn_kernel, cap=cap, flag_off=b * ne * (nt + 1)),
        grid_spec=grid_spec,
        out_shape=jax.ShapeDtypeStruct((b, ne, cap, d), BF16),
        compiler_params=pltpu.CompilerParams(dimension_semantics=("arbitrary", "arbitrary", "arbitrary"),
                                             vmem_limit_bytes=VMEM_LIMIT),
        name="ffn",
    )(lo_i, t, scl4, aff4, wg, wu, wd)


def _combine_kernel(lo_ref, h1_ref, sclt_ref, ye_hbm, g_ref, b_ref, o_ref, win, xwin, acc_sc, sem, xsem,
                    *, cap, flag_off):
    b = pl.program_id(0)
    tau = pl.program_id(1)
    nb = pl.num_programs(0)
    nt = pl.num_programs(1)
    ne = win.shape[1]
    d = win.shape[3]
    step = b * nt + tau
    slot = step & 1

    def pieces(bi, ti, e):
        base = (bi * ne + e) * (nt + 1) + ti
        lo = lo_ref[base]
        n_sel = lo_ref[base + 1] - lo
        n_pieces = jnp.maximum(jnp.right_shift((lo & 15) + n_sel + (PIECE - 1), PIECE_SHIFT), 1)
        first = jnp.minimum(lo - (lo & 15), cap - n_pieces * PIECE)
        return lo, first, n_pieces

    def copy(bi, e, first, p, buf, s):
        start = pl.multiple_of(first + p * PIECE, 16)
        return pltpu.make_async_copy(ye_hbm.at[bi, e, pl.ds(start, PIECE), :], buf, s)

    def first_pieces(bi, ti, sl):
        return [copy(bi, e, pieces(bi, ti, e)[1], 0, win.at[sl, e], sem.at[sl, e]) for e in range(ne)]

    @pl.when(step == 0)
    def _prime():
        for cp in first_pieces(b, tau, slot):
            cp.start()

    @pl.when(step + 1 < nb * nt)
    def _prefetch():
        wrap = tau + 1 == nt
        for cp in first_pieces(jnp.where(wrap, b + 1, b), jnp.where(wrap, 0, tau + 1), 1 - slot):
            cp.start()

    for cp in first_pieces(b, tau, slot):
        cp.wait()

    sclt = sclt_ref[0]
    col = lax.broadcasted_iota(jnp.int32, (1, ne * PIECE), 1)
    grp = jnp.right_shift(col, PIECE_SHIFT)
    expand = jnp.where(lax.broadcasted_iota(jnp.int32, (ne, ne * PIECE), 0) == grp, 1.0, 0.0).astype(BF16)
    scl = _dot(sclt.astype(BF16), expand)
    shift = jnp.zeros(col.shape, F32)
    for e in range(ne):
        lo, first, _ = pieces(b, tau, e)
        shift = jnp.where(grp == e, (lo - first - 1).astype(F32), shift)
    r = (col & (PIECE - 1)).astype(F32)
    onehot = jnp.where((scl > 0.0) & ((scl + shift) == r), 1.0, 0.0).astype(BF16)
    acc_sc[...] = DEEPNORM_ALPHA * h1_ref[0] + _dot(onehot, win[slot].reshape(ne * PIECE, d))

    @pl.when(lo_ref[flag_off + step] > 0)
    def _rare():
        for e in range(ne):
            lo, first, n_pieces = pieces(b, tau, e)

            def extra(p, c, e=e, lo=lo, first=first):
                cp = copy(b, e, first, p, xwin, xsem.at[0])
                cp.start()
                cp.wait()
                se = sclt_ref[0][:, e:e + 1]
                rr = lax.broadcasted_iota(jnp.int32, (se.shape[0], PIECE), 1).astype(F32)
                oh = jnp.where((se > 0.0) & ((se + (lo - first - p * PIECE - 1).astype(F32)) == rr), 1.0, 0.0)
                acc_sc[...] += _dot(oh.astype(BF16), xwin[...])
                return c

            lax.fori_loop(1, n_pieces, extra, 0)

    o_ref[0] = _ln(acc_sc[...], g_ref[...], b_ref[...])


def _combine(lo_i, h1, sclt, ye, g, bb, cap):
    b, s, d = h1.shape
    ne = sclt.shape[2]
    rt = ROUTE_TILE
    grid_spec = pltpu.PrefetchScalarGridSpec(
        num_scalar_prefetch=1,
        grid=(b, s // rt),
        in_specs=[
            pl.BlockSpec((1, rt, d), lambda bi, ti, lo: (bi, ti, 0)),
            pl.BlockSpec((1, rt, ne), lambda bi, ti, lo: (bi, ti, 0)),
            pl.BlockSpec(memory_space=pl.ANY),
            pl.BlockSpec(g.shape, lambda bi, ti, lo: (0, 0)),
            pl.BlockSpec(bb.shape, lambda bi, ti, lo: (0, 0)),
        ],
        out_specs=pl.BlockSpec((1, rt, d), lambda bi, ti, lo: (bi, ti, 0)),
        scratch_shapes=[pltpu.VMEM((2, ne, PIECE, d), BF16), pltpu.VMEM((PIECE, d), BF16),
                        pltpu.VMEM((rt, d), F32),
                        pltpu.SemaphoreType.DMA((2, ne)), pltpu.SemaphoreType.DMA((1,))],
    )
    return pl.pallas_call(
        functools.partial(_combine_kernel, cap=cap, flag_off=b * ne * (s // rt + 1) + b * ne),
        grid_spec=grid_spec,
        out_shape=jax.ShapeDtypeStruct((b, s, d), F32),
        compiler_params=pltpu.CompilerParams(dimension_semantics=("arbitrary", "arbitrary"),
                                             vmem_limit_bytes=VMEM_LIMIT),
        name="combine",
    )(lo_i, h1, sclt, ye, g, bb)


def _rope_tables(n_tokens):
    half = HEAD_DIM // 2
    inv_freq = ROPE_THETA ** (-jnp.arange(0, half, 2, dtype=F32) / half)
    rows = n_tokens // GRID_W
    row = jnp.repeat(jnp.arange(rows, dtype=F32), GRID_W)
    colv = jnp.tile(jnp.arange(GRID_W, dtype=F32), rows)
    ang = jnp.concatenate([row[:, None] * inv_freq, colv[:, None] * inv_freq], axis=-1)
    return jnp.cos(ang), jnp.sin(ang)


def kernel(x, meta_tokens, ln_in_g, ln_in_b, w_in, q_norm_g, k_norm_g, ssm_a_re, ssm_a_im, ssm_log_dt,
           ssm_b_re, ssm_b_im, ssm_c_re, ssm_c_im, ssm_d, w_glu, b_glu, w_attn_br, w_ssm_br, w_o,
           ln1_g, ln1_b, w_router, w_gate_e, w_up_e, w_down_e, ln2_g, ln2_b):
    b, s, d = x.shape
    aw = N_HEADS * HEAD_DIM
    kw = N_KV_HEADS * HEAD_DIM
    sw = d // 2
    g = sw // SSM_GROUP
    half = HEAD_DIM // 2
    cap = CAPACITY_FACTOR * s // N_EXPERTS
    rows = b * s
    assert w_in.shape[0] == DEPTH == 1 and s % (2 * 512) == 0 and d == 2 * aw, (w_in.shape, x.shape)
    l = 0

    perm = np.concatenate([np.arange(0, HEAD_DIM, 2), np.arange(1, HEAD_DIM, 2)])
    qcols = np.concatenate([h * HEAD_DIM + perm for h in range(N_HEADS)])
    kcols = aw + np.concatenate([h * HEAD_DIM + perm for h in range(N_KV_HEADS)])
    wl = w_in[l]
    w_t = jnp.concatenate([wl[:, qcols], wl[:, aw + kw:aw + 2 * kw]], axis=1).T.astype(BF16)
    w_n = jnp.concatenate([wl[:, kcols], wl[:, aw + 2 * kw:aw + 2 * kw + sw]], axis=1).astype(BF16)
    w_gates = wl[:, aw + 2 * kw + sw:].astype(BF16)
    qg3 = (jnp.tile(q_norm_g[l][perm], N_HEADS) * (HEAD_DIM ** -0.5 * math.log2(math.e))).reshape(N_HEADS, HEAD_DIM, 1)
    kg = jnp.tile(k_norm_g[l][perm], N_KV_HEADS)[None, :]
    bd = jnp.asarray(np.kron(np.eye(N_KV_HEADS), np.full((HEAD_DIM, HEAD_DIM), 1.0 / HEAD_DIM)), BF16)
    lng = ln_in_g[None, :]
    lnb = ln_in_b[None, :]
    cos, sin = _rope_tables(s)
    cosr = jnp.tile(cos, (1, 2 * N_KV_HEADS))
    sinr = jnp.tile(jnp.concatenate([-sin, sin], axis=1), (1, N_KV_HEADS))
    cost = cos.T
    sint = sin.T

    x2 = x.reshape(rows, d)
    tm_in = 1024
    qt, kk, vt, u2 = _inproj(x2, cosr, sinr, cost, sint, w_t, w_n, lng, lnb, qg3, kg, bd, tm_in, s // tm_in)
    meta_p = jnp.pad(meta_tokens, ((0, META_PAD - N_META), (0, 0)))
    ones_r = jnp.ones((META_PAD, kw), F32)
    _, km, vtm, um = _inproj(meta_p, ones_r, jnp.zeros_like(ones_r), jnp.ones((half, META_PAD), F32),
                             jnp.zeros((half, META_PAD), F32), w_t, w_n, lng, lnb, qg3, kg, bd, META_PAD, 1)
    um = um[:N_META]
    km = km[:, :N_META, :]
    vtm = vtm[:, :, :N_META]

    ya = _attention(qt, kk, vt, km, vtm, b, 512, 512)

    t_chunk = SSM_CHUNK
    hgrp = SSM_GROUP
    n = t_chunk * hgrp
    arow = jnp.stack([ssm_a_re[l], ssm_a_im[l]], axis=2)
    ldt = ssm_log_dt[l][:, :, None, None]
    bt = jnp.stack([ssm_b_re[l], ssm_b_im[l]], axis=2)
    bt = jnp.swapaxes(bt, 3, 4)
    ct = jnp.stack([ssm_c_re[l], ssm_c_im[l]], axis=2)
    ct = jnp.swapaxes(ct, 3, 4)
    mm, ww, vv, at = _ssm_prep(arow, ldt, bt, ct)
    nc = s // t_chunk
    um_g = um.reshape(N_META, g, hgrp).transpose(1, 0, 2).reshape(g, 1, N_META * hgrp)
    um_g = jnp.pad(um_g, ((0, 0), (0, 7), (n - N_META * hgrp, 0)))
    dvec = jnp.tile(ssm_d[l], (1, t_chunk))[:, None, :]
    ys = _ssm(u2.reshape(b, s, sw), um_g, mm, ww, vv, at, dvec).reshape(rows, sw)

    wr_hi = w_router[l].astype(BF16)
    wr_lo = (w_router[l] - wr_hi.astype(F32)).astype(BF16)
    lane_pad = ((0, 0), (0, ROUTER_LANES - N_EXPERTS))
    wr = jnp.concatenate([jnp.pad(wr_hi, lane_pad), jnp.pad(wr_lo, lane_pad)], axis=1)
    h1, tok, afft = _merge(
        x2, ya, ys, w_gates, w_glu[l].astype(BF16), b_glu[l][None, :],
        w_attn_br[l].astype(BF16), w_ssm_br[l].astype(BF16), w_o[l].astype(BF16),
        lng, lnb, ln1_g[l][None, :], ln1_b[l][None, :], wr, N_EXPERTS, 512)

    tri = jnp.asarray(np.triu(np.ones((ROUTE_TILE, ROUTE_TILE), np.float32)), BF16)
    scl, lo_f = _route(afft, tri, b, cap)
    nt = s // ROUTE_TILE
    lo3 = jnp.pad(lo_f.astype(jnp.int32), ((0, 0), (0, 0), (0, 1)), constant_values=cap)
    seg = lo3[:, :, 1:] - lo3[:, :, :-1]
    multi = lambda align: ((lo3[:, :, :-1] & (align - 1)) + seg + (PIECE - 1)) // PIECE > 1
    lo_i = jnp.concatenate([lo3.reshape(-1),
                            jnp.any(multi(8), axis=2).astype(jnp.int32).reshape(-1),
                            jnp.any(multi(16), axis=1).astype(jnp.int32).reshape(-1)])
    ye = _ffn(lo_i, tok.reshape(b, s, d), scl.reshape(b, N_EXPERTS, nt, ROUTE_TILE),
              afft.reshape(N_EXPERTS, b, nt, ROUTE_TILE), w_gate_e, w_up_e, w_down_e, l, cap, 512)
    out = _combine(lo_i, h1.reshape(b, s, d), jnp.swapaxes(scl, 1, 2), ye,
                   ln2_g[l][None, :], ln2_b[l][None, :], cap)
    return out
```

```python
import functools
import math

import numpy as np
import jax
import jax.numpy as jnp
from jax import lax
from jax.experimental import pallas as pl
from jax.experimental.pallas import tpu as pltpu

F32 = jnp.float32
BF16 = jnp.bfloat16

N_META = 16
GRID_W = 64
N_HEADS = 8
N_KV_HEADS = 2
HEAD_DIM = 64
ROPE_THETA = 10000.0
SSM_GROUP = 16
SSM_STATE = 64
N_EXPERTS = 16
CAPACITY_FACTOR = 2
LN_EPS = 1e-5
QK_EPS = 1e-6
DEPTH = 1
DEEPNORM_ALPHA = (2.0 * DEPTH) ** 0.25

SSM_CHUNK = 32
SSM_GROUPS_TOGETHER = 4
ROUTE_TILE = 256
ROUTER_LANES = 128
PIECE_SHIFT = 6
PIECE = 1 << PIECE_SHIFT
META_PAD = 128
V_ROWS = HEAD_DIM + 16
VMEM_LIMIT = 56 * 1024 * 1024


def _ln(x, g, b):
    mu = jnp.mean(x, axis=-1, keepdims=True)
    xc = x - mu
    var = jnp.mean(xc * xc, axis=-1, keepdims=True)
    return xc * lax.rsqrt(var + LN_EPS) * g + b


def _sigmoid(x):
    return 1.0 / (1.0 + jnp.exp(-x))


def _split(t):
    hi = t.astype(BF16)
    lo = (t - hi.astype(F32)).astype(BF16)
    return hi, lo


def _dot(a, b):
    return jnp.dot(a, b, preferred_element_type=F32)


def _dot_nt(a, b):
    return lax.dot_general(a, b, (((1,), (1,)), ((), ())), preferred_element_type=F32)


def _inproj_kernel(x_ref, cosr_ref, sinr_ref, cost_ref, sint_ref, wt_ref, wn_ref, lng_ref, lnb_ref,
                   qg_ref, kg_ref, bd_ref, qt_ref, k_ref, vt_ref, u_ref):
    aw = N_HEADS * HEAD_DIM
    kw = N_KV_HEADS * HEAD_DIM
    half = HEAD_DIM // 2
    h = _ln(x_ref[...], lng_ref[...], lnb_ref[...])
    hb = h.astype(BF16)
    pt = _dot_nt(wt_ref[...], hb)
    pn = _dot(hb, wn_ref[...])
    tm = hb.shape[0]

    qt = pt[0:aw].reshape(N_HEADS, HEAD_DIM, tm)
    ms = jnp.mean(qt * qt, axis=1, keepdims=True)
    qn = qt * lax.rsqrt(ms + QK_EPS) * qg_ref[...]
    x0 = qn[:, 0:half, :]
    x1 = qn[:, half:, :]
    c = cost_ref[...][None]
    s = sint_ref[...][None]
    qr = jnp.concatenate([x0 * c - x1 * s, x0 * s + x1 * c], axis=1)
    qt_ref[...] = qr.reshape(aw, tm).astype(BF16)
    vrow = lax.broadcasted_iota(jnp.int32, (N_KV_HEADS, V_ROWS - HEAD_DIM, tm), 1)
    vt_ref[...] = jnp.concatenate([pt[aw:aw + kw].reshape(N_KV_HEADS, HEAD_DIM, tm),
                                   jnp.where(vrow == 0, 1.0, 0.0)], axis=1).astype(BF16)

    kk = pn[:, 0:kw]
    hi, lo = _split(kk * kk)
    bd = bd_ref[...]
    msk = _dot(hi, bd) + _dot(lo, bd)
    kn = kk * lax.rsqrt(msk + QK_EPS) * kg_ref[...]
    lane = lax.broadcasted_iota(jnp.int32, kn.shape, 1)
    first = (lane & (HEAD_DIM - 1)) < half
    partner = jnp.where(first, pltpu.roll(kn, kw - half, 1), pltpu.roll(kn, half, 1))
    kr = (kn * cosr_ref[...] + partner * sinr_ref[...]).astype(BF16)
    for g in range(N_KV_HEADS):
        k_ref[g] = kr[:, g * HEAD_DIM:(g + 1) * HEAD_DIM]
    u_ref[...] = pn[:, kw:]


def _inproj(x2, cosr, sinr, cost, sint, wt, wn, lng, lnb, qg3, kg, bd, tm, n_tab_blocks):
    rows, d = x2.shape
    aw = N_HEADS * HEAD_DIM
    kw = N_KV_HEADS * HEAD_DIM
    uw = wn.shape[1] - kw
    half = HEAD_DIM // 2
    full = lambda a: pl.BlockSpec(a.shape, lambda i: (0,) * a.ndim)
    return pl.pallas_call(
        _inproj_kernel,
        grid=(rows // tm,),
        in_specs=[
            pl.BlockSpec((tm, d), lambda i: (i, 0)),
            pl.BlockSpec((tm, kw), lambda i: (i % n_tab_blocks, 0)),
            pl.BlockSpec((tm, kw), lambda i: (i % n_tab_blocks, 0)),
            pl.BlockSpec((half, tm), lambda i: (0, i % n_tab_blocks)),
            pl.BlockSpec((half, tm), lambda i: (0, i % n_tab_blocks)),
            full(wt), full(wn), full(lng), full(lnb), full(qg3), full(kg), full(bd),
        ],
        out_specs=[
            pl.BlockSpec((aw, tm), lambda i: (0, i)),
            pl.BlockSpec((N_KV_HEADS, tm, HEAD_DIM), lambda i: (0, i, 0)),
            pl.BlockSpec((N_KV_HEADS, V_ROWS, tm), lambda i: (0, 0, i)),
            pl.BlockSpec((tm, uw), lambda i: (i, 0)),
        ],
        out_shape=[
            jax.ShapeDtypeStruct((aw, rows), BF16),
            jax.ShapeDtypeStruct((N_KV_HEADS, rows, HEAD_DIM), BF16),
            jax.ShapeDtypeStruct((N_KV_HEADS, V_ROWS, rows), BF16),
            jax.ShapeDtypeStruct((rows, uw), F32),
        ],
        compiler_params=pltpu.CompilerParams(dimension_semantics=("arbitrary",),
                                             vmem_limit_bytes=VMEM_LIMIT),
        name="inproj",
    )(x2, cosr, sinr, cost, sint, wt, wn, lng, lnb, qg3, kg, bd)


def _attn_kernel(qt_ref, k0_ref, k1_ref, vta_ref, vtb_ref, km_ref, vtm_ref, o_ref, m_sc, acc_sc, s_sc, mb_sc):
    j = pl.program_id(2)
    last = pl.num_programs(2) - 1
    grp = N_HEADS // N_KV_HEADS
    heads = lambda h: slice(h * HEAD_DIM, (h + 1) * HEAD_DIM)

    def score(k_ref, slot, h):
        s = _dot(k_ref[h // grp], qt_ref[heads(h), :])
        s_sc[slot, h] = s
        mb_sc[slot, h:h + 1, :] = jnp.max(s, axis=0, keepdims=True)

    def softmax_pv(h, s, m_blk, vt_g):
        m_prev = m_sc[h:h + 1, :]
        m_new = jnp.maximum(m_prev, m_blk)
        alpha = jnp.exp2(m_prev - m_new)
        p = jnp.exp2(s - m_new).astype(BF16)
        acc_sc[h] = alpha * acc_sc[h] + _dot(vt_g, p)
        m_sc[h:h + 1, :] = m_new

    def consume(vt_ref, slot, h):
        softmax_pv(h, s_sc[slot, h], mb_sc[slot, h:h + 1, :], vt_ref[h // grp])

    @pl.when(j == 0)
    def _first():
        m_sc[...] = jnp.full(m_sc.shape, -jnp.inf, F32)
        acc_sc[...] = jnp.zeros(acc_sc.shape, F32)
        meta = [_dot(km_ref[h // grp], qt_ref[heads(h), :]) for h in range(N_HEADS)]
        for h in range(N_HEADS):
            score(k0_ref, 0, h)
            softmax_pv(h, meta[h], jnp.max(meta[h], axis=0, keepdims=True), vtm_ref[h // grp])
        for h in range(N_HEADS):
            score(k1_ref, 1, h)
            consume(vtb_ref, 0, h)

    @pl.when((j > 0) & (j < last))
    def _middle():
        for h in range(N_HEADS):
            score(k0_ref, 0, h)
            consume(vta_ref, 1, h)
        for h in range(N_HEADS):
            score(k1_ref, 1, h)
            consume(vtb_ref, 0, h)

    @pl.when(j == last)
    def _last():
        for h in range(N_HEADS):
            consume(vta_ref, 1, h)
        acc = acc_sc[...]
        tq = acc.shape[2]
        out_t = (acc[:, 0:HEAD_DIM, :] / acc[:, HEAD_DIM:HEAD_DIM + 1, :]).reshape(N_HEADS * HEAD_DIM, tq)
        o_ref[...] = out_t.T.astype(BF16)


def _attention(qt, kk, vt, km, vtm, nb, tq, tk):
    aw, rows = qt.shape
    s = rows // nb
    nq = s // tq
    nkb = s // tk
    kblk = lambda f: pl.BlockSpec((N_KV_HEADS, tk, HEAD_DIM),
                                  lambda bi, qi, j: (0, bi * nkb + jnp.clip(f(j), 0, nkb - 1), 0))
    vblk = lambda f: pl.BlockSpec((N_KV_HEADS, V_ROWS, tk),
                                  lambda bi, qi, j: (0, 0, bi * nkb + jnp.clip(f(j), 0, nkb - 1)))
    return pl.pallas_call(
        _attn_kernel,
        grid=(nb, nq, nkb // 2 + 1),
        in_specs=[
            pl.BlockSpec((aw, tq), lambda bi, qi, j: (0, bi * nq + qi)),
            kblk(lambda j: 2 * j), kblk(lambda j: 2 * j + 1),
            vblk(lambda j: 2 * j - 1), vblk(lambda j: 2 * j),
            pl.BlockSpec(km.shape, lambda bi, qi, j: (0, 0, 0)),
            pl.BlockSpec(vtm.shape, lambda bi, qi, j: (0, 0, 0)),
        ],
        out_specs=pl.BlockSpec((tq, aw), lambda bi, qi, j: (bi * nq + qi, 0)),
        out_shape=jax.ShapeDtypeStruct((rows, aw), BF16),
        scratch_shapes=[
            pltpu.VMEM((N_HEADS, tq), F32),
            pltpu.VMEM((N_HEADS, V_ROWS, tq), F32),
            pltpu.VMEM((2, N_HEADS, tk, tq), F32),
            pltpu.VMEM((2, N_HEADS, tq), F32),
        ],
        compiler_params=pltpu.CompilerParams(
            dimension_semantics=("arbitrary", "arbitrary", "arbitrary"),
            vmem_limit_bytes=VMEM_LIMIT),
        name="attn",
    )(qt, kk, kk, vt, vt, km, vtm)


def _ssm_prep_kernel(*refs):
    for d in range(2):
        _ssm_prep_direction(d, *refs)


def _ssm_prep_direction(d, arow_ref, ldt_ref, bt_ref, ct_ref, m_ref, w_ref, v_ref, at_ref):
    t_chunk = SSM_CHUNK
    shift = int(math.log2(SSM_GROUP))
    df = float(d)
    dt = jnp.exp(ldt_ref[d, 0])

    def abar(ar, ai):
        mag = jnp.exp(ar * dt)
        ang = ai * dt
        return mag * jnp.cos(ang), mag * jnp.sin(ang)

    nbits = int(math.log2(t_chunk))

    def ipow(br, bi, e):
        pr = pi = None
        for k in range(nbits):
            bit = ((e >> k) & 1) == 1
            if pr is None:
                pr, pi = jnp.where(bit, br, 1.0), jnp.where(bit, bi, 0.0)
            else:
                pr, pi = jnp.where(bit, pr * br - pi * bi, pr), jnp.where(bit, pr * bi + pi * br, pi)
            br, bi = br * br - bi * bi, 2.0 * br * bi
        return pr, pi, br, bi

    ar_r = arow_ref[d, 0, 0:1, :]
    ai_r = arow_ref[d, 0, 1:2, :]
    abr, abi = abar(ar_r, ai_r)
    nr = abr - 1.0
    ni = abi
    den = ar_r * ar_r + ai_r * ai_r
    cr = (nr * ar_r + ni * ai_r) / den
    ci = (ni * ar_r - nr * ai_r) / den
    nstate = bt_ref.shape[4]
    nrow = t_chunk * SSM_GROUP

    def tile_rows(x):
        return jnp.broadcast_to(x[None], (t_chunk, SSM_GROUP, nstate)).reshape(nrow, nstate)

    btr = tile_rows(bt_ref[d, 0, 0])
    bti = tile_rows(bt_ref[d, 0, 1])
    bbr = cr * btr - ci * bti
    bbi = cr * bti + ci * btr
    jr = (lax.broadcasted_iota(jnp.int32, (nrow, 1), 0) >> shift).astype(F32)
    lj = jr + df * ((t_chunk - 1) - 2.0 * jr)
    step = lax.broadcasted_iota(jnp.int32, (t_chunk, 1), 0)
    lstep = (t_chunk - 1) - step if d == 1 else step

    def rep_rows(x):
        return jnp.broadcast_to(x[:, None, :], (t_chunk, SSM_GROUP, nstate)).reshape(nrow, nstate)

    mag2 = abr * abr + abi * abi
    enr, eni, _, _ = ipow(abr / mag2, -abi / mag2, lstep)
    enr, eni = rep_rows(enr), rep_rows(eni)
    bmr = enr * bbr - eni * bbi
    bmi = enr * bbi + eni * bbr
    ewr, ewi, atr, ati = ipow(abr, abi, (t_chunk - 1) - lstep)
    ewr, ewi = rep_rows(ewr), rep_rows(ewi)
    w_ref[d, 0] = jnp.concatenate([ewr * bbr - ewi * bbi, ewr * bbi + ewi * bbr], axis=1).astype(BF16)
    at_ref[d, 0] = jnp.concatenate([atr, ati], axis=1)

    ncol = t_chunk * SSM_GROUP
    lane = lax.broadcasted_iota(jnp.int32, (1, ncol), 1)
    tci = lane >> shift
    tile = jnp.where(lax.broadcasted_iota(jnp.int32, (SSM_GROUP, ncol), 0) == (lane & (SSM_GROUP - 1)),
                     1.0, 0.0).astype(BF16)

    def tile_cols(x):
        hi, lo = _split(x)
        lo2 = (x - hi.astype(F32) - lo.astype(F32)).astype(BF16)
        return _dot(hi, tile) + _dot(lo, tile) + _dot(lo2, tile)

    ctr = tile_cols(ct_ref[d, 0, 0])
    cti = tile_cols(ct_ref[d, 0, 1])
    tc = tci.astype(F32)
    lt = tc + df * ((t_chunk - 1) - 2.0 * tc)
    stepl = lax.broadcasted_iota(jnp.int32, (1, t_chunk), 1)
    lstepl = (t_chunk - 1) - stepl if d == 1 else stepl
    diag = (lax.broadcasted_iota(jnp.int32, (nstate, nstate), 0)
            == lax.broadcasted_iota(jnp.int32, (nstate, nstate), 1))
    abr_c = jnp.sum(jnp.where(diag, abr, 0.0), axis=1, keepdims=True)
    abi_c = jnp.sum(jnp.where(diag, abi, 0.0), axis=1, keepdims=True)
    rep = jnp.where(lax.broadcasted_iota(jnp.int32, (t_chunk, ncol), 0) == tci, 1.0, 0.0).astype(BF16)

    def rep_cols(x):
        hi, lo = _split(x)
        lo2 = (x - hi.astype(F32) - lo.astype(F32)).astype(BF16)
        return _dot(hi, rep) + _dot(lo, rep) + _dot(lo2, rep)

    ecr, eci = [rep_cols(x) for x in ipow(abr_c, abi_c, lstepl)[:2]]
    cmr = ctr * ecr - cti * eci
    cmi = ctr * eci + cti * ecr
    lhs_hi, lhs_lo = _split(jnp.concatenate([bmr, -bmi], axis=1))
    rhs_hi, rhs_lo = _split(jnp.concatenate([cmr, cmi], axis=0))
    m = _dot(lhs_hi, rhs_hi) + _dot(lhs_hi, rhs_lo) + _dot(lhs_lo, rhs_hi)
    m_ref[d, 0] = jnp.where(lj <= lt, m, 0.0).astype(BF16)
    c1r = cmr * abr_c - cmi * abi_c
    c1i = cmr * abi_c + cmi * abr_c
    v_ref[d, 0] = jnp.concatenate([c1r, -c1i], axis=0).astype(BF16)


def _ssm_prep(arow, ldt, bt, ct):
    nd, g = arow.shape[0], arow.shape[1]
    p = SSM_STATE
    n = SSM_CHUNK * SSM_GROUP
    blk = lambda a: pl.BlockSpec((nd, 1) + a.shape[2:], lambda gi: (0, gi) + (0,) * (a.ndim - 2))
    return pl.pallas_call(
        _ssm_prep_kernel,
        grid=(g,),
        in_specs=[blk(arow), blk(ldt), blk(bt), blk(ct)],
        out_specs=[
            pl.BlockSpec((nd, 1, n, n), lambda gi: (0, gi, 0, 0)),
            pl.BlockSpec((nd, 1, n, 2 * p), lambda gi: (0, gi, 0, 0)),
            pl.BlockSpec((nd, 1, 2 * p, n), lambda gi: (0, gi, 0, 0)),
            pl.BlockSpec((nd, 1, 1, 2 * p), lambda gi: (0, gi, 0, 0)),
        ],
        out_shape=[
            jax.ShapeDtypeStruct((nd, g, n, n), BF16),
            jax.ShapeDtypeStruct((nd, g, n, 2 * p), BF16),
            jax.ShapeDtypeStruct((nd, g, 2 * p, n), BF16),
            jax.ShapeDtypeStruct((nd, g, 1, 2 * p), F32),
        ],
        compiler_params=pltpu.CompilerParams(dimension_semantics=("arbitrary",)),
        name="ssm_prep",
    )(arow, ldt, bt, ct)


def _chunk_carry(chains):
    p = SSM_STATE
    z0, a0 = chains[0][0], chains[0][1]
    nc = z0.shape[0]
    row = lax.broadcasted_iota(jnp.int32, z0.shape, 0)
    is_re = lax.broadcasted_iota(jnp.int32, a0.shape, 1) < p
    sign = jnp.where(is_re, -1.0, 1.0)

    def parts(ap):
        sw = pltpu.roll(ap, p, 1)
        return jnp.where(is_re, ap, sw), sign * jnp.where(is_re, sw, ap)

    def cmul(x, ar_full, ai_sgn):
        return x * ar_full + pltpu.roll(x, p, 1) * ai_sgn

    coef = [parts(a) for _, a, _, _ in chains]
    firsts = [(nc - 1) if rev else 0 for _, _, _, rev in chains]
    es = [z + jnp.where(row == f, cmul(s0, *c), 0.0) for (z, _, s0, _), f, c in zip(chains, firsts, coef)]
    k = 1
    while k < nc:
        for i, (_, _, _, rev) in enumerate(chains):
            if rev:
                sh = jnp.where(row < nc - k, pltpu.roll(es[i], nc - k, 0), 0.0)
            else:
                sh = jnp.where(row >= k, pltpu.roll(es[i], k, 0), 0.0)
            es[i] = es[i] + cmul(sh, *coef[i])
        k *= 2
        if k < nc:
            coef = [parts(cmul(jnp.where(is_re, ar, sign * ai), ar, ai)) for ar, ai in coef]
    outs = []
    for e, (_, _, s0, rev), f in zip(es, chains, firsts):
        outs.append(jnp.where(row == f, s0, pltpu.roll(e, nc - 1 if rev else 1, 0)))
    return outs


def _ssm_kernel(u_ref, um_ref, m_ref, w_ref, v_ref, at_ref, dvec_ref, y_ref, ug_sc, yg_sc):
    p = SSM_STATE
    t_chunk = SSM_CHUNK
    hgrp = SSM_GROUP
    gb = u_ref.shape[2] // hgrp
    per_tile = u_ref.shape[2] // hgrp
    nc = u_ref.shape[1] // t_chunk
    lane_blk = lax.broadcasted_iota(jnp.int32, (nc, u_ref.shape[2]), 1) >> int(math.log2(hgrp))

    width = u_ref.shape[2]

    def block_transpose(arrs):
        arrs = list(arrs)
        d = len(arrs) // 2
        while d >= 1:
            low_bit = (lane_blk & d) == 0
            for i in range(len(arrs)):
                if i & d == 0:
                    lo, hi = arrs[i], arrs[i + d]
                    arrs[i] = jnp.where(low_bit, lo, pltpu.roll(hi, d * hgrp, 1))
                    arrs[i + d] = jnp.where(low_bit, pltpu.roll(lo, width - d * hgrp, 1), hi)
            d //= 2
        return arrs

    for q in range(t_chunk // per_tile):
        steps = [u_ref[0, pl.ds(q * per_tile + r, nc, stride=t_chunk), :] for r in range(per_tile)]
        for g, tile in enumerate(block_transpose(steps)):
            ug_sc[g, :, q * width:(q + 1) * width] = tile

    def groups(i, carry):
        gs = [i * SSM_GROUPS_TOGETHER + k for k in range(SSM_GROUPS_TOGETHER)]
        us = [ug_sc[g] for g in gs]
        ubs = [u.astype(BF16) for u in us]
        yin = [[_dot(ub, m_ref[d, g]) for d in range(2)] for g, ub in zip(gs, ubs)]
        chains = []
        for g, ub in zip(gs, ubs):
            for d in range(2):
                if d == 0:
                    s0 = _dot(um_ref[g].astype(BF16), w_ref[0, g])[0:1, :]
                else:
                    s0 = jnp.zeros((1, 2 * p), F32)
                chains.append((_dot(ub, w_ref[d, g]), at_ref[d, g], s0, d == 1))
        s_in = _chunk_carry(chains)
        for k, g in enumerate(gs):
            y = dvec_ref[g] * us[k] + yin[k][0] + yin[k][1]
            for d in range(2):
                y = y + _dot(s_in[2 * k + d].astype(BF16), v_ref[d, g])
            yg_sc[g] = y
        return carry

    lax.fori_loop(0, gb // SSM_GROUPS_TOGETHER, groups, 0)

    for q in range(t_chunk // per_tile):
        srcs = [yg_sc[g, :, q * width:(q + 1) * width] for g in range(gb)]
        for r, row in enumerate(block_transpose(srcs)):
            y_ref[0, pl.ds(q * per_tile + r, nc, stride=t_chunk), :] = row


def _ssm(u3, um2, m, w, v, at, dvec):
    b, s, sw = u3.shape
    g = sw // SSM_GROUP
    n = SSM_CHUNK * SSM_GROUP
    p = SSM_STATE
    lanes = 128
    gb = lanes // SSM_GROUP
    return pl.pallas_call(
        _ssm_kernel,
        grid=(b, sw // lanes),
        in_specs=[
            pl.BlockSpec((1, s, lanes), lambda bi, gi: (bi, 0, gi)),
            pl.BlockSpec((gb, 8, n), lambda bi, gi: (gi, 0, 0)),
            pl.BlockSpec((2, gb, n, n), lambda bi, gi: (0, gi, 0, 0)),
            pl.BlockSpec((2, gb, n, 2 * p), lambda bi, gi: (0, gi, 0, 0)),
            pl.BlockSpec((2, gb, 2 * p, n), lambda bi, gi: (0, gi, 0, 0)),
            pl.BlockSpec((2, gb, 1, 2 * p), lambda bi, gi: (0, gi, 0, 0)),
            pl.BlockSpec((gb, 1, n), lambda bi, gi: (gi, 0, 0)),
        ],
        out_specs=pl.BlockSpec((1, s, lanes), lambda bi, gi: (bi, 0, gi)),
        out_shape=jax.ShapeDtypeStruct((b, s, sw), F32),
        scratch_shapes=[pltpu.VMEM((gb, s // SSM_CHUNK, n), F32), pltpu.VMEM((gb, s // SSM_CHUNK, n), F32)],
        compiler_params=pltpu.CompilerParams(dimension_semantics=("arbitrary", "arbitrary"),
                                             vmem_limit_bytes=VMEM_LIMIT),
        name="ssm",
    )(u3, um2, m, w, v, at, dvec)


def _merge_kernel(x_ref, ya_ref, ys_ref, wgt_ref, wglu_ref, bglu_ref, wab_ref, wsb_ref, wo_ref,
                  lng_ref, lnb_ref, l1g_ref, l1b_ref, wr_ref,
                  h1_ref, tok_ref, afft_ref):
    d = x_ref.shape[1]
    h = _ln(x_ref[...], lng_ref[...], lnb_ref[...])
    gates = _dot(h.astype(BF16), wgt_ref[...])
    ga = _sigmoid(gates[:, 0:d])
    gs = _sigmoid(gates[:, d:2 * d])
    ys = ys_ref[...]
    y = 0.5 * ys * (1.0 + jnp.tanh(math.sqrt(2.0 / math.pi) * (ys + 0.044715 * (ys * ys * ys))))
    yg = y * _sigmoid(_dot(y.astype(BF16), wglu_ref[...]) + bglu_ref[...])
    merged = ga * _dot(ya_ref[...], wab_ref[...]) + gs * _dot(yg.astype(BF16), wsb_ref[...])
    h1 = _ln(DEEPNORM_ALPHA * h + _dot(merged.astype(BF16), wo_ref[...]), l1g_ref[...], l1b_ref[...])
    h1_ref[...] = h1
    hi, lo = _split(h1)
    tok_ref[...] = hi
    ne = afft_ref.shape[0]
    cross = _dot(hi, wr_ref[...])
    logits = cross[:, 0:ROUTER_LANES] + cross[:, ROUTER_LANES:] + _dot(lo, wr_ref[:, 0:ROUTER_LANES])
    lane = lax.broadcasted_iota(jnp.int32, logits.shape, 1)
    logits = jnp.where(lane < ne, logits, -jnp.inf)
    ex = jnp.exp(logits - jnp.max(logits, axis=1, keepdims=True))
    aff = ex / jnp.sum(ex, axis=1, keepdims=True)
    afft_ref[...] = aff.T[0:ne, :]


def _merge(x2, ya, ys, wgt, wglu, bglu, wab, wsb, wo, lng, lnb, l1g, l1b, wr, ne, tm):
    rows, d = x2.shape
    full = lambda a: pl.BlockSpec(a.shape, lambda i: (0,) * a.ndim)
    return pl.pallas_call(
        _merge_kernel,
        grid=(rows // tm,),
        in_specs=[
            pl.BlockSpec((tm, d), lambda i: (i, 0)),
            pl.BlockSpec((tm, ya.shape[1]), lambda i: (i, 0)),
            pl.BlockSpec((tm, ys.shape[1]), lambda i: (i, 0)),
            full(wgt), full(wglu), full(bglu), full(wab), full(wsb), full(wo),
            full(lng), full(lnb), full(l1g), full(l1b), full(wr),
        ],
        out_specs=[
            pl.BlockSpec((tm, d), lambda i: (i, 0)),
            pl.BlockSpec((tm, d), lambda i: (i, 0)),
            pl.BlockSpec((ne, tm), lambda i: (0, i)),
        ],
        out_shape=[
            jax.ShapeDtypeStruct((rows, d), F32),
            jax.ShapeDtypeStruct((rows, d), BF16),
            jax.ShapeDtypeStruct((ne, rows), F32),
        ],
        compiler_params=pltpu.CompilerParams(dimension_semantics=("arbitrary",),
                                             vmem_limit_bytes=VMEM_LIMIT),
        name="merge",
    )(x2, ya, ys, wgt, wglu, bglu, wab, wsb, wo, lng, lnb, l1g, l1b, wr)


def _route_kernel(afft_ref, tri_ref, scl_ref, lo_ref, *, cap):
    aff = afft_ref[...]
    ne, s = aff.shape
    capf = float(cap)

    def as_float(bits):
        return lax.bitcast_convert_type(bits, F32)

    def search(i, t):
        cand = t | jnp.left_shift(jnp.int32(1), 30 - i)
        cnt = jnp.sum(jnp.where(aff >= as_float(cand), 1.0, 0.0), axis=1, keepdims=True)
        return jnp.where(cnt >= capf, cand, t)

    thr_bits = lax.fori_loop(0, 31, search, jnp.zeros((ne, 1), jnp.int32))
    gt = aff >= as_float(thr_bits + 1)
    eq = (aff >= as_float(thr_bits)) & jnp.logical_not(gt)
    need = capf - jnp.sum(jnp.where(gt, 1.0, 0.0), axis=1, keepdims=True)
    tri = tri_ref[...]
    rt = ROUTE_TILE
    nt = s // rt
    col = lax.broadcasted_iota(jnp.int32, (ne, nt), 1)
    carry_eq = jnp.zeros((ne, 1), F32)
    carry_sel = jnp.zeros((ne, 1), F32)
    lo_val = jnp.zeros((ne, nt), F32)
    for t in range(nt):
        sl = slice(t * rt, (t + 1) * rt)
        eq_b = eq[:, sl]
        ceq = _dot(jnp.where(eq_b, 1.0, 0.0).astype(BF16), tri)
        sel_b = gt[:, sl] | (eq_b & ((ceq + carry_eq) <= need))
        carry_eq = carry_eq + ceq[:, rt - 1:rt]
        csel = _dot(jnp.where(sel_b, 1.0, 0.0).astype(BF16), tri)
        scl_ref[0, :, sl] = jnp.where(sel_b, csel, 0.0)
        lo_val = jnp.where(col == t, carry_sel, lo_val)
        carry_sel = carry_sel + csel[:, rt - 1:rt]
    lo_ref[0] = lo_val


def _route(afft, tri, nb, cap):
    ne, rows = afft.shape
    s = rows // nb
    nt = s // ROUTE_TILE
    return pl.pallas_call(
        functools.partial(_route_kernel, cap=cap),
        grid=(nb,),
        in_specs=[
            pl.BlockSpec((ne, s), lambda b: (0, b)),
            pl.BlockSpec(tri.shape, lambda b: (0, 0)),
        ],
        out_specs=[
            pl.BlockSpec((1, ne, s), lambda b: (b, 0, 0)),
            pl.BlockSpec((1, ne, nt), lambda b: (b, 0, 0)),
        ],
        out_shape=[
            jax.ShapeDtypeStruct((nb, ne, s), F32),
            jax.ShapeDtypeStruct((nb, ne, nt), F32),
        ],
        compiler_params=pltpu.CompilerParams(dimension_semantics=("arbitrary",)),
        name="route",
    )(afft, tri)


def _ffn_kernel(lo_ref, t_ref, scl_ref, aff_ref, wg_ref, wu_ref, wd_ref, ye_ref, xe_sc, xb_sc, gate_sc, acc_sc,
                *, cap, flag_off):
    fc = pl.program_id(2)

    @pl.when(fc == 0)
    def _gather():
        _ffn_gather(lo_ref, t_ref, scl_ref, aff_ref, xe_sc, xb_sc, gate_sc, cap=cap, flag_off=flag_off)
        acc_sc[...] = jnp.zeros(acc_sc.shape, F32)

    xb = xb_sc[...]
    g = _dot(xb, wg_ref[0, 0].astype(BF16))
    u = _dot(xb, wu_ref[0, 0].astype(BF16))
    hh = (g * _sigmoid(g)) * u
    acc_sc[...] += _dot(hh.astype(BF16), wd_ref[0, 0].astype(BF16))

    @pl.when(fc == pl.num_programs(2) - 1)
    def _emit():
        ye_ref[0, 0] = (acc_sc[...] * gate_sc[0:cap, :]).astype(BF16)


def _ffn_gather(lo_ref, t_ref, scl_ref, aff_ref, xe_sc, xb_sc, gate_sc, *, cap, flag_off):
    b = pl.program_id(0)
    e = pl.program_id(1)
    ne = pl.num_programs(1)
    rt = ROUTE_TILE
    nt = t_ref.shape[1] // rt
    xe_sc[...] = jnp.zeros(xe_sc.shape, F32)
    gate_sc[...] = jnp.zeros(gate_sc.shape, F32)
    r = lax.broadcasted_iota(jnp.int32, (PIECE, rt), 0).astype(F32)

    def tile_info(tau):
        base = (b * ne + e) * (nt + 1) + tau
        lo = lo_ref[base]
        n_pieces = jnp.right_shift((lo & 7) + (lo_ref[base + 1] - lo) + (PIECE - 1), PIECE_SHIFT)
        return lo, n_pieces

    def add_piece(tau, lo, p):
        off = lo & 7
        scl = scl_ref[0, 0, pl.ds(tau, 1), :]
        tok = t_ref[0, pl.ds(pl.multiple_of(tau * rt, rt), rt), :]
        shift = (off - 1 - p * PIECE).astype(F32)
        chosen = (scl > 0.0) & ((scl + shift) == r)
        rows = pl.ds(pl.multiple_of(lo - off + p * PIECE, 8), PIECE)
        xe_sc[rows, :] += _dot(jnp.where(chosen, 1.0, 0.0).astype(BF16), tok)
        gate_sc[rows, :] += jnp.sum(jnp.where(chosen, aff_ref[0, 0, pl.ds(tau, 1), :], 0.0), axis=1, keepdims=True)

    def first_piece(tau, carry):
        add_piece(tau, tile_info(tau)[0], 0)
        return carry

    def more_pieces(tau, carry):
        lo, n_pieces = tile_info(tau)
        lax.fori_loop(1, n_pieces, lambda p, c: (add_piece(tau, lo, p), c)[1], 0)
        return carry

    lax.fori_loop(0, nt, first_piece, 0, unroll=4)

    @pl.when(lo_ref[flag_off + b * ne + e] > 0)
    def _rare():
        lax.fori_loop(0, nt, more_pieces, 0)

    xb_sc[...] = xe_sc[0:cap, :].astype(BF16)


def _ffn(lo_i, t, scl4, aff4, wg, wu, wd, layer, cap, f_chunk):
    b, s, d = t.shape
    _, ne, _, f = wg.shape
    nt = s // ROUTE_TILE
    grid_spec = pltpu.PrefetchScalarGridSpec(
        num_scalar_prefetch=1,
        grid=(b, ne, f // f_chunk),
        in_specs=[
            pl.BlockSpec((1, s, d), lambda bi, ei, fi, lo: (bi, 0, 0), pipeline_mode=pl.Buffered(1)),
            pl.BlockSpec((1, 1, nt, ROUTE_TILE), lambda bi, ei, fi, lo: (bi, ei, 0, 0)),
            pl.BlockSpec((1, 1, nt, ROUTE_TILE), lambda bi, ei, fi, lo: (ei, bi, 0, 0)),
            pl.BlockSpec((1, 1, d, f_chunk), lambda bi, ei, fi, lo: (layer, ei, 0, fi)),
            pl.BlockSpec((1, 1, d, f_chunk), lambda bi, ei, fi, lo: (layer, ei, 0, fi)),
            pl.BlockSpec((1, 1, f_chunk, d), lambda bi, ei, fi, lo: (layer, ei, fi, 0)),
        ],
        out_specs=pl.BlockSpec((1, 1, cap, d), lambda bi, ei, fi, lo: (bi, ei, 0, 0)),
        scratch_shapes=[pltpu.VMEM((cap + PIECE, d), F32), pltpu.VMEM((cap, d), BF16),
                        pltpu.VMEM((cap + PIECE, 1), F32), pltpu.VMEM((cap, d), F32)],
    )
    return pl.pallas_call(
        functools.partial(_ffn_kernel, cap=cap, flag_off=b * ne * (nt + 1)),
        grid_spec=grid_spec,
        out_shape=jax.ShapeDtypeStruct((b, ne, cap, d), BF16),
        compiler_params=pltpu.CompilerParams(dimension_semantics=("arbitrary", "arbitrary", "arbitrary"),
                                             vmem_limit_bytes=VMEM_LIMIT),
        name="ffn",
    )(lo_i, t, scl4, aff4, wg, wu, wd)


def _combine_kernel(lo_ref, h1_ref, sclt_ref, ye_hbm, g_ref, b_ref, o_ref, win, xwin, acc_sc, sem, xsem,
                    *, cap, flag_off):
    b = pl.program_id(0)
    tau = pl.program_id(1)
    nb = pl.num_programs(0)
    nt = pl.num_programs(1)
    ne = win.shape[1]
    d = win.shape[3]
    step = b * nt + tau
    slot = step & 1

    def pieces(bi, ti, e):
        base = (bi * ne + e) * (nt + 1) + ti
        lo = lo_ref[base]
        n_sel = lo_ref[base + 1] - lo
        n_pieces = jnp.maximum(jnp.right_shift((lo & 15) + n_sel + (PIECE - 1), PIECE_SHIFT), 1)
        first = jnp.minimum(lo - (lo & 15), cap - n_pieces * PIECE)
        return lo, first, n_pieces

    def copy(bi, e, first, p, buf, s):
        start = pl.multiple_of(first + p * PIECE, 16)
        return pltpu.make_async_copy(ye_hbm.at[bi, e, pl.ds(start, PIECE), :], buf, s)

    def first_pieces(bi, ti, sl):
        return [copy(bi, e, pieces(bi, ti, e)[1], 0, win.at[sl, e], sem.at[sl, e]) for e in range(ne)]

    @pl.when(step == 0)
    def _prime():
        for i, cp in enumerate(first_pieces(b, tau, slot)):
            cp.start(priority=i % 2)

    @pl.when(step + 1 < nb * nt)
    def _prefetch():
        wrap = tau + 1 == nt
        nxt = first_pieces(jnp.where(wrap, b + 1, b), jnp.where(wrap, 0, tau + 1), 1 - slot)
        for i, cp in enumerate(nxt):
            cp.start(priority=i % 2)

    for cp in first_pieces(b, tau, slot):
        cp.wait()

    sclt = sclt_ref[0]
    col = lax.broadcasted_iota(jnp.int32, (1, ne * PIECE), 1)
    grp = jnp.right_shift(col, PIECE_SHIFT)
    expand = jnp.where(lax.broadcasted_iota(jnp.int32, (ne, ne * PIECE), 0) == grp, 1.0, 0.0).astype(BF16)
    scl = _dot(sclt.astype(BF16), expand)
    shift = jnp.zeros(col.shape, F32)
    for e in range(ne):
        lo, first, _ = pieces(b, tau, e)
        shift = jnp.where(grp == e, (lo - first - 1).astype(F32), shift)
    r = (col & (PIECE - 1)).astype(F32)
    onehot = jnp.where((scl > 0.0) & ((scl + shift) == r), 1.0, 0.0).astype(BF16)
    acc_sc[...] = DEEPNORM_ALPHA * h1_ref[0] + _dot(onehot, win[slot].reshape(ne * PIECE, d))

    @pl.when(lo_ref[flag_off + step] > 0)
    def _rare():
        for e in range(ne):
            lo, first, n_pieces = pieces(b, tau, e)

            def extra(p, c, e=e, lo=lo, first=first):
                cp = copy(b, e, first, p, xwin, xsem.at[0])
                cp.start()
                cp.wait()
                se = sclt_ref[0][:, e:e + 1]
                rr = lax.broadcasted_iota(jnp.int32, (se.shape[0], PIECE), 1).astype(F32)
                oh = jnp.where((se > 0.0) & ((se + (lo - first - p * PIECE - 1).astype(F32)) == rr), 1.0, 0.0)
                acc_sc[...] += _dot(oh.astype(BF16), xwin[...])
                return c

            lax.fori_loop(1, n_pieces, extra, 0)

    o_ref[0] = _ln(acc_sc[...], g_ref[...], b_ref[...])


def _combine(lo_i, h1, sclt, ye, g, bb, cap):
    b, s, d = h1.shape
    ne = sclt.shape[2]
    rt = ROUTE_TILE
    grid_spec = pltpu.PrefetchScalarGridSpec(
        num_scalar_prefetch=1,
        grid=(b, s // rt),
        in_specs=[
            pl.BlockSpec((1, rt, d), lambda bi, ti, lo: (bi, ti, 0)),
            pl.BlockSpec((1, rt, ne), lambda bi, ti, lo: (bi, ti, 0)),
            pl.BlockSpec(memory_space=pl.ANY),
            pl.BlockSpec(g.shape, lambda bi, ti, lo: (0, 0)),
            pl.BlockSpec(bb.shape, lambda bi, ti, lo: (0, 0)),
        ],
        out_specs=pl.BlockSpec((1, rt, d), lambda bi, ti, lo: (bi, ti, 0)),
        scratch_shapes=[pltpu.VMEM((2, ne, PIECE, d), BF16), pltpu.VMEM((PIECE, d), BF16),
                        pltpu.VMEM((rt, d), F32),
                        pltpu.SemaphoreType.DMA((2, ne)), pltpu.SemaphoreType.DMA((1,))],
    )
    return pl.pallas_call(
        functools.partial(_combine_kernel, cap=cap, flag_off=b * ne * (s // rt + 1) + b * ne),
        grid_spec=grid_spec,
        out_shape=jax.ShapeDtypeStruct((b, s, d), F32),
        compiler_params=pltpu.CompilerParams(dimension_semantics=("arbitrary", "arbitrary"),
                                             vmem_limit_bytes=VMEM_LIMIT),
        name="combine",
    )(lo_i, h1, sclt, ye, g, bb)


def _rope_tables(n_tokens):
    half = HEAD_DIM // 2
    inv_freq = ROPE_THETA ** (-jnp.arange(0, half, 2, dtype=F32) / half)
    rows = n_tokens // GRID_W
    row = jnp.repeat(jnp.arange(rows, dtype=F32), GRID_W)
    colv = jnp.tile(jnp.arange(GRID_W, dtype=F32), rows)
    ang = jnp.concatenate([row[:, None] * inv_freq, colv[:, None] * inv_freq], axis=-1)
    return jnp.cos(ang), jnp.sin(ang)


def kernel(x, meta_tokens, ln_in_g, ln_in_b, w_in, q_norm_g, k_norm_g, ssm_a_re, ssm_a_im, ssm_log_dt,
           ssm_b_re, ssm_b_im, ssm_c_re, ssm_c_im, ssm_d, w_glu, b_glu, w_attn_br, w_ssm_br, w_o,
           ln1_g, ln1_b, w_router, w_gate_e, w_up_e, w_down_e, ln2_g, ln2_b):
    b, s, d = x.shape
    aw = N_HEADS * HEAD_DIM
    kw = N_KV_HEADS * HEAD_DIM
    sw = d // 2
    g = sw // SSM_GROUP
    half = HEAD_DIM // 2
    cap = CAPACITY_FACTOR * s // N_EXPERTS
    rows = b * s
    assert w_in.shape[0] == DEPTH == 1 and s % (2 * 512) == 0 and d == 2 * aw, (w_in.shape, x.shape)
    l = 0

    perm = np.concatenate([np.arange(0, HEAD_DIM, 2), np.arange(1, HEAD_DIM, 2)])
    qcols = np.concatenate([h * HEAD_DIM + perm for h in range(N_HEADS)])
    kcols = aw + np.concatenate([h * HEAD_DIM + perm for h in range(N_KV_HEADS)])
    wl = w_in[l]
    w_t = jnp.concatenate([wl[:, qcols], wl[:, aw + kw:aw + 2 * kw]], axis=1).T.astype(BF16)
    w_n = jnp.concatenate([wl[:, kcols], wl[:, aw + 2 * kw:aw + 2 * kw + sw]], axis=1).astype(BF16)
    w_gates = wl[:, aw + 2 * kw + sw:].astype(BF16)
    qg3 = (jnp.tile(q_norm_g[l][perm], N_HEADS) * (HEAD_DIM ** -0.5 * math.log2(math.e))).reshape(N_HEADS, HEAD_DIM, 1)
    kg = jnp.tile(k_norm_g[l][perm], N_KV_HEADS)[None, :]
    bd = jnp.asarray(np.kron(np.eye(N_KV_HEADS), np.full((HEAD_DIM, HEAD_DIM), 1.0 / HEAD_DIM)), BF16)
    lng = ln_in_g[None, :]
    lnb = ln_in_b[None, :]
    cos, sin = _rope_tables(s)
    cosr = jnp.tile(cos, (1, 2 * N_KV_HEADS))
    sinr = jnp.tile(jnp.concatenate([-sin, sin], axis=1), (1, N_KV_HEADS))
    cost = cos.T
    sint = sin.T

    x2 = x.reshape(rows, d)
    tm_in = 1024
    qt, kk, vt, u2 = _inproj(x2, cosr, sinr, cost, sint, w_t, w_n, lng, lnb, qg3, kg, bd, tm_in, s // tm_in)
    meta_p = jnp.pad(meta_tokens, ((0, META_PAD - N_META), (0, 0)))
    ones_r = jnp.ones((META_PAD, kw), F32)
    _, km, vtm, um = _inproj(meta_p, ones_r, jnp.zeros_like(ones_r), jnp.ones((half, META_PAD), F32),
                             jnp.zeros((half, META_PAD), F32), w_t, w_n, lng, lnb, qg3, kg, bd, META_PAD, 1)
    um = um[:N_META]
    km = km[:, :N_META, :]
    vtm = vtm[:, :, :N_META]

    ya = _attention(qt, kk, vt, km, vtm, b, 512, 512)

    t_chunk = SSM_CHUNK
    hgrp = SSM_GROUP
    n = t_chunk * hgrp
    arow = jnp.stack([ssm_a_re[l], ssm_a_im[l]], axis=2)
    ldt = ssm_log_dt[l][:, :, None, None]
    bt = jnp.stack([ssm_b_re[l], ssm_b_im[l]], axis=2)
    bt = jnp.swapaxes(bt, 3, 4)
    ct = jnp.stack([ssm_c_re[l], ssm_c_im[l]], axis=2)
    ct = jnp.swapaxes(ct, 3, 4)
    mm, ww, vv, at = _ssm_prep(arow, ldt, bt, ct)
    nc = s // t_chunk
    um_g = um.reshape(N_META, g, hgrp).transpose(1, 0, 2).reshape(g, 1, N_META * hgrp)
    um_g = jnp.pad(um_g, ((0, 0), (0, 7), (n - N_META * hgrp, 0)))
    dvec = jnp.tile(ssm_d[l], (1, t_chunk))[:, None, :]
    ys = _ssm(u2.reshape(b, s, sw), um_g, mm, ww, vv, at, dvec).reshape(rows, sw)

    wr_hi = w_router[l].astype(BF16)
    wr_lo = (w_router[l] - wr_hi.astype(F32)).astype(BF16)
    lane_pad = ((0, 0), (0, ROUTER_LANES - N_EXPERTS))
    wr = jnp.concatenate([jnp.pad(wr_hi, lane_pad), jnp.pad(wr_lo, lane_pad)], axis=1)
    h1, tok, afft = _merge(
        x2, ya, ys, w_gates, w_glu[l].astype(BF16), b_glu[l][None, :],
        w_attn_br[l].astype(BF16), w_ssm_br[l].astype(BF16), w_o[l].astype(BF16),
        lng, lnb, ln1_g[l][None, :], ln1_b[l][None, :], wr, N_EXPERTS, 512)

    tri = jnp.asarray(np.triu(np.ones((ROUTE_TILE, ROUTE_TILE), np.float32)), BF16)
    scl, lo_f = _route(afft, tri, b, cap)
    nt = s // ROUTE_TILE
    lo3 = jnp.pad(lo_f.astype(jnp.int32), ((0, 0), (0, 0), (0, 1)), constant_values=cap)
    seg = lo3[:, :, 1:] - lo3[:, :, :-1]
    multi = lambda align: ((lo3[:, :, :-1] & (align - 1)) + seg + (PIECE - 1)) // PIECE > 1
    lo_i = jnp.concatenate([lo3.reshape(-1),
                            jnp.any(multi(8), axis=2).astype(jnp.int32).reshape(-1),
                            jnp.any(multi(16), axis=1).astype(jnp.int32).reshape(-1)])
    ye = _ffn(lo_i, tok.reshape(b, s, d), scl.reshape(b, N_EXPERTS, nt, ROUTE_TILE),
              afft.reshape(N_EXPERTS, b, nt, ROUTE_TILE), w_gate_e, w_up_e, w_down_e, l, cap, 512)
    out = _combine(lo_i, h1.reshape(b, s, d), jnp.swapaxes(scl, 1, 2), ye,
                   ln2_g[l][None, :], ln2_b[l][None, :], cap)
    return out
```
